```python
import jax, jax.numpy as jnp
from jax import lax
import numpy as np

D_MODEL = 1024
BATCH = 8
SEQ = 8192
DEPTH = 1

CHUNK = 64
Q_BLOCK = 128
SB_HEAD_DIM = 64
SB_HEADS = (D_MODEL // 2) // SB_HEAD_DIM
SB_WIDTH = SB_HEADS * SB_HEAD_DIM
GLA_HEADS = 4
GLA_DV = (D_MODEL // 2) // GLA_HEADS
GLA_DK = GLA_DV // 2
GLA_VWIDTH = GLA_HEADS * GLA_DV
GLA_KWIDTH = GLA_HEADS * GLA_DK
GLA_GATE_RANK = 16
GLA_TAU = 16.0
MIX_WIDTH = SB_WIDTH + GLA_VWIDTH
D_FF = 2816
CONV_WIDTH = 3
N_MOD = 6
EPS = 1e-6
IN_SPLITS = (SB_WIDTH, SB_WIDTH, SB_WIDTH,
             GLA_KWIDTH, GLA_KWIDTH, GLA_VWIDTH,
             GLA_VWIDTH, GLA_GATE_RANK)
IN_WIDTH = sum(IN_SPLITS)

kernel_name = "hybrid_stickbreak_gla_convffn_adaln"


def rms_norm(x, gain):
    xf = x.astype(jnp.float32)
    n = xf * lax.rsqrt(jnp.mean(xf * xf, axis=-1, keepdims=True) + EPS)
    return (n * gain.astype(jnp.float32)).astype(x.dtype)


def modulate(h, shift, scale):
    return h * (1 + scale[:, None, :]) + shift[:, None, :]


def stick_breaking_attention(q, k, v):
    B, S, H, d = q.shape
    q = q.transpose(0, 2, 1, 3)
    k = k.transpose(0, 2, 1, 3)
    v = v.transpose(0, 2, 1, 3)
    scale = d ** -0.5
    outs = []
    for i in range(S // Q_BLOCK):
        start = i * Q_BLOCK
        end = start + Q_BLOCK
        qb = q[:, :, start:end]
        kb = k[:, :, :end]
        vb = v[:, :, :end]
        z = jnp.einsum('bhqd,bhkd->bhqk', qb, kb).astype(jnp.float32) * scale
        t_pos = start + jnp.arange(Q_BLOCK)
        s_pos = jnp.arange(end)
        mask = s_pos[None, :] < t_pos[:, None]
        log_one_minus = jnp.where(mask, jax.nn.log_sigmoid(-z), 0.0)
        shifted = jnp.concatenate(
            [log_one_minus[..., 1:], jnp.zeros_like(log_one_minus[..., :1])], axis=-1)
        between = lax.cumsum(shifted, axis=3, reverse=True)
        weights = jnp.where(mask, jnp.exp(jax.nn.log_sigmoid(z) + between), 0.0)
        outs.append(jnp.einsum('bhqk,bhkd->bhqd', weights.astype(vb.dtype), vb))
    o = jnp.concatenate(outs, axis=2)
    return o.transpose(0, 2, 1, 3).reshape(B, S, H * d)


def gla_chunk_causal(q, k, v, log_alpha):
    B, S, H, dk = q.shape
    dv = v.shape[-1]
    nc = S // CHUNK
    qc = q.reshape(B, nc, CHUNK, H, dk).astype(jnp.float32) * (dk ** -0.5)
    kc = k.reshape(B, nc, CHUNK, H, dk).astype(jnp.float32)
    vc = v.reshape(B, nc, CHUNK, H, dv).astype(jnp.float32)
    la = log_alpha.reshape(B, nc, CHUNK, H, dk).astype(jnp.float32)
    cum = jnp.cumsum(la, axis=2)
    total = cum[:, :, -1]
    k_dec = kc * jnp.exp(total[:, :, None] - cum)
    chunk_kv = jnp.einsum('bnchk,bnchv->nbhkv', k_dec, vc)
    chunk_decay = jnp.exp(total).transpose(1, 0, 2, 3)

    def step(state, inp):
        decay, kv = inp
        state = decay[..., None] * state + kv
        return state, state

    s0 = jnp.zeros((B, H, dk, dv), jnp.float32)
    _, states = lax.scan(step, s0, (chunk_decay, chunk_kv))
    o = jnp.einsum('bnchk,nbhkv->bnchv', qc, states)
    return o.reshape(B, S, H * dv)


def causal_depthwise_conv(u, w, b):
    S = u.shape[1]
    up = jnp.pad(u, ((0, 0), (CONV_WIDTH - 1, 0), (0, 0)))
    out = b
    for j in range(CONV_WIDTH):
        out = out + w[j] * up[:, j:j + S]
    return out


def _fwd_setup_inputs(seed: int = 0) -> dict:
    key = jax.random.key(seed)
    ks = jax.random.split(key, 18)

    def nrm(k, shape, scale):
        return jax.random.normal(k, shape, jnp.float32) * scale

    L, D = DEPTH, D_MODEL
    return {
        "x": nrm(ks[0], (BATCH, SEQ, D), 1.0),
        "c": nrm(ks[1], (BATCH, D), 1.0),
        "w_ada": nrm(ks[2], (L, D, N_MOD * D), 0.5 * D ** -0.5),
        "b_ada": nrm(ks[3], (L, N_MOD * D), 0.01),
        "g_norm1": 1.0 + nrm(ks[4], (L, D), 0.02),
        "w_in": nrm(ks[5], (L, D, IN_WIDTH), D ** -0.5),
        "w_fg2": nrm(ks[6], (L, GLA_GATE_RANK, GLA_KWIDTH), GLA_GATE_RANK ** -0.5),
        "b_fg2": nrm(ks[7], (L, GLA_KWIDTH), 0.1),
        "g_gla_out": 1.0 + nrm(ks[8], (L, GLA_VWIDTH), 0.02),
        "w_out": nrm(ks[9], (L, MIX_WIDTH, D), MIX_WIDTH ** -0.5),
        "g_norm2": 1.0 + nrm(ks[10], (L, D), 0.02),
        "w_up": nrm(ks[11], (L, D, 2 * D_FF), D ** -0.5),
        "w_conv": nrm(ks[12], (L, CONV_WIDTH, 2 * D_FF), CONV_WIDTH ** -0.5),
        "b_conv": nrm(ks[13], (L, 2 * D_FF), 0.01),
        "w_down": nrm(ks[14], (L, D_FF, D), D_FF ** -0.5),
        "g_final": 1.0 + nrm(ks[15], (D,), 0.02),
    }


def _fwd_reference(x, c, w_ada, b_ada, g_norm1, w_in, w_fg2, b_fg2, g_gla_out, w_out,
              g_norm2, w_up, w_conv, b_conv, w_down, g_final):
    B, S, _ = x.shape
    offsets = np.cumsum(IN_SPLITS)[:-1].tolist()
    for l in range(DEPTH):
        mod = jax.nn.silu(c) @ w_ada[l] + b_ada[l]
        shift1, scale1, gate1, shift2, scale2, gate2 = jnp.split(mod, N_MOD, axis=-1)

        h = modulate(rms_norm(x, g_norm1[l]), shift1, scale1)
        proj = h @ w_in[l]
        sb_q, sb_k, sb_v, gq, gk, gv, gg, gf = jnp.split(proj, offsets, axis=-1)

        o_sb = stick_breaking_attention(
            sb_q.reshape(B, S, SB_HEADS, SB_HEAD_DIM),
            sb_k.reshape(B, S, SB_HEADS, SB_HEAD_DIM),
            sb_v.reshape(B, S, SB_HEADS, SB_HEAD_DIM))

        log_alpha = jax.nn.log_sigmoid(
            (gf @ w_fg2[l] + b_fg2[l]).astype(jnp.float32)) / GLA_TAU
        o_gla = gla_chunk_causal(
            gq.reshape(B, S, GLA_HEADS, GLA_DK),
            gk.reshape(B, S, GLA_HEADS, GLA_DK),
            gv.reshape(B, S, GLA_HEADS, GLA_DV),
            log_alpha.reshape(B, S, GLA_HEADS, GLA_DK))
        oh = o_gla.reshape(B, S, GLA_HEADS, GLA_DV)
        oh = oh * lax.rsqrt(jnp.mean(oh * oh, axis=-1, keepdims=True) + EPS)
        o_gla = (oh.reshape(B, S, GLA_VWIDTH) * g_gla_out[l].astype(jnp.float32)
                 ).astype(x.dtype) * jax.nn.silu(gg)

        mixed = jnp.concatenate([o_sb.astype(x.dtype), o_gla], axis=-1) @ w_out[l]
        x = x + (1 + gate1[:, None, :]) * mixed

        h2 = modulate(rms_norm(x, g_norm2[l]), shift2, scale2)
        u = causal_depthwise_conv(h2 @ w_up[l], w_conv[l], b_conv[l])
        val, gte = jnp.split(u, 2, axis=-1)
        x = x + (1 + gate2[:, None, :]) * ((val * jax.nn.silu(gte)) @ w_down[l])
    return rms_norm(x, g_final)


import jax as _jax
import jax.numpy as _jnp

TWIN_FORMAT = 'train_step'
FWD_PARAMS = ['x', 'c', 'w_ada', 'b_ada', 'g_norm1', 'w_in', 'w_fg2', 'b_fg2', 'g_gla_out', 'w_out', 'g_norm2', 'w_up', 'w_conv', 'b_conv', 'w_down', 'g_final']
TWIN_WEIGHTS = ['w_ada', 'b_ada', 'g_norm1', 'w_in', 'w_fg2', 'b_fg2', 'g_gla_out', 'w_out', 'g_norm2', 'w_up', 'w_conv', 'b_conv', 'w_down', 'g_final']
TWIN_DIFF_INPUT = 'x'
TWIN_INPUTS = ['x', 'c', 'w_ada', 'b_ada', 'g_norm1', 'w_in', 'w_fg2', 'b_fg2', 'g_gla_out', 'w_out', 'g_norm2', 'w_up', 'w_conv', 'b_conv', 'w_down', 'g_final', 'loss_target', 'm_w_ada', 'm_b_ada', 'm_g_norm1', 'm_w_in', 'm_w_fg2', 'm_b_fg2', 'm_g_gla_out', 'm_w_out', 'm_g_norm2', 'm_w_up', 'm_w_conv', 'm_b_conv', 'm_w_down', 'm_g_final', 'v_w_ada', 'v_b_ada', 'v_g_norm1', 'v_w_in', 'v_w_fg2', 'v_b_fg2', 'v_g_gla_out', 'v_w_out', 'v_g_norm2', 'v_w_up', 'v_w_conv', 'v_b_conv', 'v_w_down', 'v_g_final']
TWIN_OUTPUTS = ['loss', 'grad_x', 'grad_w_ada', 'grad_b_ada', 'grad_g_norm1', 'grad_w_in', 'grad_w_fg2', 'grad_b_fg2', 'grad_g_gla_out', 'grad_w_out', 'grad_g_norm2', 'grad_w_up', 'grad_w_conv', 'grad_b_conv', 'grad_w_down', 'grad_g_final', 'delta_w_ada', 'delta_b_ada', 'delta_g_norm1', 'delta_w_in', 'delta_w_fg2', 'delta_b_fg2', 'delta_g_gla_out', 'delta_w_out', 'delta_g_norm2', 'delta_w_up', 'delta_w_conv', 'delta_b_conv', 'delta_w_down', 'delta_g_final', 'new_m_w_ada', 'new_m_b_ada', 'new_m_g_norm1', 'new_m_w_in', 'new_m_w_fg2', 'new_m_b_fg2', 'new_m_g_gla_out', 'new_m_w_out', 'new_m_g_norm2', 'new_m_w_up', 'new_m_w_conv', 'new_m_b_conv', 'new_m_w_down', 'new_m_g_final', 'new_v_w_ada', 'new_v_b_ada', 'new_v_g_norm1', 'new_v_w_in', 'new_v_w_fg2', 'new_v_b_fg2', 'new_v_g_gla_out', 'new_v_w_out', 'new_v_g_norm2', 'new_v_w_up', 'new_v_w_conv', 'new_v_b_conv', 'new_v_w_down', 'new_v_g_final']
TWIN_LEAF_KINDS = {'loss': 'loss', 'grad_x': 'grad_x', 'grad_w_ada': 'grad_w', 'grad_b_ada': 'grad_w', 'grad_g_norm1': 'grad_w', 'grad_w_in': 'grad_w', 'grad_w_fg2': 'grad_w', 'grad_b_fg2': 'grad_w', 'grad_g_gla_out': 'grad_w', 'grad_w_out': 'grad_w', 'grad_g_norm2': 'grad_w', 'grad_w_up': 'grad_w', 'grad_w_conv': 'grad_w', 'grad_b_conv': 'grad_w', 'grad_w_down': 'grad_w', 'grad_g_final': 'grad_w', 'delta_w_ada': 'delta_w', 'delta_b_ada': 'delta_w', 'delta_g_norm1': 'delta_w', 'delta_w_in': 'delta_w', 'delta_w_fg2': 'delta_w', 'delta_b_fg2': 'delta_w', 'delta_g_gla_out': 'delta_w', 'delta_w_out': 'delta_w', 'delta_g_norm2': 'delta_w', 'delta_w_up': 'delta_w', 'delta_w_conv': 'delta_w', 'delta_b_conv': 'delta_w', 'delta_w_down': 'delta_w', 'delta_g_final': 'delta_w', 'new_m_w_ada': 'new_m', 'new_m_b_ada': 'new_m', 'new_m_g_norm1': 'new_m', 'new_m_w_in': 'new_m', 'new_m_w_fg2': 'new_m', 'new_m_b_fg2': 'new_m', 'new_m_g_gla_out': 'new_m', 'new_m_w_out': 'new_m', 'new_m_g_norm2': 'new_m', 'new_m_w_up': 'new_m', 'new_m_w_conv': 'new_m', 'new_m_b_conv': 'new_m', 'new_m_w_down': 'new_m', 'new_m_g_final': 'new_m', 'new_v_w_ada': 'new_v', 'new_v_b_ada': 'new_v', 'new_v_g_norm1': 'new_v', 'new_v_w_in': 'new_v', 'new_v_w_fg2': 'new_v', 'new_v_b_fg2': 'new_v', 'new_v_g_gla_out': 'new_v', 'new_v_w_out': 'new_v', 'new_v_g_norm2': 'new_v', 'new_v_w_up': 'new_v', 'new_v_w_conv': 'new_v', 'new_v_b_conv': 'new_v', 'new_v_w_down': 'new_v', 'new_v_g_final': 'new_v'}


def _forward(args):
    return _fwd_reference(*[args[k] for k in FWD_PARAMS])


def _output_shape():
    def fwd():
        inp = _fwd_setup_inputs(0)
        return _fwd_reference(*[inp[k] for k in FWD_PARAMS])
    out = _jax.eval_shape(fwd)
    return out.shape, out.dtype

N_MICROBATCH = 1
ADAM_LR = 0.001
ADAM_B1 = 0.9
ADAM_B2 = 0.999
ADAM_EPS = 1e-08
ADAM_WD = 0.01
ADAM_STEP = 10
PER_EXAMPLE_BATCH_AXIS = {'x': 0, 'c': 0, 'loss_target': 0}
SHARED_INPUTS = []
_WEIGHT_DTYPES = {'w_ada': _jnp.float32, 'b_ada': _jnp.float32, 'g_norm1': _jnp.float32, 'w_in': _jnp.float32, 'w_fg2': _jnp.float32, 'b_fg2': _jnp.float32, 'g_gla_out': _jnp.float32, 'w_out': _jnp.float32, 'g_norm2': _jnp.float32, 'w_up': _jnp.float32, 'w_conv': _jnp.float32, 'b_conv': _jnp.float32, 'w_down': _jnp.float32, 'g_final': _jnp.float32}
MOMENT_SCALE = {'w_ada': 1.422688e-01, 'b_ada': 2.982869e-01, 'g_norm1': 2.554039e-01, 'w_in': 1.603667e-01, 'w_fg2': 4.174256e-02, 'b_fg2': 1.352351e-01, 'g_gla_out': 1.615884e-01, 'w_out': 1.688107e-01, 'g_norm2': 1.786856e-01, 'w_up': 8.063960e-02, 'w_conv': 8.053999e-02, 'b_conv': 7.112309e-02, 'w_down': 1.324211e-01, 'g_final': 6.426262e+01}


def _to_microbatches(a, axis):
    t = _jnp.moveaxis(a, axis, 0)
    t = t.reshape((N_MICROBATCH, t.shape[0] // N_MICROBATCH) + t.shape[1:])
    return _jnp.moveaxis(t, 1, axis + 1)


def setup_inputs(seed: int = 0) -> dict:
    inp = _fwd_setup_inputs(seed)
    key = _jax.random.fold_in(_jax.random.key(seed), 7919)
    shape, _ = _output_shape()
    out = dict(inp)
    out["loss_target"] = _jax.random.normal(_jax.random.fold_in(key, 0), shape, _jnp.float32)
    for i, name in enumerate(TWIN_WEIGHTS):
        w = inp[name].astype(_jnp.float32)
        if MOMENT_SCALE is None:
            s = _jnp.sqrt(_jnp.mean(_jnp.square(w)) + 1e-30)
        else:
            s = MOMENT_SCALE[name]
        km, kv = _jax.random.split(_jax.random.fold_in(key, i + 1))
        out[name] = w
        out["m_" + name] = s * _jax.random.normal(km, w.shape, _jnp.float32)
        out["v_" + name] = (s * s) * _jax.random.uniform(kv, w.shape, _jnp.float32, 0.5, 1.5)
    if N_MICROBATCH > 1:
        for name, axis in PER_EXAMPLE_BATCH_AXIS.items():
            out[name] = _to_microbatches(out[name], axis)
    return {'x': out['x'], 'c': out['c'], 'w_ada': out['w_ada'], 'b_ada': out['b_ada'], 'g_norm1': out['g_norm1'], 'w_in': out['w_in'], 'w_fg2': out['w_fg2'], 'b_fg2': out['b_fg2'], 'g_gla_out': out['g_gla_out'], 'w_out': out['w_out'], 'g_norm2': out['g_norm2'], 'w_up': out['w_up'], 'w_conv': out['w_conv'], 'b_conv': out['b_conv'], 'w_down': out['w_down'], 'g_final': out['g_final'], 'loss_target': out['loss_target'], 'm_w_ada': out['m_w_ada'], 'm_b_ada': out['m_b_ada'], 'm_g_norm1': out['m_g_norm1'], 'm_w_in': out['m_w_in'], 'm_w_fg2': out['m_w_fg2'], 'm_b_fg2': out['m_b_fg2'], 'm_g_gla_out': out['m_g_gla_out'], 'm_w_out': out['m_w_out'], 'm_g_norm2': out['m_g_norm2'], 'm_w_up': out['m_w_up'], 'm_w_conv': out['m_w_conv'], 'm_b_conv': out['m_b_conv'], 'm_w_down': out['m_w_down'], 'm_g_final': out['m_g_final'], 'v_w_ada': out['v_w_ada'], 'v_b_ada': out['v_b_ada'], 'v_g_norm1': out['v_g_norm1'], 'v_w_in': out['v_w_in'], 'v_w_fg2': out['v_w_fg2'], 'v_b_fg2': out['v_b_fg2'], 'v_g_gla_out': out['v_g_gla_out'], 'v_w_out': out['v_w_out'], 'v_g_norm2': out['v_g_norm2'], 'v_w_up': out['v_w_up'], 'v_w_conv': out['v_w_conv'], 'v_b_conv': out['v_b_conv'], 'v_w_down': out['v_w_down'], 'v_g_final': out['v_g_final']}


def _loss(weights, diff, rest, loss_target):
    with _jax.named_scope("forward"):
        args = {**rest, TWIN_DIFF_INPUT: diff, **{k: w.astype(_WEIGHT_DTYPES[k]) for k, w in weights.items()}}
        y = _forward(args)
    with _jax.named_scope("loss_head"):
        err = _jnp.square(y.astype(_jnp.float32) - loss_target)
        return 0.5 * _jnp.sum(_jnp.mean(err, axis=-1)) if err.ndim else 0.5 * err


def _adamw(w, g, m, v):
    m = ADAM_B1 * m + (1.0 - ADAM_B1) * g
    v = ADAM_B2 * v + (1.0 - ADAM_B2) * _jnp.square(g)
    m_hat = m / (1.0 - ADAM_B1 ** ADAM_STEP)
    v_hat = v / (1.0 - ADAM_B2 ** ADAM_STEP)
    delta = -ADAM_LR * (m_hat / (_jnp.sqrt(v_hat) + ADAM_EPS) + ADAM_WD * w)
    return delta, m, v


def reference(x, c, w_ada, b_ada, g_norm1, w_in, w_fg2, b_fg2, g_gla_out, w_out, g_norm2, w_up, w_conv, b_conv, w_down, g_final, loss_target, m_w_ada, m_b_ada, m_g_norm1, m_w_in, m_w_fg2, m_b_fg2, m_g_gla_out, m_w_out, m_g_norm2, m_w_up, m_w_conv, m_b_conv, m_w_down, m_g_final, v_w_ada, v_b_ada, v_g_norm1, v_w_in, v_w_fg2, v_b_fg2, v_g_gla_out, v_w_out, v_g_norm2, v_w_up, v_w_conv, v_b_conv, v_w_down, v_g_final):
    given = dict(x=x, c=c, w_ada=w_ada, b_ada=b_ada, g_norm1=g_norm1, w_in=w_in, w_fg2=w_fg2, b_fg2=b_fg2, g_gla_out=g_gla_out, w_out=w_out, g_norm2=g_norm2, w_up=w_up, w_conv=w_conv, b_conv=b_conv, w_down=w_down, g_final=g_final, loss_target=loss_target, m_w_ada=m_w_ada, m_b_ada=m_b_ada, m_g_norm1=m_g_norm1, m_w_in=m_w_in, m_w_fg2=m_w_fg2, m_b_fg2=m_b_fg2, m_g_gla_out=m_g_gla_out, m_w_out=m_w_out, m_g_norm2=m_g_norm2, m_w_up=m_w_up, m_w_conv=m_w_conv, m_b_conv=m_b_conv, m_w_down=m_w_down, m_g_final=m_g_final, v_w_ada=v_w_ada, v_b_ada=v_b_ada, v_g_norm1=v_g_norm1, v_w_in=v_w_in, v_w_fg2=v_w_fg2, v_b_fg2=v_b_fg2, v_g_gla_out=v_g_gla_out, v_w_out=v_w_out, v_g_norm2=v_g_norm2, v_w_up=v_w_up, v_w_conv=v_w_conv, v_b_conv=v_b_conv, v_w_down=v_w_down, v_g_final=v_g_final)
    weights = {n: given[n] for n in TWIN_WEIGHTS}
    shared = {n: given[n] for n in SHARED_INPUTS}
    per_example = {n: given[n] for n in ['x', 'c']}
    grad_fn = _jax.value_and_grad(_loss, argnums=(0, 1))

    def one_microbatch(ex, loss_target):
        ex = dict(ex)
        diff = ex.pop(TWIN_DIFF_INPUT)
        return grad_fn(weights, diff, {**shared, **ex}, loss_target)

    if N_MICROBATCH == 1:
        loss, (grad_w, grad_x) = one_microbatch(per_example, given["loss_target"])
    else:
        def body(carry, xs):
            loss_sum, grad_sum = carry
            l_k, (gw_k, gx_k) = one_microbatch(xs[0], xs[1])
            with _jax.named_scope("update"):
                return (loss_sum + l_k, _jax.tree.map(_jnp.add, grad_sum, gw_k)), gx_k

        init = (_jnp.zeros((), _jnp.float32), _jax.tree.map(_jnp.zeros_like, weights))
        (loss, grad_w), grad_x = _jax.lax.scan(body, init, (per_example, given["loss_target"]))
    with _jax.named_scope("update"):
        delta_w, new_m, new_v = {}, {}, {}
        for n in TWIN_WEIGHTS:
            delta_w[n], new_m[n], new_v[n] = _adamw(weights[n], grad_w[n], given["m_" + n], given["v_" + n])
    return (loss, grad_x, *[grad_w[n] for n in TWIN_WEIGHTS], *[delta_w[n] for n in TWIN_WEIGHTS],
            *[new_m[n] for n in TWIN_WEIGHTS], *[new_v[n] for n in TWIN_WEIGHTS])
```

```python
import numpy as np

import jax
import jax.numpy as jnp
from jax import lax
from jax.experimental import pallas as pl
from jax.experimental.pallas import tpu as pltpu

F32, BF16 = jnp.float32, jnp.bfloat16
N_DEV = 8
D = 1024
SB_W = 512
GLA_KW, GLA_VW = 256, 512
RANK = 16
IN_W = 3088
IN_WP = 3200
D_FF = 2816
FF_T = 256
N_FT = D_FF // FF_T
EPS = 1e-6
SB_B = 256
CHUNK = 64
GLA_T = 512
VMEM_LIMIT = 56 * 1024 * 1024

LR, B1, B2, ADAM_EPS, WD, STEP = 0.001, 0.9, 0.999, 1e-08, 0.01, 10


def _params(dims=None, vmem=True):
    kw = {}
    if dims is not None:
        kw["dimension_semantics"] = dims
    if vmem:
        kw["vmem_limit_bytes"] = VMEM_LIMIT
    return pltpu.CompilerParams(**kw)


def _dot(a, b):
    return jnp.dot(a, b, preferred_element_type=F32)


def _dot_nt(a, b):
    return lax.dot_general(a, b, (((1,), (1,)), ((), ())), preferred_element_type=F32)


def _dot_tn(a, b):
    return lax.dot_general(a, b, (((0,), (0,)), ((), ())), preferred_element_type=F32)


def _hilo(x):
    hi = x.astype(BF16)
    lo = (x - hi.astype(F32)).astype(BF16)
    return hi, lo


def _sigmoid(x):
    return 1.0 / (1.0 + jnp.exp(-x))


def _log_sigmoid(x):
    return jnp.minimum(x, 0.0) - jnp.log(1.0 + jnp.exp(-jnp.abs(x)))


def _rms(x, g):
    n = x * lax.rsqrt(jnp.mean(x * x, axis=-1, keepdims=True) + EPS)
    return n * g


def _norm_mod(x, g, scale, shift):
    return _rms(x, g) * (1.0 + scale) + shift


def _exchange(x, name, gather):
    shape = x.shape if gather else x.shape[1:]

    def body(x_ref, out_ref, send_sems, recv_sems, local_sem):
        ix, iy, ic = lax.axis_index("x"), lax.axis_index("y"), lax.axis_index("c")
        me = 4 * ix + 2 * iy + ic
        peers = []
        for k in range(1, N_DEV):
            px = 1 - ix if k & 4 else ix
            py = 1 - iy if k & 2 else iy
            pc = 1 - ic if k & 1 else ic
            peers.append(((px, py, pc), 4 * px + 2 * py + pc))

        def copy(k, dev, src_slot, dst_slot):
            return pltpu.make_async_remote_copy(
                src_ref=x_ref if gather else x_ref.at[src_slot],
                dst_ref=out_ref.at[dst_slot],
                send_sem=send_sems.at[k],
                recv_sem=recv_sems.at[k],
                device_id=dev,
                device_id_type=pl.DeviceIdType.MESH,
            )

        mine = pltpu.make_async_copy(x_ref if gather else x_ref.at[me], out_ref.at[me], local_sem)
        mine.start()
        sends = [copy(k, dev, pid, me) for k, (dev, pid) in enumerate(peers)]
        for cp in sends:
            cp.start()
        for k, (dev, pid) in enumerate(peers):
            copy(k, dev, pid, pid).wait_recv()
        for cp in sends:
            cp.wait_send()
        mine.wait()

    return pl.pallas_call(
        body,
        name=name,
        out_shape=jax.ShapeDtypeStruct((N_DEV,) + tuple(shape), x.dtype),
        in_specs=[pl.BlockSpec(memory_space=pl.ANY)],
        out_specs=pl.BlockSpec(memory_space=pl.ANY),
        scratch_shapes=[
            pltpu.SemaphoreType.DMA((N_DEV - 1,)),
            pltpu.SemaphoreType.DMA((N_DEV - 1,)),
            pltpu.SemaphoreType.DMA(()),
        ],
    )(x)


def _adam(gparts, w, m, v, name, tr):
    n, rows, cols = gparts.shape
    c1 = 1.0 - B1 ** STEP
    c2 = 1.0 - B2 ** STEP

    def body(gp_ref, w_ref, m_ref, v_ref, g_ref, d_ref, nm_ref, nv_ref):
        g = gp_ref[0]
        for j in range(1, n):
            g = g + gp_ref[j]
        m_new = B1 * m_ref[...] + (1.0 - B1) * g
        v_new = B2 * v_ref[...] + (1.0 - B2) * (g * g)
        m_hat = m_new / c1
        v_hat = v_new / c2
        g_ref[...] = g
        d_ref[...] = -LR * (m_hat / (jnp.sqrt(v_hat) + ADAM_EPS) + WD * w_ref[...])
        nm_ref[...] = m_new
        nv_ref[...] = v_new

    blk = pl.BlockSpec((tr, cols), lambda i: (i, 0))
    return pl.pallas_call(
        body,
        name=name,
        grid=(rows // tr,),
        out_shape=[jax.ShapeDtypeStruct((rows, cols), F32)] * 4,
        in_specs=[pl.BlockSpec((n, tr, cols), lambda i: (0, i, 0)), blk, blk, blk],
        out_specs=[blk] * 4,
        compiler_params=_params(("parallel",)),
    )(gparts, w, m, v)


def _mm(a, b, name, out_dtype, tm, tn):
    m, k = a.shape
    _, n = b.shape

    def body(a_ref, b_ref, o_ref):
        o_ref[...] = _dot(a_ref[...].astype(BF16), b_ref[...].astype(BF16)).astype(out_dtype)

    return pl.pallas_call(
        body,
        name=name,
        grid=(n // tn, m // tm),
        out_shape=jax.ShapeDtypeStruct((m, n), out_dtype),
        in_specs=[pl.BlockSpec((tm, k), lambda j, i: (i, 0)), pl.BlockSpec((k, tn), lambda j, i: (0, j))],
        out_specs=pl.BlockSpec((tm, tn), lambda j, i: (i, j)),
        compiler_params=_params(("parallel", "parallel")),
    )(a, b)


def _mm_tn(a, b, name, tm, tn, tk):
    s, m = a.shape
    _, n = b.shape

    def body(a_ref, b_ref, o_ref):
        @pl.when(pl.program_id(2) == 0)
        def _():
            o_ref[...] = jnp.zeros_like(o_ref)

        o_ref[...] += _dot_tn(a_ref[...].astype(BF16), b_ref[...].astype(BF16))

    return pl.pallas_call(
        body,
        name=name,
        grid=(n // tn, m // tm, s // tk),
        out_shape=jax.ShapeDtypeStruct((m, n), F32),
        in_specs=[pl.BlockSpec((tk, tm), lambda j, i, k: (k, i)), pl.BlockSpec((tk, tn), lambda j, i, k: (k, j))],
        out_specs=pl.BlockSpec((tm, tn), lambda j, i, k: (i, j)),
        compiler_params=_params(("parallel", "parallel", "arbitrary")),
    )(a, b)


def _rows_call(fn, name, rows, params, out_rows, out_accs, ts):
    s = rows[0].shape[0]
    nr, npar, no = len(rows), len(params), len(out_rows)

    def body(*refs):
        r, p = refs[:nr], refs[nr:nr + npar]
        o, acc = refs[nr + npar:nr + npar + no], refs[nr + npar + no:]
        outs, sums = fn(*[t[...] for t in r], *[t[...] for t in p])
        for ref, val in zip(o, outs):
            ref[...] = val.astype(ref.dtype)
        if acc:
            @pl.when(pl.program_id(0) == 0)
            def _():
                for ref in acc:
                    ref[...] = jnp.zeros_like(ref)

            for ref, val in zip(acc, sums):
                ref[...] += val

    return pl.pallas_call(
        body,
        name=name,
        grid=(s // ts,),
        out_shape=[jax.ShapeDtypeStruct((s, w), dt) for w, dt in out_rows]
        + [jax.ShapeDtypeStruct(shape, F32) for shape in out_accs],
        in_specs=[pl.BlockSpec((ts, t.shape[1]), lambda i: (i, 0)) for t in rows]
        + [pl.BlockSpec(t.shape, lambda i: (0, 0)) for t in params],
        out_specs=[pl.BlockSpec((ts, w), lambda i: (i, 0)) for w, _ in out_rows]
        + [pl.BlockSpec(shape, lambda i: (0, 0)) for shape in out_accs],
        compiler_params=_params(("arbitrary",)),
    )(*rows, *params)


def _fwd_norm1(x, g, scale, shift):
    return (_norm_mod(x, g, scale, shift),), ()


def _resid_norm(x, mixed, gate, g, scale, shift):
    x1 = x + (1.0 + gate) * mixed
    return x1, _norm_mod(x1, g, scale, shift)


def _fwd_resid_norm(x, mixed, gate, g, scale, shift):
    return _resid_norm(x, mixed, gate, g, scale, shift), ()


def _final(x1, ffn, tgt, gate, g):
    def head(x1, ffn, gate, g):
        return _rms(x1 + (1.0 + gate) * ffn, g)

    y, vjp = jax.vjp(head, x1, ffn, gate, g)
    err = y - tgt
    dx2, dffn, dgate, dg = vjp(err * (1.0 / D))
    sq = jnp.sum(jnp.sum(err * err, axis=1, keepdims=True), axis=0, keepdims=True)
    loss = jnp.broadcast_to(sq * (0.5 / D), (1, 128))
    return (dx2, dffn), (dgate, dg, loss)


def _bwd_resid_norm(x, mixed, dx2, dh2, gate, g, scale, shift):
    _, vjp = jax.vjp(_resid_norm, x, mixed, gate, g, scale, shift)
    dx, dmixed, dgate, dg, dscale, dshift = vjp((dx2, dh2))
    return (dx, dmixed), (dgate, dg, dscale, dshift)


def _bwd_norm1(x, dh, dx1, g, scale, shift):
    _, vjp = jax.vjp(_norm_mod, x, g, scale, shift)
    dx, dg, dscale, dshift = vjp(dh)
    return (dx1 + dx,), (dg, dscale, dshift)


def _ada_fwd(c16, w):
    def body(c_ref, w_ref, o_ref):
        c = c_ref[...]
        o_ref[...] = _dot((c * _sigmoid(c)).astype(BF16), w_ref[...].astype(BF16))

    return pl.pallas_call(
        body, name="ada_fwd", out_shape=jax.ShapeDtypeStruct((c16.shape[0], w.shape[1]), F32),
        compiler_params=_params(),
    )(c16, w)


def _ada_bwd(c16, dmod16):
    def body(c_ref, d_ref, o_ref):
        c = c_ref[...]
        o_ref[...] = _dot_tn((c * _sigmoid(c)).astype(BF16), d_ref[...].astype(BF16))

    return pl.pallas_call(
        body, name="ada_bwd", out_shape=jax.ShapeDtypeStruct((c16.shape[1], dmod16.shape[1]), F32),
        compiler_params=_params(),
    )(c16, dmod16)


def _conv_taps(cur, halo, first):
    keep = jnp.where(first, 0.0, 1.0)
    h1 = halo[7:8] * keep
    h2 = halo[6:7] * keep
    row = lax.broadcasted_iota(jnp.int32, cur.shape, 0)
    s1 = jnp.where(row == 0, h1, pltpu.roll(cur, 1, 0))
    s2 = jnp.where(row == 0, h2, jnp.where(row == 1, h1, pltpu.roll(cur, 2, 0)))
    return s1, s2


def _conv_u(cur, s1, s2, wc, bc):
    return bc + wc[0:1] * s2 + wc[1:2] * s1 + wc[2:3] * cur


def _ffn_specs(ts, halo_rows):
    w2 = 2 * FF_T
    cur = pl.BlockSpec((ts, w2), lambda j, i: (i, j))
    halo = pl.BlockSpec((8, w2), lambda j, i: (jnp.maximum(i * (ts // 8) - 1, 0), j))
    nxt = pl.BlockSpec((8, w2), lambda j, i: (jnp.minimum((i + 1) * (ts // 8), halo_rows - 1), j))
    wc = pl.BlockSpec((8, w2), lambda j, i: (0, j))
    bc = pl.BlockSpec((1, w2), lambda j, i: (0, j))
    act = pl.BlockSpec((ts, FF_T), lambda j, i: (i, j))
    return cur, halo, nxt, wc, bc, act


def _conv_glu_fwd(up, wc, bc, ts):
    s = up.shape[0]
    cur, halo, _, wcs, bcs, act = _ffn_specs(ts, s // 8)

    def body(cur_ref, halo_ref, wc_ref, bc_ref, a_ref):
        x = cur_ref[...]
        s1, s2 = _conv_taps(x, halo_ref[...], pl.program_id(1) == 0)
        u = _conv_u(x, s1, s2, wc_ref[...], bc_ref[...])
        val, gte = u[:, :FF_T], u[:, FF_T:]
        a_ref[...] = (val * (gte * _sigmoid(gte))).astype(BF16)

    return pl.pallas_call(
        body, name="conv_glu_fwd", grid=(N_FT, s // ts),
        out_shape=jax.ShapeDtypeStruct((s, D_FF), BF16),
        in_specs=[cur, halo, wcs, bcs], out_specs=act,
        compiler_params=_params(("parallel", "arbitrary")),
    )(up, up, wc, bc)


def _conv_glu_bwd(up, da, wc, bc, ts):
    s = up.shape[0]
    cur, halo, _, wcs, bcs, act = _ffn_specs(ts, s // 8)

    def body(cur_ref, halo_ref, da_ref, wc_ref, bc_ref, du_ref, dwc_ref, dbc_ref):
        x = cur_ref[...]
        s1, s2 = _conv_taps(x, halo_ref[...], pl.program_id(1) == 0)
        u = _conv_u(x, s1, s2, wc_ref[...], bc_ref[...])
        val, gte = u[:, :FF_T], u[:, FF_T:]
        da = da_ref[...]
        sg = _sigmoid(gte)
        dval = da * (gte * sg)
        dgte = da * val * (sg * (1.0 + gte * (1.0 - sg)))
        du = jnp.concatenate([dval, dgte], axis=1)
        du_ref[...] = du

        @pl.when(pl.program_id(1) == 0)
        def _():
            dwc_ref[...] = jnp.zeros_like(dwc_ref)
            dbc_ref[...] = jnp.zeros_like(dbc_ref)

        dwc_ref[0:1, :] += jnp.sum(du * s2, axis=0, keepdims=True)
        dwc_ref[1:2, :] += jnp.sum(du * s1, axis=0, keepdims=True)
        dwc_ref[2:3, :] += jnp.sum(du * x, axis=0, keepdims=True)
        dbc_ref[...] += jnp.sum(du, axis=0, keepdims=True)

    return pl.pallas_call(
        body, name="conv_glu_bwd", grid=(N_FT, s // ts),
        out_shape=[jax.ShapeDtypeStruct((s, 2 * D_FF), F32), jax.ShapeDtypeStruct((8, 2 * D_FF), F32),
                   jax.ShapeDtypeStruct((1, 2 * D_FF), F32)],
        in_specs=[cur, halo, act, wcs, bcs], out_specs=[cur, wcs, bcs],
        compiler_params=_params(("parallel", "arbitrary")),
    )(up, up, da, wc, bc)


def _conv_bwd_input(du, wc, ts):
    s = du.shape[0]
    cur, _, nxt, wcs, _, _ = _ffn_specs(ts, s // 8)
    nblk = s // ts

    def body(cur_ref, nxt_ref, wc_ref, o_ref):
        x = cur_ref[...]
        keep = jnp.where(pl.program_id(1) == nblk - 1, 0.0, 1.0)
        n0 = nxt_ref[0:1] * keep
        n1 = nxt_ref[1:2] * keep
        row = lax.broadcasted_iota(jnp.int32, x.shape, 0)
        a1 = jnp.where(row == ts - 1, n0, pltpu.roll(x, ts - 1, 0))
        a2 = jnp.where(row == ts - 2, n0, jnp.where(row == ts - 1, n1, pltpu.roll(x, ts - 2, 0)))
        wc = wc_ref[...]
        o_ref[...] = (wc[2:3] * x + wc[1:2] * a1 + wc[0:1] * a2).astype(BF16)

    return pl.pallas_call(
        body, name="conv_bwd_input", grid=(N_FT, nblk),
        out_shape=jax.ShapeDtypeStruct((s, 2 * D_FF), BF16),
        in_specs=[cur, nxt, wcs], out_specs=cur,
        compiler_params=_params(("parallel", "arbitrary")),
    )(du, du, wc)


def _tri(kind):
    b = SB_B
    m = {"lower_strict": np.tril(np.ones((b, b)), -1), "upper_incl": np.triu(np.ones((b, b)), 0),
         "upper_strict": np.triu(np.ones((b, b)), 1)}[kind]
    return jnp.asarray(np.concatenate([m, np.ones((b, 128))], axis=1), BF16)


def _key_sums(x, tri):
    hi, lo = _hilo(x)
    cb = _dot(hi, tri) + _dot(lo, tri)
    return cb[:, :SB_B], cb[:, SB_B:]


def _sb_fwd(proj):
    s = proj.shape[0]
    b = SB_B

    def body(q_ref, k_ref, v_ref, tri_ref, o_ref, t_ref, c_ref, a_ref):
        i = pl.program_id(1)
        lane = lax.broadcasted_iota(jnp.int32, (b, 128), 1)
        heads = (lane < 64, lane >= 64)
        causal = lax.broadcasted_iota(jnp.int32, (b, b), 1) < lax.broadcasted_iota(jnp.int32, (b, b), 0)
        q = q_ref[...] * 0.125
        qm = [jnp.where(h, q, 0.0).astype(BF16) for h in heads]
        c_ref[...] = jnp.zeros_like(c_ref)
        a_ref[...] = jnp.zeros_like(a_ref)

        def step(jj, masked):
            rows = pl.ds(pl.multiple_of(jj * b, b), b)
            kb = k_ref[rows, :].astype(BF16)
            vb = v_ref[rows, :]
            for hh in range(2):
                z = _dot_nt(qm[hh], kb)
                lg = _log_sigmoid(-z)
                if masked:
                    lg = jnp.where(causal, lg, 0.0)
                after, total = _key_sums(lg, tri_ref[...])
                c = c_ref[hh]
                w = jnp.exp(lg + z + after + jnp.concatenate([c, c], axis=1))
                if masked:
                    w = jnp.where(causal, w, 0.0)
                a_ref[hh] += _dot(w.astype(BF16), jnp.where(heads[hh], vb, 0.0).astype(BF16))
                c_ref[hh] = c + total

        step(i, True)

        def loop(it, carry):
            step(i - 1 - it, False)
            return carry

        lax.fori_loop(0, i, loop, 0)
        o_ref[...] = a_ref[0] + a_ref[1]
        t_ref[...] = jnp.concatenate([c_ref[0], c_ref[1]], axis=1)

    return pl.pallas_call(
        body, name="sb_fwd", grid=(4, s // b),
        out_shape=[jax.ShapeDtypeStruct((s, SB_W), F32), jax.ShapeDtypeStruct((s, 2 * SB_W), F32)],
        in_specs=[pl.BlockSpec((b, 128), lambda p, i: (i, p)),
                  pl.BlockSpec((s, 128), lambda p, i: (0, 4 + p)),
                  pl.BlockSpec((s, 128), lambda p, i: (0, 8 + p)),
                  pl.BlockSpec((b, b + 128), lambda p, i: (0, 0))],
        out_specs=[pl.BlockSpec((b, 128), lambda p, i: (i, p)), pl.BlockSpec((b, 256), lambda p, i: (i, p))],
        scratch_shapes=[pltpu.VMEM((2, b, 128), F32), pltpu.VMEM((2, b, 128), F32)],
        compiler_params=_params(("parallel", "arbitrary")),
    )(proj, proj, proj, _tri("lower_strict"))


def _sb_bwd(proj, do, tot):
    s = proj.shape[0]
    b = SB_B

    def body(q_ref, k_ref, v_ref, do_ref, t_ref, ti_ref, ts_ref, dq_ref, dk_ref, dv_ref, cl_ref, ce_ref, a_ref):
        i = pl.program_id(1)
        lane = lax.broadcasted_iota(jnp.int32, (b, 128), 1)
        heads = (lane < 64, lane >= 64)
        causal = lax.broadcasted_iota(jnp.int32, (b, b), 1) < lax.broadcasted_iota(jnp.int32, (b, b), 0)
        q = q_ref[...] * 0.125
        do = do_ref[...]
        qm = [jnp.where(h, q, 0.0).astype(BF16) for h in heads]
        dom = [jnp.where(h, do, 0.0).astype(BF16) for h in heads]
        cl_ref[...] = jnp.zeros_like(cl_ref)
        ce_ref[...] = jnp.zeros_like(ce_ref)
        a_ref[...] = jnp.zeros_like(a_ref)

        @pl.when(i == 0)
        def _():
            dk_ref[...] = jnp.zeros_like(dk_ref)
            dv_ref[...] = jnp.zeros_like(dv_ref)

        def step(jj, masked):
            rows = pl.ds(pl.multiple_of(jj * b, b), b)
            kf = k_ref[rows, :]
            kb = kf.astype(BF16)
            vb = v_ref[rows, :].astype(BF16)
            dk = jnp.zeros((b, 128), F32)
            dv = jnp.zeros((b, 128), F32)
            for hh in range(2):
                z = _dot_nt(qm[hh], kb)
                lg = _log_sigmoid(-z)
                if masked:
                    lg = jnp.where(causal, lg, 0.0)
                upto, total = _key_sums(lg, ti_ref[...])
                cl = cl_ref[hh]
                t = t_ref[:, hh * 128:(hh + 1) * 128]
                rest = jnp.concatenate([t - cl, t - cl], axis=1) - upto
                lsz = lg + z
                w = jnp.exp(lsz + rest)
                sig = jnp.exp(lsz)
                if masked:
                    w = jnp.where(causal, w, 0.0)
                e = w * _dot_nt(dom[hh], vb)
                before, etot = _key_sums(e, ts_ref[...])
                ce = ce_ref[hh]
                dz = e - sig * (e + before + jnp.concatenate([ce, ce], axis=1))
                if masked:
                    dz = jnp.where(causal, dz, 0.0)
                dzb = dz.astype(BF16)
                a_ref[hh] += _dot(dzb, jnp.where(heads[hh], kf, 0.0).astype(BF16))
                dk = dk + _dot_tn(dzb, qm[hh])
                dv = dv + _dot_tn(w.astype(BF16), dom[hh])
                cl_ref[hh] = cl + total
                ce_ref[hh] = ce + etot
            dk_ref[rows, :] += dk
            dv_ref[rows, :] += dv

        def loop(jj, carry):
            step(jj, False)
            return carry

        lax.fori_loop(0, i, loop, 0)
        step(i, True)
        dq_ref[...] = (a_ref[0] + a_ref[1]) * 0.125

    blk = pl.BlockSpec((b, 128), lambda p, i: (i, p))
    full = pl.BlockSpec((s, 128), lambda p, i: (0, p))
    tri = pl.BlockSpec((b, b + 128), lambda p, i: (0, 0))
    return pl.pallas_call(
        body, name="sb_bwd", grid=(4, s // b),
        out_shape=[jax.ShapeDtypeStruct((s, SB_W), F32)] * 3,
        in_specs=[blk, pl.BlockSpec((s, 128), lambda p, i: (0, 4 + p)), pl.BlockSpec((s, 128), lambda p, i: (0, 8 + p)),
                  blk, pl.BlockSpec((b, 256), lambda p, i: (i, p)), tri, tri],
        out_specs=[blk, full, full],
        scratch_shapes=[pltpu.VMEM((2, b, 128), F32)] * 3,
        compiler_params=_params(("parallel", "arbitrary")),
    )(proj, proj, proj, do, tot, _tri("upper_incl"), _tri("upper_strict"))


_COL_Q, _COL_K, _COL_V, _COL_G, _COL_F = 12, 14, 8, 10, 24


def _gla_consts():
    c = CHUNK
    incl = np.tril(np.ones((c, c)), 0)
    strict = np.tril(np.ones((c, c)), -1)
    bd = np.zeros((128, 256))
    bd[:64, :128] = 1.0
    bd[64:, 128:] = 1.0
    return jnp.asarray(incl, BF16), jnp.asarray(strict, BF16), jnp.asarray(bd, F32)


def _time_sums(tri, x):
    hi, lo = _hilo(x)
    return _dot(tri, hi) + _dot(tri, lo)


def _gla_gate(o, gg, g):
    parts = []
    for h in range(2):
        oh = o[:, h * 128:(h + 1) * 128]
        parts.append(oh * lax.rsqrt(jnp.mean(oh * oh, axis=-1, keepdims=True) + EPS))
    return (jnp.concatenate(parts, axis=1) * g) * (gg * _sigmoid(gg))


def _gla_chunk(la_c, k_c, incl, ones_cv):
    cum = _time_sums(incl, la_c)
    total = cum[CHUNK - 1:CHUNK]
    edec = jnp.exp(total - cum)
    kdec = k_c * edec
    hi, lo = _hilo(la_c)
    dec = jnp.exp(_dot_tn(hi, ones_cv) + _dot_tn(lo, ones_cv))
    return edec, kdec, dec


def _gla_fwd(proj, wfg, bfg, gain):
    s = proj.shape[0]
    t = GLA_T
    nch = t // CHUNK
    incl, _, bd = _gla_consts()

    def body(q_ref, k_ref, v_ref, gg_ref, f_ref, wf_ref, bf_ref, g_ref, incl_ref, bd_ref, o_ref, st_ref, state_ref):
        @pl.when(pl.program_id(1) == 0)
        def _():
            state_ref[...] = jnp.zeros_like(state_ref)

        la = _log_sigmoid(_dot(f_ref[...].astype(BF16), wf_ref[...].astype(BF16)) + bf_ref[...]) * (1.0 / 16.0)
        ones_cv = jnp.ones((CHUNK, 256), BF16)
        state = state_ref[...]
        for cc in range(nch):
            rows = slice(cc * CHUNK, (cc + 1) * CHUNK)
            _, kdec, dec = _gla_chunk(la[rows], k_ref[rows, :], incl_ref[...], ones_cv)
            kv = _dot_tn(kdec.astype(BF16), v_ref[rows, :].astype(BF16))
            state = dec * state + bd_ref[...] * kv
            st_ref[cc] = state
            o = _dot((q_ref[rows, :] * 0.125).astype(BF16), state.astype(BF16))
            o_ref[rows, :] = _gla_gate(o, gg_ref[rows, :], g_ref[...])
        state_ref[...] = state

    def col(width, blk):
        return pl.BlockSpec((t, width), lambda p, i: (i, blk + p))

    return pl.pallas_call(
        body, name="gla_fwd", grid=(2, s // t),
        out_shape=[jax.ShapeDtypeStruct((s, GLA_VW), F32), jax.ShapeDtypeStruct((2, s // CHUNK, 128, 256), F32)],
        in_specs=[col(128, _COL_Q), col(128, _COL_K), col(256, _COL_V), col(256, _COL_G),
                  pl.BlockSpec((t, 128), lambda p, i: (i, _COL_F)),
                  pl.BlockSpec((128, 128), lambda p, i: (0, p)), pl.BlockSpec((1, 128), lambda p, i: (0, p)),
                  pl.BlockSpec((1, 256), lambda p, i: (0, p)),
                  pl.BlockSpec((CHUNK, CHUNK), lambda p, i: (0, 0)), pl.BlockSpec((128, 256), lambda p, i: (0, 0))],
        out_specs=[pl.BlockSpec((t, 256), lambda p, i: (i, p)),
                   pl.BlockSpec((None, nch, 128, 256), lambda p, i: (p, i, 0, 0))],
        scratch_shapes=[pltpu.VMEM((128, 256), F32)],
        compiler_params=_params(("parallel", "arbitrary")),
    )(proj, proj, proj, proj, proj, wfg, bfg, gain, incl, bd)


def _gla_bwd(proj, do, states, wfg, bfg, gain):
    s = proj.shape[0]
    t = GLA_T
    nch = t // CHUNK
    nblk = s // t
    incl, strict, bd = _gla_consts()

    def body(q_ref, k_ref, v_ref, gg_ref, f_ref, do_ref, st_ref, sp_ref, wf_ref, bf_ref, g_ref, incl_ref, str_ref,
             bd_ref, dq_ref, dk_ref, dv_ref, dgg_ref, df_ref, dwf_ref, dbf_ref, dg_ref, carry_ref, dfs_ref):
        i = pl.program_id(1)

        @pl.when(i == 0)
        def _():
            carry_ref[...] = jnp.zeros_like(carry_ref)
            dwf_ref[...] = jnp.zeros_like(dwf_ref)
            dbf_ref[...] = jnp.zeros_like(dbf_ref)
            dg_ref[...] = jnp.zeros_like(dg_ref)

        fb = f_ref[...].astype(BF16)
        f = _dot(fb, wf_ref[...].astype(BF16)) + bf_ref[...]
        la = _log_sigmoid(f) * (1.0 / 16.0)
        dla_df = _sigmoid(-f) * (1.0 / 16.0)
        ones_cv = jnp.ones((CHUNK, 256), BF16)
        ones_8v = jnp.ones((8, 256), BF16)
        first_block = jnp.where(i == nblk - 1, 0.0, 1.0)
        carry = carry_ref[...]
        dgain = jnp.zeros((1, 256), F32)
        for cc in reversed(range(nch)):
            rows = slice(cc * CHUNK, (cc + 1) * CHUNK)
            k_c = k_ref[rows, :]
            edec, kdec, dec = _gla_chunk(la[rows], k_c, incl_ref[...], ones_cv)
            state = st_ref[cc]
            prev = st_ref[cc - 1] if cc > 0 else sp_ref[0] * first_block
            qs = (q_ref[rows, :] * 0.125).astype(BF16)
            sb16 = state.astype(BF16)
            o = _dot(qs, sb16)
            _, vjp = jax.vjp(_gla_gate, o, gg_ref[rows, :], g_ref[...])
            do_c, dgg_c, dg_c = vjp(do_ref[rows, :])
            dgain = dgain + dg_c
            dgg_ref[rows, :] = dgg_c
            do16 = do_c.astype(BF16)
            dq_ref[rows, :] = _dot_nt(do16, sb16) * 0.125
            grad = bd_ref[...] * _dot_tn(qs, do16) + carry
            g16 = grad.astype(BF16)
            dv_ref[rows, :] = _dot(kdec.astype(BF16), g16)
            dkdec = _dot_nt(v_ref[rows, :].astype(BF16), g16)
            hi, lo = _hilo(grad * prev * dec)
            ddec = (_dot_nt(ones_8v, hi) + _dot_nt(ones_8v, lo))[0:1]
            dk_ref[rows, :] = dkdec * edec
            dla = _time_sums(str_ref[...], dkdec * kdec) + ddec
            dfs_ref[rows, :] = dla * dla_df[rows]
            carry = dec * grad
        carry_ref[...] = carry
        df = dfs_ref[...]
        df16 = df.astype(BF16)
        df_ref[...] = _dot_nt(df16, wf_ref[...].astype(BF16))
        dwf_ref[...] += _dot_tn(fb, df16)
        dbf_ref[...] += jnp.sum(df, axis=0, keepdims=True)
        dg_ref[...] += dgain

    def col(width, blk):
        return pl.BlockSpec((t, width), lambda p, i: (nblk - 1 - i, blk + p))

    def pair(width):
        return pl.BlockSpec((t, width), lambda p, i: (nblk - 1 - i, p))

    return pl.pallas_call(
        body, name="gla_bwd", grid=(2, nblk),
        out_shape=[jax.ShapeDtypeStruct((s, GLA_KW), F32), jax.ShapeDtypeStruct((s, GLA_KW), F32),
                   jax.ShapeDtypeStruct((s, GLA_VW), F32), jax.ShapeDtypeStruct((s, GLA_VW), F32),
                   jax.ShapeDtypeStruct((2, s, 128), F32), jax.ShapeDtypeStruct((128, GLA_KW), F32),
                   jax.ShapeDtypeStruct((1, GLA_KW), F32), jax.ShapeDtypeStruct((1, GLA_VW), F32)],
        in_specs=[col(128, _COL_Q), col(128, _COL_K), col(256, _COL_V), col(256, _COL_G),
                  pl.BlockSpec((t, 128), lambda p, i: (nblk - 1 - i, _COL_F)),
                  pl.BlockSpec((t, 256), lambda p, i: (nblk - 1 - i, SB_W // 256 + p)),
                  pl.BlockSpec((None, nch, 128, 256), lambda p, i: (p, nblk - 1 - i, 0, 0)),
                  pl.BlockSpec((None, 1, 128, 256), lambda p, i: (p, jnp.maximum((nblk - 1 - i) * nch - 1, 0), 0, 0)),
                  pl.BlockSpec((128, 128), lambda p, i: (0, p)), pl.BlockSpec((1, 128), lambda p, i: (0, p)),
                  pl.BlockSpec((1, 256), lambda p, i: (0, p)),
                  pl.BlockSpec((CHUNK, CHUNK), lambda p, i: (0, 0)), pl.BlockSpec((CHUNK, CHUNK), lambda p, i: (0, 0)),
                  pl.BlockSpec((128, 256), lambda p, i: (0, 0))],
        out_specs=[pair(128), pair(128), pair(256), pair(256),
                   pl.BlockSpec((None, t, 128), lambda p, i: (p, nblk - 1 - i, 0)),
                   pl.BlockSpec((128, 128), lambda p, i: (0, p)), pl.BlockSpec((1, 128), lambda p, i: (0, p)),
                   pl.BlockSpec((1, 256), lambda p, i: (0, p))],
        scratch_shapes=[pltpu.VMEM((128, 256), F32), pltpu.VMEM((t, 128), F32)],
        compiler_params=_params(("parallel", "arbitrary")),
    )(proj, proj, proj, proj, proj, do, states, states, wfg, bfg, gain, incl, strict, bd)


def _pad_flat(parts, rows, cols, dtype):
    flat = jnp.concatenate([p.reshape(-1).astype(dtype) for p in parts])
    return jnp.pad(flat, (0, rows * cols - flat.shape[0])).reshape(rows, cols)


def _pad_flat8(parts, rows, cols):
    flat = jnp.concatenate([p.reshape(N_DEV, -1) for p in parts], axis=1)
    return jnp.pad(flat, ((0, 0), (0, rows * cols - flat.shape[1]))).reshape(N_DEV, rows, cols)


def _split_flat(flat, shapes):
    flat = flat.reshape(-1)
    out, off = [], 0
    for shape in shapes:
        n = int(np.prod(shape))
        out.append(flat[off:off + n].reshape(shape))
        off += n
    return out


def _cols_by_dev(a):
    r, c = a.shape
    return a.reshape(r, N_DEV, c // N_DEV).transpose(1, 0, 2)


def _cols_from_dev(a):
    _, r, n = a.shape
    return a.transpose(1, 0, 2).reshape(r, N_DEV * n)


def _ff_pair(a):
    lead = a.shape[:-1]
    return a.reshape(*lead, 2, N_FT, FF_T).swapaxes(-3, -2).reshape(*lead, 2 * D_FF)


def _ff_unpair(a):
    lead = a.shape[:-1]
    return a.reshape(*lead, N_FT, 2, FF_T).swapaxes(-3, -2).reshape(*lead, 2 * D_FF)


def kernel(x, c, w_ada, b_ada, g_norm1, w_in, w_fg2, b_fg2, g_gla_out, w_out, g_norm2, w_up, w_conv, b_conv, w_down, g_final, loss_target, m_w_ada, m_b_ada, m_g_norm1, m_w_in, m_w_fg2, m_b_fg2, m_g_gla_out, m_w_out, m_g_norm2, m_w_up, m_w_conv, m_b_conv, m_w_down, m_g_final, v_w_ada, v_b_ada, v_g_norm1, v_w_in, v_w_fg2, v_b_fg2, v_g_gla_out, v_w_out, v_g_norm2, v_w_up, v_w_conv, v_b_conv, v_w_down, v_g_final):
    s = x.shape[1]
    me = 4 * lax.axis_index("x") + 2 * lax.axis_index("y") + lax.axis_index("c")
    xs, tgt = x[0], loss_target[0]
    ts = min(512, s)

    small = _exchange(_pad_flat([c, w_conv, w_fg2], 8, 512, F32), "gather_small", True).reshape(N_DEV, -1)
    c_all = small[:, :D]
    wconv = _cols_from_dev(small[:, D:D + 3 * 704].reshape(N_DEV, 3, 704))
    wfg = _cols_from_dev(small[:, D + 3 * 704:D + 3 * 704 + RANK * 32].reshape(N_DEV, RANK, 32))
    n_in, n_out, n_up, n_down = D * 386, 128 * D, D * 704, 352 * D
    big = _exchange(_pad_flat([w_in, w_out, w_up, w_down], 1584, 1024, BF16), "gather_weights", True).reshape(N_DEV, -1)
    o1, o2, o3 = n_in, n_in + n_out, n_in + n_out + n_up
    win = jnp.pad(_cols_from_dev(big[:, :o1].reshape(N_DEV, D, 386)), ((0, 0), (0, IN_WP - IN_W)))
    wout = big[:, o1:o2].reshape(D, D)
    wup = _ff_pair(_cols_from_dev(big[:, o2:o3].reshape(N_DEV, D, 704)))
    wdown = big[:, o3:o3 + n_down].reshape(D_FF, D)
    win_t, wout_t, wup_t, wdown_t = win.T, wout.T, wup.T, wdown.T
    wconv_p = jnp.pad(_ff_pair(wconv), ((0, 5), (0, 0)))
    bconv_p = _ff_pair(b_conv)
    wfg_p = jnp.pad(wfg, ((0, 128 - RANK), (0, 0)))

    c16 = jnp.pad(c_all, ((0, 8), (0, 0)))
    modp = _ada_fwd(c16, w_ada[0])[:N_DEV]
    mod_all = _exchange(modp, "gather_mod", True)
    mod = lax.dynamic_index_in_dim(mod_all, me, axis=1, keepdims=False).reshape(1, 6 * D) + b_ada
    shift1, scale1, gate1, shift2, scale2, gate2 = [mod[:, k * D:(k + 1) * D] for k in range(6)]

    (h,) = _rows_call(_fwd_norm1, "norm1", [xs], [g_norm1, scale1, shift1], [(D, BF16)], [], ts)
    proj = _mm(h, win, "in_proj", F32, ts, 640)
    o_sb, sb_tot = _sb_fwd(proj)
    o_gla, states = _gla_fwd(proj, wfg_p, b_fg2, g_gla_out)
    cat = jnp.concatenate([o_sb, o_gla], axis=1).astype(BF16)
    mixed = _mm(cat, wout, "out_proj", F32, ts, 512)
    x1, h2 = _rows_call(_fwd_resid_norm, "resid_norm2", [xs, mixed], [gate1, g_norm2, scale2, shift2],
                        [(D, F32), (D, BF16)], [], ts)
    up = _mm(h2, wup, "up_proj", F32, ts, 512)
    act = _conv_glu_fwd(up, wconv_p, bconv_p, ts)
    ffn = _mm(act, wdown, "down_proj", F32, ts, 512)
    dx2, dffn, dgate2, dg_final, loss_part = _rows_call(
        _final, "final", [x1, ffn, tgt], [gate2, g_final.reshape(1, D)],
        [(D, F32), (D, BF16)], [(1, D), (1, D), (1, 128)], ts)

    dw_down = _mm_tn(act, dffn, "down_wgrad", 1408, 1024, ts)
    da = _mm(dffn, wdown_t, "down_dgrad", F32, ts, 256)
    du, dwconv_p, dbconv_p = _conv_glu_bwd(up, da, wconv_p, bconv_p, ts)
    dup = _conv_bwd_input(du, wconv_p, ts)
    dw_up = _mm_tn(h2, dup, "up_wgrad", 1024, 512, ts)
    dh2 = _mm(dup, wup_t, "up_dgrad", F32, ts, 512)
    dx1, dmixed, dgate1, dg_norm2, dscale2, dshift2 = _rows_call(
        _bwd_resid_norm, "resid_norm2_bwd", [xs, mixed, dx2, dh2], [gate1, g_norm2, scale2, shift2],
        [(D, F32), (D, BF16)], [(1, D)] * 4, ts)
    dw_out = _mm_tn(cat, dmixed, "out_wgrad", 1024, 512, ts)
    dcat = _mm(dmixed, wout_t, "out_dgrad", F32, ts, 512)
    dq, dk, dv = _sb_bwd(proj, dcat, sb_tot)
    dgq, dgk, dgv, dgg, dgf, dwfg_p, dbfg, dg_gla = _gla_bwd(proj, dcat, states, wfg_p, b_fg2, g_gla_out)
    dproj = jnp.concatenate([dq, dk, dv, dgq, dgk, dgv, dgg, dgf[0] + dgf[1]], axis=1).astype(BF16)
    dw_in = _mm_tn(h, dproj, "in_wgrad", 1024, 640, ts)
    dh = _mm(dproj, win_t, "in_dgrad", F32, ts, 512)
    grad_x, dg_norm1, dscale1, dshift1 = _rows_call(
        _bwd_norm1, "norm1_bwd", [xs, dh, dx1], [g_norm1, scale1, shift1], [(D, F32)], [(1, D)] * 3, ts)

    dmod = jnp.concatenate([dshift1, dscale1, dgate1, dshift2, dscale2, dgate2], axis=1)
    small_parts = [dmod, dg_norm1, dbfg, dg_gla, dg_norm2, _ff_unpair(dbconv_p), dg_final, loss_part[:, :1]]
    small_shapes = [(1, 6 * D), (1, D), (1, GLA_KW), (1, GLA_VW), (1, D), (1, 2 * D_FF), (D,)]
    sg = _exchange(_pad_flat(small_parts, 24, 1024, F32), "gather_small_grads", True)
    n_small = sum(int(np.prod(sh)) for sh in small_shapes)
    loss = jnp.sum(sg.reshape(N_DEV, -1)[:, n_small])
    small_w = [b_ada, g_norm1, b_fg2, g_gla_out, g_norm2, b_conv, g_final]
    small_m = [m_b_ada, m_g_norm1, m_b_fg2, m_g_gla_out, m_g_norm2, m_b_conv, m_g_final]
    small_v = [v_b_ada, v_g_norm1, v_b_fg2, v_g_gla_out, v_g_norm2, v_b_conv, v_g_final]
    s_out = _adam(sg, _pad_flat(small_w, 24, 1024, F32), _pad_flat(small_m, 24, 1024, F32),
                  _pad_flat(small_v, 24, 1024, F32), "adam_small", 24)
    s_g, s_d, s_m, s_v = [_split_flat(t, small_shapes) for t in s_out]

    dmod_all = sg.reshape(N_DEV, -1)[:, :6 * D].reshape(N_DEV, N_DEV, 768)
    dmod_mine = lax.dynamic_index_in_dim(dmod_all, me, axis=1, keepdims=False)
    dw_ada = _ada_bwd(c16, jnp.pad(dmod_mine, ((0, 8), (0, 0))))
    a_out = _adam(dw_ada.reshape(1, 768, 1024), w_ada.reshape(768, 1024), m_w_ada.reshape(768, 1024),
                  v_w_ada.reshape(768, 1024), "adam_ada", 256)
    a_g, a_d, a_m, a_v = [t.reshape(1, D, 768) for t in a_out]

    big_shapes = [(1, D, 386), (1, RANK, 32), (1, 128, D), (1, D, 704), (1, 3, 704), (1, 352, D)]
    send = _pad_flat8([
        _cols_by_dev(dw_in[:, :IN_W]), _cols_by_dev(dwfg_p[:RANK]), dw_out.reshape(N_DEV, 128, D),
        _cols_by_dev(_ff_unpair(dw_up)), _cols_by_dev(_ff_unpair(dwconv_p[:3])), dw_down.reshape(N_DEV, 352, D),
    ], 1600, 1024)
    recv = _exchange(send, "scatter_grads", False)
    big_w = [w_in, w_fg2, w_out, w_up, w_conv, w_down]
    big_m = [m_w_in, m_w_fg2, m_w_out, m_w_up, m_w_conv, m_w_down]
    big_v = [v_w_in, v_w_fg2, v_w_out, v_w_up, v_w_conv, v_w_down]
    b_out = _adam(recv, _pad_flat(big_w, 1600, 1024, F32), _pad_flat(big_m, 1600, 1024, F32),
                  _pad_flat(big_v, 1600, 1024, F32), "adam_big", 200)
    b_g, b_d, b_m, b_v = [_split_flat(t, big_shapes) for t in b_out]

    def ordered(a, sm, bg):
        return [a, sm[0], sm[1], bg[0], bg[1], sm[2], sm[3], bg[2], sm[4], bg[3], bg[4], sm[5], bg[5], sm[6]]

    return (loss, grad_x[None], *ordered(a_g, s_g, b_g), *ordered(a_d, s_d, b_d),
            *ordered(a_m, s_m, b_m), *ordered(a_v, s_v, b_v))
```

```python
import numpy as np

import jax
import jax.numpy as jnp
from jax import lax
from jax.experimental import pallas as pl
from jax.experimental.pallas import tpu as pltpu

F32, BF16 = jnp.float32, jnp.bfloat16
N_DEV = 8
D = 1024
SB_W = 512
GLA_KW, GLA_VW = 256, 512
RANK = 16
IN_W = 3088
IN_WP = 3200
D_FF = 2816
FF_T = 256
N_FT = D_FF // FF_T
EPS = 1e-6
SB_B = 256
SB_DEAD = -110.0
CHUNK = 64
GLA_T = 512
VMEM_LIMIT = 56 * 1024 * 1024

LR, B1, B2, ADAM_EPS, WD, STEP = 0.001, 0.9, 0.999, 1e-08, 0.01, 10


def _params(dims=None, vmem=True):
    kw = {}
    if dims is not None:
        kw["dimension_semantics"] = dims
    if vmem:
        kw["vmem_limit_bytes"] = VMEM_LIMIT
    return pltpu.CompilerParams(**kw)


def _dot(a, b):
    return jnp.dot(a, b, preferred_element_type=F32)


def _dot_nt(a, b):
    return lax.dot_general(a, b, (((1,), (1,)), ((), ())), preferred_element_type=F32)


def _dot_tn(a, b):
    return lax.dot_general(a, b, (((0,), (0,)), ((), ())), preferred_element_type=F32)


def _hilo(x):
    hi = x.astype(BF16)
    lo = (x - hi.astype(F32)).astype(BF16)
    return hi, lo


def _sigmoid(x):
    return 1.0 / (1.0 + jnp.exp(-x))


def _log_sigmoid(x):
    return jnp.minimum(x, 0.0) - jnp.log(1.0 + jnp.exp(-jnp.abs(x)))


def _rms(x, g):
    n = x * lax.rsqrt(jnp.mean(x * x, axis=-1, keepdims=True) + EPS)
    return n * g


def _norm_mod(x, g, scale, shift):
    return _rms(x, g) * (1.0 + scale) + shift


N_PEER = N_DEV - 1


def _exchange_copies(x_refs, out_refs, send_sems, recv_sems, local_sems, gather):
    ix, iy, ic = lax.axis_index("x"), lax.axis_index("y"), lax.axis_index("c")
    me = 4 * ix + 2 * iy + ic
    peers = []
    for k in range(1, N_DEV):
        px = 1 - ix if k & 4 else ix
        py = 1 - iy if k & 2 else iy
        pc = 1 - ic if k & 1 else ic
        peers.append(((px, py, pc), 4 * px + 2 * py + pc))

    def copy(a, k, dev, src_slot, dst_slot):
        return pltpu.make_async_remote_copy(
            src_ref=x_refs[a] if gather else x_refs[a].at[src_slot],
            dst_ref=out_refs[a].at[dst_slot],
            send_sem=send_sems.at[a * N_PEER + k],
            recv_sem=recv_sems.at[a * N_PEER + k],
            device_id=dev,
            device_id_type=pl.DeviceIdType.MESH,
        )

    n = len(x_refs)
    mine = [pltpu.make_async_copy(x_refs[a] if gather else x_refs[a].at[me], out_refs[a].at[me], local_sems.at[a])
            for a in range(n)]
    sends = [copy(a, k, dev, pid, me) for a in range(n) for k, (dev, pid) in enumerate(peers)]
    recvs = [copy(a, k, dev, pid, pid) for a in range(n) for k, (dev, pid) in enumerate(peers)]
    return mine, sends, recvs


def _exchange_scratch(n):
    return [pltpu.SemaphoreType.DMA((n * N_PEER,)), pltpu.SemaphoreType.DMA((n * N_PEER,)),
            pltpu.SemaphoreType.DMA((n,))]


def _exchange_shapes(arrays, gather):
    return [jax.ShapeDtypeStruct((N_DEV,) + tuple(x.shape if gather else x.shape[1:]), x.dtype) for x in arrays]


def _exchange(arrays, name, gather):
    n = len(arrays)

    def body(*refs):
        mine, sends, recvs = _exchange_copies(refs[:n], refs[n:2 * n], *refs[2 * n:], gather)
        for cp in mine + sends:
            cp.start()
        for cp in recvs:
            cp.wait_recv()
        for cp in sends:
            cp.wait_send()
        for cp in mine:
            cp.wait()

    return pl.pallas_call(
        body,
        name=name,
        out_shape=_exchange_shapes(arrays, gather),
        in_specs=[pl.BlockSpec(memory_space=pl.ANY)] * n,
        out_specs=[pl.BlockSpec(memory_space=pl.ANY)] * n,
        scratch_shapes=_exchange_scratch(n),
    )(*arrays)


def _adam_math(g, w, m, v):
    m_new = B1 * m + (1.0 - B1) * g
    v_new = B2 * v + (1.0 - B2) * (g * g)
    m_hat = m_new / (1.0 - B1 ** STEP)
    v_hat = v_new / (1.0 - B2 ** STEP)
    return -LR * (m_hat / (jnp.sqrt(v_hat) + ADAM_EPS) + WD * w), m_new, v_new


def _adam(gparts, w, m, v, name, tr):
    n, rows, cols = gparts.shape

    def body(gp_ref, w_ref, m_ref, v_ref, g_ref, d_ref, nm_ref, nv_ref):
        g = gp_ref[0]
        for j in range(1, n):
            g = g + gp_ref[j]
        g_ref[...] = g
        d_ref[...], nm_ref[...], nv_ref[...] = _adam_math(g, w_ref[...], m_ref[...], v_ref[...])

    blk = pl.BlockSpec((tr, cols), lambda i: (i, 0))
    return pl.pallas_call(
        body,
        name=name,
        grid=(rows // tr,),
        out_shape=[jax.ShapeDtypeStruct((rows, cols), F32)] * 4,
        in_specs=[pl.BlockSpec((n, tr, cols), lambda i: (0, i, 0)), blk, blk, blk],
        out_specs=[blk] * 4,
        compiler_params=_params(("parallel",)),
    )(gparts, w, m, v)


def _adam_rows(parts, ws, ms, vs, name):
    n = parts.shape[0]
    k = len(ws)
    widths = [w.shape[1] for w in ws]

    def body(*refs):
        p_ref, w_refs, m_refs, v_refs = refs[0], refs[1:1 + k], refs[1 + k:1 + 2 * k], refs[1 + 2 * k:1 + 3 * k]
        outs = refs[1 + 3 * k:]
        total = p_ref[0:1, :]
        for j in range(1, n):
            total = total + p_ref[j:j + 1, :]
        off = 0
        for a, width in enumerate(widths):
            g = total[:, off:off + width]
            off += width
            outs[4 * a][...] = g
            outs[4 * a + 1][...], outs[4 * a + 2][...], outs[4 * a + 3][...] = _adam_math(
                g, w_refs[a][...], m_refs[a][...], v_refs[a][...])

    return pl.pallas_call(
        body, name=name,
        out_shape=[jax.ShapeDtypeStruct((1, width), F32) for width in widths for _ in range(4)],
        compiler_params=_params(),
    )(parts, *ws, *ms, *vs)


def _mm(a, b, name, out_dtype, tm, tn):
    m, k = a.shape
    _, n = b.shape

    def body(a_ref, b_ref, o_ref):
        o_ref[...] = _dot(a_ref[...].astype(BF16), b_ref[...].astype(BF16)).astype(out_dtype)

    return pl.pallas_call(
        body,
        name=name,
        grid=(n // tn, m // tm),
        out_shape=jax.ShapeDtypeStruct((m, n), out_dtype),
        in_specs=[pl.BlockSpec((tm, k), lambda j, i: (i, 0)), pl.BlockSpec((k, tn), lambda j, i: (0, j))],
        out_specs=pl.BlockSpec((tm, tn), lambda j, i: (i, j)),
        compiler_params=_params(("parallel", "parallel")),
    )(a, b)


def _mm_nt(a, b, name, out_dtype, tm, tn):
    m, k = a.shape
    n, _ = b.shape

    def body(a_ref, b_ref, o_ref):
        o_ref[...] = _dot_nt(a_ref[...].astype(BF16), b_ref[...].astype(BF16)).astype(out_dtype)

    return pl.pallas_call(
        body,
        name=name,
        grid=(n // tn, m // tm),
        out_shape=jax.ShapeDtypeStruct((m, n), out_dtype),
        in_specs=[pl.BlockSpec((tm, k), lambda j, i: (i, 0)), pl.BlockSpec((tn, k), lambda j, i: (j, 0))],
        out_specs=pl.BlockSpec((tm, tn), lambda j, i: (i, j)),
        compiler_params=_params(("parallel", "parallel")),
    )(a, b)


def _mm_tn(a, b, name, tm, tn, tk):
    s, m = a.shape
    _, n = b.shape

    def body(a_ref, b_ref, o_ref):
        @pl.when(pl.program_id(2) == 0)
        def _():
            o_ref[...] = jnp.zeros_like(o_ref)

        o_ref[...] += _dot_tn(a_ref[...].astype(BF16), b_ref[...].astype(BF16))

    return pl.pallas_call(
        body,
        name=name,
        grid=(n // tn, m // tm, s // tk),
        out_shape=jax.ShapeDtypeStruct((m, n), F32),
        in_specs=[pl.BlockSpec((tk, tm), lambda j, i, k: (k, i)), pl.BlockSpec((tk, tn), lambda j, i, k: (k, j))],
        out_specs=pl.BlockSpec((tm, tn), lambda j, i, k: (i, j)),
        compiler_params=_params(("parallel", "parallel", "arbitrary")),
    )(a, b)


def _rows_call(fn, name, rows, params, out_rows, out_accs, ts):
    s = rows[0].shape[0]
    nr, npar, no = len(rows), len(params), len(out_rows)

    def body(*refs):
        r, p = refs[:nr], refs[nr:nr + npar]
        o, acc = refs[nr + npar:nr + npar + no], refs[nr + npar + no:]
        outs, sums = fn(*[t[...] for t in r], *[t[...] for t in p])
        for ref, val in zip(o, outs):
            ref[...] = val.astype(ref.dtype)
        if acc:
            @pl.when(pl.program_id(0) == 0)
            def _():
                for ref in acc:
                    ref[...] = jnp.zeros_like(ref)

            for ref, val in zip(acc, sums):
                ref[...] += val

    return pl.pallas_call(
        body,
        name=name,
        grid=(s // ts,),
        out_shape=[jax.ShapeDtypeStruct((s, w), dt) for w, dt in out_rows]
        + [jax.ShapeDtypeStruct(shape, F32) for shape in out_accs],
        in_specs=[pl.BlockSpec((ts, t.shape[1]), lambda i: (i, 0)) for t in rows]
        + [pl.BlockSpec(t.shape, lambda i: (0, 0)) for t in params],
        out_specs=[pl.BlockSpec((ts, w), lambda i: (i, 0)) for w, _ in out_rows]
        + [pl.BlockSpec(shape, lambda i: (0, 0)) for shape in out_accs],
        compiler_params=_params(("arbitrary",)),
    )(*rows, *params)


def _fwd_norm1(x, g, scale, shift):
    return (_norm_mod(x, g, scale, shift),), ()


def _resid_norm(x, mixed, gate, g, scale, shift):
    x1 = x + (1.0 + gate) * mixed
    return x1, _norm_mod(x1, g, scale, shift)


def _fwd_resid_norm(x, mixed, gate, g, scale, shift):
    return _resid_norm(x, mixed, gate, g, scale, shift), ()


def _final(x1, ffn, tgt, gate, g):
    def head(x1, ffn, gate, g):
        return _rms(x1 + (1.0 + gate) * ffn, g)

    y, vjp = jax.vjp(head, x1, ffn, gate, g)
    err = y - tgt
    dx2, dffn, dgate, dg = vjp(err * (1.0 / D))
    sq = jnp.sum(jnp.sum(err * err, axis=1, keepdims=True), axis=0, keepdims=True)
    loss = jnp.broadcast_to(sq * (0.5 / D), (1, 128))
    return (dx2, dffn), (dgate, dg, loss)


def _bwd_resid_norm(x, mixed, dx2, dh2, gate, g, scale, shift):
    _, vjp = jax.vjp(_resid_norm, x, mixed, gate, g, scale, shift)
    dx, dmixed, dgate, dg, dscale, dshift = vjp((dx2, dh2))
    return (dx, dmixed), (dgate, dg, dscale, dshift)


def _bwd_norm1(x, dh, dx1, g, scale, shift):
    _, vjp = jax.vjp(_norm_mod, x, g, scale, shift)
    dx, dg, dscale, dshift = vjp(dh)
    return (dx1 + dx,), (dg, dscale, dshift)


def _ada_fwd(c16, w):
    def body(c_ref, w_ref, o_ref):
        c = c_ref[...]
        o_ref[...] = _dot((c * _sigmoid(c)).astype(BF16), w_ref[...].astype(BF16))

    return pl.pallas_call(
        body, name="ada_fwd", out_shape=jax.ShapeDtypeStruct((c16.shape[0], w.shape[1]), F32),
        compiler_params=_params(),
    )(c16, w)


def _ada_bwd(c16, dmod16):
    def body(c_ref, d_ref, o_ref):
        c = c_ref[...]
        o_ref[...] = _dot_tn((c * _sigmoid(c)).astype(BF16), d_ref[...].astype(BF16))

    return pl.pallas_call(
        body, name="ada_bwd", out_shape=jax.ShapeDtypeStruct((c16.shape[1], dmod16.shape[1]), F32),
        compiler_params=_params(),
    )(c16, dmod16)


def _conv_taps(cur, halo, first):
    keep = jnp.where(first, 0.0, 1.0)
    h1 = halo[7:8] * keep
    h2 = halo[6:7] * keep
    row = lax.broadcasted_iota(jnp.int32, cur.shape, 0)
    s1 = jnp.where(row == 0, h1, pltpu.roll(cur, 1, 0))
    s2 = jnp.where(row == 0, h2, jnp.where(row == 1, h1, pltpu.roll(cur, 2, 0)))
    return s1, s2


def _conv_u(cur, s1, s2, wc, bc):
    return bc + wc[0:1] * s2 + wc[1:2] * s1 + wc[2:3] * cur


def _ffn_specs(ts, halo_rows):
    w2 = 2 * FF_T
    cur = pl.BlockSpec((ts, w2), lambda j, i: (i, j))
    halo = pl.BlockSpec((8, w2), lambda j, i: (jnp.maximum(i * (ts // 8) - 1, 0), j))
    nxt = pl.BlockSpec((8, w2), lambda j, i: (jnp.minimum((i + 1) * (ts // 8), halo_rows - 1), j))
    wc = pl.BlockSpec((8, w2), lambda j, i: (0, j))
    bc = pl.BlockSpec((1, w2), lambda j, i: (0, j))
    act = pl.BlockSpec((ts, FF_T), lambda j, i: (i, j))
    return cur, halo, nxt, wc, bc, act


def _conv_glu_fwd(up, wc, bc, ts):
    s = up.shape[0]
    cur, halo, _, wcs, bcs, act = _ffn_specs(ts, s // 8)

    def body(cur_ref, halo_ref, wc_ref, bc_ref, a_ref):
        x = cur_ref[...]
        s1, s2 = _conv_taps(x, halo_ref[...], pl.program_id(1) == 0)
        u = _conv_u(x, s1, s2, wc_ref[...], bc_ref[...])
        val, gte = u[:, :FF_T], u[:, FF_T:]
        a_ref[...] = (val * (gte * _sigmoid(gte))).astype(BF16)

    return pl.pallas_call(
        body, name="conv_glu_fwd", grid=(N_FT, s // ts),
        out_shape=jax.ShapeDtypeStruct((s, D_FF), BF16),
        in_specs=[cur, halo, wcs, bcs], out_specs=act,
        compiler_params=_params(("parallel", "arbitrary")),
    )(up, up, wc, bc)


def _conv_glu_bwd(up, da, wc, bc, ts):
    s = up.shape[0]
    cur, halo, _, wcs, bcs, act = _ffn_specs(ts, s // 8)

    def body(cur_ref, halo_ref, da_ref, wc_ref, bc_ref, du_ref, dwc_ref, dbc_ref):
        x = cur_ref[...]
        s1, s2 = _conv_taps(x, halo_ref[...], pl.program_id(1) == 0)
        u = _conv_u(x, s1, s2, wc_ref[...], bc_ref[...])
        val, gte = u[:, :FF_T], u[:, FF_T:]
        da = da_ref[...]
        sg = _sigmoid(gte)
        dval = da * (gte * sg)
        dgte = da * val * (sg * (1.0 + gte * (1.0 - sg)))
        du = jnp.concatenate([dval, dgte], axis=1)
        du_ref[...] = du

        @pl.when(pl.program_id(1) == 0)
        def _():
            dwc_ref[...] = jnp.zeros_like(dwc_ref)
            dbc_ref[...] = jnp.zeros_like(dbc_ref)

        dwc_ref[0:1, :] += jnp.sum(du * s2, axis=0, keepdims=True)
        dwc_ref[1:2, :] += jnp.sum(du * s1, axis=0, keepdims=True)
        dwc_ref[2:3, :] += jnp.sum(du * x, axis=0, keepdims=True)
        dbc_ref[...] += jnp.sum(du, axis=0, keepdims=True)

    return pl.pallas_call(
        body, name="conv_glu_bwd", grid=(N_FT, s // ts),
        out_shape=[jax.ShapeDtypeStruct((s, 2 * D_FF), F32), jax.ShapeDtypeStruct((8, 2 * D_FF), F32),
                   jax.ShapeDtypeStruct((1, 2 * D_FF), F32)],
        in_specs=[cur, halo, act, wcs, bcs], out_specs=[cur, wcs, bcs],
        compiler_params=_params(("parallel", "arbitrary")),
    )(up, up, da, wc, bc)


def _conv_bwd_input(du, wc, ts):
    s = du.shape[0]
    cur, _, nxt, wcs, _, _ = _ffn_specs(ts, s // 8)
    nblk = s // ts

    def body(cur_ref, nxt_ref, wc_ref, o_ref):
        x = cur_ref[...]
        keep = jnp.where(pl.program_id(1) == nblk - 1, 0.0, 1.0)
        n0 = nxt_ref[0:1] * keep
        n1 = nxt_ref[1:2] * keep
        row = lax.broadcasted_iota(jnp.int32, x.shape, 0)
        a1 = jnp.where(row == ts - 1, n0, pltpu.roll(x, ts - 1, 0))
        a2 = jnp.where(row == ts - 2, n0, jnp.where(row == ts - 1, n1, pltpu.roll(x, ts - 2, 0)))
        wc = wc_ref[...]
        o_ref[...] = (wc[2:3] * x + wc[1:2] * a1 + wc[0:1] * a2).astype(BF16)

    return pl.pallas_call(
        body, name="conv_bwd_input", grid=(N_FT, nblk),
        out_shape=jax.ShapeDtypeStruct((s, 2 * D_FF), BF16),
        in_specs=[cur, nxt, wcs], out_specs=cur,
        compiler_params=_params(("parallel", "arbitrary")),
    )(du, du, wc)


def _tri(kind):
    b = SB_B
    m = {"lower_strict": np.tril(np.ones((b, b)), -1), "upper_incl": np.triu(np.ones((b, b)), 0),
         "upper_strict": np.triu(np.ones((b, b)), 1)}[kind]
    return jnp.asarray(np.concatenate([m, np.ones((b, 128))], axis=1), BF16)


def _key_sums(x, tri):
    hi, lo = _hilo(x)
    cb = _dot(hi, tri) + _dot(lo, tri)
    return cb[:, :SB_B], cb[:, SB_B:]


def _sb_fwd(proj):
    s = proj.shape[0]
    b = SB_B

    def body(q_ref, k_ref, v_ref, tri_ref, o_ref, t_ref, first_ref, c_ref, a_ref):
        i = pl.program_id(1)
        lane = lax.broadcasted_iota(jnp.int32, (b, 128), 1)
        heads = (lane < 64, lane >= 64)
        causal = lax.broadcasted_iota(jnp.int32, (b, b), 1) < lax.broadcasted_iota(jnp.int32, (b, b), 0)
        q = q_ref[...] * 0.125
        qm = [jnp.where(h, q, 0.0).astype(BF16) for h in heads]
        c_ref[...] = jnp.zeros_like(c_ref)
        a_ref[...] = jnp.zeros_like(a_ref)

        def step(jj, masked):
            rows = pl.ds(pl.multiple_of(jj * b, b), b)
            kb = k_ref[rows, :].astype(BF16)
            vb = v_ref[rows, :]
            for hh in range(2):
                z = _dot_nt(qm[hh], kb)
                lg = _log_sigmoid(-z)
                if masked:
                    lg = jnp.where(causal, lg, 0.0)
                after, total = _key_sums(lg, tri_ref[...])
                c = c_ref[hh]
                w = jnp.exp(lg + z + after + jnp.concatenate([c, c], axis=1))
                if masked:
                    w = jnp.where(causal, w, 0.0)
                a_ref[hh] += _dot(w.astype(BF16), jnp.where(heads[hh], vb, 0.0).astype(BF16))
                c_ref[hh] = c + total

        def largest_sum():
            return jnp.max(jnp.maximum(c_ref[0], c_ref[1]))

        step(i, True)

        def more(state):
            jj, top = state
            return jnp.logical_and(jj >= 0, top > SB_DEAD)

        def walk(state):
            jj, _ = state
            step(jj, False)
            return jj - 1, largest_sum()

        jj, _ = lax.while_loop(more, walk, (i - 1, largest_sum()))
        o_ref[...] = a_ref[0] + a_ref[1]
        t_ref[...] = jnp.concatenate([c_ref[0], c_ref[1]], axis=1)
        first_ref[pl.program_id(0), i] = (jj + 1).astype(F32)

    return pl.pallas_call(
        body, name="sb_fwd", grid=(4, s // b),
        out_shape=[jax.ShapeDtypeStruct((s, SB_W), F32), jax.ShapeDtypeStruct((s, 2 * SB_W), F32),
                   jax.ShapeDtypeStruct((4, s // b), F32)],
        in_specs=[pl.BlockSpec((b, 128), lambda p, i: (i, p)),
                  pl.BlockSpec((s, 128), lambda p, i: (0, 4 + p)),
                  pl.BlockSpec((s, 128), lambda p, i: (0, 8 + p)),
                  pl.BlockSpec((b, b + 128), lambda p, i: (0, 0))],
        out_specs=[pl.BlockSpec((b, 128), lambda p, i: (i, p)), pl.BlockSpec((b, 256), lambda p, i: (i, p)),
                   pl.BlockSpec(memory_space=pltpu.SMEM)],
        scratch_shapes=[pltpu.VMEM((2, b, 128), F32), pltpu.VMEM((2, b, 128), F32)],
        compiler_params=_params(("arbitrary", "arbitrary")),
    )(proj, proj, proj, _tri("lower_strict"))


def _sb_bwd(proj, do, tot, first):
    s = proj.shape[0]
    b = SB_B

    def body(q_ref, k_ref, v_ref, do_ref, t_ref, first_ref, ti_ref, ts_ref, dq_ref, dk_ref, dv_ref, cl_ref, ce_ref,
             a_ref):
        i = pl.program_id(1)
        first = jnp.clip(first_ref[pl.program_id(0), i].astype(jnp.int32), 0, i)
        lane = lax.broadcasted_iota(jnp.int32, (b, 128), 1)
        heads = (lane < 64, lane >= 64)
        causal = lax.broadcasted_iota(jnp.int32, (b, b), 1) < lax.broadcasted_iota(jnp.int32, (b, b), 0)
        q = q_ref[...] * 0.125
        do = do_ref[...]
        qm = [jnp.where(h, q, 0.0).astype(BF16) for h in heads]
        dom = [jnp.where(h, do, 0.0).astype(BF16) for h in heads]
        cl_ref[...] = jnp.zeros_like(cl_ref)
        ce_ref[...] = jnp.zeros_like(ce_ref)
        a_ref[...] = jnp.zeros_like(a_ref)

        @pl.when(i == 0)
        def _():
            dk_ref[...] = jnp.zeros_like(dk_ref)
            dv_ref[...] = jnp.zeros_like(dv_ref)

        def step(jj, masked):
            rows = pl.ds(pl.multiple_of(jj * b, b), b)
            kf = k_ref[rows, :]
            kb = kf.astype(BF16)
            vb = v_ref[rows, :].astype(BF16)
            dk = jnp.zeros((b, 128), F32)
            dv = jnp.zeros((b, 128), F32)
            for hh in range(2):
                z = _dot_nt(qm[hh], kb)
                lg = _log_sigmoid(-z)
                if masked:
                    lg = jnp.where(causal, lg, 0.0)
                upto, total = _key_sums(lg, ti_ref[...])
                cl = cl_ref[hh]
                t = t_ref[:, hh * 128:(hh + 1) * 128]
                rest = jnp.concatenate([t - cl, t - cl], axis=1) - upto
                lsz = lg + z
                w = jnp.exp(lsz + rest)
                sig = jnp.exp(lsz)
                if masked:
                    w = jnp.where(causal, w, 0.0)
                e = w * _dot_nt(dom[hh], vb)
                before, etot = _key_sums(e, ts_ref[...])
                ce = ce_ref[hh]
                dz = e - sig * (e + before + jnp.concatenate([ce, ce], axis=1))
                if masked:
                    dz = jnp.where(causal, dz, 0.0)
                dzb = dz.astype(BF16)
                a_ref[hh] += _dot(dzb, jnp.where(heads[hh], kf, 0.0).astype(BF16))
                dk = dk + _dot_tn(dzb, qm[hh])
                dv = dv + _dot_tn(w.astype(BF16), dom[hh])
                cl_ref[hh] = cl + total
                ce_ref[hh] = ce + etot
            dk_ref[rows, :] += dk
            dv_ref[rows, :] += dv

        def loop(jj, carry):
            step(jj, False)
            return carry

        lax.fori_loop(first, i, loop, 0)
        step(i, True)
        dq_ref[...] = (a_ref[0] + a_ref[1]) * 0.125

    blk = pl.BlockSpec((b, 128), lambda p, i: (i, p))
    full = pl.BlockSpec((s, 128), lambda p, i: (0, p))
    tri = pl.BlockSpec((b, b + 128), lambda p, i: (0, 0))
    return pl.pallas_call(
        body, name="sb_bwd", grid=(4, s // b),
        out_shape=[jax.ShapeDtypeStruct((s, SB_W), F32)] * 3,
        in_specs=[blk, pl.BlockSpec((s, 128), lambda p, i: (0, 4 + p)), pl.BlockSpec((s, 128), lambda p, i: (0, 8 + p)),
                  blk, pl.BlockSpec((b, 256), lambda p, i: (i, p)), pl.BlockSpec(memory_space=pltpu.SMEM), tri, tri],
        out_specs=[blk, full, full],
        scratch_shapes=[pltpu.VMEM((2, b, 128), F32)] * 3,
        compiler_params=_params(("arbitrary", "arbitrary")),
    )(proj, proj, proj, do, tot, first, _tri("upper_incl"), _tri("upper_strict"))


_COL_Q, _COL_K, _COL_V, _COL_G, _COL_F = 12, 14, 8, 10, 24


def _gla_consts():
    c = CHUNK
    incl = np.tril(np.ones((c, c)), 0)
    strict = np.tril(np.ones((c, c)), -1)
    bd = np.zeros((128, 256))
    bd[:64, :128] = 1.0
    bd[64:, 128:] = 1.0
    return jnp.asarray(incl, BF16), jnp.asarray(strict, BF16), jnp.asarray(bd, F32)


def _time_sums(tri, x):
    hi, lo = _hilo(x)
    return _dot(tri, hi) + _dot(tri, lo)


def _gla_gate(o, gg, g):
    parts = []
    for h in range(2):
        oh = o[:, h * 128:(h + 1) * 128]
        parts.append(oh * lax.rsqrt(jnp.mean(oh * oh, axis=-1, keepdims=True) + EPS))
    return (jnp.concatenate(parts, axis=1) * g) * (gg * _sigmoid(gg))


def _gla_chunk(la_c, k_c, incl, ones_cv):
    cum = _time_sums(incl, la_c)
    total = cum[CHUNK - 1:CHUNK]
    edec = jnp.exp(total - cum)
    kdec = k_c * edec
    hi, lo = _hilo(la_c)
    dec = jnp.exp(_dot_tn(hi, ones_cv) + _dot_tn(lo, ones_cv))
    return edec, kdec, dec


def _gla_fwd(proj, wfg, bfg, gain):
    s = proj.shape[0]
    t = GLA_T
    nch = t // CHUNK
    incl, _, bd = _gla_consts()

    def body(q_ref, k_ref, v_ref, gg_ref, f_ref, wf_ref, bf_ref, g_ref, incl_ref, bd_ref, o_ref, st_ref, state_ref):
        @pl.when(pl.program_id(1) == 0)
        def _():
            state_ref[...] = jnp.zeros_like(state_ref)

        la = _log_sigmoid(_dot(f_ref[...].astype(BF16), wf_ref[...].astype(BF16)) + bf_ref[...]) * (1.0 / 16.0)
        ones_cv = jnp.ones((CHUNK, 256), BF16)
        state = state_ref[...]
        for cc in range(nch):
            rows = slice(cc * CHUNK, (cc + 1) * CHUNK)
            _, kdec, dec = _gla_chunk(la[rows], k_ref[rows, :], incl_ref[...], ones_cv)
            kv = _dot_tn(kdec.astype(BF16), v_ref[rows, :].astype(BF16))
            state = dec * state + bd_ref[...] * kv
            st_ref[cc] = state
            o = _dot((q_ref[rows, :] * 0.125).astype(BF16), state.astype(BF16))
            o_ref[rows, :] = _gla_gate(o, gg_ref[rows, :], g_ref[...])
        state_ref[...] = state

    def col(width, blk):
        return pl.BlockSpec((t, width), lambda p, i: (i, blk + p))

    return pl.pallas_call(
        body, name="gla_fwd", grid=(2, s // t),
        out_shape=[jax.ShapeDtypeStruct((s, GLA_VW), F32), jax.ShapeDtypeStruct((2, s // CHUNK, 128, 256), F32)],
        in_specs=[col(128, _COL_Q), col(128, _COL_K), col(256, _COL_V), col(256, _COL_G),
                  pl.BlockSpec((t, 128), lambda p, i: (i, _COL_F)),
                  pl.BlockSpec((128, 128), lambda p, i: (0, p)), pl.BlockSpec((1, 128), lambda p, i: (0, p)),
                  pl.BlockSpec((1, 256), lambda p, i: (0, p)),
                  pl.BlockSpec((CHUNK, CHUNK), lambda p, i: (0, 0)), pl.BlockSpec((128, 256), lambda p, i: (0, 0))],
        out_specs=[pl.BlockSpec((t, 256), lambda p, i: (i, p)),
                   pl.BlockSpec((None, nch, 128, 256), lambda p, i: (p, i, 0, 0))],
        scratch_shapes=[pltpu.VMEM((128, 256), F32)],
        compiler_params=_params(("parallel", "arbitrary")),
    )(proj, proj, proj, proj, proj, wfg, bfg, gain, incl, bd)


def _gla_bwd(proj, do, states, wfg, bfg, gain):
    s = proj.shape[0]
    t = GLA_T
    nch = t // CHUNK
    nblk = s // t
    incl, strict, bd = _gla_consts()

    def body(q_ref, k_ref, v_ref, gg_ref, f_ref, do_ref, st_ref, sp_ref, wf_ref, bf_ref, g_ref, incl_ref, str_ref,
             bd_ref, dq_ref, dk_ref, dv_ref, dgg_ref, df_ref, dwf_ref, dbf_ref, dg_ref, carry_ref, dfs_ref):
        i = pl.program_id(1)

        @pl.when(i == 0)
        def _():
            carry_ref[...] = jnp.zeros_like(carry_ref)
            dwf_ref[...] = jnp.zeros_like(dwf_ref)
            dbf_ref[...] = jnp.zeros_like(dbf_ref)
            dg_ref[...] = jnp.zeros_like(dg_ref)

        fb = f_ref[...].astype(BF16)
        f = _dot(fb, wf_ref[...].astype(BF16)) + bf_ref[...]
        la = _log_sigmoid(f) * (1.0 / 16.0)
        dla_df = _sigmoid(-f) * (1.0 / 16.0)
        ones_cv = jnp.ones((CHUNK, 256), BF16)
        ones_8v = jnp.ones((8, 256), BF16)
        first_block = jnp.where(i == nblk - 1, 0.0, 1.0)
        carry = carry_ref[...]
        dgain = jnp.zeros((1, 256), F32)
        for cc in reversed(range(nch)):
            rows = slice(cc * CHUNK, (cc + 1) * CHUNK)
            k_c = k_ref[rows, :]
            edec, kdec, dec = _gla_chunk(la[rows], k_c, incl_ref[...], ones_cv)
            state = st_ref[cc]
            prev = st_ref[cc - 1] if cc > 0 else sp_ref[0] * first_block
            qs = (q_ref[rows, :] * 0.125).astype(BF16)
            sb16 = state.astype(BF16)
            o = _dot(qs, sb16)
            _, vjp = jax.vjp(_gla_gate, o, gg_ref[rows, :], g_ref[...])
            do_c, dgg_c, dg_c = vjp(do_ref[rows, :])
            dgain = dgain + dg_c
            dgg_ref[rows, :] = dgg_c
            do16 = do_c.astype(BF16)
            dq_ref[rows, :] = _dot_nt(do16, sb16) * 0.125
            grad = bd_ref[...] * _dot_tn(qs, do16) + carry
            g16 = grad.astype(BF16)
            dv_ref[rows, :] = _dot(kdec.astype(BF16), g16)
            dkdec = _dot_nt(v_ref[rows, :].astype(BF16), g16)
            hi, lo = _hilo(grad * prev * dec)
            ddec = (_dot_nt(ones_8v, hi) + _dot_nt(ones_8v, lo))[0:1]
            dk_ref[rows, :] = dkdec * edec
            dla = _time_sums(str_ref[...], dkdec * kdec) + ddec
            dfs_ref[rows, :] = dla * dla_df[rows]
            carry = dec * grad
        carry_ref[...] = carry
        df = dfs_ref[...]
        df16 = df.astype(BF16)
        df_ref[...] = _dot_nt(df16, wf_ref[...].astype(BF16))
        dwf_ref[...] += _dot_tn(fb, df16)
        dbf_ref[...] += jnp.sum(df, axis=0, keepdims=True)
        dg_ref[...] += dgain

    def col(width, blk):
        return pl.BlockSpec((t, width), lambda p, i: (nblk - 1 - i, blk + p))

    def pair(width):
        return pl.BlockSpec((t, width), lambda p, i: (nblk - 1 - i, p))

    return pl.pallas_call(
        body, name="gla_bwd", grid=(2, nblk),
        out_shape=[jax.ShapeDtypeStruct((s, GLA_KW), F32), jax.ShapeDtypeStruct((s, GLA_KW), F32),
                   jax.ShapeDtypeStruct((s, GLA_VW), F32), jax.ShapeDtypeStruct((s, GLA_VW), F32),
                   jax.ShapeDtypeStruct((2, s, 128), F32), jax.ShapeDtypeStruct((128, GLA_KW), F32),
                   jax.ShapeDtypeStruct((1, GLA_KW), F32), jax.ShapeDtypeStruct((1, GLA_VW), F32)],
        in_specs=[col(128, _COL_Q), col(128, _COL_K), col(256, _COL_V), col(256, _COL_G),
                  pl.BlockSpec((t, 128), lambda p, i: (nblk - 1 - i, _COL_F)),
                  pl.BlockSpec((t, 256), lambda p, i: (nblk - 1 - i, SB_W // 256 + p)),
                  pl.BlockSpec((None, nch, 128, 256), lambda p, i: (p, nblk - 1 - i, 0, 0)),
                  pl.BlockSpec((None, 1, 128, 256), lambda p, i: (p, jnp.maximum((nblk - 1 - i) * nch - 1, 0), 0, 0)),
                  pl.BlockSpec((128, 128), lambda p, i: (0, p)), pl.BlockSpec((1, 128), lambda p, i: (0, p)),
                  pl.BlockSpec((1, 256), lambda p, i: (0, p)),
                  pl.BlockSpec((CHUNK, CHUNK), lambda p, i: (0, 0)), pl.BlockSpec((CHUNK, CHUNK), lambda p, i: (0, 0)),
                  pl.BlockSpec((128, 256), lambda p, i: (0, 0))],
        out_specs=[pair(128), pair(128), pair(256), pair(256),
                   pl.BlockSpec((None, t, 128), lambda p, i: (p, nblk - 1 - i, 0)),
                   pl.BlockSpec((128, 128), lambda p, i: (0, p)), pl.BlockSpec((1, 128), lambda p, i: (0, p)),
                   pl.BlockSpec((1, 256), lambda p, i: (0, p))],
        scratch_shapes=[pltpu.VMEM((128, 256), F32), pltpu.VMEM((t, 128), F32)],
        compiler_params=_params(("parallel", "arbitrary")),
    )(proj, proj, proj, proj, proj, do, states, states, wfg, bfg, gain, incl, strict, bd)


def _cols_by_dev(a):
    r, c = a.shape
    return a.reshape(r, N_DEV, c // N_DEV).transpose(1, 0, 2)


def _cols_from_dev(a):
    _, r, n = a.shape
    return a.transpose(1, 0, 2).reshape(r, N_DEV * n)


def _ff_pair(a):
    lead = a.shape[:-1]
    return a.reshape(*lead, 2, N_FT, FF_T).swapaxes(-3, -2).reshape(*lead, 2 * D_FF)


def _ff_unpair(a):
    lead = a.shape[:-1]
    return a.reshape(*lead, N_FT, 2, FF_T).swapaxes(-3, -2).reshape(*lead, 2 * D_FF)


def kernel(x, c, w_ada, b_ada, g_norm1, w_in, w_fg2, b_fg2, g_gla_out, w_out, g_norm2, w_up, w_conv, b_conv, w_down, g_final, loss_target, m_w_ada, m_b_ada, m_g_norm1, m_w_in, m_w_fg2, m_b_fg2, m_g_gla_out, m_w_out, m_g_norm2, m_w_up, m_w_conv, m_b_conv, m_w_down, m_g_final, v_w_ada, v_b_ada, v_g_norm1, v_w_in, v_w_fg2, v_b_fg2, v_g_gla_out, v_w_out, v_g_norm2, v_w_up, v_w_conv, v_b_conv, v_w_down, v_g_final):
    s = x.shape[1]
    me = 4 * lax.axis_index("x") + 2 * lax.axis_index("y") + lax.axis_index("c")
    xs, tgt = x[0], loss_target[0]
    ts = min(512, s)

    c_all, wconv_g, wfg_g = _exchange([c, w_conv[0], w_fg2[0]], "gather_small", True)
    win_g, wout_g, wup_g, wdown_g = _exchange(
        [w_in[0].astype(BF16), w_out[0].astype(BF16), w_up[0].astype(BF16), w_down[0].astype(BF16)],
        "gather_weights", True)
    win = jnp.pad(_cols_from_dev(win_g), ((0, 0), (0, IN_WP - IN_W)))
    wout = wout_g.reshape(D, D)
    wup = _ff_pair(_cols_from_dev(wup_g))
    wdown = wdown_g.reshape(D_FF, D)
    wconv_p = jnp.pad(_ff_pair(_cols_from_dev(wconv_g)), ((0, 5), (0, 0)))
    bconv_p = _ff_pair(b_conv)
    wfg_p = jnp.pad(_cols_from_dev(wfg_g), ((0, 128 - RANK), (0, 0)))

    c16 = jnp.pad(c_all.reshape(N_DEV, D), ((0, 8), (0, 0)))
    modp = _ada_fwd(c16, w_ada[0])[:N_DEV]
    (mod_all,) = _exchange([modp], "gather_mod", True)
    mod = lax.dynamic_index_in_dim(mod_all, me, axis=1, keepdims=False).reshape(1, 6 * D) + b_ada
    shift1, scale1, gate1, shift2, scale2, gate2 = [mod[:, k * D:(k + 1) * D] for k in range(6)]

    (h,) = _rows_call(_fwd_norm1, "norm1", [xs], [g_norm1, scale1, shift1], [(D, BF16)], [], ts)
    proj = _mm(h, win, "in_proj", F32, ts, 640)
    o_sb, sb_tot, sb_first = _sb_fwd(proj)
    o_gla, states = _gla_fwd(proj, wfg_p, b_fg2, g_gla_out)
    cat = jnp.concatenate([o_sb, o_gla], axis=1).astype(BF16)
    mixed = _mm(cat, wout, "out_proj", F32, ts, 512)
    x1, h2 = _rows_call(_fwd_resid_norm, "resid_norm2", [xs, mixed], [gate1, g_norm2, scale2, shift2],
                        [(D, F32), (D, BF16)], [], ts)
    up = _mm(h2, wup, "up_proj", F32, ts, 512)
    act = _conv_glu_fwd(up, wconv_p, bconv_p, ts)
    ffn = _mm(act, wdown, "down_proj", F32, ts, 512)
    dx2, dffn, dgate2, dg_final, loss_part = _rows_call(
        _final, "final", [x1, ffn, tgt], [gate2, g_final.reshape(1, D)],
        [(D, F32), (D, BF16)], [(1, D), (1, D), (1, 128)], ts)

    dw_down = _mm_tn(act, dffn, "down_wgrad", 1408, 1024, ts)
    da = _mm_nt(dffn, wdown, "down_dgrad", F32, ts, 256)
    du, dwconv_p, dbconv_p = _conv_glu_bwd(up, da, wconv_p, bconv_p, ts)
    dup = _conv_bwd_input(du, wconv_p, ts)
    dw_up = _mm_tn(h2, dup, "up_wgrad", 1024, 512, ts)
    dh2 = _mm_nt(dup, wup, "up_dgrad", F32, ts, 512)
    dx1, dmixed, dgate1, dg_norm2, dscale2, dshift2 = _rows_call(
        _bwd_resid_norm, "resid_norm2_bwd", [xs, mixed, dx2, dh2], [gate1, g_norm2, scale2, shift2],
        [(D, F32), (D, BF16)], [(1, D)] * 4, ts)
    dw_out = _mm_tn(cat, dmixed, "out_wgrad", 1024, 512, ts)
    dcat = _mm_nt(dmixed, wout, "out_dgrad", F32, ts, 512)
    dq, dk, dv = _sb_bwd(proj, dcat, sb_tot, sb_first)
    dgq, dgk, dgv, dgg, dgf, dwfg_p, dbfg, dg_gla = _gla_bwd(proj, dcat, states, wfg_p, b_fg2, g_gla_out)
    dproj = jnp.concatenate([dq, dk, dv, dgq, dgk, dgv, dgg, dgf[0] + dgf[1]], axis=1).astype(BF16)
    dw_in = _mm_tn(h, dproj, "in_wgrad", 1024, 640, ts)
    dh = _mm_nt(dproj, win, "in_dgrad", F32, ts, 512)
    grad_x, dg_norm1, dscale1, dshift1 = _rows_call(
        _bwd_norm1, "norm1_bwd", [xs, dh, dx1], [g_norm1, scale1, shift1], [(D, F32)], [(1, D)] * 3, ts)

    dmod = jnp.concatenate([dshift1, dscale1, dgate1, dshift2, dscale2, dgate2], axis=1)
    small_parts = [dmod, dg_norm1, dbfg, dg_gla, dg_norm2, _ff_unpair(dbconv_p), dg_final, loss_part]
    n_small = sum(p.shape[1] for p in small_parts[:-1])
    (sg,) = _exchange([jnp.concatenate(small_parts, axis=1)], "gather_small_grads", True)
    sg = sg.reshape(N_DEV, n_small + 128)
    loss = jnp.sum(sg[:, n_small])
    small_w = [b_ada, g_norm1, b_fg2, g_gla_out, g_norm2, b_conv, g_final.reshape(1, D)]
    small_m = [m_b_ada, m_g_norm1, m_b_fg2, m_g_gla_out, m_g_norm2, m_b_conv, m_g_final.reshape(1, D)]
    small_v = [v_b_ada, v_g_norm1, v_b_fg2, v_g_gla_out, v_g_norm2, v_b_conv, v_g_final.reshape(1, D)]
    s_out = list(_adam_rows(sg, small_w, small_m, small_v, "adam_small"))
    s_out[24:] = [t.reshape(D) for t in s_out[24:]]
    s_g, s_d, s_m, s_v = [s_out[k::4] for k in range(4)]

    dmod_all = sg[:, :6 * D].reshape(N_DEV, N_DEV, 768)
    dmod_mine = lax.dynamic_index_in_dim(dmod_all, me, axis=1, keepdims=False)
    dw_ada = _ada_bwd(c16, jnp.pad(dmod_mine, ((0, 8), (0, 0))))
    a_g, a_d, a_m, a_v = [t[None] for t in _adam(dw_ada[None], w_ada[0], m_w_ada[0], v_w_ada[0], "adam_ada", 256)]

    recv = _exchange([
        _cols_by_dev(dw_in[:, :IN_W]), _cols_by_dev(dwfg_p[:RANK]), dw_out.reshape(N_DEV, 128, D),
        _cols_by_dev(_ff_unpair(dw_up)), _cols_by_dev(_ff_unpair(dwconv_p[:3])), dw_down.reshape(N_DEV, 352, D),
    ], "scatter_grads", False)
    big_w = [w_in, w_fg2, w_out, w_up, w_conv, w_down]
    big_m = [m_w_in, m_w_fg2, m_w_out, m_w_up, m_w_conv, m_w_down]
    big_v = [v_w_in, v_w_fg2, v_w_out, v_w_up, v_w_conv, v_w_down]
    big_name = ["adam_in", "adam_fg2", "adam_out", "adam_up", "adam_conv", "adam_down"]
    big_rows = [256, RANK, 128, 256, 3, 176]
    b_out = [_adam(r, w[0], m[0], v[0], name, tr)
             for r, w, m, v, name, tr in zip(recv, big_w, big_m, big_v, big_name, big_rows)]
    b_g, b_d, b_m, b_v = [[o[k][None] for o in b_out] for k in range(4)]

    def ordered(a, sm, bg):
        return [a, sm[0], sm[1], bg[0], bg[1], sm[2], sm[3], bg[2], sm[4], bg[3], bg[4], sm[5], bg[5], sm[6]]

    return (loss, grad_x[None], *ordered(a_g, s_g, b_g), *ordered(a_d, s_d, b_d),
            *ordered(a_m, s_m, b_m), *ordered(a_v, s_v, b_v))
```

```python
import numpy as np

import jax
import jax.numpy as jnp
from jax import lax
from jax.experimental import pallas as pl
from jax.experimental.pallas import tpu as pltpu

F32, BF16 = jnp.float32, jnp.bfloat16
N_DEV = 8
D = 1024
SB_W = 512
GLA_KW, GLA_VW = 256, 512
RANK = 16
IN_W = 3088
IN_WP = 3200
D_FF = 2816
FF_T = 256
N_FT = D_FF // FF_T
EPS = 1e-6
SB_B = 256
SB_DEAD = -110.0
CHUNK = 64
GLA_T = 512
VMEM_LIMIT = 56 * 1024 * 1024

LR, B1, B2, ADAM_EPS, WD, STEP = 0.001, 0.9, 0.999, 1e-08, 0.01, 10


def _params(dims=None, vmem=True):
    kw = {}
    if dims is not None:
        kw["dimension_semantics"] = dims
    if vmem:
        kw["vmem_limit_bytes"] = VMEM_LIMIT
    return pltpu.CompilerParams(**kw)


def _dot(a, b):
    return jnp.dot(a, b, preferred_element_type=F32)


def _dot_nt(a, b):
    return lax.dot_general(a, b, (((1,), (1,)), ((), ())), preferred_element_type=F32)


def _dot_tn(a, b):
    return lax.dot_general(a, b, (((0,), (0,)), ((), ())), preferred_element_type=F32)


def _hilo(x):
    hi = x.astype(BF16)
    lo = (x - hi.astype(F32)).astype(BF16)
    return hi, lo


def _sigmoid(x):
    return 1.0 / (1.0 + jnp.exp(-x))


def _log_sigmoid(x):
    return jnp.minimum(x, 0.0) - jnp.log(1.0 + jnp.exp(-jnp.abs(x)))


def _rms(x, g):
    n = x * lax.rsqrt(jnp.mean(x * x, axis=-1, keepdims=True) + EPS)
    return n * g


def _norm_mod(x, g, scale, shift):
    return _rms(x, g) * (1.0 + scale) + shift


N_PEER = N_DEV - 1


def _exchange_copies(x_refs, out_refs, send_sems, recv_sems, local_sems, gather):
    ix, iy, ic = lax.axis_index("x"), lax.axis_index("y"), lax.axis_index("c")
    me = 4 * ix + 2 * iy + ic
    peers = []
    for k in range(1, N_DEV):
        px = 1 - ix if k & 4 else ix
        py = 1 - iy if k & 2 else iy
        pc = 1 - ic if k & 1 else ic
        peers.append(((px, py, pc), 4 * px + 2 * py + pc))

    def copy(a, k, dev, src_slot, dst_slot):
        return pltpu.make_async_remote_copy(
            src_ref=x_refs[a] if gather else x_refs[a].at[src_slot],
            dst_ref=out_refs[a].at[dst_slot],
            send_sem=send_sems.at[a * N_PEER + k],
            recv_sem=recv_sems.at[a * N_PEER + k],
            device_id=dev,
            device_id_type=pl.DeviceIdType.MESH,
        )

    n = len(x_refs)
    mine = [pltpu.make_async_copy(x_refs[a] if gather else x_refs[a].at[me], out_refs[a].at[me], local_sems.at[a])
            for a in range(n)]
    sends = [copy(a, k, dev, pid, me) for a in range(n) for k, (dev, pid) in enumerate(peers)]
    recvs = [copy(a, k, dev, pid, pid) for a in range(n) for k, (dev, pid) in enumerate(peers)]
    return mine, sends, recvs


def _exchange_scratch(n):
    return [pltpu.SemaphoreType.DMA((n * N_PEER,)), pltpu.SemaphoreType.DMA((n * N_PEER,)),
            pltpu.SemaphoreType.DMA((n,))]


def _exchange_shapes(arrays, gather):
    return [jax.ShapeDtypeStruct((N_DEV,) + tuple(x.shape if gather else x.shape[1:]), x.dtype) for x in arrays]


def _exchange(arrays, name, gather):
    n = len(arrays)

    def body(*refs):
        mine, sends, recvs = _exchange_copies(refs[:n], refs[n:2 * n], *refs[2 * n:], gather)
        for cp in mine + sends:
            cp.start()
        for cp in recvs:
            cp.wait_recv()
        for cp in sends:
            cp.wait_send()
        for cp in mine:
            cp.wait()

    return pl.pallas_call(
        body,
        name=name,
        out_shape=_exchange_shapes(arrays, gather),
        in_specs=[pl.BlockSpec(memory_space=pl.ANY)] * n,
        out_specs=[pl.BlockSpec(memory_space=pl.ANY)] * n,
        scratch_shapes=_exchange_scratch(n),
    )(*arrays)


def _hosted_call(body, comm, *, name, grid, in_specs, out_specs, out_shape, scratch_shapes, dims, args):
    if comm is None:
        outs = pl.pallas_call(body, name=name, grid=grid, in_specs=in_specs, out_specs=out_specs, out_shape=out_shape,
                              scratch_shapes=scratch_shapes, compiler_params=_params(dims))(*args)
        return outs, []
    arrays, gather = comm
    n_in, n_out, n_scr, nc = len(in_specs), len(out_specs), len(scratch_shapes), len(arrays)

    def hosted(*refs):
        ins, c_in = refs[:n_in], refs[n_in:n_in + nc]
        outs, c_out = refs[n_in + nc:n_in + nc + n_out], refs[n_in + nc + n_out:n_in + 2 * nc + n_out]
        rest = refs[n_in + 2 * nc + n_out:]
        scratch, sems = rest[:n_scr], rest[n_scr:]
        mine, sends, recvs = _exchange_copies(c_in, c_out, *sems, gather)
        first = pl.program_id(0) == 0
        last = pl.program_id(0) == grid[0] - 1
        for axis in range(1, len(grid)):
            first = jnp.logical_and(first, pl.program_id(axis) == 0)
            last = jnp.logical_and(last, pl.program_id(axis) == grid[axis] - 1)

        @pl.when(first)
        def _():
            for cp in mine + sends:
                cp.start()

        body(*ins, *outs, *scratch)

        @pl.when(last)
        def _():
            for cp in recvs:
                cp.wait_recv()
            for cp in sends:
                cp.wait_send()
            for cp in mine:
                cp.wait()

    any_spec = pl.BlockSpec(memory_space=pl.ANY)
    outs = pl.pallas_call(
        hosted, name=name, grid=grid,
        in_specs=list(in_specs) + [any_spec] * nc,
        out_specs=list(out_specs) + [any_spec] * nc,
        out_shape=list(out_shape) + _exchange_shapes(arrays, gather),
        scratch_shapes=list(scratch_shapes) + _exchange_scratch(nc),
        compiler_params=_params(tuple("arbitrary" for _ in grid)),
    )(*args, *arrays)
    return outs[:n_out], outs[n_out:]


def _adam_math(g, w, m, v):
    m_new = B1 * m + (1.0 - B1) * g
    v_new = B2 * v + (1.0 - B2) * (g * g)
    m_hat = m_new / (1.0 - B1 ** STEP)
    v_hat = v_new / (1.0 - B2 ** STEP)
    return -LR * (m_hat / (jnp.sqrt(v_hat) + ADAM_EPS) + WD * w), m_new, v_new


def _adam(gparts, w, m, v, name, tr):
    n, rows, cols = gparts.shape

    def body(gp_ref, w_ref, m_ref, v_ref, g_ref, d_ref, nm_ref, nv_ref):
        g = gp_ref[0]
        for j in range(1, n):
            g = g + gp_ref[j]
        g_ref[...] = g
        d_ref[...], nm_ref[...], nv_ref[...] = _adam_math(g, w_ref[...], m_ref[...], v_ref[...])

    blk = pl.BlockSpec((tr, cols), lambda i: (i, 0))
    return pl.pallas_call(
        body,
        name=name,
        grid=(rows // tr,),
        out_shape=[jax.ShapeDtypeStruct((rows, cols), F32)] * 4,
        in_specs=[pl.BlockSpec((n, tr, cols), lambda i: (0, i, 0)), blk, blk, blk],
        out_specs=[blk] * 4,
        compiler_params=_params(("parallel",)),
    )(gparts, w, m, v)


def _adam_rows(parts, ws, ms, vs, name):
    n = parts.shape[0]
    k = len(ws)
    widths = [w.shape[1] for w in ws]

    def body(*refs):
        p_ref, w_refs, m_refs, v_refs = refs[0], refs[1:1 + k], refs[1 + k:1 + 2 * k], refs[1 + 2 * k:1 + 3 * k]
        outs = refs[1 + 3 * k:]
        total = p_ref[0:1, :]
        for j in range(1, n):
            total = total + p_ref[j:j + 1, :]
        off = 0
        for a, width in enumerate(widths):
            g = total[:, off:off + width]
            off += width
            outs[4 * a][...] = g
            outs[4 * a + 1][...], outs[4 * a + 2][...], outs[4 * a + 3][...] = _adam_math(
                g, w_refs[a][...], m_refs[a][...], v_refs[a][...])

    return pl.pallas_call(
        body, name=name,
        out_shape=[jax.ShapeDtypeStruct((1, width), F32) for width in widths for _ in range(4)],
        compiler_params=_params(),
    )(parts, *ws, *ms, *vs)


def _mm(a, b, name, out_dtype, tm, tn):
    m, k = a.shape
    _, n = b.shape

    def body(a_ref, b_ref, o_ref):
        o_ref[...] = _dot(a_ref[...].astype(BF16), b_ref[...].astype(BF16)).astype(out_dtype)

    return pl.pallas_call(
        body,
        name=name,
        grid=(n // tn, m // tm),
        out_shape=jax.ShapeDtypeStruct((m, n), out_dtype),
        in_specs=[pl.BlockSpec((tm, k), lambda j, i: (i, 0)), pl.BlockSpec((k, tn), lambda j, i: (0, j))],
        out_specs=pl.BlockSpec((tm, tn), lambda j, i: (i, j)),
        compiler_params=_params(("parallel", "parallel")),
    )(a, b)


def _mm_nt(a, b, name, out_dtype, tm, tn, comm=None):
    m, k = a.shape
    n, _ = b.shape

    def body(a_ref, b_ref, o_ref):
        o_ref[...] = _dot_nt(a_ref[...].astype(BF16), b_ref[...].astype(BF16)).astype(out_dtype)

    (out,), got = _hosted_call(
        body, comm, name=name, grid=(n // tn, m // tm),
        out_shape=[jax.ShapeDtypeStruct((m, n), out_dtype)],
        in_specs=[pl.BlockSpec((tm, k), lambda j, i: (i, 0)), pl.BlockSpec((tn, k), lambda j, i: (j, 0))],
        out_specs=[pl.BlockSpec((tm, tn), lambda j, i: (i, j))],
        scratch_shapes=[], dims=("parallel", "parallel"), args=(a, b))
    return out, got


def _mm_tn(a, b, name, tm, tn, tk):
    s, m = a.shape
    _, n = b.shape

    def body(a_ref, b_ref, o_ref):
        @pl.when(pl.program_id(2) == 0)
        def _():
            o_ref[...] = jnp.zeros_like(o_ref)

        o_ref[...] += _dot_tn(a_ref[...].astype(BF16), b_ref[...].astype(BF16))

    return pl.pallas_call(
        body,
        name=name,
        grid=(n // tn, m // tm, s // tk),
        out_shape=jax.ShapeDtypeStruct((m, n), F32),
        in_specs=[pl.BlockSpec((tk, tm), lambda j, i, k: (k, i)), pl.BlockSpec((tk, tn), lambda j, i, k: (k, j))],
        out_specs=pl.BlockSpec((tm, tn), lambda j, i, k: (i, j)),
        compiler_params=_params(("parallel", "parallel", "arbitrary")),
    )(a, b)


def _rows_call(fn, name, rows, params, out_rows, out_accs, ts):
    s = rows[0].shape[0]
    nr, npar, no = len(rows), len(params), len(out_rows)

    def body(*refs):
        r, p = refs[:nr], refs[nr:nr + npar]
        o, acc = refs[nr + npar:nr + npar + no], refs[nr + npar + no:]
        outs, sums = fn(*[t[...] for t in r], *[t[...] for t in p])
        for ref, val in zip(o, outs):
            ref[...] = val.astype(ref.dtype)
        if acc:
            @pl.when(pl.program_id(0) == 0)
            def _():
                for ref in acc:
                    ref[...] = jnp.zeros_like(ref)

            for ref, val in zip(acc, sums):
                ref[...] += val

    return pl.pallas_call(
        body,
        name=name,
        grid=(s // ts,),
        out_shape=[jax.ShapeDtypeStruct((s, w), dt) for w, dt in out_rows]
        + [jax.ShapeDtypeStruct(shape, F32) for shape in out_accs],
        in_specs=[pl.BlockSpec((ts, t.shape[1]), lambda i: (i, 0)) for t in rows]
        + [pl.BlockSpec(t.shape, lambda i: (0, 0)) for t in params],
        out_specs=[pl.BlockSpec((ts, w), lambda i: (i, 0)) for w, _ in out_rows]
        + [pl.BlockSpec(shape, lambda i: (0, 0)) for shape in out_accs],
        compiler_params=_params(("arbitrary",)),
    )(*rows, *params)


def _fwd_norm1(x, g, scale, shift):
    return (_norm_mod(x, g, scale, shift),), ()


def _resid_norm(x, mixed, gate, g, scale, shift):
    x1 = x + (1.0 + gate) * mixed
    return x1, _norm_mod(x1, g, scale, shift)


def _fwd_resid_norm(x, mixed, gate, g, scale, shift):
    return _resid_norm(x, mixed, gate, g, scale, shift), ()


def _final(x1, ffn, tgt, gate, g):
    def head(x1, ffn, gate, g):
        return _rms(x1 + (1.0 + gate) * ffn, g)

    y, vjp = jax.vjp(head, x1, ffn, gate, g)
    err = y - tgt
    dx2, dffn, dgate, dg = vjp(err * (1.0 / D))
    sq = jnp.sum(jnp.sum(err * err, axis=1, keepdims=True), axis=0, keepdims=True)
    loss = jnp.broadcast_to(sq * (0.5 / D), (1, 128))
    return (dx2, dffn), (dgate, dg, loss)


def _bwd_resid_norm(x, mixed, dx2, dh2, gate, g, scale, shift):
    _, vjp = jax.vjp(_resid_norm, x, mixed, gate, g, scale, shift)
    dx, dmixed, dgate, dg, dscale, dshift = vjp((dx2, dh2))
    return (dx, dmixed), (dgate, dg, dscale, dshift)


def _bwd_norm1(x, dh, dx1, g, scale, shift):
    _, vjp = jax.vjp(_norm_mod, x, g, scale, shift)
    dx, dg, dscale, dshift = vjp(dh)
    return (dx1 + dx,), (dg, dscale, dshift)


def _ada_fwd(c16, w):
    def body(c_ref, w_ref, o_ref):
        c = c_ref[...]
        o_ref[...] = _dot((c * _sigmoid(c)).astype(BF16), w_ref[...].astype(BF16))

    return pl.pallas_call(
        body, name="ada_fwd", out_shape=jax.ShapeDtypeStruct((c16.shape[0], w.shape[1]), F32),
        compiler_params=_params(),
    )(c16, w)


def _ada_bwd(c16, dmod16):
    def body(c_ref, d_ref, o_ref):
        c = c_ref[...]
        o_ref[...] = _dot_tn((c * _sigmoid(c)).astype(BF16), d_ref[...].astype(BF16))

    return pl.pallas_call(
        body, name="ada_bwd", out_shape=jax.ShapeDtypeStruct((c16.shape[1], dmod16.shape[1]), F32),
        compiler_params=_params(),
    )(c16, dmod16)


def _conv_taps(cur, halo, first):
    keep = jnp.where(first, 0.0, 1.0)
    h1 = halo[7:8] * keep
    h2 = halo[6:7] * keep
    row = lax.broadcasted_iota(jnp.int32, cur.shape, 0)
    s1 = jnp.where(row == 0, h1, pltpu.roll(cur, 1, 0))
    s2 = jnp.where(row == 0, h2, jnp.where(row == 1, h1, pltpu.roll(cur, 2, 0)))
    return s1, s2


def _conv_u(cur, s1, s2, wc, bc):
    return bc + wc[0:1] * s2 + wc[1:2] * s1 + wc[2:3] * cur


def _ffn_specs(ts, halo_rows):
    w2 = 2 * FF_T
    cur = pl.BlockSpec((ts, w2), lambda j, i: (i, j))
    halo = pl.BlockSpec((8, w2), lambda j, i: (jnp.maximum(i * (ts // 8) - 1, 0), j))
    nxt = pl.BlockSpec((8, w2), lambda j, i: (jnp.minimum((i + 1) * (ts // 8), halo_rows - 1), j))
    wc = pl.BlockSpec((8, w2), lambda j, i: (0, j))
    bc = pl.BlockSpec((1, w2), lambda j, i: (0, j))
    act = pl.BlockSpec((ts, FF_T), lambda j, i: (i, j))
    return cur, halo, nxt, wc, bc, act


def _conv_glu_fwd(up, wc, bc, ts):
    s = up.shape[0]
    cur, halo, _, wcs, bcs, act = _ffn_specs(ts, s // 8)

    def body(cur_ref, halo_ref, wc_ref, bc_ref, a_ref):
        x = cur_ref[...]
        s1, s2 = _conv_taps(x, halo_ref[...], pl.program_id(1) == 0)
        u = _conv_u(x, s1, s2, wc_ref[...], bc_ref[...])
        val, gte = u[:, :FF_T], u[:, FF_T:]
        a_ref[...] = (val * (gte * _sigmoid(gte))).astype(BF16)

    return pl.pallas_call(
        body, name="conv_glu_fwd", grid=(N_FT, s // ts),
        out_shape=jax.ShapeDtypeStruct((s, D_FF), BF16),
        in_specs=[cur, halo, wcs, bcs], out_specs=act,
        compiler_params=_params(("parallel", "arbitrary")),
    )(up, up, wc, bc)


def _glu_bwd(u, da):
    val, gte = u[:, :FF_T], u[:, FF_T:]
    sg = _sigmoid(gte)
    return jnp.concatenate([da * (gte * sg), da * val * (sg * (1.0 + gte * (1.0 - sg)))], axis=1)


def _conv_glu_bwd(up, da, wc, bc, ts, comm=None):
    s = up.shape[0]
    nblk = s // ts
    cur, halo, nxt, wcs, bcs, act = _ffn_specs(ts, s // 8)
    act_nxt = pl.BlockSpec((8, FF_T), lambda j, i: (jnp.minimum((i + 1) * (ts // 8), s // 8 - 1), j))

    def body(cur_ref, halo_ref, nxt_ref, da_ref, dan_ref, wc_ref, bc_ref, dup_ref, dwc_ref, dbc_ref):
        x = cur_ref[...]
        wc, bc = wc_ref[...], bc_ref[...]
        s1, s2 = _conv_taps(x, halo_ref[...], pl.program_id(1) == 0)
        du = _glu_bwd(_conv_u(x, s1, s2, wc, bc), da_ref[...])
        ext = jnp.concatenate([x[ts - 8:], nxt_ref[...]], axis=0)
        u_n = _conv_u(ext, pltpu.roll(ext, 1, 0), pltpu.roll(ext, 2, 0), wc, bc)[8:]
        du_n = _glu_bwd(u_n, dan_ref[...]) * jnp.where(pl.program_id(1) == nblk - 1, 0.0, 1.0)
        row = lax.broadcasted_iota(jnp.int32, x.shape, 0)
        a1 = jnp.where(row == ts - 1, du_n[0:1], pltpu.roll(du, ts - 1, 0))
        a2 = jnp.where(row == ts - 2, du_n[0:1], jnp.where(row == ts - 1, du_n[1:2], pltpu.roll(du, ts - 2, 0)))
        dup_ref[...] = (wc[2:3] * du + wc[1:2] * a1 + wc[0:1] * a2).astype(BF16)

        @pl.when(pl.program_id(1) == 0)
        def _():
            dwc_ref[...] = jnp.zeros_like(dwc_ref)
            dbc_ref[...] = jnp.zeros_like(dbc_ref)

        dwc_ref[0:1, :] += jnp.sum(du * s2, axis=0, keepdims=True)
        dwc_ref[1:2, :] += jnp.sum(du * s1, axis=0, keepdims=True)
        dwc_ref[2:3, :] += jnp.sum(du * x, axis=0, keepdims=True)
        dbc_ref[...] += jnp.sum(du, axis=0, keepdims=True)

    return _hosted_call(
        body, comm, name="conv_glu_bwd", grid=(N_FT, nblk),
        out_shape=[jax.ShapeDtypeStruct((s, 2 * D_FF), BF16), jax.ShapeDtypeStruct((8, 2 * D_FF), F32),
                   jax.ShapeDtypeStruct((1, 2 * D_FF), F32)],
        in_specs=[cur, halo, nxt, act, act_nxt, wcs, bcs], out_specs=[cur, wcs, bcs],
        scratch_shapes=[], dims=("parallel", "arbitrary"), args=(up, up, up, da, da, wc, bc))


def _tri(kind):
    b = SB_B
    m = {"lower_strict": np.tril(np.ones((b, b)), -1), "upper_incl": np.triu(np.ones((b, b)), 0),
         "upper_strict": np.triu(np.ones((b, b)), 1)}[kind]
    return jnp.asarray(np.concatenate([m, np.ones((b, 128))], axis=1), BF16)


def _key_sums(x, tri):
    hi, lo = _hilo(x)
    cb = _dot(hi, tri) + _dot(lo, tri)
    return cb[:, :SB_B], cb[:, SB_B:]


def _sb_fwd(proj, comm=None):
    s = proj.shape[0]
    b = SB_B

    def body(q_ref, k_ref, v_ref, tri_ref, o_ref, t_ref, first_ref, c_ref, a_ref):
        i = pl.program_id(1)
        lane = lax.broadcasted_iota(jnp.int32, (b, 128), 1)
        heads = (lane < 64, lane >= 64)
        causal = lax.broadcasted_iota(jnp.int32, (b, b), 1) < lax.broadcasted_iota(jnp.int32, (b, b), 0)
        q = q_ref[...] * 0.125
        qm = [jnp.where(h, q, 0.0).astype(BF16) for h in heads]
        c_ref[...] = jnp.zeros_like(c_ref)
        a_ref[...] = jnp.zeros_like(a_ref)

        def step(jj, masked):
            rows = pl.ds(pl.multiple_of(jj * b, b), b)
            kb = k_ref[rows, :].astype(BF16)
            vb = v_ref[rows, :]
            for hh in range(2):
                z = _dot_nt(qm[hh], kb)
                lg = _log_sigmoid(-z)
                if masked:
                    lg = jnp.where(causal, lg, 0.0)
                after, total = _key_sums(lg, tri_ref[...])
                c = c_ref[hh]
                w = jnp.exp(lg + z + after + jnp.concatenate([c, c], axis=1))
                if masked:
                    w = jnp.where(causal, w, 0.0)
                a_ref[hh] += _dot(w.astype(BF16), jnp.where(heads[hh], vb, 0.0).astype(BF16))
                c_ref[hh] = c + total

        def largest_sum():
            return jnp.max(jnp.maximum(c_ref[0], c_ref[1]))

        step(i, True)

        def more(state):
            jj, top = state
            return jnp.logical_and(jj >= 0, top > SB_DEAD)

        def walk(state):
            jj, _ = state
            step(jj, False)
            return jj - 1, largest_sum()

        jj, _ = lax.while_loop(more, walk, (i - 1, largest_sum()))
        o_ref[...] = a_ref[0] + a_ref[1]
        t_ref[...] = jnp.concatenate([c_ref[0], c_ref[1]], axis=1)
        first_ref[pl.program_id(0), i] = (jj + 1).astype(F32)

    return _hosted_call(
        body, comm, name="sb_fwd", grid=(4, s // b),
        out_shape=[jax.ShapeDtypeStruct((s, SB_W), F32), jax.ShapeDtypeStruct((s, 2 * SB_W), F32),
                   jax.ShapeDtypeStruct((4, s // b), F32)],
        in_specs=[pl.BlockSpec((b, 128), lambda p, i: (i, p)),
                  pl.BlockSpec((s, 128), lambda p, i: (0, 4 + p)),
                  pl.BlockSpec((s, 128), lambda p, i: (0, 8 + p)),
                  pl.BlockSpec((b, b + 128), lambda p, i: (0, 0))],
        out_specs=[pl.BlockSpec((b, 128), lambda p, i: (i, p)), pl.BlockSpec((b, 256), lambda p, i: (i, p)),
                   pl.BlockSpec(memory_space=pltpu.SMEM)],
        scratch_shapes=[pltpu.VMEM((2, b, 128), F32), pltpu.VMEM((2, b, 128), F32)],
        dims=("arbitrary", "arbitrary"), args=(proj, proj, proj, _tri("lower_strict")))


def _sb_bwd(proj, do, tot, first, comm=None):
    s = proj.shape[0]
    b = SB_B

    def body(q_ref, k_ref, v_ref, do_ref, t_ref, first_ref, ti_ref, ts_ref, dq_ref, dk_ref, dv_ref, cl_ref, ce_ref,
             a_ref):
        i = pl.program_id(1)
        first = jnp.clip(first_ref[pl.program_id(0), i].astype(jnp.int32), 0, i)
        lane = lax.broadcasted_iota(jnp.int32, (b, 128), 1)
        heads = (lane < 64, lane >= 64)
        causal = lax.broadcasted_iota(jnp.int32, (b, b), 1) < lax.broadcasted_iota(jnp.int32, (b, b), 0)
        q = q_ref[...] * 0.125
        do = do_ref[...]
        qm = [jnp.where(h, q, 0.0).astype(BF16) for h in heads]
        dom = [jnp.where(h, do, 0.0).astype(BF16) for h in heads]
        cl_ref[...] = jnp.zeros_like(cl_ref)
        ce_ref[...] = jnp.zeros_like(ce_ref)
        a_ref[...] = jnp.zeros_like(a_ref)

        @pl.when(i == 0)
        def _():
            dk_ref[...] = jnp.zeros_like(dk_ref)
            dv_ref[...] = jnp.zeros_like(dv_ref)

        def step(jj, masked):
            rows = pl.ds(pl.multiple_of(jj * b, b), b)
            kf = k_ref[rows, :]
            kb = kf.astype(BF16)
            vb = v_ref[rows, :].astype(BF16)
            dk = jnp.zeros((b, 128), F32)
            dv = jnp.zeros((b, 128), F32)
            for hh in range(2):
                z = _dot_nt(qm[hh], kb)
                lg = _log_sigmoid(-z)
                if masked:
                    lg = jnp.where(causal, lg, 0.0)
                upto, total = _key_sums(lg, ti_ref[...])
                cl = cl_ref[hh]
                t = t_ref[:, hh * 128:(hh + 1) * 128]
                rest = jnp.concatenate([t - cl, t - cl], axis=1) - upto
                lsz = lg + z
                w = jnp.exp(lsz + rest)
                sig = jnp.exp(lsz)
                if masked:
                    w = jnp.where(causal, w, 0.0)
                e = w * _dot_nt(dom[hh], vb)
                before, etot = _key_sums(e, ts_ref[...])
                ce = ce_ref[hh]
                dz = e - sig * (e + before + jnp.concatenate([ce, ce], axis=1))
                if masked:
                    dz = jnp.where(causal, dz, 0.0)
                dzb = dz.astype(BF16)
                a_ref[hh] += _dot(dzb, jnp.where(heads[hh], kf, 0.0).astype(BF16))
                dk = dk + _dot_tn(dzb, qm[hh])
                dv = dv + _dot_tn(w.astype(BF16), dom[hh])
                cl_ref[hh] = cl + total
                ce_ref[hh] = ce + etot
            dk_ref[rows, :] += dk
            dv_ref[rows, :] += dv

        def loop(jj, carry):
            step(jj, False)
            return carry

        lax.fori_loop(first, i, loop, 0)
        step(i, True)
        dq_ref[...] = (a_ref[0] + a_ref[1]) * 0.125

    blk = pl.BlockSpec((b, 128), lambda p, i: (i, p))
    full = pl.BlockSpec((s, 128), lambda p, i: (0, p))
    tri = pl.BlockSpec((b, b + 128), lambda p, i: (0, 0))
    return _hosted_call(
        body, comm, name="sb_bwd", grid=(4, s // b),
        out_shape=[jax.ShapeDtypeStruct((s, SB_W), F32)] * 3,
        in_specs=[blk, pl.BlockSpec((s, 128), lambda p, i: (0, 4 + p)), pl.BlockSpec((s, 128), lambda p, i: (0, 8 + p)),
                  blk, pl.BlockSpec((b, 256), lambda p, i: (i, p)), pl.BlockSpec(memory_space=pltpu.SMEM), tri, tri],
        out_specs=[blk, full, full],
        scratch_shapes=[pltpu.VMEM((2, b, 128), F32)] * 3,
        dims=("arbitrary", "arbitrary"),
        args=(proj, proj, proj, do, tot, first, _tri("upper_incl"), _tri("upper_strict")))


_COL_Q, _COL_K, _COL_V, _COL_G, _COL_F = 12, 14, 8, 10, 24


def _gla_consts():
    c = CHUNK
    incl = np.tril(np.ones((c, c)), 0)
    strict = np.tril(np.ones((c, c)), -1)
    bd = np.zeros((128, 256))
    bd[:64, :128] = 1.0
    bd[64:, 128:] = 1.0
    return jnp.asarray(incl, BF16), jnp.asarray(strict, BF16), jnp.asarray(bd, F32)


def _time_sums(tri, x):
    hi, lo = _hilo(x)
    return _dot(tri, hi) + _dot(tri, lo)


def _gla_gate(o, gg, g):
    parts = []
    for h in range(2):
        oh = o[:, h * 128:(h + 1) * 128]
        parts.append(oh * lax.rsqrt(jnp.mean(oh * oh, axis=-1, keepdims=True) + EPS))
    return (jnp.concatenate(parts, axis=1) * g) * (gg * _sigmoid(gg))


def _gla_chunk(la_c, k_c, incl, ones_cv):
    cum = _time_sums(incl, la_c)
    total = cum[CHUNK - 1:CHUNK]
    edec = jnp.exp(total - cum)
    kdec = k_c * edec
    hi, lo = _hilo(la_c)
    dec = jnp.exp(_dot_tn(hi, ones_cv) + _dot_tn(lo, ones_cv))
    return edec, kdec, dec


def _gla_fwd(proj, wfg, bfg, gain):
    s = proj.shape[0]
    t = GLA_T
    nch = t // CHUNK
    incl, _, bd = _gla_consts()

    def body(q_ref, k_ref, v_ref, gg_ref, f_ref, wf_ref, bf_ref, g_ref, incl_ref, bd_ref, o_ref, st_ref, state_ref):
        @pl.when(pl.program_id(1) == 0)
        def _():
            state_ref[...] = jnp.zeros_like(state_ref)

        la = _log_sigmoid(_dot(f_ref[...].astype(BF16), wf_ref[...].astype(BF16)) + bf_ref[...]) * (1.0 / 16.0)
        ones_cv = jnp.ones((CHUNK, 256), BF16)
        state = state_ref[...]
        for cc in range(nch):
            rows = slice(cc * CHUNK, (cc + 1) * CHUNK)
            _, kdec, dec = _gla_chunk(la[rows], k_ref[rows, :], incl_ref[...], ones_cv)
            kv = _dot_tn(kdec.astype(BF16), v_ref[rows, :].astype(BF16))
            state = dec * state + bd_ref[...] * kv
            st_ref[cc] = state
            o = _dot((q_ref[rows, :] * 0.125).astype(BF16), state.astype(BF16))
            o_ref[rows, :] = _gla_gate(o, gg_ref[rows, :], g_ref[...])
        state_ref[...] = state

    def col(width, blk):
        return pl.BlockSpec((t, width), lambda p, i: (i, blk + p))

    return pl.pallas_call(
        body, name="gla_fwd", grid=(2, s // t),
        out_shape=[jax.ShapeDtypeStruct((s, GLA_VW), F32), jax.ShapeDtypeStruct((2, s // CHUNK, 128, 256), F32)],
        in_specs=[col(128, _COL_Q), col(128, _COL_K), col(256, _COL_V), col(256, _COL_G),
                  pl.BlockSpec((t, 128), lambda p, i: (i, _COL_F)),
                  pl.BlockSpec((128, 128), lambda p, i: (0, p)), pl.BlockSpec((1, 128), lambda p, i: (0, p)),
                  pl.BlockSpec((1, 256), lambda p, i: (0, p)),
                  pl.BlockSpec((CHUNK, CHUNK), lambda p, i: (0, 0)), pl.BlockSpec((128, 256), lambda p, i: (0, 0))],
        out_specs=[pl.BlockSpec((t, 256), lambda p, i: (i, p)),
                   pl.BlockSpec((None, nch, 128, 256), lambda p, i: (p, i, 0, 0))],
        scratch_shapes=[pltpu.VMEM((128, 256), F32)],
        compiler_params=_params(("parallel", "arbitrary")),
    )(proj, proj, proj, proj, proj, wfg, bfg, gain, incl, bd)


def _gla_bwd(proj, do, states, wfg, bfg, gain):
    s = proj.shape[0]
    t = GLA_T
    nch = t // CHUNK
    nblk = s // t
    incl, strict, bd = _gla_consts()

    def body(q_ref, k_ref, v_ref, gg_ref, f_ref, do_ref, st_ref, sp_ref, wf_ref, bf_ref, g_ref, incl_ref, str_ref,
             bd_ref, dq_ref, dk_ref, dv_ref, dgg_ref, df_ref, dwf_ref, dbf_ref, dg_ref, carry_ref, dfs_ref):
        i = pl.program_id(1)

        @pl.when(i == 0)
        def _():
            carry_ref[...] = jnp.zeros_like(carry_ref)
            dwf_ref[...] = jnp.zeros_like(dwf_ref)
            dbf_ref[...] = jnp.zeros_like(dbf_ref)
            dg_ref[...] = jnp.zeros_like(dg_ref)

        fb = f_ref[...].astype(BF16)
        f = _dot(fb, wf_ref[...].astype(BF16)) + bf_ref[...]
        la = _log_sigmoid(f) * (1.0 / 16.0)
        dla_df = _sigmoid(-f) * (1.0 / 16.0)
        ones_cv = jnp.ones((CHUNK, 256), BF16)
        ones_8v = jnp.ones((8, 256), BF16)
        first_block = jnp.where(i == nblk - 1, 0.0, 1.0)
        carry = carry_ref[...]
        dgain = jnp.zeros((1, 256), F32)
        for cc in reversed(range(nch)):
            rows = slice(cc * CHUNK, (cc + 1) * CHUNK)
            k_c = k_ref[rows, :]
            edec, kdec, dec = _gla_chunk(la[rows], k_c, incl_ref[...], ones_cv)
            state = st_ref[cc]
            prev = st_ref[cc - 1] if cc > 0 else sp_ref[0] * first_block
            qs = (q_ref[rows, :] * 0.125).astype(BF16)
            sb16 = state.astype(BF16)
            o = _dot(qs, sb16)
            _, vjp = jax.vjp(_gla_gate, o, gg_ref[rows, :], g_ref[...])
            do_c, dgg_c, dg_c = vjp(do_ref[rows, :])
            dgain = dgain + dg_c
            dgg_ref[rows, :] = dgg_c
            do16 = do_c.astype(BF16)
            dq_ref[rows, :] = _dot_nt(do16, sb16) * 0.125
            grad = bd_ref[...] * _dot_tn(qs, do16) + carry
            g16 = grad.astype(BF16)
            dv_ref[rows, :] = _dot(kdec.astype(BF16), g16)
            dkdec = _dot_nt(v_ref[rows, :].astype(BF16), g16)
            hi, lo = _hilo(grad * prev * dec)
            ddec = (_dot_nt(ones_8v, hi) + _dot_nt(ones_8v, lo))[0:1]
            dk_ref[rows, :] = dkdec * edec
            dla = _time_sums(str_ref[...], dkdec * kdec) + ddec
            dfs_ref[rows, :] = dla * dla_df[rows]
            carry = dec * grad
        carry_ref[...] = carry
        df = dfs_ref[...]
        df16 = df.astype(BF16)
        df_ref[...] = _dot_nt(df16, wf_ref[...].astype(BF16))
        dwf_ref[...] += _dot_tn(fb, df16)
        dbf_ref[...] += jnp.sum(df, axis=0, keepdims=True)
        dg_ref[...] += dgain

    def col(width, blk):
        return pl.BlockSpec((t, width), lambda p, i: (nblk - 1 - i, blk + p))

    def pair(width):
        return pl.BlockSpec((t, width), lambda p, i: (nblk - 1 - i, p))

    return pl.pallas_call(
        body, name="gla_bwd", grid=(2, nblk),
        out_shape=[jax.ShapeDtypeStruct((s, GLA_KW), F32), jax.ShapeDtypeStruct((s, GLA_KW), F32),
                   jax.ShapeDtypeStruct((s, GLA_VW), F32), jax.ShapeDtypeStruct((s, GLA_VW), F32),
                   jax.ShapeDtypeStruct((2, s, 128), F32), jax.ShapeDtypeStruct((128, GLA_KW), F32),
                   jax.ShapeDtypeStruct((1, GLA_KW), F32), jax.ShapeDtypeStruct((1, GLA_VW), F32)],
        in_specs=[col(128, _COL_Q), col(128, _COL_K), col(256, _COL_V), col(256, _COL_G),
                  pl.BlockSpec((t, 128), lambda p, i: (nblk - 1 - i, _COL_F)),
                  pl.BlockSpec((t, 256), lambda p, i: (nblk - 1 - i, SB_W // 256 + p)),
                  pl.BlockSpec((None, nch, 128, 256), lambda p, i: (p, nblk - 1 - i, 0, 0)),
                  pl.BlockSpec((None, 1, 128, 256), lambda p, i: (p, jnp.maximum((nblk - 1 - i) * nch - 1, 0), 0, 0)),
                  pl.BlockSpec((128, 128), lambda p, i: (0, p)), pl.BlockSpec((1, 128), lambda p, i: (0, p)),
                  pl.BlockSpec((1, 256), lambda p, i: (0, p)),
                  pl.BlockSpec((CHUNK, CHUNK), lambda p, i: (0, 0)), pl.BlockSpec((CHUNK, CHUNK), lambda p, i: (0, 0)),
                  pl.BlockSpec((128, 256), lambda p, i: (0, 0))],
        out_specs=[pair(128), pair(128), pair(256), pair(256),
                   pl.BlockSpec((None, t, 128), lambda p, i: (p, nblk - 1 - i, 0)),
                   pl.BlockSpec((128, 128), lambda p, i: (0, p)), pl.BlockSpec((1, 128), lambda p, i: (0, p)),
                   pl.BlockSpec((1, 256), lambda p, i: (0, p))],
        scratch_shapes=[pltpu.VMEM((128, 256), F32), pltpu.VMEM((t, 128), F32)],
        compiler_params=_params(("parallel", "arbitrary")),
    )(proj, proj, proj, proj, proj, do, states, states, wfg, bfg, gain, incl, strict, bd)


def _cols_by_dev(a):
    r, c = a.shape
    return a.reshape(r, N_DEV, c // N_DEV).transpose(1, 0, 2)


def _cols_from_dev(a):
    _, r, n = a.shape
    return a.transpose(1, 0, 2).reshape(r, N_DEV * n)


def _ff_pair(a):
    lead = a.shape[:-1]
    return a.reshape(*lead, 2, N_FT, FF_T).swapaxes(-3, -2).reshape(*lead, 2 * D_FF)


def _ff_unpair(a):
    lead = a.shape[:-1]
    return a.reshape(*lead, N_FT, 2, FF_T).swapaxes(-3, -2).reshape(*lead, 2 * D_FF)


def kernel(x, c, w_ada, b_ada, g_norm1, w_in, w_fg2, b_fg2, g_gla_out, w_out, g_norm2, w_up, w_conv, b_conv, w_down, g_final, loss_target, m_w_ada, m_b_ada, m_g_norm1, m_w_in, m_w_fg2, m_b_fg2, m_g_gla_out, m_w_out, m_g_norm2, m_w_up, m_w_conv, m_b_conv, m_w_down, m_g_final, v_w_ada, v_b_ada, v_g_norm1, v_w_in, v_w_fg2, v_b_fg2, v_g_gla_out, v_w_out, v_g_norm2, v_w_up, v_w_conv, v_b_conv, v_w_down, v_g_final):
    s = x.shape[1]
    me = 4 * lax.axis_index("x") + 2 * lax.axis_index("y") + lax.axis_index("c")
    xs, tgt = x[0], loss_target[0]
    ts = min(512, s)
    tm = min(1024, s)

    c_all, wconv_g, wfg_g, win_g = _exchange([c, w_conv[0], w_fg2[0], w_in[0].astype(BF16)], "gather_first", True)
    win = jnp.pad(_cols_from_dev(win_g), ((0, 0), (0, IN_WP - IN_W)))
    wconv_p = jnp.pad(_ff_pair(_cols_from_dev(wconv_g)), ((0, 5), (0, 0)))
    bconv_p = _ff_pair(b_conv)
    wfg_p = jnp.pad(_cols_from_dev(wfg_g), ((0, 128 - RANK), (0, 0)))

    c16 = jnp.pad(c_all.reshape(N_DEV, D), ((0, 8), (0, 0)))
    modp = _ada_fwd(c16, w_ada[0])[:N_DEV]
    (mod_all,) = _exchange([modp], "gather_mod", True)
    mod = lax.dynamic_index_in_dim(mod_all, me, axis=1, keepdims=False).reshape(1, 6 * D) + b_ada
    shift1, scale1, gate1, shift2, scale2, gate2 = [mod[:, k * D:(k + 1) * D] for k in range(6)]

    (h,) = _rows_call(_fwd_norm1, "norm1", [xs], [g_norm1, scale1, shift1], [(D, BF16)], [], ts)
    proj = _mm(h, win, "in_proj", F32, tm, 640)
    (o_sb, sb_tot, sb_first), (wout_g, wup_g, wdown_g) = _sb_fwd(
        proj, comm=([w_out[0].astype(BF16), w_up[0].astype(BF16), w_down[0].astype(BF16)], True))
    wout = wout_g.reshape(D, D)
    wup = _ff_pair(_cols_from_dev(wup_g))
    wdown = wdown_g.reshape(D_FF, D)
    o_gla, states = _gla_fwd(proj, wfg_p, b_fg2, g_gla_out)
    cat = jnp.concatenate([o_sb, o_gla], axis=1).astype(BF16)
    mixed = _mm(cat, wout, "out_proj", F32, tm, 512)
    x1, h2 = _rows_call(_fwd_resid_norm, "resid_norm2", [xs, mixed], [gate1, g_norm2, scale2, shift2],
                        [(D, F32), (D, BF16)], [], ts)
    up = _mm(h2, wup, "up_proj", F32, tm, 1408)
    act = _conv_glu_fwd(up, wconv_p, bconv_p, ts)
    ffn = _mm(act, wdown, "down_proj", F32, tm, 512)
    dx2, dffn, dgate2, dg_final, loss_part = _rows_call(
        _final, "final", [x1, ffn, tgt], [gate2, g_final.reshape(1, D)],
        [(D, F32), (D, BF16)], [(1, D), (1, D), (1, 128)], ts)

    dw_down = _mm_tn(act, dffn, "down_wgrad", 1408, 1024, tm)
    da, _ = _mm_nt(dffn, wdown, "down_dgrad", F32, tm, 1408)
    (dup, dwconv_p, dbconv_p), (r_down,) = _conv_glu_bwd(
        up, da, wconv_p, bconv_p, ts, comm=([dw_down.reshape(N_DEV, 352, D)], False))
    dw_up = _mm_tn(h2, dup, "up_wgrad", 1024, 1408, tm)
    dh2, _ = _mm_nt(dup, wup, "up_dgrad", F32, ts, 512)
    dx1, dmixed, dgate1, dg_norm2, dscale2, dshift2 = _rows_call(
        _bwd_resid_norm, "resid_norm2_bwd", [xs, mixed, dx2, dh2], [gate1, g_norm2, scale2, shift2],
        [(D, F32), (D, BF16)], [(1, D)] * 4, ts)
    dw_out = _mm_tn(cat, dmixed, "out_wgrad", 1024, 1024, tm)
    dcat, _ = _mm_nt(dmixed, wout, "out_dgrad", F32, tm, 512)
    (dq, dk, dv), (r_up, r_out, r_conv) = _sb_bwd(
        proj, dcat, sb_tot, sb_first,
        comm=([_cols_by_dev(_ff_unpair(dw_up)), dw_out.reshape(N_DEV, 128, D), _cols_by_dev(_ff_unpair(dwconv_p[:3]))],
              False))
    dgq, dgk, dgv, dgg, dgf, dwfg_p, dbfg, dg_gla = _gla_bwd(proj, dcat, states, wfg_p, b_fg2, g_gla_out)
    dproj = jnp.concatenate([dq, dk, dv, dgq, dgk, dgv, dgg, dgf[0] + dgf[1]], axis=1).astype(BF16)
    dw_in = _mm_tn(h, dproj, "in_wgrad", 1024, 640, tm)
    dh, (r_in, r_fg) = _mm_nt(dproj, win, "in_dgrad", F32, tm, 512,
                              comm=([_cols_by_dev(dw_in[:, :IN_W]), _cols_by_dev(dwfg_p[:RANK])], False))
    grad_x, dg_norm1, dscale1, dshift1 = _rows_call(
        _bwd_norm1, "norm1_bwd", [xs, dh, dx1], [g_norm1, scale1, shift1], [(D, F32)], [(1, D)] * 3, ts)

    dmod = jnp.concatenate([dshift1, dscale1, dgate1, dshift2, dscale2, dgate2], axis=1)
    small_parts = [dmod, dg_norm1, dbfg, dg_gla, dg_norm2, _ff_unpair(dbconv_p), dg_final, loss_part]
    n_small = sum(p.shape[1] for p in small_parts[:-1])
    (sg,) = _exchange([jnp.concatenate(small_parts, axis=1)], "gather_small_grads", True)
    sg = sg.reshape(N_DEV, n_small + 128)
    loss = jnp.sum(sg[:, n_small])
    small_w = [b_ada, g_norm1, b_fg2, g_gla_out, g_norm2, b_conv, g_final.reshape(1, D)]
    small_m = [m_b_ada, m_g_norm1, m_b_fg2, m_g_gla_out, m_g_norm2, m_b_conv, m_g_final.reshape(1, D)]
    small_v = [v_b_ada, v_g_norm1, v_b_fg2, v_g_gla_out, v_g_norm2, v_b_conv, v_g_final.reshape(1, D)]
    s_out = list(_adam_rows(sg, small_w, small_m, small_v, "adam_small"))
    s_out[24:] = [t.reshape(D) for t in s_out[24:]]
    s_g, s_d, s_m, s_v = [s_out[k::4] for k in range(4)]

    dmod_all = sg[:, :6 * D].reshape(N_DEV, N_DEV, 768)
    dmod_mine = lax.dynamic_index_in_dim(dmod_all, me, axis=1, keepdims=False)
    dw_ada = _ada_bwd(c16, jnp.pad(dmod_mine, ((0, 8), (0, 0))))
    a_g, a_d, a_m, a_v = [t[None] for t in _adam(dw_ada[None], w_ada[0], m_w_ada[0], v_w_ada[0], "adam_ada", 256)]

    recv = [r_in, r_fg, r_out, r_up, r_conv, r_down]
    big_w = [w_in, w_fg2, w_out, w_up, w_conv, w_down]
    big_m = [m_w_in, m_w_fg2, m_w_out, m_w_up, m_w_conv, m_w_down]
    big_v = [v_w_in, v_w_fg2, v_w_out, v_w_up, v_w_conv, v_w_down]
    big_name = ["adam_in", "adam_fg2", "adam_out", "adam_up", "adam_conv", "adam_down"]
    big_rows = [256, RANK, 128, 256, 3, 176]
    b_out = [_adam(r, w[0], m[0], v[0], name, tr)
             for r, w, m, v, name, tr in zip(recv, big_w, big_m, big_v, big_name, big_rows)]
    b_g, b_d, b_m, b_v = [[o[k][None] for o in b_out] for k in range(4)]

    def ordered(a, sm, bg):
        return [a, sm[0], sm[1], bg[0], bg[1], sm[2], sm[3], bg[2], sm[4], bg[3], bg[4], sm[5], bg[5], sm[6]]

    return (loss, grad_x[None], *ordered(a_g, s_g, b_g), *ordered(a_d, s_d, b_d),
            *ordered(a_m, s_m, b_m), *ordered(a_v, s_v, b_v))
```

```python
import numpy as np

import jax
import jax.numpy as jnp
from jax import lax
from jax.experimental import pallas as pl
from jax.experimental.pallas import tpu as pltpu

F32, BF16 = jnp.float32, jnp.bfloat16
N_DEV = 8
D = 1024
SB_W = 512
GLA_KW, GLA_VW = 256, 512
RANK = 16
IN_W = 3088
IN_WP = 3200
D_FF = 2816
FF_T = 256
N_FT = D_FF // FF_T
EPS = 1e-6
SB_B = 256
SB_DEAD = -110.0
CHUNK = 64
GLA_T = 512
VMEM_LIMIT = 56 * 1024 * 1024

LR, B1, B2, ADAM_EPS, WD, STEP = 0.001, 0.9, 0.999, 1e-08, 0.01, 10


def _params(dims=None, vmem=True):
    kw = {}
    if dims is not None:
        kw["dimension_semantics"] = dims
    if vmem:
        kw["vmem_limit_bytes"] = VMEM_LIMIT
    return pltpu.CompilerParams(**kw)


def _dot(a, b):
    return jnp.dot(a, b, preferred_element_type=F32)


def _dot_nt(a, b):
    return lax.dot_general(a, b, (((1,), (1,)), ((), ())), preferred_element_type=F32)


def _dot_tn(a, b):
    return lax.dot_general(a, b, (((0,), (0,)), ((), ())), preferred_element_type=F32)


def _hilo(x):
    hi = x.astype(BF16)
    lo = (x - hi.astype(F32)).astype(BF16)
    return hi, lo


def _sigmoid(x):
    return 1.0 / (1.0 + jnp.exp(-x))


def _log_sigmoid(x):
    return jnp.minimum(x, 0.0) - jnp.log(1.0 + jnp.exp(-jnp.abs(x)))


def _rms(x, g):
    n = x * lax.rsqrt(jnp.mean(x * x, axis=-1, keepdims=True) + EPS)
    return n * g


def _norm_mod(x, g, scale, shift):
    return _rms(x, g) * (1.0 + scale) + shift


N_PEER = N_DEV - 1


def _exchange_copies(x_refs, out_refs, send_sems, recv_sems, local_sems, gather):
    ix, iy, ic = lax.axis_index("x"), lax.axis_index("y"), lax.axis_index("c")
    me = 4 * ix + 2 * iy + ic
    peers = []
    for k in range(1, N_DEV):
        px = 1 - ix if k & 4 else ix
        py = 1 - iy if k & 2 else iy
        pc = 1 - ic if k & 1 else ic
        peers.append(((px, py, pc), 4 * px + 2 * py + pc))

    def copy(a, k, dev, src_slot, dst_slot):
        return pltpu.make_async_remote_copy(
            src_ref=x_refs[a] if gather else x_refs[a].at[src_slot],
            dst_ref=out_refs[a].at[dst_slot],
            send_sem=send_sems.at[a * N_PEER + k],
            recv_sem=recv_sems.at[a * N_PEER + k],
            device_id=dev,
            device_id_type=pl.DeviceIdType.MESH,
        )

    n = len(x_refs)
    mine = [pltpu.make_async_copy(x_refs[a] if gather else x_refs[a].at[me], out_refs[a].at[me], local_sems.at[a])
            for a in range(n)]
    sends = [copy(a, k, dev, pid, me) for a in range(n) for k, (dev, pid) in enumerate(peers)]
    recvs = [copy(a, k, dev, pid, pid) for a in range(n) for k, (dev, pid) in enumerate(peers)]
    return mine, sends, recvs


def _exchange_scratch(n):
    return [pltpu.SemaphoreType.DMA((n * N_PEER,)), pltpu.SemaphoreType.DMA((n * N_PEER,)),
            pltpu.SemaphoreType.DMA((n,))]


def _exchange_shapes(arrays, gather):
    return [jax.ShapeDtypeStruct((N_DEV,) + tuple(x.shape if gather else x.shape[1:]), x.dtype) for x in arrays]


def _exchange(arrays, name, gather):
    n = len(arrays)

    def body(*refs):
        mine, sends, recvs = _exchange_copies(refs[:n], refs[n:2 * n], *refs[2 * n:], gather)
        for cp in mine + sends:
            cp.start()
        for cp in recvs:
            cp.wait_recv()
        for cp in sends:
            cp.wait_send()
        for cp in mine:
            cp.wait()

    return pl.pallas_call(
        body,
        name=name,
        out_shape=_exchange_shapes(arrays, gather),
        in_specs=[pl.BlockSpec(memory_space=pl.ANY)] * n,
        out_specs=[pl.BlockSpec(memory_space=pl.ANY)] * n,
        scratch_shapes=_exchange_scratch(n),
    )(*arrays)


def _hosted_call(body, comm, *, name, grid, in_specs, out_specs, out_shape, scratch_shapes, dims, args):
    if comm is None:
        outs = pl.pallas_call(body, name=name, grid=grid, in_specs=in_specs, out_specs=out_specs, out_shape=out_shape,
                              scratch_shapes=scratch_shapes, compiler_params=_params(dims))(*args)
        return outs, []
    arrays, gather = comm
    n_in, n_out, n_scr, nc = len(in_specs), len(out_specs), len(scratch_shapes), len(arrays)

    def hosted(*refs):
        ins, c_in = refs[:n_in], refs[n_in:n_in + nc]
        outs, c_out = refs[n_in + nc:n_in + nc + n_out], refs[n_in + nc + n_out:n_in + 2 * nc + n_out]
        rest = refs[n_in + 2 * nc + n_out:]
        scratch, sems = rest[:n_scr], rest[n_scr:]
        mine, sends, recvs = _exchange_copies(c_in, c_out, *sems, gather)
        first = pl.program_id(0) == 0
        last = pl.program_id(0) == grid[0] - 1
        for axis in range(1, len(grid)):
            first = jnp.logical_and(first, pl.program_id(axis) == 0)
            last = jnp.logical_and(last, pl.program_id(axis) == grid[axis] - 1)

        @pl.when(first)
        def _():
            for cp in mine + sends:
                cp.start()

        body(*ins, *outs, *scratch)

        @pl.when(last)
        def _():
            for cp in recvs:
                cp.wait_recv()
            for cp in sends:
                cp.wait_send()
            for cp in mine:
                cp.wait()

    any_spec = pl.BlockSpec(memory_space=pl.ANY)
    outs = pl.pallas_call(
        hosted, name=name, grid=grid,
        in_specs=list(in_specs) + [any_spec] * nc,
        out_specs=list(out_specs) + [any_spec] * nc,
        out_shape=list(out_shape) + _exchange_shapes(arrays, gather),
        scratch_shapes=list(scratch_shapes) + _exchange_scratch(nc),
        compiler_params=_params(tuple("arbitrary" for _ in grid)),
    )(*args, *arrays)
    return outs[:n_out], outs[n_out:]


def _adam_math(g, w, m, v):
    m_new = B1 * m + (1.0 - B1) * g
    v_new = B2 * v + (1.0 - B2) * (g * g)
    m_hat = m_new / (1.0 - B1 ** STEP)
    v_hat = v_new / (1.0 - B2 ** STEP)
    return -LR * (m_hat / (jnp.sqrt(v_hat) + ADAM_EPS) + WD * w), m_new, v_new


def _adam(gparts, w, m, v, name, tr):
    n, rows, cols = gparts.shape

    def body(gp_ref, w_ref, m_ref, v_ref, g_ref, d_ref, nm_ref, nv_ref):
        g = gp_ref[0].astype(F32)
        for j in range(1, n):
            g = g + gp_ref[j].astype(F32)
        g_ref[...] = g
        d_ref[...], nm_ref[...], nv_ref[...] = _adam_math(g, w_ref[...], m_ref[...], v_ref[...])

    blk = pl.BlockSpec((tr, cols), lambda i: (i, 0))
    return pl.pallas_call(
        body,
        name=name,
        grid=(rows // tr,),
        out_shape=[jax.ShapeDtypeStruct((rows, cols), F32)] * 4,
        in_specs=[pl.BlockSpec((n, tr, cols), lambda i: (0, i, 0)), blk, blk, blk],
        out_specs=[blk] * 4,
        compiler_params=_params(("parallel",)),
    )(gparts, w, m, v)


def _adam_rows(parts, ws, ms, vs, name):
    n = parts.shape[0]
    k = len(ws)
    widths = [w.shape[1] for w in ws]

    def body(*refs):
        p_ref, w_refs, m_refs, v_refs = refs[0], refs[1:1 + k], refs[1 + k:1 + 2 * k], refs[1 + 2 * k:1 + 3 * k]
        outs = refs[1 + 3 * k:]
        total = p_ref[0:1, :]
        for j in range(1, n):
            total = total + p_ref[j:j + 1, :]
        off = 0
        for a, width in enumerate(widths):
            g = total[:, off:off + width]
            off += width
            outs[4 * a][...] = g
            outs[4 * a + 1][...], outs[4 * a + 2][...], outs[4 * a + 3][...] = _adam_math(
                g, w_refs[a][...], m_refs[a][...], v_refs[a][...])

    return pl.pallas_call(
        body, name=name,
        out_shape=[jax.ShapeDtypeStruct((1, width), F32) for width in widths for _ in range(4)],
        compiler_params=_params(),
    )(parts, *ws, *ms, *vs)


def _mm(a, b, name, out_dtype, tm, tn, n_out=1):
    m, k = a.shape
    n = b.shape[1] // n_out
    nt = n // tn

    def body(a_ref, *refs):
        a_blk = a_ref[...].astype(BF16)
        for b_ref, o_ref in zip(refs[:n_out], refs[n_out:]):
            o_ref[...] = _dot(a_blk, b_ref[...].astype(BF16)).astype(out_dtype)

    outs = pl.pallas_call(
        body,
        name=name,
        grid=(nt, m // tm),
        out_shape=[jax.ShapeDtypeStruct((m, n), out_dtype)] * n_out,
        in_specs=[pl.BlockSpec((tm, k), lambda j, i: (i, 0))]
        + [pl.BlockSpec((k, tn), lambda j, i, g=g: (0, g * nt + j)) for g in range(n_out)],
        out_specs=[pl.BlockSpec((tm, tn), lambda j, i: (i, j))] * n_out,
        compiler_params=_params(("parallel", "parallel")),
    )(a, *([b] * n_out))
    return outs[0] if n_out == 1 else outs


def _mm_nt(a_list, b, name, out_dtype, tm, tn, comm=None):
    n_a = len(a_list)
    m, k = a_list[0].shape
    n = b.shape[0]

    def body(*refs):
        o_ref = refs[2 * n_a]
        acc = _dot_nt(refs[0][...].astype(BF16), refs[n_a][...].astype(BF16))
        for g in range(1, n_a):
            acc = acc + _dot_nt(refs[g][...].astype(BF16), refs[n_a + g][...].astype(BF16))
        o_ref[...] = acc.astype(out_dtype)

    (out,), got = _hosted_call(
        body, comm, name=name, grid=(n // tn, m // tm),
        out_shape=[jax.ShapeDtypeStruct((m, n), out_dtype)],
        in_specs=[pl.BlockSpec((tm, k), lambda j, i: (i, 0))] * n_a
        + [pl.BlockSpec((tn, k), lambda j, i, g=g: (j, g)) for g in range(n_a)],
        out_specs=[pl.BlockSpec((tm, tn), lambda j, i: (i, j))],
        scratch_shapes=[], dims=("parallel", "parallel"), args=(*a_list, *([b] * n_a)))
    return out, got


def _mm_tn(a, b_list, name, tm, tn, tk):
    s, m = a.shape
    n = b_list[0].shape[1]
    n_b = len(b_list)

    def body(a_ref, *refs):
        b_refs, o_refs = refs[:n_b], refs[n_b:]

        @pl.when(pl.program_id(2) == 0)
        def _():
            for o_ref in o_refs:
                o_ref[...] = jnp.zeros_like(o_ref)

        a_blk = a_ref[...].astype(BF16)
        for b_ref, o_ref in zip(b_refs, o_refs):
            o_ref[...] += _dot_tn(a_blk, b_ref[...].astype(BF16))

    return pl.pallas_call(
        body,
        name=name,
        grid=(n // tn, m // tm, s // tk),
        out_shape=[jax.ShapeDtypeStruct((m, n), F32)] * n_b,
        in_specs=[pl.BlockSpec((tk, tm), lambda j, i, k: (k, i))] + [pl.BlockSpec((tk, tn), lambda j, i, k: (k, j))] * n_b,
        out_specs=[pl.BlockSpec((tm, tn), lambda j, i, k: (i, j))] * n_b,
        compiler_params=_params(("parallel", "parallel", "arbitrary")),
    )(a, *b_list)


def _rows_call(fn, name, rows, params, out_rows, out_accs, ts, comm=None):
    s = rows[0].shape[0]
    nr, npar, no = len(rows), len(params), len(out_rows)

    def body(*refs):
        r, p = refs[:nr], refs[nr:nr + npar]
        o, acc = refs[nr + npar:nr + npar + no], refs[nr + npar + no:]
        outs, sums = fn(*[t[...] for t in r], *[t[...] for t in p])
        for ref, val in zip(o, outs):
            ref[...] = val.astype(ref.dtype)
        if acc:
            @pl.when(pl.program_id(0) == 0)
            def _():
                for ref in acc:
                    ref[...] = jnp.zeros_like(ref)

            for ref, val in zip(acc, sums):
                ref[...] += val

    outs, got = _hosted_call(
        body, comm, name=name, grid=(s // ts,),
        out_shape=[jax.ShapeDtypeStruct((s, w), dt) for w, dt in out_rows]
        + [jax.ShapeDtypeStruct(shape, F32) for shape in out_accs],
        in_specs=[pl.BlockSpec((ts, t.shape[1]), lambda i: (i, 0)) for t in rows]
        + [pl.BlockSpec(t.shape, lambda i: (0, 0)) for t in params],
        out_specs=[pl.BlockSpec((ts, w), lambda i: (i, 0)) for w, _ in out_rows]
        + [pl.BlockSpec(shape, lambda i: (0, 0)) for shape in out_accs],
        scratch_shapes=[], dims=("arbitrary",), args=(*rows, *params))
    return outs if comm is None else (outs, got)


def _fwd_norm1(x, g, scale, shift):
    return (_norm_mod(x, g, scale, shift),), ()


def _resid_norm(x, mixed, gate, g, scale, shift):
    x1 = x + (1.0 + gate) * mixed
    return x1, _norm_mod(x1, g, scale, shift)


def _fwd_resid_norm(x, mixed, gate, g, scale, shift):
    return _resid_norm(x, mixed, gate, g, scale, shift), ()


def _final(x1, ffn, tgt, gate, g):
    def head(x1, ffn, gate, g):
        return _rms(x1 + (1.0 + gate) * ffn, g)

    y, vjp = jax.vjp(head, x1, ffn, gate, g)
    err = y - tgt
    dx2, dffn, dgate, dg = vjp(err * (1.0 / D))
    sq = jnp.sum(jnp.sum(err * err, axis=1, keepdims=True), axis=0, keepdims=True)
    loss = jnp.broadcast_to(sq * (0.5 / D), (1, 128))
    return (dx2, dffn), (dgate, dg, loss)


def _bwd_resid_norm(x, mixed, dx2, dh2, gate, g, scale, shift):
    _, vjp = jax.vjp(_resid_norm, x, mixed, gate, g, scale, shift)
    dx, dmixed, dgate, dg, dscale, dshift = vjp((dx2, dh2))
    return (dx, dmixed), (dgate, dg, dscale, dshift)


def _bwd_norm1(x, dh, dx1, g, scale, shift):
    _, vjp = jax.vjp(_norm_mod, x, g, scale, shift)
    dx, dg, dscale, dshift = vjp(dh)
    return (dx1 + dx,), (dg, dscale, dshift)


def _ada_fwd(c16, w):
    def body(c_ref, w_ref, o_ref):
        c = c_ref[...]
        o_ref[...] = _dot((c * _sigmoid(c)).astype(BF16), w_ref[...].astype(BF16))

    return pl.pallas_call(
        body, name="ada_fwd", out_shape=jax.ShapeDtypeStruct((c16.shape[0], w.shape[1]), F32),
        compiler_params=_params(),
    )(c16, w)


def _ada_bwd(c16, dmod16):
    def body(c_ref, d_ref, o_ref):
        c = c_ref[...]
        o_ref[...] = _dot_tn((c * _sigmoid(c)).astype(BF16), d_ref[...].astype(BF16))

    return pl.pallas_call(
        body, name="ada_bwd", out_shape=jax.ShapeDtypeStruct((c16.shape[1], dmod16.shape[1]), F32),
        compiler_params=_params(),
    )(c16, dmod16)


def _shift_rows(x, k, top):
    return jnp.concatenate([top, pltpu.roll(x, k, 0)[8:]], axis=0)


def _conv_taps(cur, halo, first):
    keep = jnp.where(first, 0.0, 1.0)
    ext = jnp.concatenate([halo * keep, cur[:8]], axis=0)
    s1 = _shift_rows(cur, 1, pltpu.roll(ext, 1, 0)[8:])
    s2 = _shift_rows(cur, 2, pltpu.roll(ext, 2, 0)[8:])
    return s1, s2


def _conv_u(cur, s1, s2, wc, bc):
    return bc + wc[0:1] * s2 + wc[1:2] * s1 + wc[2:3] * cur


def _ffn_specs(ts, s):
    cur = pl.BlockSpec((ts, FF_T), lambda j, i: (i, j))
    halo = pl.BlockSpec((8, FF_T), lambda j, i: (jnp.maximum(i * (ts // 8) - 1, 0), j))
    nxt = pl.BlockSpec((8, FF_T), lambda j, i: (jnp.minimum((i + 1) * (ts // 8), s // 8 - 1), j))
    wc = [pl.BlockSpec((8, FF_T), lambda j, i, h=h: (0, h * N_FT + j)) for h in range(2)]
    bc = [pl.BlockSpec((1, FF_T), lambda j, i, h=h: (0, h * N_FT + j)) for h in range(2)]
    return cur, halo, nxt, wc, bc


def _conv_glu_fwd(up_v, up_g, wc, bc, ts):
    s = up_v.shape[0]
    cur, halo, _, wcs, bcs = _ffn_specs(ts, s)

    def body(v_ref, vh_ref, g_ref, gh_ref, wcv_ref, wcg_ref, bcv_ref, bcg_ref, a_ref):
        first = pl.program_id(1) == 0
        xv, xg = v_ref[...], g_ref[...]
        val = _conv_u(xv, *_conv_taps(xv, vh_ref[...], first), wcv_ref[...], bcv_ref[...])
        gte = _conv_u(xg, *_conv_taps(xg, gh_ref[...], first), wcg_ref[...], bcg_ref[...])
        a_ref[...] = (val * (gte * _sigmoid(gte))).astype(BF16)

    return pl.pallas_call(
        body, name="conv_glu_fwd", grid=(N_FT, s // ts),
        out_shape=jax.ShapeDtypeStruct((s, D_FF), BF16),
        in_specs=[cur, halo, cur, halo, *wcs, *bcs], out_specs=cur,
        compiler_params=_params(("parallel", "arbitrary")),
    )(up_v, up_v, up_g, up_g, wc, wc, bc, bc)


def _glu_bwd(val, gte, da):
    sg = _sigmoid(gte)
    return da * (gte * sg), da * val * (sg * (1.0 + gte * (1.0 - sg)))


def _conv_glu_bwd(up_v, up_g, da, wc, bc, ts, comm=None):
    s = up_v.shape[0]
    nblk = s // ts
    cur, halo, nxt, wcs, bcs = _ffn_specs(ts, s)
    acc_w = pl.BlockSpec((8, FF_T), lambda j, i: (0, j))
    acc_b = pl.BlockSpec((1, FF_T), lambda j, i: (0, j))

    def body(v_ref, vh_ref, vn_ref, g_ref, gh_ref, gn_ref, da_ref, dan_ref, wcv_ref, wcg_ref, bcv_ref, bcg_ref,
             dupv_ref, dupg_ref, dwcv_ref, dwcg_ref, dbcv_ref, dbcg_ref):
        first = pl.program_id(1) == 0
        keep_next = jnp.where(pl.program_id(1) == nblk - 1, 0.0, 1.0)
        halves = ((v_ref, vh_ref, vn_ref, wcv_ref, bcv_ref), (g_ref, gh_ref, gn_ref, wcg_ref, bcg_ref))
        xs, taps, us, us_next = [], [], [], []
        for x_ref, h_ref, n_ref, wc_ref, bc_ref in halves:
            x, wc, bc = x_ref[...], wc_ref[...], bc_ref[...]
            s1, s2 = _conv_taps(x, h_ref[...], first)
            ext = jnp.concatenate([x[ts - 8:], n_ref[...]], axis=0)
            xs.append(x)
            taps.append((s1, s2))
            us.append(_conv_u(x, s1, s2, wc, bc))
            us_next.append(_conv_u(ext, pltpu.roll(ext, 1, 0), pltpu.roll(ext, 2, 0), wc, bc)[8:])
        dus = _glu_bwd(us[0], us[1], da_ref[...])
        dus_next = _glu_bwd(us_next[0], us_next[1], dan_ref[...] * keep_next)

        @pl.when(first)
        def _():
            for ref in (dwcv_ref, dwcg_ref, dbcv_ref, dbcg_ref):
                ref[...] = jnp.zeros_like(ref)

        outs = ((dupv_ref, dwcv_ref, dbcv_ref), (dupg_ref, dwcg_ref, dbcg_ref))
        for h in range(2):
            du, du_n, x, (s1, s2) = dus[h], dus_next[h], xs[h], taps[h]
            wc = halves[h][3][...]
            dup_ref, dwc_ref, dbc_ref = outs[h]
            ext = jnp.concatenate([du[ts - 8:], du_n], axis=0)
            a1 = jnp.concatenate([pltpu.roll(du, ts - 1, 0)[:ts - 8], pltpu.roll(ext, 15, 0)[:8]], axis=0)
            a2 = jnp.concatenate([pltpu.roll(du, ts - 2, 0)[:ts - 8], pltpu.roll(ext, 14, 0)[:8]], axis=0)
            dup_ref[...] = (wc[2:3] * du + wc[1:2] * a1 + wc[0:1] * a2).astype(BF16)
            dwc_ref[0:1, :] += jnp.sum(du * s2, axis=0, keepdims=True)
            dwc_ref[1:2, :] += jnp.sum(du * s1, axis=0, keepdims=True)
            dwc_ref[2:3, :] += jnp.sum(du * x, axis=0, keepdims=True)
            dbc_ref[...] += jnp.sum(du, axis=0, keepdims=True)

    return _hosted_call(
        body, comm, name="conv_glu_bwd", grid=(N_FT, nblk),
        out_shape=[jax.ShapeDtypeStruct((s, D_FF), BF16)] * 2 + [jax.ShapeDtypeStruct((8, D_FF), F32)] * 2
        + [jax.ShapeDtypeStruct((1, D_FF), F32)] * 2,
        in_specs=[cur, halo, nxt, cur, halo, nxt, cur, nxt, *wcs, *bcs],
        out_specs=[cur, cur, acc_w, acc_w, acc_b, acc_b],
        scratch_shapes=[], dims=("parallel", "arbitrary"),
        args=(up_v, up_v, up_v, up_g, up_g, up_g, da, da, wc, wc, bc, bc))


def _tri(kind):
    b = SB_B
    m = {"lower_strict": np.tril(np.ones((b, b)), -1), "upper_incl": np.triu(np.ones((b, b)), 0),
         "upper_strict": np.triu(np.ones((b, b)), 1)}[kind]
    return jnp.asarray(np.concatenate([m, np.ones((b, 128))], axis=1), BF16)


def _key_sums(x, tri):
    hi, lo = _hilo(x)
    cb = _dot(hi, tri) + _dot(lo, tri)
    return cb[:, :SB_B], cb[:, SB_B:]


def _sb_fwd(proj, comm=None):
    s = proj.shape[0]
    b = SB_B

    def body(q_ref, k_ref, v_ref, tri_ref, o_ref, t_ref, first_ref, c_ref, a_ref):
        i = pl.program_id(1)
        lane = lax.broadcasted_iota(jnp.int32, (b, 128), 1)
        heads = (lane < 64, lane >= 64)
        causal = lax.broadcasted_iota(jnp.int32, (b, b), 1) < lax.broadcasted_iota(jnp.int32, (b, b), 0)
        q = q_ref[...] * 0.125
        qm = [jnp.where(h, q, 0.0).astype(BF16) for h in heads]
        c_ref[...] = jnp.zeros_like(c_ref)
        a_ref[...] = jnp.zeros_like(a_ref)

        def step(jj, masked):
            rows = pl.ds(pl.multiple_of(jj * b, b), b)
            kb = k_ref[rows, :].astype(BF16)
            vb = v_ref[rows, :]
            for hh in range(2):
                z = _dot_nt(qm[hh], kb)
                lg = _log_sigmoid(-z)
                if masked:
                    lg = jnp.where(causal, lg, 0.0)
                after, total = _key_sums(lg, tri_ref[...])
                c = c_ref[hh]
                w = jnp.exp(lg + z + after + jnp.concatenate([c, c], axis=1))
                if masked:
                    w = jnp.where(causal, w, 0.0)
                a_ref[hh] += _dot(w.astype(BF16), jnp.where(heads[hh], vb, 0.0).astype(BF16))
                c_ref[hh] = c + total

        def largest_sum():
            return jnp.max(jnp.maximum(c_ref[0], c_ref[1]))

        step(i, True)

        def more(state):
            jj, top = state
            return jnp.logical_and(jj >= 0, top > SB_DEAD)

        def walk(state):
            jj, _ = state
            step(jj, False)
            return jj - 1, largest_sum()

        jj, _ = lax.while_loop(more, walk, (i - 1, largest_sum()))
        o_ref[...] = a_ref[0] + a_ref[1]
        t_ref[...] = jnp.concatenate([c_ref[0], c_ref[1]], axis=1)
        first_ref[pl.program_id(0), i] = (jj + 1).astype(F32)

    return _hosted_call(
        body, comm, name="sb_fwd", grid=(4, s // b),
        out_shape=[jax.ShapeDtypeStruct((s, SB_W), F32), jax.ShapeDtypeStruct((s, 2 * SB_W), F32),
                   jax.ShapeDtypeStruct((4, s // b), F32)],
        in_specs=[pl.BlockSpec((b, 128), lambda p, i: (i, p)),
                  pl.BlockSpec((s, 128), lambda p, i: (0, 4 + p)),
                  pl.BlockSpec((s, 128), lambda p, i: (0, 8 + p)),
                  pl.BlockSpec((b, b + 128), lambda p, i: (0, 0))],
        out_specs=[pl.BlockSpec((b, 128), lambda p, i: (i, p)), pl.BlockSpec((b, 256), lambda p, i: (i, p)),
                   pl.BlockSpec(memory_space=pltpu.SMEM)],
        scratch_shapes=[pltpu.VMEM((2, b, 128), F32), pltpu.VMEM((2, b, 128), F32)],
        dims=("arbitrary", "arbitrary"), args=(proj, proj, proj, _tri("lower_strict")))


def _sb_bwd(proj, do, tot, first, comm=None):
    s = proj.shape[0]
    b = SB_B

    def body(q_ref, k_ref, v_ref, do_ref, t_ref, first_ref, ti_ref, ts_ref, dq_ref, dk_ref, dv_ref, cl_ref, ce_ref,
             a_ref):
        i = pl.program_id(1)
        first = jnp.clip(first_ref[pl.program_id(0), i].astype(jnp.int32), 0, i)
        lane = lax.broadcasted_iota(jnp.int32, (b, 128), 1)
        heads = (lane < 64, lane >= 64)
        causal = lax.broadcasted_iota(jnp.int32, (b, b), 1) < lax.broadcasted_iota(jnp.int32, (b, b), 0)
        q = q_ref[...] * 0.125
        do = do_ref[...]
        qm = [jnp.where(h, q, 0.0).astype(BF16) for h in heads]
        dom = [jnp.where(h, do, 0.0).astype(BF16) for h in heads]
        cl_ref[...] = jnp.zeros_like(cl_ref)
        ce_ref[...] = jnp.zeros_like(ce_ref)
        a_ref[...] = jnp.zeros_like(a_ref)

        @pl.when(i == 0)
        def _():
            dk_ref[...] = jnp.zeros_like(dk_ref)
            dv_ref[...] = jnp.zeros_like(dv_ref)

        def step(jj, masked):
            rows = pl.ds(pl.multiple_of(jj * b, b), b)
            kf = k_ref[rows, :]
            kb = kf.astype(BF16)
            vb = v_ref[rows, :].astype(BF16)
            dk = jnp.zeros((b, 128), F32)
            dv = jnp.zeros((b, 128), F32)
            for hh in range(2):
                z = _dot_nt(qm[hh], kb)
                lg = _log_sigmoid(-z)
                if masked:
                    lg = jnp.where(causal, lg, 0.0)
                upto, total = _key_sums(lg, ti_ref[...])
                cl = cl_ref[hh]
                t = t_ref[:, hh * 128:(hh + 1) * 128]
                rest = jnp.concatenate([t - cl, t - cl], axis=1) - upto
                lsz = lg + z
                w = jnp.exp(lsz + rest)
                sig = jnp.exp(lsz)
                if masked:
                    w = jnp.where(causal, w, 0.0)
                e = w * _dot_nt(dom[hh], vb)
                before, etot = _key_sums(e, ts_ref[...])
                ce = ce_ref[hh]
                dz = e - sig * (e + before + jnp.concatenate([ce, ce], axis=1))
                if masked:
                    dz = jnp.where(causal, dz, 0.0)
                dzb = dz.astype(BF16)
                a_ref[hh] += _dot(dzb, jnp.where(heads[hh], kf, 0.0).astype(BF16))
                dk = dk + _dot_tn(dzb, qm[hh])
                dv = dv + _dot_tn(w.astype(BF16), dom[hh])
                cl_ref[hh] = cl + total
                ce_ref[hh] = ce + etot
            dk_ref[rows, :] += dk
            dv_ref[rows, :] += dv

        def loop(jj, carry):
            step(jj, False)
            return carry

        lax.fori_loop(first, i, loop, 0)
        step(i, True)
        dq_ref[...] = (a_ref[0] + a_ref[1]) * 0.125

    blk = pl.BlockSpec((b, 128), lambda p, i: (i, p))
    full = pl.BlockSpec((s, 128), lambda p, i: (0, p))
    tri = pl.BlockSpec((b, b + 128), lambda p, i: (0, 0))
    return _hosted_call(
        body, comm, name="sb_bwd", grid=(4, s // b),
        out_shape=[jax.ShapeDtypeStruct((s, SB_W), F32)] * 3,
        in_specs=[blk, pl.BlockSpec((s, 128), lambda p, i: (0, 4 + p)), pl.BlockSpec((s, 128), lambda p, i: (0, 8 + p)),
                  blk, pl.BlockSpec((b, 256), lambda p, i: (i, p)), pl.BlockSpec(memory_space=pltpu.SMEM), tri, tri],
        out_specs=[blk, full, full],
        scratch_shapes=[pltpu.VMEM((2, b, 128), F32)] * 3,
        dims=("arbitrary", "arbitrary"),
        args=(proj, proj, proj, do, tot, first, _tri("upper_incl"), _tri("upper_strict")))


_COL_Q, _COL_K, _COL_V, _COL_G, _COL_F = 12, 14, 8, 10, 24


def _gla_consts():
    c = CHUNK
    incl = np.tril(np.ones((c, c)), 0)
    strict = np.tril(np.ones((c, c)), -1)
    bd = np.zeros((128, 256))
    bd[:64, :128] = 1.0
    bd[64:, 128:] = 1.0
    return jnp.asarray(incl, BF16), jnp.asarray(strict, BF16), jnp.asarray(bd, F32)


def _time_sums(tri, x):
    hi, lo = _hilo(x)
    return _dot(tri, hi) + _dot(tri, lo)


def _gla_gate(o, gg, g):
    parts = []
    for h in range(2):
        oh = o[:, h * 128:(h + 1) * 128]
        parts.append(oh * lax.rsqrt(jnp.mean(oh * oh, axis=-1, keepdims=True) + EPS))
    return (jnp.concatenate(parts, axis=1) * g) * (gg * _sigmoid(gg))


def _gla_chunk(la_c, k_c, incl, ones_cv):
    cum = _time_sums(incl, la_c)
    total = cum[CHUNK - 1:CHUNK]
    edec = jnp.exp(total - cum)
    kdec = k_c * edec
    hi, lo = _hilo(la_c)
    dec = jnp.exp(_dot_tn(hi, ones_cv) + _dot_tn(lo, ones_cv))
    return edec, kdec, dec


def _gla_fwd(proj, wfg, bfg, gain):
    s = proj.shape[0]
    t = GLA_T
    nch = t // CHUNK
    incl, _, bd = _gla_consts()

    def body(q_ref, k_ref, v_ref, gg_ref, f_ref, wf_ref, bf_ref, g_ref, incl_ref, bd_ref, o_ref, st_ref, state_ref):
        @pl.when(pl.program_id(1) == 0)
        def _():
            state_ref[...] = jnp.zeros_like(state_ref)

        la = _log_sigmoid(_dot(f_ref[...].astype(BF16), wf_ref[...].astype(BF16)) + bf_ref[...]) * (1.0 / 16.0)
        ones_cv = jnp.ones((CHUNK, 256), BF16)
        state = state_ref[...]
        for cc in range(nch):
            rows = slice(cc * CHUNK, (cc + 1) * CHUNK)
            _, kdec, dec = _gla_chunk(la[rows], k_ref[rows, :], incl_ref[...], ones_cv)
            kv = _dot_tn(kdec.astype(BF16), v_ref[rows, :].astype(BF16))
            state = dec * state + bd_ref[...] * kv
            st_ref[cc] = state
            o = _dot((q_ref[rows, :] * 0.125).astype(BF16), state.astype(BF16))
            o_ref[rows, :] = _gla_gate(o, gg_ref[rows, :], g_ref[...])
        state_ref[...] = state

    def col(width, blk):
        return pl.BlockSpec((t, width), lambda p, i: (i, blk + p))

    return pl.pallas_call(
        body, name="gla_fwd", grid=(2, s // t),
        out_shape=[jax.ShapeDtypeStruct((s, GLA_VW), F32), jax.ShapeDtypeStruct((2, s // CHUNK, 128, 256), F32)],
        in_specs=[col(128, _COL_Q), col(128, _COL_K), col(256, _COL_V), col(256, _COL_G),
                  pl.BlockSpec((t, 128), lambda p, i: (i, _COL_F)),
                  pl.BlockSpec((128, 128), lambda p, i: (0, p)), pl.BlockSpec((1, 128), lambda p, i: (0, p)),
                  pl.BlockSpec((1, 256), lambda p, i: (0, p)),
                  pl.BlockSpec((CHUNK, CHUNK), lambda p, i: (0, 0)), pl.BlockSpec((128, 256), lambda p, i: (0, 0))],
        out_specs=[pl.BlockSpec((t, 256), lambda p, i: (i, p)),
                   pl.BlockSpec((None, nch, 128, 256), lambda p, i: (p, i, 0, 0))],
        scratch_shapes=[pltpu.VMEM((128, 256), F32)],
        compiler_params=_params(("parallel", "arbitrary")),
    )(proj, proj, proj, proj, proj, wfg, bfg, gain, incl, bd)


def _gla_bwd(proj, do, states, wfg, bfg, gain):
    s = proj.shape[0]
    t = GLA_T
    nch = t // CHUNK
    nblk = s // t
    incl, strict, bd = _gla_consts()

    def body(q_ref, k_ref, v_ref, gg_ref, f_ref, do_ref, st_ref, sp_ref, wf_ref, bf_ref, g_ref, incl_ref, str_ref,
             bd_ref, dq_ref, dk_ref, dv_ref, dgg_ref, df_ref, dwf_ref, dbf_ref, dg_ref, carry_ref, dfs_ref):
        i = pl.program_id(1)

        @pl.when(i == 0)
        def _():
            carry_ref[...] = jnp.zeros_like(carry_ref)
            dwf_ref[...] = jnp.zeros_like(dwf_ref)
            dbf_ref[...] = jnp.zeros_like(dbf_ref)
            dg_ref[...] = jnp.zeros_like(dg_ref)

        fb = f_ref[...].astype(BF16)
        f = _dot(fb, wf_ref[...].astype(BF16)) + bf_ref[...]
        la = _log_sigmoid(f) * (1.0 / 16.0)
        dla_df = _sigmoid(-f) * (1.0 / 16.0)
        ones_cv = jnp.ones((CHUNK, 256), BF16)
        ones_8v = jnp.ones((8, 256), BF16)
        first_block = jnp.where(i == nblk - 1, 0.0, 1.0)
        carry = carry_ref[...]
        dgain = jnp.zeros((1, 256), F32)
        for cc in reversed(range(nch)):
            rows = slice(cc * CHUNK, (cc + 1) * CHUNK)
            k_c = k_ref[rows, :]
            edec, kdec, dec = _gla_chunk(la[rows], k_c, incl_ref[...], ones_cv)
            state = st_ref[cc]
            prev = st_ref[cc - 1] if cc > 0 else sp_ref[0] * first_block
            qs = (q_ref[rows, :] * 0.125).astype(BF16)
            sb16 = state.astype(BF16)
            o = _dot(qs, sb16)
            _, vjp = jax.vjp(_gla_gate, o, gg_ref[rows, :], g_ref[...])
            do_c, dgg_c, dg_c = vjp(do_ref[rows, :])
            dgain = dgain + dg_c
            dgg_ref[rows, :] = dgg_c
            do16 = do_c.astype(BF16)
            dq_ref[rows, :] = _dot_nt(do16, sb16) * 0.125
            grad = bd_ref[...] * _dot_tn(qs, do16) + carry
            g16 = grad.astype(BF16)
            dv_ref[rows, :] = _dot(kdec.astype(BF16), g16)
            dkdec = _dot_nt(v_ref[rows, :].astype(BF16), g16)
            hi, lo = _hilo(grad * prev * dec)
            ddec = (_dot_nt(ones_8v, hi) + _dot_nt(ones_8v, lo))[0:1]
            dk_ref[rows, :] = dkdec * edec
            dla = _time_sums(str_ref[...], dkdec * kdec) + ddec
            dfs_ref[rows, :] = dla * dla_df[rows]
            carry = dec * grad
        carry_ref[...] = carry
        df = dfs_ref[...]
        df16 = df.astype(BF16)
        df_ref[...] = _dot_nt(df16, wf_ref[...].astype(BF16))
        dwf_ref[...] += _dot_tn(fb, df16)
        dbf_ref[...] += jnp.sum(df, axis=0, keepdims=True)
        dg_ref[...] += dgain

    def col(width, blk):
        return pl.BlockSpec((t, width), lambda p, i: (nblk - 1 - i, blk + p))

    def pair(width):
        return pl.BlockSpec((t, width), lambda p, i: (nblk - 1 - i, p))

    return pl.pallas_call(
        body, name="gla_bwd", grid=(2, nblk),
        out_shape=[jax.ShapeDtypeStruct((s, GLA_KW), F32), jax.ShapeDtypeStruct((s, GLA_KW), F32),
                   jax.ShapeDtypeStruct((s, GLA_VW), F32), jax.ShapeDtypeStruct((s, GLA_VW), F32),
                   jax.ShapeDtypeStruct((2, s, 128), F32), jax.ShapeDtypeStruct((128, GLA_KW), F32),
                   jax.ShapeDtypeStruct((1, GLA_KW), F32), jax.ShapeDtypeStruct((1, GLA_VW), F32)],
        in_specs=[col(128, _COL_Q), col(128, _COL_K), col(256, _COL_V), col(256, _COL_G),
                  pl.BlockSpec((t, 128), lambda p, i: (nblk - 1 - i, _COL_F)),
                  pl.BlockSpec((t, 256), lambda p, i: (nblk - 1 - i, SB_W // 256 + p)),
                  pl.BlockSpec((None, nch, 128, 256), lambda p, i: (p, nblk - 1 - i, 0, 0)),
                  pl.BlockSpec((None, 1, 128, 256), lambda p, i: (p, jnp.maximum((nblk - 1 - i) * nch - 1, 0), 0, 0)),
                  pl.BlockSpec((128, 128), lambda p, i: (0, p)), pl.BlockSpec((1, 128), lambda p, i: (0, p)),
                  pl.BlockSpec((1, 256), lambda p, i: (0, p)),
                  pl.BlockSpec((CHUNK, CHUNK), lambda p, i: (0, 0)), pl.BlockSpec((CHUNK, CHUNK), lambda p, i: (0, 0)),
                  pl.BlockSpec((128, 256), lambda p, i: (0, 0))],
        out_specs=[pair(128), pair(128), pair(256), pair(256),
                   pl.BlockSpec((None, t, 128), lambda p, i: (p, nblk - 1 - i, 0)),
                   pl.BlockSpec((128, 128), lambda p, i: (0, p)), pl.BlockSpec((1, 128), lambda p, i: (0, p)),
                   pl.BlockSpec((1, 256), lambda p, i: (0, p))],
        scratch_shapes=[pltpu.VMEM((128, 256), F32), pltpu.VMEM((t, 128), F32)],
        compiler_params=_params(("parallel", "arbitrary")),
    )(proj, proj, proj, proj, proj, do, states, states, wfg, bfg, gain, incl, strict, bd)


def _cols_by_dev(a, n_dev=N_DEV):
    r, c = a.shape
    return a.reshape(r, n_dev, c // n_dev).transpose(1, 0, 2)


def _cols_from_dev(a):
    _, r, n = a.shape
    return a.transpose(1, 0, 2).reshape(r, N_DEV * n)


def kernel(x, c, w_ada, b_ada, g_norm1, w_in, w_fg2, b_fg2, g_gla_out, w_out, g_norm2, w_up, w_conv, b_conv, w_down, g_final, loss_target, m_w_ada, m_b_ada, m_g_norm1, m_w_in, m_w_fg2, m_b_fg2, m_g_gla_out, m_w_out, m_g_norm2, m_w_up, m_w_conv, m_b_conv, m_w_down, m_g_final, v_w_ada, v_b_ada, v_g_norm1, v_w_in, v_w_fg2, v_b_fg2, v_g_gla_out, v_w_out, v_g_norm2, v_w_up, v_w_conv, v_b_conv, v_w_down, v_g_final):
    s = x.shape[1]
    me = 4 * lax.axis_index("x") + 2 * lax.axis_index("y") + lax.axis_index("c")
    xs, tgt = x[0], loss_target[0]
    ts = min(512, s)
    tm = min(1024, s)

    c_all, wconv_g, wfg_g = _exchange([c, w_conv[0], w_fg2[0]], "gather_first", True)
    wconv_p = jnp.pad(_cols_from_dev(wconv_g), ((0, 5), (0, 0)))
    wfg_p = jnp.pad(_cols_from_dev(wfg_g), ((0, 128 - RANK), (0, 0)))

    c16 = jnp.pad(c_all.reshape(N_DEV, D), ((0, 8), (0, 0)))
    modp = _ada_fwd(c16, w_ada[0])[:N_DEV]
    (mod_all,) = _exchange([modp], "gather_mod", True)
    mod = lax.dynamic_index_in_dim(mod_all, me, axis=1, keepdims=False).reshape(1, 6 * D) + b_ada
    shift1, scale1, gate1, shift2, scale2, gate2 = [mod[:, k * D:(k + 1) * D] for k in range(6)]

    (h,), (win_g,) = _rows_call(_fwd_norm1, "norm1", [xs], [g_norm1, scale1, shift1], [(D, BF16)], [], ts,
                                comm=([w_in[0].astype(BF16)], True))
    win = jnp.pad(_cols_from_dev(win_g), ((0, 0), (0, IN_WP - IN_W)))
    proj = _mm(h, win, "in_proj", F32, tm, 640)
    (o_sb, sb_tot, sb_first), (wout_g, wup_g, wdown_g) = _sb_fwd(
        proj, comm=([w_out[0].astype(BF16), w_up[0].astype(BF16), w_down[0].astype(BF16)], True))
    wout = wout_g.reshape(D, D)
    wup = _cols_from_dev(wup_g)
    wdown = wdown_g.reshape(D_FF, D)
    o_gla, states = _gla_fwd(proj, wfg_p, b_fg2, g_gla_out)
    cat = jnp.concatenate([o_sb, o_gla], axis=1).astype(BF16)
    mixed = _mm(cat, wout, "out_proj", F32, tm, 512)
    x1, h2 = _rows_call(_fwd_resid_norm, "resid_norm2", [xs, mixed], [gate1, g_norm2, scale2, shift2],
                        [(D, F32), (D, BF16)], [], ts)
    up_v, up_g = _mm(h2, wup, "up_proj", F32, tm, 1408, n_out=2)
    act = _conv_glu_fwd(up_v, up_g, wconv_p, b_conv, tm)
    ffn = _mm(act, wdown, "down_proj", F32, tm, 512)
    dx2, dffn, dgate2, dg_final, loss_part = _rows_call(
        _final, "final", [x1, ffn, tgt], [gate2, g_final.reshape(1, D)],
        [(D, F32), (D, BF16)], [(1, D), (1, D), (1, 128)], ts)

    (dw_down,) = _mm_tn(act, [dffn], "down_wgrad", 1408, 1024, tm)
    da, _ = _mm_nt([dffn], wdown, "down_dgrad", F32, tm, 1408)
    (dup_v, dup_g, dwc_v, dwc_g, dbc_v, dbc_g), (r_down,) = _conv_glu_bwd(
        up_v, up_g, da, wconv_p, b_conv, tm, comm=([dw_down.reshape(N_DEV, 352, D)], False))
    dw_up_v, dw_up_g = _mm_tn(h2, [dup_v, dup_g], "up_wgrad", 1024, 1408, tm)
    dh2, _ = _mm_nt([dup_v, dup_g], wup, "up_dgrad", F32, tm, 512)
    dx1, dmixed, dgate1, dg_norm2, dscale2, dshift2 = _rows_call(
        _bwd_resid_norm, "resid_norm2_bwd", [xs, mixed, dx2, dh2], [gate1, g_norm2, scale2, shift2],
        [(D, F32), (D, BF16)], [(1, D)] * 4, ts)
    (dw_out,) = _mm_tn(cat, [dmixed], "out_wgrad", 1024, 1024, tm)
    dcat, _ = _mm_nt([dmixed], wout, "out_dgrad", F32, tm, 512)
    dw_up_s = jnp.concatenate([_cols_by_dev(dw_up_v, 4), _cols_by_dev(dw_up_g, 4)], axis=0)
    dwconv_s = _cols_by_dev(jnp.concatenate([dwc_v[:3], dwc_g[:3]], axis=1))
    (dq, dk, dv), (r_up, r_out, r_conv) = _sb_bwd(
        proj, dcat, sb_tot, sb_first, comm=([dw_up_s, dw_out.reshape(N_DEV, 128, D), dwconv_s], False))
    dgq, dgk, dgv, dgg, dgf, dwfg_p, dbfg, dg_gla = _gla_bwd(proj, dcat, states, wfg_p, b_fg2, g_gla_out)
    dproj = jnp.concatenate([dq, dk, dv, dgq, dgk, dgv, dgg, dgf[0] + dgf[1]], axis=1).astype(BF16)
    (dw_in,) = _mm_tn(h, [dproj], "in_wgrad", 1024, 640, tm)
    dh, (r_in, r_fg) = _mm_nt([dproj], win, "in_dgrad", F32, tm, 512,
                              comm=([_cols_by_dev(dw_in[:, :IN_W]).astype(BF16), _cols_by_dev(dwfg_p[:RANK])], False))
    grad_x, dg_norm1, dscale1, dshift1 = _rows_call(
        _bwd_norm1, "norm1_bwd", [xs, dh, dx1], [g_norm1, scale1, shift1], [(D, F32)], [(1, D)] * 3, ts)

    dmod = jnp.concatenate([dshift1, dscale1, dgate1, dshift2, dscale2, dgate2], axis=1)
    small_parts = [dmod, dg_norm1, dbfg, dg_gla, dg_norm2, dbc_v, dbc_g, dg_final, loss_part]
    n_small = sum(p.shape[1] for p in small_parts[:-1])
    (sg,) = _exchange([jnp.concatenate(small_parts, axis=1)], "gather_small_grads", True)
    sg = sg.reshape(N_DEV, n_small + 128)
    loss = jnp.sum(sg[:, n_small])
    small_w = [b_ada, g_norm1, b_fg2, g_gla_out, g_norm2, b_conv, g_final.reshape(1, D)]
    small_m = [m_b_ada, m_g_norm1, m_b_fg2, m_g_gla_out, m_g_norm2, m_b_conv, m_g_final.reshape(1, D)]
    small_v = [v_b_ada, v_g_norm1, v_b_fg2, v_g_gla_out, v_g_norm2, v_b_conv, v_g_final.reshape(1, D)]
    s_out = list(_adam_rows(sg, small_w, small_m, small_v, "adam_small"))
    s_out[24:] = [t.reshape(D) for t in s_out[24:]]
    s_g, s_d, s_m, s_v = [s_out[k::4] for k in range(4)]

    dmod_all = sg[:, :6 * D].reshape(N_DEV, N_DEV, 768)
    dmod_mine = lax.dynamic_index_in_dim(dmod_all, me, axis=1, keepdims=False)
    dw_ada = _ada_bwd(c16, jnp.pad(dmod_mine, ((0, 8), (0, 0))))
    a_g, a_d, a_m, a_v = [t[None] for t in _adam(dw_ada[None], w_ada[0], m_w_ada[0], v_w_ada[0], "adam_ada", 256)]

    recv = [r_in, r_fg, r_out, r_up, r_conv, r_down]
    big_w = [w_in, w_fg2, w_out, w_up, w_conv, w_down]
    big_m = [m_w_in, m_w_fg2, m_w_out, m_w_up, m_w_conv, m_w_down]
    big_v = [v_w_in, v_w_fg2, v_w_out, v_w_up, v_w_conv, v_w_down]
    big_name = ["adam_in", "adam_fg2", "adam_out", "adam_up", "adam_conv", "adam_down"]
    big_rows = [256, RANK, 128, 256, 3, 176]
    b_out = [_adam(r, w[0], m[0], v[0], name, tr)
             for r, w, m, v, name, tr in zip(recv, big_w, big_m, big_v, big_name, big_rows)]
    b_g, b_d, b_m, b_v = [[o[k][None] for o in b_out] for k in range(4)]

    def ordered(a, sm, bg):
        return [a, sm[0], sm[1], bg[0], bg[1], sm[2], sm[3], bg[2], sm[4], bg[3], bg[4], sm[5], bg[5], sm[6]]

    return (loss, grad_x[None], *ordered(a_g, s_g, b_g), *ordered(a_d, s_d, b_d),
            *ordered(a_m, s_m, b_m), *ordered(a_v, s_v, b_v))
```

```python
import numpy as np

import jax
import jax.numpy as jnp
from jax import lax
from jax.experimental import pallas as pl
from jax.experimental.pallas import tpu as pltpu

F32, BF16 = jnp.float32, jnp.bfloat16
N_DEV = 8
D = 1024
SB_W = 512
GLA_KW, GLA_VW = 256, 512
RANK = 16
IN_W = 3088
IN_WP = 3200
D_FF = 2816
FF_T = 256
N_FT = D_FF // FF_T
EPS = 1e-6
SB_B = 256
SB_DEAD = -110.0
CHUNK = 64
GLA_T = 512
VMEM_LIMIT = 56 * 1024 * 1024

LR, B1, B2, ADAM_EPS, WD, STEP = 0.001, 0.9, 0.999, 1e-08, 0.01, 10


def _params(dims=None, vmem=True):
    kw = {}
    if dims is not None:
        kw["dimension_semantics"] = dims
    if vmem:
        kw["vmem_limit_bytes"] = VMEM_LIMIT
    return pltpu.CompilerParams(**kw)


def _dot(a, b):
    return jnp.dot(a, b, preferred_element_type=F32)


def _dot_nt(a, b):
    return lax.dot_general(a, b, (((1,), (1,)), ((), ())), preferred_element_type=F32)


def _dot_tn(a, b):
    return lax.dot_general(a, b, (((0,), (0,)), ((), ())), preferred_element_type=F32)


def _hilo(x):
    hi = x.astype(BF16)
    lo = (x - hi.astype(F32)).astype(BF16)
    return hi, lo


def _sigmoid(x):
    return 1.0 / (1.0 + jnp.exp(-x))


def _log_sigmoid(x):
    return jnp.minimum(x, 0.0) - jnp.log(1.0 + jnp.exp(-jnp.abs(x)))


def _rms(x, g):
    n = x * lax.rsqrt(jnp.mean(x * x, axis=-1, keepdims=True) + EPS)
    return n * g


def _norm_mod(x, g, scale, shift):
    return _rms(x, g) * (1.0 + scale) + shift


N_PEER = N_DEV - 1


def _exchange_copies(x_refs, out_refs, send_sems, recv_sems, local_sems, gather):
    ix, iy, ic = lax.axis_index("x"), lax.axis_index("y"), lax.axis_index("c")
    me = 4 * ix + 2 * iy + ic
    peers = []
    for k in range(1, N_DEV):
        px = 1 - ix if k & 4 else ix
        py = 1 - iy if k & 2 else iy
        pc = 1 - ic if k & 1 else ic
        peers.append(((px, py, pc), 4 * px + 2 * py + pc))

    def copy(a, k, dev, src_slot, dst_slot):
        return pltpu.make_async_remote_copy(
            src_ref=x_refs[a] if gather else x_refs[a].at[src_slot],
            dst_ref=out_refs[a].at[dst_slot],
            send_sem=send_sems.at[a * N_PEER + k],
            recv_sem=recv_sems.at[a * N_PEER + k],
            device_id=dev,
            device_id_type=pl.DeviceIdType.MESH,
        )

    n = len(x_refs)
    mine = [pltpu.make_async_copy(x_refs[a] if gather else x_refs[a].at[me], out_refs[a].at[me], local_sems.at[a])
            for a in range(n)]
    sends = [copy(a, k, dev, pid, me) for a in range(n) for k, (dev, pid) in enumerate(peers)]
    recvs = [copy(a, k, dev, pid, pid) for a in range(n) for k, (dev, pid) in enumerate(peers)]
    return mine, sends, recvs


def _exchange_scratch(n):
    return [pltpu.SemaphoreType.DMA((n * N_PEER,)), pltpu.SemaphoreType.DMA((n * N_PEER,)),
            pltpu.SemaphoreType.DMA((n,))]


def _exchange_shapes(arrays, gather):
    return [jax.ShapeDtypeStruct((N_DEV,) + tuple(x.shape if gather else x.shape[1:]), x.dtype) for x in arrays]


def _exchange(arrays, name, gather):
    n = len(arrays)

    def body(*refs):
        mine, sends, recvs = _exchange_copies(refs[:n], refs[n:2 * n], *refs[2 * n:], gather)
        for cp in mine + sends:
            cp.start()
        for cp in recvs:
            cp.wait_recv()
        for cp in sends:
            cp.wait_send()
        for cp in mine:
            cp.wait()

    return pl.pallas_call(
        body,
        name=name,
        out_shape=_exchange_shapes(arrays, gather),
        in_specs=[pl.BlockSpec(memory_space=pl.ANY)] * n,
        out_specs=[pl.BlockSpec(memory_space=pl.ANY)] * n,
        scratch_shapes=_exchange_scratch(n),
    )(*arrays)


def _hosted_call(body, comm, *, name, grid, in_specs, out_specs, out_shape, scratch_shapes, dims, args):
    if comm is None:
        outs = pl.pallas_call(body, name=name, grid=grid, in_specs=in_specs, out_specs=out_specs, out_shape=out_shape,
                              scratch_shapes=scratch_shapes, compiler_params=_params(dims))(*args)
        return outs, []
    arrays, gather = comm
    n_in, n_out, n_scr, nc = len(in_specs), len(out_specs), len(scratch_shapes), len(arrays)

    def hosted(*refs):
        ins, c_in = refs[:n_in], refs[n_in:n_in + nc]
        outs, c_out = refs[n_in + nc:n_in + nc + n_out], refs[n_in + nc + n_out:n_in + 2 * nc + n_out]
        rest = refs[n_in + 2 * nc + n_out:]
        scratch, sems = rest[:n_scr], rest[n_scr:]
        mine, sends, recvs = _exchange_copies(c_in, c_out, *sems, gather)
        first = pl.program_id(0) == 0
        last = pl.program_id(0) == grid[0] - 1
        for axis in range(1, len(grid)):
            first = jnp.logical_and(first, pl.program_id(axis) == 0)
            last = jnp.logical_and(last, pl.program_id(axis) == grid[axis] - 1)

        @pl.when(first)
        def _():
            for cp in mine + sends:
                cp.start()

        body(*ins, *outs, *scratch)

        @pl.when(last)
        def _():
            for cp in recvs:
                cp.wait_recv()
            for cp in sends:
                cp.wait_send()
            for cp in mine:
                cp.wait()

    any_spec = pl.BlockSpec(memory_space=pl.ANY)
    outs = pl.pallas_call(
        hosted, name=name, grid=grid,
        in_specs=list(in_specs) + [any_spec] * nc,
        out_specs=list(out_specs) + [any_spec] * nc,
        out_shape=list(out_shape) + _exchange_shapes(arrays, gather),
        scratch_shapes=list(scratch_shapes) + _exchange_scratch(nc),
        compiler_params=_params(tuple("arbitrary" for _ in grid)),
    )(*args, *arrays)
    return outs[:n_out], outs[n_out:]


def _adam_math(g, w, m, v):
    m_new = B1 * m + (1.0 - B1) * g
    v_new = B2 * v + (1.0 - B2) * (g * g)
    m_hat = m_new / (1.0 - B1 ** STEP)
    v_hat = v_new / (1.0 - B2 ** STEP)
    return -LR * (m_hat / (jnp.sqrt(v_hat) + ADAM_EPS) + WD * w), m_new, v_new


def _adam(gparts, w, m, v, name, tr):
    n, rows, cols = gparts.shape

    def body(gp_ref, w_ref, m_ref, v_ref, g_ref, d_ref, nm_ref, nv_ref):
        g = gp_ref[0].astype(F32)
        for j in range(1, n):
            g = g + gp_ref[j].astype(F32)
        g_ref[...] = g
        d_ref[...], nm_ref[...], nv_ref[...] = _adam_math(g, w_ref[...], m_ref[...], v_ref[...])

    blk = pl.BlockSpec((tr, cols), lambda i: (i, 0))
    return pl.pallas_call(
        body,
        name=name,
        grid=(rows // tr,),
        out_shape=[jax.ShapeDtypeStruct((rows, cols), F32)] * 4,
        in_specs=[pl.BlockSpec((n, tr, cols), lambda i: (0, i, 0)), blk, blk, blk],
        out_specs=[blk] * 4,
        compiler_params=_params(("parallel",)),
    )(gparts, w, m, v)


def _adam_rows(parts, ws, ms, vs, name):
    n = parts.shape[0]
    k = len(ws)
    widths = [w.shape[1] for w in ws]

    def body(*refs):
        p_ref, w_refs, m_refs, v_refs = refs[0], refs[1:1 + k], refs[1 + k:1 + 2 * k], refs[1 + 2 * k:1 + 3 * k]
        outs = refs[1 + 3 * k:]
        total = p_ref[0:1, :]
        for j in range(1, n):
            total = total + p_ref[j:j + 1, :]
        off = 0
        for a, width in enumerate(widths):
            g = total[:, off:off + width]
            off += width
            outs[4 * a][...] = g
            outs[4 * a + 1][...], outs[4 * a + 2][...], outs[4 * a + 3][...] = _adam_math(
                g, w_refs[a][...], m_refs[a][...], v_refs[a][...])

    return pl.pallas_call(
        body, name=name,
        out_shape=[jax.ShapeDtypeStruct((1, width), F32) for width in widths for _ in range(4)],
        compiler_params=_params(),
    )(parts, *ws, *ms, *vs)


def _mm(a, b, name, out_dtype, tm, tn, n_out=1):
    m, k = a.shape
    n = b.shape[1] // n_out
    nt = n // tn

    def body(a_ref, *refs):
        a_blk = a_ref[...].astype(BF16)
        for b_ref, o_ref in zip(refs[:n_out], refs[n_out:]):
            o_ref[...] = _dot(a_blk, b_ref[...].astype(BF16)).astype(out_dtype)

    outs = pl.pallas_call(
        body,
        name=name,
        grid=(nt, m // tm),
        out_shape=[jax.ShapeDtypeStruct((m, n), out_dtype)] * n_out,
        in_specs=[pl.BlockSpec((tm, k), lambda j, i: (i, 0))]
        + [pl.BlockSpec((k, tn), lambda j, i, g=g: (0, g * nt + j)) for g in range(n_out)],
        out_specs=[pl.BlockSpec((tm, tn), lambda j, i: (i, j))] * n_out,
        compiler_params=_params(("parallel", "parallel")),
    )(a, *([b] * n_out))
    return outs[0] if n_out == 1 else outs


def _mm_nt(a_list, b, name, out_dtype, tm, tn, comm=None):
    n_a = len(a_list)
    m, k = a_list[0].shape
    n = b.shape[0]

    def body(*refs):
        o_ref = refs[2 * n_a]
        acc = _dot_nt(refs[0][...].astype(BF16), refs[n_a][...].astype(BF16))
        for g in range(1, n_a):
            acc = acc + _dot_nt(refs[g][...].astype(BF16), refs[n_a + g][...].astype(BF16))
        o_ref[...] = acc.astype(out_dtype)

    (out,), got = _hosted_call(
        body, comm, name=name, grid=(n // tn, m // tm),
        out_shape=[jax.ShapeDtypeStruct((m, n), out_dtype)],
        in_specs=[pl.BlockSpec((tm, k), lambda j, i: (i, 0))] * n_a
        + [pl.BlockSpec((tn, k), lambda j, i, g=g: (j, g)) for g in range(n_a)],
        out_specs=[pl.BlockSpec((tm, tn), lambda j, i: (i, j))],
        scratch_shapes=[], dims=("parallel", "parallel"), args=(*a_list, *([b] * n_a)))
    return out, got


def _mm_tn(a, b_list, name, tm, tn, tk):
    s, m = a.shape
    n = b_list[0].shape[1]
    n_b = len(b_list)

    def body(a_ref, *refs):
        b_refs, o_refs = refs[:n_b], refs[n_b:]

        @pl.when(pl.program_id(2) == 0)
        def _():
            for o_ref in o_refs:
                o_ref[...] = jnp.zeros_like(o_ref)

        a_blk = a_ref[...].astype(BF16)
        for b_ref, o_ref in zip(b_refs, o_refs):
            o_ref[...] += _dot_tn(a_blk, b_ref[...].astype(BF16))

    return pl.pallas_call(
        body,
        name=name,
        grid=(n // tn, m // tm, s // tk),
        out_shape=[jax.ShapeDtypeStruct((m, n), F32)] * n_b,
        in_specs=[pl.BlockSpec((tk, tm), lambda j, i, k: (k, i))] + [pl.BlockSpec((tk, tn), lambda j, i, k: (k, j))] * n_b,
        out_specs=[pl.BlockSpec((tm, tn), lambda j, i, k: (i, j))] * n_b,
        compiler_params=_params(("parallel", "parallel", "arbitrary")),
    )(a, *b_list)


def _rows_call(fn, name, rows, params, out_rows, out_accs, ts, comm=None):
    s = rows[0].shape[0]
    nr, npar, no = len(rows), len(params), len(out_rows)

    def body(*refs):
        r, p = refs[:nr], refs[nr:nr + npar]
        o, acc = refs[nr + npar:nr + npar + no], refs[nr + npar + no:]
        outs, sums = fn(*[t[...] for t in r], *[t[...] for t in p])
        for ref, val in zip(o, outs):
            ref[...] = val.astype(ref.dtype)
        if acc:
            @pl.when(pl.program_id(0) == 0)
            def _():
                for ref in acc:
                    ref[...] = jnp.zeros_like(ref)

            for ref, val in zip(acc, sums):
                ref[...] += val

    outs, got = _hosted_call(
        body, comm, name=name, grid=(s // ts,),
        out_shape=[jax.ShapeDtypeStruct((s, w), dt) for w, dt in out_rows]
        + [jax.ShapeDtypeStruct(shape, F32) for shape in out_accs],
        in_specs=[pl.BlockSpec((ts, t.shape[1]), lambda i: (i, 0)) for t in rows]
        + [pl.BlockSpec(t.shape, lambda i: (0, 0)) for t in params],
        out_specs=[pl.BlockSpec((ts, w), lambda i: (i, 0)) for w, _ in out_rows]
        + [pl.BlockSpec(shape, lambda i: (0, 0)) for shape in out_accs],
        scratch_shapes=[], dims=("arbitrary",), args=(*rows, *params))
    return outs if comm is None else (outs, got)


def _fwd_norm1(x, g, scale, shift):
    return (_norm_mod(x, g, scale, shift),), ()


def _resid_norm(x, mixed, gate, g, scale, shift):
    x1 = x + (1.0 + gate) * mixed
    return x1, _norm_mod(x1, g, scale, shift)


def _fwd_resid_norm(x, mixed, gate, g, scale, shift):
    return _resid_norm(x, mixed, gate, g, scale, shift), ()


def _final(x1, ffn, tgt, gate, g):
    def head(x1, ffn, gate, g):
        return _rms(x1 + (1.0 + gate) * ffn, g)

    y, vjp = jax.vjp(head, x1, ffn, gate, g)
    err = y - tgt
    dx2, dffn, dgate, dg = vjp(err * (1.0 / D))
    sq = jnp.sum(jnp.sum(err * err, axis=1, keepdims=True), axis=0, keepdims=True)
    loss = jnp.broadcast_to(sq * (0.5 / D), (1, 128))
    return (dx2, dffn), (dgate, dg, loss)


def _bwd_resid_norm(x, mixed, dx2, dh2, gate, g, scale, shift):
    _, vjp = jax.vjp(_resid_norm, x, mixed, gate, g, scale, shift)
    dx, dmixed, dgate, dg, dscale, dshift = vjp((dx2, dh2))
    return (dx, dmixed), (dgate, dg, dscale, dshift)


def _bwd_norm1(x, dh, dx1, g, scale, shift):
    _, vjp = jax.vjp(_norm_mod, x, g, scale, shift)
    dx, dg, dscale, dshift = vjp(dh)
    return (dx1 + dx,), (dg, dscale, dshift)


def _ada_fwd(c16, w):
    def body(c_ref, w_ref, o_ref):
        c = c_ref[...]
        o_ref[...] = _dot((c * _sigmoid(c)).astype(BF16), w_ref[...].astype(BF16))

    return pl.pallas_call(
        body, name="ada_fwd", out_shape=jax.ShapeDtypeStruct((c16.shape[0], w.shape[1]), F32),
        compiler_params=_params(),
    )(c16, w)


def _ada_bwd(c16, dmod16):
    def body(c_ref, d_ref, o_ref):
        c = c_ref[...]
        o_ref[...] = _dot_tn((c * _sigmoid(c)).astype(BF16), d_ref[...].astype(BF16))

    return pl.pallas_call(
        body, name="ada_bwd", out_shape=jax.ShapeDtypeStruct((c16.shape[1], dmod16.shape[1]), F32),
        compiler_params=_params(),
    )(c16, dmod16)


CONV_R = 256


def _conv_window(win, wc, bc):
    s1 = pltpu.roll(win, 1, 0)
    s2 = pltpu.roll(win, 2, 0)
    u = bc + wc[0:1] * s2 + wc[1:2] * s1 + wc[2:3] * win
    return u[8:], s1[8:], s2[8:], win[8:]


def _row_windows(ref, before, after, n_after, lanes):
    ts, r = ref.shape[0], CONV_R

    def middle(i):
        return ref[pl.ds(pl.multiple_of(i * r, r) - 8, r + 8 + n_after), lanes]

    first = jnp.concatenate([before, ref[0:r + n_after, lanes]], axis=0)
    last = ref[ts - r - 8:ts, lanes] if n_after == 0 else jnp.concatenate([ref[ts - r - 8:ts, lanes], after], axis=0)
    return first, middle, last


def _ffn_specs(ts, s):
    cur = pl.BlockSpec((ts, FF_T), lambda j, i: (i, j))
    halo = pl.BlockSpec((8, FF_T), lambda j, i: (jnp.maximum(i * (ts // 8) - 1, 0), j))
    nxt = pl.BlockSpec((8, FF_T), lambda j, i: (jnp.minimum((i + 1) * (ts // 8), s // 8 - 1), j))
    wc = [pl.BlockSpec((8, FF_T), lambda j, i, h=h: (0, h * N_FT + j)) for h in range(2)]
    bc = [pl.BlockSpec((1, FF_T), lambda j, i, h=h: (0, h * N_FT + j)) for h in range(2)]
    return cur, halo, nxt, wc, bc


def _conv_glu_fwd(up_v, up_g, wc, bc, ts):
    s = up_v.shape[0]
    cur, halo, _, wcs, bcs = _ffn_specs(ts, s)

    def body(v_ref, vh_ref, g_ref, gh_ref, wcv_ref, wcg_ref, bcv_ref, bcg_ref, a_ref):
        keep = jnp.where(pl.program_id(1) == 0, 0.0, 1.0)
        for lanes in (slice(0, 128), slice(128, 256)):
            wcv, wcg, bcv, bcg = wcv_ref[:, lanes], wcg_ref[:, lanes], bcv_ref[:, lanes], bcg_ref[:, lanes]
            first_v, mid_v, _ = _row_windows(v_ref, vh_ref[:, lanes] * keep, None, 0, lanes)
            first_g, mid_g, _ = _row_windows(g_ref, gh_ref[:, lanes] * keep, None, 0, lanes)

            def emit(win_v, win_g, start, lanes=lanes, wcv=wcv, wcg=wcg, bcv=bcv, bcg=bcg):
                val = _conv_window(win_v, wcv, bcv)[0]
                gte = _conv_window(win_g, wcg, bcg)[0]
                a_ref[pl.ds(start, CONV_R), lanes] = (val * (gte * _sigmoid(gte))).astype(BF16)

            emit(first_v, first_g, 0)

            def loop(i, carry, emit=emit, mid_v=mid_v, mid_g=mid_g):
                emit(mid_v(i), mid_g(i), pl.multiple_of(i * CONV_R, CONV_R))
                return carry

            lax.fori_loop(1, ts // CONV_R, loop, 0)

    return pl.pallas_call(
        body, name="conv_glu_fwd", grid=(N_FT, s // ts),
        out_shape=jax.ShapeDtypeStruct((s, D_FF), BF16),
        in_specs=[cur, halo, cur, halo, *wcs, *bcs], out_specs=cur,
        compiler_params=_params(("parallel", "arbitrary")),
    )(up_v, up_v, up_g, up_g, wc, wc, bc, bc)


def _glu_bwd(val, gte, da):
    sg = _sigmoid(gte)
    return da * (gte * sg), da * val * (sg * (1.0 + gte * (1.0 - sg)))


def _conv_glu_bwd(up_v, up_g, da, wc, bc, ts, comm=None):
    s = up_v.shape[0]
    nblk = s // ts
    cur, halo, nxt, wcs, bcs = _ffn_specs(ts, s)
    acc_w = pl.BlockSpec((8, FF_T), lambda j, i: (0, j))
    acc_b = pl.BlockSpec((1, FF_T), lambda j, i: (0, j))

    def body(v_ref, vh_ref, vn_ref, g_ref, gh_ref, gn_ref, da_ref, dan_ref, wcv_ref, wcg_ref, bcv_ref, bcg_ref,
             dupv_ref, dupg_ref, dwcv_ref, dwcg_ref, dbcv_ref, dbcg_ref):
        first = pl.program_id(1) == 0
        keep = jnp.where(first, 0.0, 1.0)
        keep_next = jnp.where(pl.program_id(1) == nblk - 1, 0.0, 1.0)
        r = CONV_R
        n = ts // r

        @pl.when(first)
        def _():
            for ref in (dwcv_ref, dwcg_ref, dbcv_ref, dbcg_ref):
                ref[...] = jnp.zeros_like(ref)

        for lanes in (slice(0, 128), slice(128, 256)):
            wcv, wcg, bcv, bcg = wcv_ref[:, lanes], wcg_ref[:, lanes], bcv_ref[:, lanes], bcg_ref[:, lanes]
            first_v, mid_v, last_v = _row_windows(v_ref, vh_ref[:, lanes] * keep, vn_ref[:, lanes], 8, lanes)
            first_g, mid_g, last_g = _row_windows(g_ref, gh_ref[:, lanes] * keep, gn_ref[:, lanes], 8, lanes)

            def emit(win_v, win_g, da_w, start, lanes=lanes, wcv=wcv, wcg=wcg, bcv=bcv, bcg=bcg):
                u_v, s1_v, s2_v, x_v = _conv_window(win_v, wcv, bcv)
                u_g, s1_g, s2_g, x_g = _conv_window(win_g, wcg, bcg)
                du_v, du_g = _glu_bwd(u_v, u_g, da_w)
                rows = pl.ds(start, r)
                for du, s1, s2, x, wc, dup_ref, dwc_ref, dbc_ref in (
                        (du_v, s1_v, s2_v, x_v, wcv, dupv_ref, dwcv_ref, dbcv_ref),
                        (du_g, s1_g, s2_g, x_g, wcg, dupg_ref, dwcg_ref, dbcg_ref)):
                    dup = wc[2:3] * du + wc[1:2] * pltpu.roll(du, r + 7, 0) + wc[0:1] * pltpu.roll(du, r + 6, 0)
                    dup_ref[rows, lanes] = dup[:r].astype(BF16)
                    du = du[:r]
                    dwc_ref[0:1, lanes] += jnp.sum(du * s2[:r], axis=0, keepdims=True)
                    dwc_ref[1:2, lanes] += jnp.sum(du * s1[:r], axis=0, keepdims=True)
                    dwc_ref[2:3, lanes] += jnp.sum(du * x[:r], axis=0, keepdims=True)
                    dbc_ref[:, lanes] += jnp.sum(du, axis=0, keepdims=True)

            emit(first_v, first_g, da_ref[0:r + 8, lanes], 0)

            def loop(i, carry, emit=emit, mid_v=mid_v, mid_g=mid_g, lanes=lanes):
                start = pl.multiple_of(i * r, r)
                emit(mid_v(i), mid_g(i), da_ref[pl.ds(start, r + 8), lanes], start)
                return carry

            lax.fori_loop(1, n - 1, loop, 0)
            da_last = jnp.concatenate([da_ref[ts - r:ts, lanes], dan_ref[:, lanes] * keep_next], axis=0)
            emit(last_v, last_g, da_last, ts - r)

    return _hosted_call(
        body, comm, name="conv_glu_bwd", grid=(N_FT, nblk),
        out_shape=[jax.ShapeDtypeStruct((s, D_FF), BF16)] * 2 + [jax.ShapeDtypeStruct((8, D_FF), F32)] * 2
        + [jax.ShapeDtypeStruct((1, D_FF), F32)] * 2,
        in_specs=[cur, halo, nxt, cur, halo, nxt, cur, nxt, *wcs, *bcs],
        out_specs=[cur, cur, acc_w, acc_w, acc_b, acc_b],
        scratch_shapes=[], dims=("parallel", "arbitrary"),
        args=(up_v, up_v, up_v, up_g, up_g, up_g, da, da, wc, wc, bc, bc))


def _tri(kind):
    b = SB_B
    m = {"lower_strict": np.tril(np.ones((b, b)), -1), "upper_incl": np.triu(np.ones((b, b)), 0),
         "upper_strict": np.triu(np.ones((b, b)), 1)}[kind]
    return jnp.asarray(np.concatenate([m, np.ones((b, 128))], axis=1), BF16)


def _key_sums(x, tri):
    hi, lo = _hilo(x)
    cb = _dot(hi, tri) + _dot(lo, tri)
    return cb[:, :SB_B], cb[:, SB_B:]


def _sb_fwd(proj, comm=None):
    s = proj.shape[0]
    b = SB_B

    def body(q_ref, k_ref, v_ref, tri_ref, o_ref, t_ref, first_ref, c_ref, a_ref):
        i = pl.program_id(1)
        lane = lax.broadcasted_iota(jnp.int32, (b, 128), 1)
        heads = (lane < 64, lane >= 64)
        causal = lax.broadcasted_iota(jnp.int32, (b, b), 1) < lax.broadcasted_iota(jnp.int32, (b, b), 0)
        q = q_ref[...] * 0.125
        qm = [jnp.where(h, q, 0.0).astype(BF16) for h in heads]
        c_ref[...] = jnp.zeros_like(c_ref)
        a_ref[...] = jnp.zeros_like(a_ref)

        def step(jj, masked):
            rows = pl.ds(pl.multiple_of(jj * b, b), b)
            kb = k_ref[rows, :].astype(BF16)
            vb = v_ref[rows, :]
            for hh in range(2):
                z = _dot_nt(qm[hh], kb)
                lg = _log_sigmoid(-z)
                if masked:
                    lg = jnp.where(causal, lg, 0.0)
                after, total = _key_sums(lg, tri_ref[...])
                c = c_ref[hh]
                w = jnp.exp(lg + z + after + jnp.concatenate([c, c], axis=1))
                if masked:
                    w = jnp.where(causal, w, 0.0)
                a_ref[hh] += _dot(w.astype(BF16), jnp.where(heads[hh], vb, 0.0).astype(BF16))
                c_ref[hh] = c + total

        def largest_sum():
            return jnp.max(jnp.maximum(c_ref[0], c_ref[1]))

        step(i, True)

        def more(state):
            jj, top = state
            return jnp.logical_and(jj >= 0, top > SB_DEAD)

        def walk(state):
            jj, _ = state
            step(jj, False)
            return jj - 1, largest_sum()

        jj, _ = lax.while_loop(more, walk, (i - 1, largest_sum()))
        o_ref[...] = a_ref[0] + a_ref[1]
        t_ref[...] = jnp.concatenate([c_ref[0], c_ref[1]], axis=1)
        first_ref[pl.program_id(0), i] = (jj + 1).astype(F32)

    return _hosted_call(
        body, comm, name="sb_fwd", grid=(4, s // b),
        out_shape=[jax.ShapeDtypeStruct((s, SB_W), F32), jax.ShapeDtypeStruct((s, 2 * SB_W), F32),
                   jax.ShapeDtypeStruct((4, s // b), F32)],
        in_specs=[pl.BlockSpec((b, 128), lambda p, i: (i, p)),
                  pl.BlockSpec((s, 128), lambda p, i: (0, 4 + p)),
                  pl.BlockSpec((s, 128), lambda p, i: (0, 8 + p)),
                  pl.BlockSpec((b, b + 128), lambda p, i: (0, 0))],
        out_specs=[pl.BlockSpec((b, 128), lambda p, i: (i, p)), pl.BlockSpec((b, 256), lambda p, i: (i, p)),
                   pl.BlockSpec(memory_space=pltpu.SMEM)],
        scratch_shapes=[pltpu.VMEM((2, b, 128), F32), pltpu.VMEM((2, b, 128), F32)],
        dims=("arbitrary", "arbitrary"), args=(proj, proj, proj, _tri("lower_strict")))


def _sb_bwd(proj, do, tot, first, comm=None):
    s = proj.shape[0]
    b = SB_B

    def body(q_ref, k_ref, v_ref, do_ref, t_ref, first_ref, ti_ref, ts_ref, dq_ref, dk_ref, dv_ref, cl_ref, ce_ref,
             a_ref):
        i = pl.program_id(1)
        first = jnp.clip(first_ref[pl.program_id(0), i].astype(jnp.int32), 0, i)
        lane = lax.broadcasted_iota(jnp.int32, (b, 128), 1)
        heads = (lane < 64, lane >= 64)
        causal = lax.broadcasted_iota(jnp.int32, (b, b), 1) < lax.broadcasted_iota(jnp.int32, (b, b), 0)
        q = q_ref[...] * 0.125
        do = do_ref[...]
        qm = [jnp.where(h, q, 0.0).astype(BF16) for h in heads]
        dom = [jnp.where(h, do, 0.0).astype(BF16) for h in heads]
        cl_ref[...] = jnp.zeros_like(cl_ref)
        ce_ref[...] = jnp.zeros_like(ce_ref)
        a_ref[...] = jnp.zeros_like(a_ref)

        @pl.when(i == 0)
        def _():
            dk_ref[...] = jnp.zeros_like(dk_ref)
            dv_ref[...] = jnp.zeros_like(dv_ref)

        def step(jj, masked):
            rows = pl.ds(pl.multiple_of(jj * b, b), b)
            kf = k_ref[rows, :]
            kb = kf.astype(BF16)
            vb = v_ref[rows, :].astype(BF16)
            dk = jnp.zeros((b, 128), F32)
            dv = jnp.zeros((b, 128), F32)
            for hh in range(2):
                z = _dot_nt(qm[hh], kb)
                lg = _log_sigmoid(-z)
                if masked:
                    lg = jnp.where(causal, lg, 0.0)
                upto, total = _key_sums(lg, ti_ref[...])
                cl = cl_ref[hh]
                t = t_ref[:, hh * 128:(hh + 1) * 128]
                rest = jnp.concatenate([t - cl, t - cl], axis=1) - upto
                lsz = lg + z
                w = jnp.exp(lsz + rest)
                sig = jnp.exp(lsz)
                if masked:
                    w = jnp.where(causal, w, 0.0)
                e = w * _dot_nt(dom[hh], vb)
                before, etot = _key_sums(e, ts_ref[...])
                ce = ce_ref[hh]
                dz = e - sig * (e + before + jnp.concatenate([ce, ce], axis=1))
                if masked:
                    dz = jnp.where(causal, dz, 0.0)
                dzb = dz.astype(BF16)
                a_ref[hh] += _dot(dzb, jnp.where(heads[hh], kf, 0.0).astype(BF16))
                dk = dk + _dot_tn(dzb, qm[hh])
                dv = dv + _dot_tn(w.astype(BF16), dom[hh])
                cl_ref[hh] = cl + total
                ce_ref[hh] = ce + etot
            dk_ref[rows, :] += dk
            dv_ref[rows, :] += dv

        def loop(jj, carry):
            step(jj, False)
            return carry

        lax.fori_loop(first, i, loop, 0)
        step(i, True)
        dq_ref[...] = (a_ref[0] + a_ref[1]) * 0.125

    blk = pl.BlockSpec((b, 128), lambda p, i: (i, p))
    full = pl.BlockSpec((s, 128), lambda p, i: (0, p))
    tri = pl.BlockSpec((b, b + 128), lambda p, i: (0, 0))
    return _hosted_call(
        body, comm, name="sb_bwd", grid=(4, s // b),
        out_shape=[jax.ShapeDtypeStruct((s, SB_W), F32)] * 3,
        in_specs=[blk, pl.BlockSpec((s, 128), lambda p, i: (0, 4 + p)), pl.BlockSpec((s, 128), lambda p, i: (0, 8 + p)),
                  blk, pl.BlockSpec((b, 256), lambda p, i: (i, p)), pl.BlockSpec(memory_space=pltpu.SMEM), tri, tri],
        out_specs=[blk, full, full],
        scratch_shapes=[pltpu.VMEM((2, b, 128), F32)] * 3,
        dims=("arbitrary", "arbitrary"),
        args=(proj, proj, proj, do, tot, first, _tri("upper_incl"), _tri("upper_strict")))


_COL_Q, _COL_K, _COL_V, _COL_G, _COL_F = 12, 14, 8, 10, 24


def _gla_consts():
    c = CHUNK
    incl = np.tril(np.ones((c, c)), 0)
    strict = np.tril(np.ones((c, c)), -1)
    bd = np.zeros((128, 256))
    bd[:64, :128] = 1.0
    bd[64:, 128:] = 1.0
    return jnp.asarray(incl, BF16), jnp.asarray(strict, BF16), jnp.asarray(bd, F32)


def _time_sums(tri, x):
    hi, lo = _hilo(x)
    return _dot(tri, hi) + _dot(tri, lo)


def _gla_gate(o, gg, g):
    parts = []
    for h in range(2):
        oh = o[:, h * 128:(h + 1) * 128]
        parts.append(oh * lax.rsqrt(jnp.mean(oh * oh, axis=-1, keepdims=True) + EPS))
    return (jnp.concatenate(parts, axis=1) * g) * (gg * _sigmoid(gg))


def _gla_chunk(la_c, k_c, incl, ones_cv):
    cum = _time_sums(incl, la_c)
    total = cum[CHUNK - 1:CHUNK]
    edec = jnp.exp(total - cum)
    kdec = k_c * edec
    hi, lo = _hilo(la_c)
    dec = jnp.exp(_dot_tn(hi, ones_cv) + _dot_tn(lo, ones_cv))
    return edec, kdec, dec


def _gla_fwd(proj, wfg, bfg, gain):
    s = proj.shape[0]
    t = GLA_T
    nch = t // CHUNK
    incl, _, bd = _gla_consts()

    def body(q_ref, k_ref, v_ref, gg_ref, f_ref, wf_ref, bf_ref, g_ref, incl_ref, bd_ref, o_ref, st_ref, state_ref):
        @pl.when(pl.program_id(0) == 0)
        def _():
            state_ref[...] = jnp.zeros_like(state_ref)

        la = _log_sigmoid(_dot(f_ref[...].astype(BF16), wf_ref[...].astype(BF16)) + bf_ref[...]) * (1.0 / 16.0)
        ones_cv = jnp.ones((CHUNK, 256), BF16)
        states = [state_ref[0], state_ref[1]]
        for cc in range(nch):
            rows = slice(cc * CHUNK, (cc + 1) * CHUNK)
            for p in range(2):
                kl, vl = slice(p * 128, (p + 1) * 128), slice(p * 256, (p + 1) * 256)
                _, kdec, dec = _gla_chunk(la[rows, kl], k_ref[rows, kl], incl_ref[...], ones_cv)
                kv = _dot_tn(kdec.astype(BF16), v_ref[rows, vl].astype(BF16))
                states[p] = dec * states[p] + bd_ref[...] * kv
                st_ref[p, cc] = states[p]
                o = _dot((q_ref[rows, kl] * 0.125).astype(BF16), states[p].astype(BF16))
                o_ref[rows, vl] = _gla_gate(o, gg_ref[rows, vl], g_ref[:, vl])
        state_ref[0] = states[0]
        state_ref[1] = states[1]

    def col(width, blk):
        return pl.BlockSpec((t, width), lambda i: (i, blk))

    def whole(shape):
        return pl.BlockSpec(shape, lambda i: tuple(0 for _ in shape))

    return pl.pallas_call(
        body, name="gla_fwd", grid=(s // t,),
        out_shape=[jax.ShapeDtypeStruct((s, GLA_VW), F32), jax.ShapeDtypeStruct((2, s // CHUNK, 128, 256), F32)],
        in_specs=[col(256, _COL_Q // 2), col(256, _COL_K // 2), col(512, _COL_V // 2), col(512, _COL_G // 2),
                  col(128, _COL_F), whole((128, GLA_KW)), whole((1, GLA_KW)), whole((1, GLA_VW)),
                  whole((CHUNK, CHUNK)), whole((128, 256))],
        out_specs=[col(512, 0), pl.BlockSpec((2, nch, 128, 256), lambda i: (0, i, 0, 0))],
        scratch_shapes=[pltpu.VMEM((2, 128, 256), F32)],
        compiler_params=_params(("arbitrary",)),
    )(proj, proj, proj, proj, proj, wfg, bfg, gain, incl, bd)


def _gla_bwd(proj, do, states, wfg, bfg, gain):
    s = proj.shape[0]
    t = GLA_T
    nch = t // CHUNK
    nblk = s // t
    incl, strict, bd = _gla_consts()

    def body(q_ref, k_ref, v_ref, gg_ref, f_ref, do_ref, st_ref, sp_ref, wf_ref, bf_ref, g_ref, incl_ref, str_ref,
             bd_ref, dq_ref, dk_ref, dv_ref, dgg_ref, df_ref, dwf_ref, dbf_ref, dg_ref, carry_ref, dfs_ref):
        i = pl.program_id(0)

        @pl.when(i == 0)
        def _():
            carry_ref[...] = jnp.zeros_like(carry_ref)
            dwf_ref[...] = jnp.zeros_like(dwf_ref)
            dbf_ref[...] = jnp.zeros_like(dbf_ref)
            dg_ref[...] = jnp.zeros_like(dg_ref)

        fb = f_ref[...].astype(BF16)
        wf = wf_ref[...].astype(BF16)
        f = _dot(fb, wf) + bf_ref[...]
        la = _log_sigmoid(f) * (1.0 / 16.0)
        dla_df = _sigmoid(-f) * (1.0 / 16.0)
        ones_cv = jnp.ones((CHUNK, 256), BF16)
        ones_8v = jnp.ones((8, 256), BF16)
        first_block = jnp.where(i == nblk - 1, 0.0, 1.0)
        carries = [carry_ref[0], carry_ref[1]]
        dgains = [jnp.zeros((1, 256), F32), jnp.zeros((1, 256), F32)]
        for cc in reversed(range(nch)):
            rows = slice(cc * CHUNK, (cc + 1) * CHUNK)
            for p in range(2):
                kl, vl = slice(p * 128, (p + 1) * 128), slice(p * 256, (p + 1) * 256)
                edec, kdec, dec = _gla_chunk(la[rows, kl], k_ref[rows, kl], incl_ref[...], ones_cv)
                state = st_ref[p, cc]
                prev = st_ref[p, cc - 1] if cc > 0 else sp_ref[p, 0] * first_block
                qs = (q_ref[rows, kl] * 0.125).astype(BF16)
                sb16 = state.astype(BF16)
                o = _dot(qs, sb16)
                _, vjp = jax.vjp(_gla_gate, o, gg_ref[rows, vl], g_ref[:, vl])
                do_c, dgg_c, dg_c = vjp(do_ref[rows, vl])
                dgains[p] = dgains[p] + dg_c
                dgg_ref[rows, vl] = dgg_c
                do16 = do_c.astype(BF16)
                dq_ref[rows, kl] = _dot_nt(do16, sb16) * 0.125
                grad = bd_ref[...] * _dot_tn(qs, do16) + carries[p]
                g16 = grad.astype(BF16)
                dv_ref[rows, vl] = _dot(kdec.astype(BF16), g16)
                dkdec = _dot_nt(v_ref[rows, vl].astype(BF16), g16)
                hi, lo = _hilo(grad * prev * dec)
                ddec = (_dot_nt(ones_8v, hi) + _dot_nt(ones_8v, lo))[0:1]
                dk_ref[rows, kl] = dkdec * edec
                dla = _time_sums(str_ref[...], dkdec * kdec) + ddec
                dfs_ref[rows, kl] = dla * dla_df[rows, kl]
                carries[p] = dec * grad
        carry_ref[0] = carries[0]
        carry_ref[1] = carries[1]
        df = dfs_ref[...]
        df16 = df.astype(BF16)
        df_ref[...] = _dot_nt(df16, wf)
        dwf_ref[...] += _dot_tn(fb, df16)
        dbf_ref[...] += jnp.sum(df, axis=0, keepdims=True)
        dg_ref[...] += jnp.concatenate(dgains, axis=1)

    def col(width, blk):
        return pl.BlockSpec((t, width), lambda i: (nblk - 1 - i, blk))

    def whole(shape):
        return pl.BlockSpec(shape, lambda i: tuple(0 for _ in shape))

    return pl.pallas_call(
        body, name="gla_bwd", grid=(nblk,),
        out_shape=[jax.ShapeDtypeStruct((s, GLA_KW), F32), jax.ShapeDtypeStruct((s, GLA_KW), F32),
                   jax.ShapeDtypeStruct((s, GLA_VW), F32), jax.ShapeDtypeStruct((s, GLA_VW), F32),
                   jax.ShapeDtypeStruct((s, 128), F32), jax.ShapeDtypeStruct((128, GLA_KW), F32),
                   jax.ShapeDtypeStruct((1, GLA_KW), F32), jax.ShapeDtypeStruct((1, GLA_VW), F32)],
        in_specs=[col(256, _COL_Q // 2), col(256, _COL_K // 2), col(512, _COL_V // 2), col(512, _COL_G // 2),
                  col(128, _COL_F), col(512, 1),
                  pl.BlockSpec((2, nch, 128, 256), lambda i: (0, nblk - 1 - i, 0, 0)),
                  pl.BlockSpec((2, 1, 128, 256), lambda i: (0, jnp.maximum((nblk - 1 - i) * nch - 1, 0), 0, 0)),
                  whole((128, GLA_KW)), whole((1, GLA_KW)), whole((1, GLA_VW)),
                  whole((CHUNK, CHUNK)), whole((CHUNK, CHUNK)), whole((128, 256))],
        out_specs=[col(256, 0), col(256, 0), col(512, 0), col(512, 0), col(128, 0),
                   whole((128, GLA_KW)), whole((1, GLA_KW)), whole((1, GLA_VW))],
        scratch_shapes=[pltpu.VMEM((2, 128, 256), F32), pltpu.VMEM((t, GLA_KW), F32)],
        compiler_params=_params(("arbitrary",)),
    )(proj, proj, proj, proj, proj, do, states, states, wfg, bfg, gain, incl, strict, bd)


def _cols_by_dev(a, n_dev=N_DEV):
    r, c = a.shape
    return a.reshape(r, n_dev, c // n_dev).transpose(1, 0, 2)


def _cols_from_dev(a):
    _, r, n = a.shape
    return a.transpose(1, 0, 2).reshape(r, N_DEV * n)


def kernel(x, c, w_ada, b_ada, g_norm1, w_in, w_fg2, b_fg2, g_gla_out, w_out, g_norm2, w_up, w_conv, b_conv, w_down, g_final, loss_target, m_w_ada, m_b_ada, m_g_norm1, m_w_in, m_w_fg2, m_b_fg2, m_g_gla_out, m_w_out, m_g_norm2, m_w_up, m_w_conv, m_b_conv, m_w_down, m_g_final, v_w_ada, v_b_ada, v_g_norm1, v_w_in, v_w_fg2, v_b_fg2, v_g_gla_out, v_w_out, v_g_norm2, v_w_up, v_w_conv, v_b_conv, v_w_down, v_g_final):
    s = x.shape[1]
    me = 4 * lax.axis_index("x") + 2 * lax.axis_index("y") + lax.axis_index("c")
    xs, tgt = x[0], loss_target[0]
    ts = min(512, s)
    tm = min(1024, s)

    c_all, wconv_g, wfg_g = _exchange([c, w_conv[0], w_fg2[0]], "gather_first", True)
    wconv_p = jnp.pad(_cols_from_dev(wconv_g), ((0, 5), (0, 0)))
    wfg_p = jnp.pad(_cols_from_dev(wfg_g), ((0, 128 - RANK), (0, 0)))

    c16 = jnp.pad(c_all.reshape(N_DEV, D), ((0, 8), (0, 0)))
    modp = _ada_fwd(c16, w_ada[0])[:N_DEV]
    (mod_all,) = _exchange([modp], "gather_mod", True)
    mod = lax.dynamic_index_in_dim(mod_all, me, axis=1, keepdims=False).reshape(1, 6 * D) + b_ada
    shift1, scale1, gate1, shift2, scale2, gate2 = [mod[:, k * D:(k + 1) * D] for k in range(6)]

    (h,), (win_g,) = _rows_call(_fwd_norm1, "norm1", [xs], [g_norm1, scale1, shift1], [(D, BF16)], [], ts,
                                comm=([w_in[0].astype(BF16)], True))
    win = jnp.pad(_cols_from_dev(win_g), ((0, 0), (0, IN_WP - IN_W)))
    proj = _mm(h, win, "in_proj", F32, tm, 640)
    (o_sb, sb_tot, sb_first), (wout_g, wup_g, wdown_g) = _sb_fwd(
        proj, comm=([w_out[0].astype(BF16), w_up[0].astype(BF16), w_down[0].astype(BF16)], True))
    wout = wout_g.reshape(D, D)
    wup = _cols_from_dev(wup_g)
    wdown = wdown_g.reshape(D_FF, D)
    o_gla, states = _gla_fwd(proj, wfg_p, b_fg2, g_gla_out)
    cat = jnp.concatenate([o_sb, o_gla], axis=1).astype(BF16)
    mixed = _mm(cat, wout, "out_proj", F32, tm, 512)
    x1, h2 = _rows_call(_fwd_resid_norm, "resid_norm2", [xs, mixed], [gate1, g_norm2, scale2, shift2],
                        [(D, F32), (D, BF16)], [], ts)
    up_v, up_g = _mm(h2, wup, "up_proj", F32, tm, 1408, n_out=2)
    act = _conv_glu_fwd(up_v, up_g, wconv_p, b_conv, tm)
    ffn = _mm(act, wdown, "down_proj", F32, tm, 512)
    dx2, dffn, dgate2, dg_final, loss_part = _rows_call(
        _final, "final", [x1, ffn, tgt], [gate2, g_final.reshape(1, D)],
        [(D, F32), (D, BF16)], [(1, D), (1, D), (1, 128)], ts)

    (dw_down,) = _mm_tn(act, [dffn], "down_wgrad", 1408, 1024, tm)
    da, _ = _mm_nt([dffn], wdown, "down_dgrad", F32, tm, 1408)
    (dup_v, dup_g, dwc_v, dwc_g, dbc_v, dbc_g), (r_down,) = _conv_glu_bwd(
        up_v, up_g, da, wconv_p, b_conv, tm, comm=([dw_down.reshape(N_DEV, 352, D)], False))
    dw_up_v, dw_up_g = _mm_tn(h2, [dup_v, dup_g], "up_wgrad", 1024, 1408, tm)
    dh2, _ = _mm_nt([dup_v, dup_g], wup, "up_dgrad", F32, tm, 512)
    dx1, dmixed, dgate1, dg_norm2, dscale2, dshift2 = _rows_call(
        _bwd_resid_norm, "resid_norm2_bwd", [xs, mixed, dx2, dh2], [gate1, g_norm2, scale2, shift2],
        [(D, F32), (D, BF16)], [(1, D)] * 4, ts)
    (dw_out,) = _mm_tn(cat, [dmixed], "out_wgrad", 1024, 1024, tm)
    dcat, _ = _mm_nt([dmixed], wout, "out_dgrad", F32, tm, 512)
    dw_up_s = jnp.concatenate([_cols_by_dev(dw_up_v, 4), _cols_by_dev(dw_up_g, 4)], axis=0)
    dwconv_s = _cols_by_dev(jnp.concatenate([dwc_v[:3], dwc_g[:3]], axis=1))
    (dq, dk, dv), (r_up, r_out, r_conv) = _sb_bwd(
        proj, dcat, sb_tot, sb_first, comm=([dw_up_s, dw_out.reshape(N_DEV, 128, D), dwconv_s], False))
    dgq, dgk, dgv, dgg, dgf, dwfg_p, dbfg, dg_gla = _gla_bwd(proj, dcat, states, wfg_p, b_fg2, g_gla_out)
    dproj = jnp.concatenate([dq, dk, dv, dgq, dgk, dgv, dgg, dgf], axis=1).astype(BF16)
    (dw_in,) = _mm_tn(h, [dproj], "in_wgrad", 1024, 640, tm)
    dh, (r_in, r_fg) = _mm_nt([dproj], win, "in_dgrad", F32, tm, 512,
                              comm=([_cols_by_dev(dw_in[:, :IN_W]).astype(BF16), _cols_by_dev(dwfg_p[:RANK])], False))
    grad_x, dg_norm1, dscale1, dshift1 = _rows_call(
        _bwd_norm1, "norm1_bwd", [xs, dh, dx1], [g_norm1, scale1, shift1], [(D, F32)], [(1, D)] * 3, ts)

    dmod = jnp.concatenate([dshift1, dscale1, dgate1, dshift2, dscale2, dgate2], axis=1)
    small_parts = [dmod, dg_norm1, dbfg, dg_gla, dg_norm2, dbc_v, dbc_g, dg_final, loss_part]
    n_small = sum(p.shape[1] for p in small_parts[:-1])
    (sg,) = _exchange([jnp.concatenate(small_parts, axis=1)], "gather_small_grads", True)
    sg = sg.reshape(N_DEV, n_small + 128)
    loss = jnp.sum(sg[:, n_small])
    small_w = [b_ada, g_norm1, b_fg2, g_gla_out, g_norm2, b_conv, g_final.reshape(1, D)]
    small_m = [m_b_ada, m_g_norm1, m_b_fg2, m_g_gla_out, m_g_norm2, m_b_conv, m_g_final.reshape(1, D)]
    small_v = [v_b_ada, v_g_norm1, v_b_fg2, v_g_gla_out, v_g_norm2, v_b_conv, v_g_final.reshape(1, D)]
    s_out = list(_adam_rows(sg, small_w, small_m, small_v, "adam_small"))
    s_out[24:] = [t.reshape(D) for t in s_out[24:]]
    s_g, s_d, s_m, s_v = [s_out[k::4] for k in range(4)]

    dmod_all = sg[:, :6 * D].reshape(N_DEV, N_DEV, 768)
    dmod_mine = lax.dynamic_index_in_dim(dmod_all, me, axis=1, keepdims=False)
    dw_ada = _ada_bwd(c16, jnp.pad(dmod_mine, ((0, 8), (0, 0))))
    a_g, a_d, a_m, a_v = [t[None] for t in _adam(dw_ada[None], w_ada[0], m_w_ada[0], v_w_ada[0], "adam_ada", 256)]

    recv = [r_in, r_fg, r_out, r_up, r_conv, r_down]
    big_w = [w_in, w_fg2, w_out, w_up, w_conv, w_down]
    big_m = [m_w_in, m_w_fg2, m_w_out, m_w_up, m_w_conv, m_w_down]
    big_v = [v_w_in, v_w_fg2, v_w_out, v_w_up, v_w_conv, v_w_down]
    big_name = ["adam_in", "adam_fg2", "adam_out", "adam_up", "adam_conv", "adam_down"]
    big_rows = [256, RANK, 128, 256, 3, 176]
    b_out = [_adam(r, w[0], m[0], v[0], name, tr)
             for r, w, m, v, name, tr in zip(recv, big_w, big_m, big_v, big_name, big_rows)]
    b_g, b_d, b_m, b_v = [[o[k][None] for o in b_out] for k in range(4)]

    def ordered(a, sm, bg):
        return [a, sm[0], sm[1], bg[0], bg[1], sm[2], sm[3], bg[2], sm[4], bg[3], bg[4], sm[5], bg[5], sm[6]]

    return (loss, grad_x[None], *ordered(a_g, s_g, b_g), *ordered(a_d, s_d, b_d),
            *ordered(a_m, s_m, b_m), *ordered(a_v, s_v, b_v))
```

```python
import numpy as np

import jax
import jax.numpy as jnp
from jax import lax
from jax.experimental import pallas as pl
from jax.experimental.pallas import tpu as pltpu

F32, BF16 = jnp.float32, jnp.bfloat16
N_DEV = 8
D = 1024
SB_W = 512
GLA_KW, GLA_VW = 256, 512
RANK = 16
IN_W = 3088
IN_WP = 3200
D_FF = 2816
FF_T = 256
N_FT = D_FF // FF_T
EPS = 1e-6
SB_B = 256
SB_DEAD = -110.0
CHUNK = 64
GLA_T = 512
VMEM_LIMIT = 56 * 1024 * 1024

LR, B1, B2, ADAM_EPS, WD, STEP = 0.001, 0.9, 0.999, 1e-08, 0.01, 10


def _params(dims=None, vmem=True):
    kw = {}
    if dims is not None:
        kw["dimension_semantics"] = dims
    if vmem:
        kw["vmem_limit_bytes"] = VMEM_LIMIT
    return pltpu.CompilerParams(**kw)


def _dot(a, b):
    return jnp.dot(a, b, preferred_element_type=F32)


def _dot_nt(a, b):
    return lax.dot_general(a, b, (((1,), (1,)), ((), ())), preferred_element_type=F32)


def _dot_tn(a, b):
    return lax.dot_general(a, b, (((0,), (0,)), ((), ())), preferred_element_type=F32)


def _hilo(x):
    hi = x.astype(BF16)
    lo = (x - hi.astype(F32)).astype(BF16)
    return hi, lo


def _sigmoid(x):
    return 1.0 / (1.0 + jnp.exp(-x))


def _log_sigmoid(x):
    return jnp.minimum(x, 0.0) - jnp.log(1.0 + jnp.exp(-jnp.abs(x)))


def _rms(x, g):
    n = x * lax.rsqrt(jnp.mean(x * x, axis=-1, keepdims=True) + EPS)
    return n * g


def _norm_mod(x, g, scale, shift):
    return _rms(x, g) * (1.0 + scale) + shift


N_PEER = N_DEV - 1


def _exchange_copies(x_refs, out_refs, send_sems, recv_sems, local_sems, gather):
    ix, iy, ic = lax.axis_index("x"), lax.axis_index("y"), lax.axis_index("c")
    me = 4 * ix + 2 * iy + ic
    peers = []
    for k in range(1, N_DEV):
        px = 1 - ix if k & 4 else ix
        py = 1 - iy if k & 2 else iy
        pc = 1 - ic if k & 1 else ic
        peers.append(((px, py, pc), 4 * px + 2 * py + pc))

    def copy(a, k, dev, src_slot, dst_slot):
        return pltpu.make_async_remote_copy(
            src_ref=x_refs[a] if gather else x_refs[a].at[src_slot],
            dst_ref=out_refs[a].at[dst_slot],
            send_sem=send_sems.at[a * N_PEER + k],
            recv_sem=recv_sems.at[a * N_PEER + k],
            device_id=dev,
            device_id_type=pl.DeviceIdType.MESH,
        )

    n = len(x_refs)
    mine = [pltpu.make_async_copy(x_refs[a] if gather else x_refs[a].at[me], out_refs[a].at[me], local_sems.at[a])
            for a in range(n)]
    sends = [copy(a, k, dev, pid, me) for a in range(n) for k, (dev, pid) in enumerate(peers)]
    recvs = [copy(a, k, dev, pid, pid) for a in range(n) for k, (dev, pid) in enumerate(peers)]
    return mine, sends, recvs


def _exchange_scratch(n):
    return [pltpu.SemaphoreType.DMA((n * N_PEER,)), pltpu.SemaphoreType.DMA((n * N_PEER,)),
            pltpu.SemaphoreType.DMA((n,))]


def _exchange_shapes(arrays, gather):
    return [jax.ShapeDtypeStruct((N_DEV,) + tuple(x.shape if gather else x.shape[1:]), x.dtype) for x in arrays]


def _exchange(arrays, name, gather):
    n = len(arrays)

    def body(*refs):
        mine, sends, recvs = _exchange_copies(refs[:n], refs[n:2 * n], *refs[2 * n:], gather)
        for cp in mine + sends:
            cp.start()
        for cp in recvs:
            cp.wait_recv()
        for cp in sends:
            cp.wait_send()
        for cp in mine:
            cp.wait()

    return pl.pallas_call(
        body,
        name=name,
        out_shape=_exchange_shapes(arrays, gather),
        in_specs=[pl.BlockSpec(memory_space=pl.ANY)] * n,
        out_specs=[pl.BlockSpec(memory_space=pl.ANY)] * n,
        scratch_shapes=_exchange_scratch(n),
    )(*arrays)


def _hosted_call(body, comm, *, name, grid, in_specs, out_specs, out_shape, scratch_shapes, dims, args):
    if comm is None:
        outs = pl.pallas_call(body, name=name, grid=grid, in_specs=in_specs, out_specs=out_specs, out_shape=out_shape,
                              scratch_shapes=scratch_shapes, compiler_params=_params(dims))(*args)
        return outs, []
    arrays, gather = comm
    n_in, n_out, n_scr, nc = len(in_specs), len(out_specs), len(scratch_shapes), len(arrays)

    def hosted(*refs):
        ins, c_in = refs[:n_in], refs[n_in:n_in + nc]
        outs, c_out = refs[n_in + nc:n_in + nc + n_out], refs[n_in + nc + n_out:n_in + 2 * nc + n_out]
        rest = refs[n_in + 2 * nc + n_out:]
        scratch, sems = rest[:n_scr], rest[n_scr:]
        mine, sends, recvs = _exchange_copies(c_in, c_out, *sems, gather)
        first = pl.program_id(0) == 0
        last = pl.program_id(0) == grid[0] - 1
        for axis in range(1, len(grid)):
            first = jnp.logical_and(first, pl.program_id(axis) == 0)
            last = jnp.logical_and(last, pl.program_id(axis) == grid[axis] - 1)

        @pl.when(first)
        def _():
            for cp in mine + sends:
                cp.start()

        body(*ins, *outs, *scratch)

        @pl.when(last)
        def _():
            for cp in recvs:
                cp.wait_recv()
            for cp in sends:
                cp.wait_send()
            for cp in mine:
                cp.wait()

    any_spec = pl.BlockSpec(memory_space=pl.ANY)
    outs = pl.pallas_call(
        hosted, name=name, grid=grid,
        in_specs=list(in_specs) + [any_spec] * nc,
        out_specs=list(out_specs) + [any_spec] * nc,
        out_shape=list(out_shape) + _exchange_shapes(arrays, gather),
        scratch_shapes=list(scratch_shapes) + _exchange_scratch(nc),
        compiler_params=_params(tuple("arbitrary" for _ in grid)),
    )(*args, *arrays)
    return outs[:n_out], outs[n_out:]


def _adam_math(g, w, m, v):
    m_new = B1 * m + (1.0 - B1) * g
    v_new = B2 * v + (1.0 - B2) * (g * g)
    m_hat = m_new / (1.0 - B1 ** STEP)
    v_hat = v_new / (1.0 - B2 ** STEP)
    return -LR * (m_hat / (jnp.sqrt(v_hat) + ADAM_EPS) + WD * w), m_new, v_new


def _adam(gparts, w, m, v, name, tr):
    n, rows, cols = gparts.shape

    def body(gp_ref, w_ref, m_ref, v_ref, g_ref, d_ref, nm_ref, nv_ref):
        g = gp_ref[0].astype(F32)
        for j in range(1, n):
            g = g + gp_ref[j].astype(F32)
        g_ref[...] = g
        d_ref[...], nm_ref[...], nv_ref[...] = _adam_math(g, w_ref[...], m_ref[...], v_ref[...])

    blk = pl.BlockSpec((tr, cols), lambda i: (i, 0))
    return pl.pallas_call(
        body,
        name=name,
        grid=(rows // tr,),
        out_shape=[jax.ShapeDtypeStruct((rows, cols), F32)] * 4,
        in_specs=[pl.BlockSpec((n, tr, cols), lambda i: (0, i, 0)), blk, blk, blk],
        out_specs=[blk] * 4,
        compiler_params=_params(("parallel",)),
    )(gparts, w, m, v)


def _adam_rows(parts, ws, ms, vs, name):
    n = parts.shape[0]
    k = len(ws)
    widths = [w.shape[1] for w in ws]

    def body(*refs):
        p_ref, w_refs, m_refs, v_refs = refs[0], refs[1:1 + k], refs[1 + k:1 + 2 * k], refs[1 + 2 * k:1 + 3 * k]
        outs = refs[1 + 3 * k:]
        total = p_ref[0:1, :]
        for j in range(1, n):
            total = total + p_ref[j:j + 1, :]
        off = 0
        for a, width in enumerate(widths):
            g = total[:, off:off + width]
            off += width
            outs[4 * a][...] = g
            outs[4 * a + 1][...], outs[4 * a + 2][...], outs[4 * a + 3][...] = _adam_math(
                g, w_refs[a][...], m_refs[a][...], v_refs[a][...])

    return pl.pallas_call(
        body, name=name,
        out_shape=[jax.ShapeDtypeStruct((1, width), F32) for width in widths for _ in range(4)],
        compiler_params=_params(),
    )(parts, *ws, *ms, *vs)


def _mm(a, b, name, out_dtype, tm, tn, n_out=1):
    m, k = a.shape
    n = b.shape[1] // n_out
    nt = n // tn

    def body(a_ref, *refs):
        a_blk = a_ref[...].astype(BF16)
        for b_ref, o_ref in zip(refs[:n_out], refs[n_out:]):
            o_ref[...] = _dot(a_blk, b_ref[...].astype(BF16)).astype(out_dtype)

    outs = pl.pallas_call(
        body,
        name=name,
        grid=(nt, m // tm),
        out_shape=[jax.ShapeDtypeStruct((m, n), out_dtype)] * n_out,
        in_specs=[pl.BlockSpec((tm, k), lambda j, i: (i, 0))]
        + [pl.BlockSpec((k, tn), lambda j, i, g=g: (0, g * nt + j)) for g in range(n_out)],
        out_specs=[pl.BlockSpec((tm, tn), lambda j, i: (i, j))] * n_out,
        compiler_params=_params(("parallel", "parallel")),
    )(a, *([b] * n_out))
    return outs[0] if n_out == 1 else outs


def _mm_nt(a_list, b, name, out_dtype, tm, tn, comm=None):
    n_a = len(a_list)
    m, k = a_list[0].shape
    n = b.shape[0]

    def body(*refs):
        o_ref = refs[2 * n_a]
        acc = _dot_nt(refs[0][...].astype(BF16), refs[n_a][...].astype(BF16))
        for g in range(1, n_a):
            acc = acc + _dot_nt(refs[g][...].astype(BF16), refs[n_a + g][...].astype(BF16))
        o_ref[...] = acc.astype(out_dtype)

    (out,), got = _hosted_call(
        body, comm, name=name, grid=(n // tn, m // tm),
        out_shape=[jax.ShapeDtypeStruct((m, n), out_dtype)],
        in_specs=[pl.BlockSpec((tm, k), lambda j, i: (i, 0))] * n_a
        + [pl.BlockSpec((tn, k), lambda j, i, g=g: (j, g)) for g in range(n_a)],
        out_specs=[pl.BlockSpec((tm, tn), lambda j, i: (i, j))],
        scratch_shapes=[], dims=("parallel", "parallel"), args=(*a_list, *([b] * n_a)))
    return out, got


def _mm_tn(a, b_list, name, tm, tn, tk):
    s, m = a.shape
    n = b_list[0].shape[1]
    n_b = len(b_list)

    def body(a_ref, *refs):
        b_refs, o_refs = refs[:n_b], refs[n_b:]

        @pl.when(pl.program_id(2) == 0)
        def _():
            for o_ref in o_refs:
                o_ref[...] = jnp.zeros_like(o_ref)

        a_blk = a_ref[...].astype(BF16)
        for b_ref, o_ref in zip(b_refs, o_refs):
            o_ref[...] += _dot_tn(a_blk, b_ref[...].astype(BF16))

    return pl.pallas_call(
        body,
        name=name,
        grid=(n // tn, m // tm, s // tk),
        out_shape=[jax.ShapeDtypeStruct((m, n), F32)] * n_b,
        in_specs=[pl.BlockSpec((tk, tm), lambda j, i, k: (k, i))] + [pl.BlockSpec((tk, tn), lambda j, i, k: (k, j))] * n_b,
        out_specs=[pl.BlockSpec((tm, tn), lambda j, i, k: (i, j))] * n_b,
        compiler_params=_params(("parallel", "parallel", "arbitrary")),
    )(a, *b_list)


def _rows_call(fn, name, rows, params, out_rows, out_accs, ts, comm=None):
    s = rows[0].shape[0]
    nr, npar, no = len(rows), len(params), len(out_rows)

    def body(*refs):
        r, p = refs[:nr], refs[nr:nr + npar]
        o, acc = refs[nr + npar:nr + npar + no], refs[nr + npar + no:]
        outs, sums = fn(*[t[...] for t in r], *[t[...] for t in p])
        for ref, val in zip(o, outs):
            ref[...] = val.astype(ref.dtype)
        if acc:
            @pl.when(pl.program_id(0) == 0)
            def _():
                for ref in acc:
                    ref[...] = jnp.zeros_like(ref)

            for ref, val in zip(acc, sums):
                ref[...] += val

    outs, got = _hosted_call(
        body, comm, name=name, grid=(s // ts,),
        out_shape=[jax.ShapeDtypeStruct((s, w), dt) for w, dt in out_rows]
        + [jax.ShapeDtypeStruct(shape, F32) for shape in out_accs],
        in_specs=[pl.BlockSpec((ts, t.shape[1]), lambda i: (i, 0)) for t in rows]
        + [pl.BlockSpec(t.shape, lambda i: (0, 0)) for t in params],
        out_specs=[pl.BlockSpec((ts, w), lambda i: (i, 0)) for w, _ in out_rows]
        + [pl.BlockSpec(shape, lambda i: (0, 0)) for shape in out_accs],
        scratch_shapes=[], dims=("arbitrary",), args=(*rows, *params))
    return outs if comm is None else (outs, got)


def _fwd_norm1(x, g, scale, shift):
    return (_norm_mod(x, g, scale, shift),), ()


def _resid_norm(x, mixed, gate, g, scale, shift):
    x1 = x + (1.0 + gate) * mixed
    return x1, _norm_mod(x1, g, scale, shift)


def _fwd_resid_norm(x, mixed, gate, g, scale, shift):
    return _resid_norm(x, mixed, gate, g, scale, shift), ()


def _final(x1, ffn, tgt, gate, g):
    def head(x1, ffn, gate, g):
        return _rms(x1 + (1.0 + gate) * ffn, g)

    y, vjp = jax.vjp(head, x1, ffn, gate, g)
    err = y - tgt
    dx2, dffn, dgate, dg = vjp(err * (1.0 / D))
    sq = jnp.sum(jnp.sum(err * err, axis=1, keepdims=True), axis=0, keepdims=True)
    loss = jnp.broadcast_to(sq * (0.5 / D), (1, 128))
    return (dx2, dffn), (dgate, dg, loss)


def _bwd_resid_norm(x, mixed, dx2, dh2, gate, g, scale, shift):
    _, vjp = jax.vjp(_resid_norm, x, mixed, gate, g, scale, shift)
    dx, dmixed, dgate, dg, dscale, dshift = vjp((dx2, dh2))
    return (dx, dmixed), (dgate, dg, dscale, dshift)


def _bwd_norm1(x, dh, dx1, g, scale, shift):
    _, vjp = jax.vjp(_norm_mod, x, g, scale, shift)
    dx, dg, dscale, dshift = vjp(dh)
    return (dx1 + dx,), (dg, dscale, dshift)


def _ada_fwd(c16, w):
    def body(c_ref, w_ref, o_ref):
        c = c_ref[...]
        o_ref[...] = _dot((c * _sigmoid(c)).astype(BF16), w_ref[...].astype(BF16))

    return pl.pallas_call(
        body, name="ada_fwd", out_shape=jax.ShapeDtypeStruct((c16.shape[0], w.shape[1]), F32),
        compiler_params=_params(),
    )(c16, w)


def _ada_bwd(c16, dmod16):
    def body(c_ref, d_ref, o_ref):
        c = c_ref[...]
        o_ref[...] = _dot_tn((c * _sigmoid(c)).astype(BF16), d_ref[...].astype(BF16))

    return pl.pallas_call(
        body, name="ada_bwd", out_shape=jax.ShapeDtypeStruct((c16.shape[1], dmod16.shape[1]), F32),
        compiler_params=_params(),
    )(c16, dmod16)


CONV_R = 256


def _conv_window(win, wc, bc):
    s1 = pltpu.roll(win, 1, 0)
    s2 = pltpu.roll(win, 2, 0)
    u = bc + wc[0:1] * s2 + wc[1:2] * s1 + wc[2:3] * win
    return u[8:], s1[8:], s2[8:], win[8:]


def _row_windows(ref, before, after, n_after, lanes):
    ts, r = ref.shape[0], CONV_R

    def middle(i):
        return ref[pl.ds(pl.multiple_of(i * r, r) - 8, r + 8 + n_after), lanes]

    first = jnp.concatenate([before, ref[0:r + n_after, lanes]], axis=0)
    last = ref[ts - r - 8:ts, lanes] if n_after == 0 else jnp.concatenate([ref[ts - r - 8:ts, lanes], after], axis=0)
    return first, middle, last


def _ffn_specs(ts, s):
    cur = pl.BlockSpec((ts, FF_T), lambda j, i: (i, j))
    halo = pl.BlockSpec((8, FF_T), lambda j, i: (jnp.maximum(i * (ts // 8) - 1, 0), j))
    nxt = pl.BlockSpec((8, FF_T), lambda j, i: (jnp.minimum((i + 1) * (ts // 8), s // 8 - 1), j))
    wc = [pl.BlockSpec((8, FF_T), lambda j, i, h=h: (0, h * N_FT + j)) for h in range(2)]
    bc = [pl.BlockSpec((1, FF_T), lambda j, i, h=h: (0, h * N_FT + j)) for h in range(2)]
    return cur, halo, nxt, wc, bc


def _conv_glu_fwd(up_v, up_g, wc, bc, ts):
    s = up_v.shape[0]
    cur, halo, _, wcs, bcs = _ffn_specs(ts, s)

    def body(v_ref, vh_ref, g_ref, gh_ref, wcv_ref, wcg_ref, bcv_ref, bcg_ref, a_ref):
        keep = jnp.where(pl.program_id(1) == 0, 0.0, 1.0)
        for lanes in (slice(0, 128), slice(128, 256)):
            wcv, wcg, bcv, bcg = wcv_ref[:, lanes], wcg_ref[:, lanes], bcv_ref[:, lanes], bcg_ref[:, lanes]
            first_v, mid_v, _ = _row_windows(v_ref, vh_ref[:, lanes] * keep, None, 0, lanes)
            first_g, mid_g, _ = _row_windows(g_ref, gh_ref[:, lanes] * keep, None, 0, lanes)

            def emit(win_v, win_g, start, lanes=lanes, wcv=wcv, wcg=wcg, bcv=bcv, bcg=bcg):
                val = _conv_window(win_v, wcv, bcv)[0]
                gte = _conv_window(win_g, wcg, bcg)[0]
                a_ref[pl.ds(start, CONV_R), lanes] = (val * (gte * _sigmoid(gte))).astype(BF16)

            emit(first_v, first_g, 0)

            def loop(i, carry, emit=emit, mid_v=mid_v, mid_g=mid_g):
                emit(mid_v(i), mid_g(i), pl.multiple_of(i * CONV_R, CONV_R))
                return carry

            lax.fori_loop(1, ts // CONV_R, loop, 0)

    return pl.pallas_call(
        body, name="conv_glu_fwd", grid=(N_FT, s // ts),
        out_shape=jax.ShapeDtypeStruct((s, D_FF), BF16),
        in_specs=[cur, halo, cur, halo, *wcs, *bcs], out_specs=cur,
        compiler_params=_params(("parallel", "arbitrary")),
    )(up_v, up_v, up_g, up_g, wc, wc, bc, bc)


def _glu_bwd(val, gte, da):
    sg = _sigmoid(gte)
    return da * (gte * sg), da * val * (sg * (1.0 + gte * (1.0 - sg)))


def _conv_glu_bwd(up_v, up_g, da, wc, bc, ts, comm=None):
    s = up_v.shape[0]
    nblk = s // ts
    cur, halo, nxt, wcs, bcs = _ffn_specs(ts, s)
    acc_w = pl.BlockSpec((8, FF_T), lambda j, i: (0, j))
    acc_b = pl.BlockSpec((1, FF_T), lambda j, i: (0, j))

    def body(v_ref, vh_ref, vn_ref, g_ref, gh_ref, gn_ref, da_ref, dan_ref, wcv_ref, wcg_ref, bcv_ref, bcg_ref,
             dupv_ref, dupg_ref, dwcv_ref, dwcg_ref, dbcv_ref, dbcg_ref):
        first = pl.program_id(1) == 0
        keep = jnp.where(first, 0.0, 1.0)
        keep_next = jnp.where(pl.program_id(1) == nblk - 1, 0.0, 1.0)
        r = CONV_R
        n = ts // r

        @pl.when(first)
        def _():
            for ref in (dwcv_ref, dwcg_ref, dbcv_ref, dbcg_ref):
                ref[...] = jnp.zeros_like(ref)

        for lanes in (slice(0, 128), slice(128, 256)):
            wcv, wcg, bcv, bcg = wcv_ref[:, lanes], wcg_ref[:, lanes], bcv_ref[:, lanes], bcg_ref[:, lanes]
            first_v, mid_v, last_v = _row_windows(v_ref, vh_ref[:, lanes] * keep, vn_ref[:, lanes], 8, lanes)
            first_g, mid_g, last_g = _row_windows(g_ref, gh_ref[:, lanes] * keep, gn_ref[:, lanes], 8, lanes)

            def emit(win_v, win_g, da_w, start, lanes=lanes, wcv=wcv, wcg=wcg, bcv=bcv, bcg=bcg):
                u_v, s1_v, s2_v, x_v = _conv_window(win_v, wcv, bcv)
                u_g, s1_g, s2_g, x_g = _conv_window(win_g, wcg, bcg)
                du_v, du_g = _glu_bwd(u_v, u_g, da_w)
                rows = pl.ds(start, r)
                for du, s1, s2, x, wc, dup_ref, dwc_ref, dbc_ref in (
                        (du_v, s1_v, s2_v, x_v, wcv, dupv_ref, dwcv_ref, dbcv_ref),
                        (du_g, s1_g, s2_g, x_g, wcg, dupg_ref, dwcg_ref, dbcg_ref)):
                    dup = wc[2:3] * du + wc[1:2] * pltpu.roll(du, r + 7, 0) + wc[0:1] * pltpu.roll(du, r + 6, 0)
                    dup_ref[rows, lanes] = dup[:r].astype(BF16)
                    du = du[:r]
                    dwc_ref[0:1, lanes] += jnp.sum(du * s2[:r], axis=0, keepdims=True)
                    dwc_ref[1:2, lanes] += jnp.sum(du * s1[:r], axis=0, keepdims=True)
                    dwc_ref[2:3, lanes] += jnp.sum(du * x[:r], axis=0, keepdims=True)
                    dbc_ref[:, lanes] += jnp.sum(du, axis=0, keepdims=True)

            emit(first_v, first_g, da_ref[0:r + 8, lanes], 0)

            def loop(i, carry, emit=emit, mid_v=mid_v, mid_g=mid_g, lanes=lanes):
                start = pl.multiple_of(i * r, r)
                emit(mid_v(i), mid_g(i), da_ref[pl.ds(start, r + 8), lanes], start)
                return carry

            lax.fori_loop(1, n - 1, loop, 0)
            da_last = jnp.concatenate([da_ref[ts - r:ts, lanes], dan_ref[:, lanes] * keep_next], axis=0)
            emit(last_v, last_g, da_last, ts - r)

    return _hosted_call(
        body, comm, name="conv_glu_bwd", grid=(N_FT, nblk),
        out_shape=[jax.ShapeDtypeStruct((s, D_FF), BF16)] * 2 + [jax.ShapeDtypeStruct((8, D_FF), F32)] * 2
        + [jax.ShapeDtypeStruct((1, D_FF), F32)] * 2,
        in_specs=[cur, halo, nxt, cur, halo, nxt, cur, nxt, *wcs, *bcs],
        out_specs=[cur, cur, acc_w, acc_w, acc_b, acc_b],
        scratch_shapes=[], dims=("parallel", "arbitrary"),
        args=(up_v, up_v, up_v, up_g, up_g, up_g, da, da, wc, wc, bc, bc))


def _tri(kind):
    b = SB_B
    m = {"lower_strict": np.tril(np.ones((b, b)), -1), "upper_incl": np.triu(np.ones((b, b)), 0),
         "upper_strict": np.triu(np.ones((b, b)), 1)}[kind]
    return jnp.asarray(np.concatenate([m, np.ones((b, 128))], axis=1), BF16)


def _key_sums(x, tri):
    hi, lo = _hilo(x)
    cb = _dot(hi, tri) + _dot(lo, tri)
    return cb[:, :SB_B], cb[:, SB_B:]


def _sb_fwd(proj, comm=None):
    s = proj.shape[0]
    b = SB_B

    def body(q_ref, k_ref, v_ref, tri_ref, o_ref, t_ref, first_ref, c_ref, a_ref):
        i = pl.program_id(1)
        lane = lax.broadcasted_iota(jnp.int32, (b, 128), 1)
        heads = (lane < 64, lane >= 64)
        causal = lax.broadcasted_iota(jnp.int32, (b, b), 1) < lax.broadcasted_iota(jnp.int32, (b, b), 0)
        q = q_ref[...] * 0.125
        qm = [jnp.where(h, q, 0.0).astype(BF16) for h in heads]
        c_ref[...] = jnp.zeros_like(c_ref)
        a_ref[...] = jnp.zeros_like(a_ref)

        def prepare(jj, masked):
            rows = pl.ds(pl.multiple_of(jj * b, b), b)
            kb = k_ref[rows, :].astype(BF16)
            vb = v_ref[rows, :]
            out = []
            for hh in range(2):
                z = _dot_nt(qm[hh], kb)
                lg = _log_sigmoid(-z)
                if masked:
                    lg = jnp.where(causal, lg, 0.0)
                after, total = _key_sums(lg, tri_ref[...])
                out.append((lg + z + after, total, jnp.where(heads[hh], vb, 0.0).astype(BF16)))
            return out

        def walk_blocks(blocks):
            for hh in range(2):
                c, a = c_ref[hh], a_ref[hh]
                for pre, masked in blocks:
                    logw, total, vm = pre[hh]
                    w = jnp.exp(logw + jnp.concatenate([c, c], axis=1))
                    if masked:
                        w = jnp.where(causal, w, 0.0)
                    a = a + _dot(w.astype(BF16), vm)
                    c = c + total
                c_ref[hh], a_ref[hh] = c, a

        def largest_sum():
            return jnp.max(jnp.maximum(c_ref[0], c_ref[1]))

        @pl.when(i == 0)
        def _():
            walk_blocks([(prepare(i, True), True)])

        @pl.when(i > 0)
        def _():
            walk_blocks([(prepare(i, True), True), (prepare(i - 1, False), False)])

        def more(state):
            jj, top = state
            return jnp.logical_and(jj >= 0, top > SB_DEAD)

        def walk(state):
            jj, _ = state
            walk_blocks([(prepare(jj, False), False)])
            return jj - 1, largest_sum()

        jj, _ = lax.while_loop(more, walk, (jnp.maximum(i - 2, -1), largest_sum()))
        o_ref[...] = a_ref[0] + a_ref[1]
        t_ref[...] = jnp.concatenate([c_ref[0], c_ref[1]], axis=1)
        first_ref[pl.program_id(0), i] = (jj + 1).astype(F32)

    return _hosted_call(
        body, comm, name="sb_fwd", grid=(4, s // b),
        out_shape=[jax.ShapeDtypeStruct((s, SB_W), F32), jax.ShapeDtypeStruct((s, 2 * SB_W), F32),
                   jax.ShapeDtypeStruct((4, s // b), F32)],
        in_specs=[pl.BlockSpec((b, 128), lambda p, i: (i, p)),
                  pl.BlockSpec((s, 128), lambda p, i: (0, 4 + p)),
                  pl.BlockSpec((s, 128), lambda p, i: (0, 8 + p)),
                  pl.BlockSpec((b, b + 128), lambda p, i: (0, 0))],
        out_specs=[pl.BlockSpec((b, 128), lambda p, i: (i, p)), pl.BlockSpec((b, 256), lambda p, i: (i, p)),
                   pl.BlockSpec(memory_space=pltpu.SMEM)],
        scratch_shapes=[pltpu.VMEM((2, b, 128), F32), pltpu.VMEM((2, b, 128), F32)],
        dims=("arbitrary", "arbitrary"), args=(proj, proj, proj, _tri("lower_strict")))


def _sb_bwd(proj, do, tot, first, comm=None):
    s = proj.shape[0]
    b = SB_B

    def body(q_ref, k_ref, v_ref, do_ref, t_ref, first_ref, ti_ref, ts_ref, dq_ref, dk_ref, dv_ref, cl_ref, ce_ref,
             a_ref):
        i = pl.program_id(1)
        first = jnp.clip(first_ref[pl.program_id(0), i].astype(jnp.int32), 0, i)
        lane = lax.broadcasted_iota(jnp.int32, (b, 128), 1)
        heads = (lane < 64, lane >= 64)
        causal = lax.broadcasted_iota(jnp.int32, (b, b), 1) < lax.broadcasted_iota(jnp.int32, (b, b), 0)
        q = q_ref[...] * 0.125
        do = do_ref[...]
        qm = [jnp.where(h, q, 0.0).astype(BF16) for h in heads]
        dom = [jnp.where(h, do, 0.0).astype(BF16) for h in heads]
        cl_ref[...] = jnp.zeros_like(cl_ref)
        ce_ref[...] = jnp.zeros_like(ce_ref)
        a_ref[...] = jnp.zeros_like(a_ref)

        @pl.when(i == 0)
        def _():
            dk_ref[...] = jnp.zeros_like(dk_ref)
            dv_ref[...] = jnp.zeros_like(dv_ref)

        def prepare(jj, masked):
            rows = pl.ds(pl.multiple_of(jj * b, b), b)
            kf = k_ref[rows, :]
            kb = kf.astype(BF16)
            vb = v_ref[rows, :].astype(BF16)
            out = []
            for hh in range(2):
                z = _dot_nt(qm[hh], kb)
                lg = _log_sigmoid(-z)
                if masked:
                    lg = jnp.where(causal, lg, 0.0)
                upto, total = _key_sums(lg, ti_ref[...])
                lsz = lg + z
                out.append((lsz - upto, total, jnp.exp(lsz), _dot_nt(dom[hh], vb),
                            jnp.where(heads[hh], kf, 0.0).astype(BF16)))
            return rows, out

        def walk_blocks(blocks):
            grads = [[jnp.zeros((b, 128), F32), jnp.zeros((b, 128), F32)] for _ in blocks]
            for hh in range(2):
                cl, ce, a = cl_ref[hh], ce_ref[hh], a_ref[hh]
                t = t_ref[:, hh * 128:(hh + 1) * 128]
                for n, ((_, pre), masked) in enumerate(blocks):
                    logw, total, sig, dw, km = pre[hh]
                    w = jnp.exp(logw + jnp.concatenate([t - cl, t - cl], axis=1))
                    if masked:
                        w = jnp.where(causal, w, 0.0)
                    e = w * dw
                    before, etot = _key_sums(e, ts_ref[...])
                    dz = e - sig * (e + before + jnp.concatenate([ce, ce], axis=1))
                    if masked:
                        dz = jnp.where(causal, dz, 0.0)
                    dzb = dz.astype(BF16)
                    a = a + _dot(dzb, km)
                    grads[n][0] = grads[n][0] + _dot_tn(dzb, qm[hh])
                    grads[n][1] = grads[n][1] + _dot_tn(w.astype(BF16), dom[hh])
                    cl = cl + total
                    ce = ce + etot
                cl_ref[hh], ce_ref[hh], a_ref[hh] = cl, ce, a
            for ((rows, _), _), (dk, dv) in zip(blocks, grads):
                dk_ref[rows, :] += dk
                dv_ref[rows, :] += dv

        def loop(jj, carry):
            walk_blocks([(prepare(jj, False), False)])
            return carry

        lax.fori_loop(first, i - 1, loop, 0)

        @pl.when(i == 0)
        def _():
            walk_blocks([(prepare(i, True), True)])

        @pl.when(i > 0)
        def _():
            walk_blocks([(prepare(i - 1, False), False), (prepare(i, True), True)])

        dq_ref[...] = (a_ref[0] + a_ref[1]) * 0.125

    blk = pl.BlockSpec((b, 128), lambda p, i: (i, p))
    full = pl.BlockSpec((s, 128), lambda p, i: (0, p))
    tri = pl.BlockSpec((b, b + 128), lambda p, i: (0, 0))
    return _hosted_call(
        body, comm, name="sb_bwd", grid=(4, s // b),
        out_shape=[jax.ShapeDtypeStruct((s, SB_W), F32)] * 3,
        in_specs=[blk, pl.BlockSpec((s, 128), lambda p, i: (0, 4 + p)), pl.BlockSpec((s, 128), lambda p, i: (0, 8 + p)),
                  blk, pl.BlockSpec((b, 256), lambda p, i: (i, p)), pl.BlockSpec(memory_space=pltpu.SMEM), tri, tri],
        out_specs=[blk, full, full],
        scratch_shapes=[pltpu.VMEM((2, b, 128), F32)] * 3,
        dims=("arbitrary", "arbitrary"),
        args=(proj, proj, proj, do, tot, first, _tri("upper_incl"), _tri("upper_strict")))


_COL_Q, _COL_K, _COL_V, _COL_G, _COL_F = 12, 14, 8, 10, 24


def _gla_consts():
    c = CHUNK
    incl = np.tril(np.ones((c, c)), 0)
    strict = np.tril(np.ones((c, c)), -1)
    bd = np.zeros((128, 256))
    bd[:64, :128] = 1.0
    bd[64:, 128:] = 1.0
    return jnp.asarray(incl, BF16), jnp.asarray(strict, BF16), jnp.asarray(bd, F32)


def _time_sums(tri, x):
    hi, lo = _hilo(x)
    return _dot(tri, hi) + _dot(tri, lo)


def _gla_gate(o, gg, g):
    parts = []
    for h in range(2):
        oh = o[:, h * 128:(h + 1) * 128]
        parts.append(oh * lax.rsqrt(jnp.mean(oh * oh, axis=-1, keepdims=True) + EPS))
    return (jnp.concatenate(parts, axis=1) * g) * (gg * _sigmoid(gg))


def _gla_chunk(la_c, k_c, incl, ones_cv):
    cum = _time_sums(incl, la_c)
    total = cum[CHUNK - 1:CHUNK]
    edec = jnp.exp(total - cum)
    kdec = k_c * edec
    hi, lo = _hilo(la_c)
    dec = jnp.exp(_dot_tn(hi, ones_cv) + _dot_tn(lo, ones_cv))
    return edec, kdec, dec


def _gla_fwd(proj, wfg, bfg, gain):
    s = proj.shape[0]
    t = GLA_T
    nch = t // CHUNK
    incl, _, bd = _gla_consts()

    def body(q_ref, k_ref, v_ref, gg_ref, f_ref, wf_ref, bf_ref, g_ref, incl_ref, bd_ref, o_ref, st_ref, state_ref):
        @pl.when(pl.program_id(0) == 0)
        def _():
            state_ref[...] = jnp.zeros_like(state_ref)

        la = _log_sigmoid(_dot(f_ref[...].astype(BF16), wf_ref[...].astype(BF16)) + bf_ref[...]) * (1.0 / 16.0)
        ones_cv = jnp.ones((CHUNK, 256), BF16)
        states = [state_ref[0], state_ref[1]]
        for cc in range(nch):
            rows = slice(cc * CHUNK, (cc + 1) * CHUNK)
            for p in range(2):
                kl, vl = slice(p * 128, (p + 1) * 128), slice(p * 256, (p + 1) * 256)
                _, kdec, dec = _gla_chunk(la[rows, kl], k_ref[rows, kl], incl_ref[...], ones_cv)
                kv = _dot_tn(kdec.astype(BF16), v_ref[rows, vl].astype(BF16))
                states[p] = dec * states[p] + bd_ref[...] * kv
                st_ref[p, cc] = states[p]
                o = _dot((q_ref[rows, kl] * 0.125).astype(BF16), states[p].astype(BF16))
                o_ref[rows, vl] = _gla_gate(o, gg_ref[rows, vl], g_ref[:, vl])
        state_ref[0] = states[0]
        state_ref[1] = states[1]

    def col(width, blk):
        return pl.BlockSpec((t, width), lambda i: (i, blk))

    def whole(shape):
        return pl.BlockSpec(shape, lambda i: tuple(0 for _ in shape))

    return pl.pallas_call(
        body, name="gla_fwd", grid=(s // t,),
        out_shape=[jax.ShapeDtypeStruct((s, GLA_VW), F32), jax.ShapeDtypeStruct((2, s // CHUNK, 128, 256), F32)],
        in_specs=[col(256, _COL_Q // 2), col(256, _COL_K // 2), col(512, _COL_V // 2), col(512, _COL_G // 2),
                  col(128, _COL_F), whole((128, GLA_KW)), whole((1, GLA_KW)), whole((1, GLA_VW)),
                  whole((CHUNK, CHUNK)), whole((128, 256))],
        out_specs=[col(512, 0), pl.BlockSpec((2, nch, 128, 256), lambda i: (0, i, 0, 0))],
        scratch_shapes=[pltpu.VMEM((2, 128, 256), F32)],
        compiler_params=_params(("arbitrary",)),
    )(proj, proj, proj, proj, proj, wfg, bfg, gain, incl, bd)


def _gla_bwd(proj, do, states, wfg, bfg, gain):
    s = proj.shape[0]
    t = GLA_T
    nch = t // CHUNK
    nblk = s // t
    incl, strict, bd = _gla_consts()

    def body(q_ref, k_ref, v_ref, gg_ref, f_ref, do_ref, st_ref, sp_ref, wf_ref, bf_ref, g_ref, incl_ref, str_ref,
             bd_ref, dq_ref, dk_ref, dv_ref, dgg_ref, df_ref, dwf_ref, dbf_ref, dg_ref, carry_ref, dfs_ref):
        i = pl.program_id(0)

        @pl.when(i == 0)
        def _():
            carry_ref[...] = jnp.zeros_like(carry_ref)
            dwf_ref[...] = jnp.zeros_like(dwf_ref)
            dbf_ref[...] = jnp.zeros_like(dbf_ref)
            dg_ref[...] = jnp.zeros_like(dg_ref)

        fb = f_ref[...].astype(BF16)
        wf = wf_ref[...].astype(BF16)
        f = _dot(fb, wf) + bf_ref[...]
        la = _log_sigmoid(f) * (1.0 / 16.0)
        dla_df = _sigmoid(-f) * (1.0 / 16.0)
        ones_cv = jnp.ones((CHUNK, 256), BF16)
        ones_8v = jnp.ones((8, 256), BF16)
        first_block = jnp.where(i == nblk - 1, 0.0, 1.0)
        carries = [carry_ref[0], carry_ref[1]]
        dgains = [jnp.zeros((1, 256), F32), jnp.zeros((1, 256), F32)]
        for cc in reversed(range(nch)):
            rows = slice(cc * CHUNK, (cc + 1) * CHUNK)
            for p in range(2):
                kl, vl = slice(p * 128, (p + 1) * 128), slice(p * 256, (p + 1) * 256)
                edec, kdec, dec = _gla_chunk(la[rows, kl], k_ref[rows, kl], incl_ref[...], ones_cv)
                state = st_ref[p, cc]
                prev = st_ref[p, cc - 1] if cc > 0 else sp_ref[p, 0] * first_block
                qs = (q_ref[rows, kl] * 0.125).astype(BF16)
                sb16 = state.astype(BF16)
                o = _dot(qs, sb16)
                _, vjp = jax.vjp(_gla_gate, o, gg_ref[rows, vl], g_ref[:, vl])
                do_c, dgg_c, dg_c = vjp(do_ref[rows, vl])
                dgains[p] = dgains[p] + dg_c
                dgg_ref[rows, vl] = dgg_c
                do16 = do_c.astype(BF16)
                dq_ref[rows, kl] = _dot_nt(do16, sb16) * 0.125
                grad = bd_ref[...] * _dot_tn(qs, do16) + carries[p]
                g16 = grad.astype(BF16)
                dv_ref[rows, vl] = _dot(kdec.astype(BF16), g16)
                dkdec = _dot_nt(v_ref[rows, vl].astype(BF16), g16)
                hi, lo = _hilo(grad * prev * dec)
                ddec = (_dot_nt(ones_8v, hi) + _dot_nt(ones_8v, lo))[0:1]
                dk_ref[rows, kl] = dkdec * edec
                dla = _time_sums(str_ref[...], dkdec * kdec) + ddec
                dfs_ref[rows, kl] = dla * dla_df[rows, kl]
                carries[p] = dec * grad
        carry_ref[0] = carries[0]
        carry_ref[1] = carries[1]
        df = dfs_ref[...]
        df16 = df.astype(BF16)
        df_ref[...] = _dot_nt(df16, wf)
        dwf_ref[...] += _dot_tn(fb, df16)
        dbf_ref[...] += jnp.sum(df, axis=0, keepdims=True)
        dg_ref[...] += jnp.concatenate(dgains, axis=1)

    def col(width, blk):
        return pl.BlockSpec((t, width), lambda i: (nblk - 1 - i, blk))

    def whole(shape):
        return pl.BlockSpec(shape, lambda i: tuple(0 for _ in shape))

    return pl.pallas_call(
        body, name="gla_bwd", grid=(nblk,),
        out_shape=[jax.ShapeDtypeStruct((s, GLA_KW), F32), jax.ShapeDtypeStruct((s, GLA_KW), F32),
                   jax.ShapeDtypeStruct((s, GLA_VW), F32), jax.ShapeDtypeStruct((s, GLA_VW), F32),
                   jax.ShapeDtypeStruct((s, 128), F32), jax.ShapeDtypeStruct((128, GLA_KW), F32),
                   jax.ShapeDtypeStruct((1, GLA_KW), F32), jax.ShapeDtypeStruct((1, GLA_VW), F32)],
        in_specs=[col(256, _COL_Q // 2), col(256, _COL_K // 2), col(512, _COL_V // 2), col(512, _COL_G // 2),
                  col(128, _COL_F), col(512, 1),
                  pl.BlockSpec((2, nch, 128, 256), lambda i: (0, nblk - 1 - i, 0, 0)),
                  pl.BlockSpec((2, 1, 128, 256), lambda i: (0, jnp.maximum((nblk - 1 - i) * nch - 1, 0), 0, 0)),
                  whole((128, GLA_KW)), whole((1, GLA_KW)), whole((1, GLA_VW)),
                  whole((CHUNK, CHUNK)), whole((CHUNK, CHUNK)), whole((128, 256))],
        out_specs=[col(256, 0), col(256, 0), col(512, 0), col(512, 0), col(128, 0),
                   whole((128, GLA_KW)), whole((1, GLA_KW)), whole((1, GLA_VW))],
        scratch_shapes=[pltpu.VMEM((2, 128, 256), F32), pltpu.VMEM((t, GLA_KW), F32)],
        compiler_params=_params(("arbitrary",)),
    )(proj, proj, proj, proj, proj, do, states, states, wfg, bfg, gain, incl, strict, bd)


def _cols_by_dev(a, n_dev=N_DEV):
    r, c = a.shape
    return a.reshape(r, n_dev, c // n_dev).transpose(1, 0, 2)


def _cols_from_dev(a):
    _, r, n = a.shape
    return a.transpose(1, 0, 2).reshape(r, N_DEV * n)


def kernel(x, c, w_ada, b_ada, g_norm1, w_in, w_fg2, b_fg2, g_gla_out, w_out, g_norm2, w_up, w_conv, b_conv, w_down, g_final, loss_target, m_w_ada, m_b_ada, m_g_norm1, m_w_in, m_w_fg2, m_b_fg2, m_g_gla_out, m_w_out, m_g_norm2, m_w_up, m_w_conv, m_b_conv, m_w_down, m_g_final, v_w_ada, v_b_ada, v_g_norm1, v_w_in, v_w_fg2, v_b_fg2, v_g_gla_out, v_w_out, v_g_norm2, v_w_up, v_w_conv, v_b_conv, v_w_down, v_g_final):
    s = x.shape[1]
    me = 4 * lax.axis_index("x") + 2 * lax.axis_index("y") + lax.axis_index("c")
    xs, tgt = x[0], loss_target[0]
    ts = min(512, s)
    tm = min(1024, s)

    c_all, wconv_g, wfg_g = _exchange([c, w_conv[0], w_fg2[0]], "gather_first", True)
    wconv_p = jnp.pad(_cols_from_dev(wconv_g), ((0, 5), (0, 0)))
    wfg_p = jnp.pad(_cols_from_dev(wfg_g), ((0, 128 - RANK), (0, 0)))

    c16 = jnp.pad(c_all.reshape(N_DEV, D), ((0, 8), (0, 0)))
    modp = _ada_fwd(c16, w_ada[0])[:N_DEV]
    (mod_all,) = _exchange([modp], "gather_mod", True)
    mod = lax.dynamic_index_in_dim(mod_all, me, axis=1, keepdims=False).reshape(1, 6 * D) + b_ada
    shift1, scale1, gate1, shift2, scale2, gate2 = [mod[:, k * D:(k + 1) * D] for k in range(6)]

    (h,), (win_g,) = _rows_call(_fwd_norm1, "norm1", [xs], [g_norm1, scale1, shift1], [(D, BF16)], [], ts,
                                comm=([w_in[0].astype(BF16)], True))
    win = jnp.pad(_cols_from_dev(win_g), ((0, 0), (0, IN_WP - IN_W)))
    proj = _mm(h, win, "in_proj", F32, ts, IN_WP)
    (o_sb, sb_tot, sb_first), (wout_g, wup_g, wdown_g) = _sb_fwd(
        proj, comm=([w_out[0].astype(BF16), w_up[0].astype(BF16), w_down[0].astype(BF16)], True))
    wout = wout_g.reshape(D, D)
    wup = _cols_from_dev(wup_g)
    wdown = wdown_g.reshape(D_FF, D)
    o_gla, states = _gla_fwd(proj, wfg_p, b_fg2, g_gla_out)
    cat = jnp.concatenate([o_sb, o_gla], axis=1).astype(BF16)
    mixed = _mm(cat, wout, "out_proj", F32, tm, 512)
    x1, h2 = _rows_call(_fwd_resid_norm, "resid_norm2", [xs, mixed], [gate1, g_norm2, scale2, shift2],
                        [(D, F32), (D, BF16)], [], ts)
    up_v, up_g = _mm(h2, wup, "up_proj", F32, tm, 1408, n_out=2)
    act = _conv_glu_fwd(up_v, up_g, wconv_p, b_conv, tm)
    ffn = _mm(act, wdown, "down_proj", F32, tm, 512)
    dx2, dffn, dgate2, dg_final, loss_part = _rows_call(
        _final, "final", [x1, ffn, tgt], [gate2, g_final.reshape(1, D)],
        [(D, F32), (D, BF16)], [(1, D), (1, D), (1, 128)], ts)

    (dw_down,) = _mm_tn(act, [dffn], "down_wgrad", 1408, 1024, tm)
    da, _ = _mm_nt([dffn], wdown, "down_dgrad", F32, tm, 1408)
    (dup_v, dup_g, dwc_v, dwc_g, dbc_v, dbc_g), (r_down,) = _conv_glu_bwd(
        up_v, up_g, da, wconv_p, b_conv, tm, comm=([dw_down.reshape(N_DEV, 352, D)], False))
    dw_up_v, dw_up_g = _mm_tn(h2, [dup_v, dup_g], "up_wgrad", 1024, 1408, tm)
    dh2, _ = _mm_nt([dup_v, dup_g], wup, "up_dgrad", F32, tm, 512)
    dx1, dmixed, dgate1, dg_norm2, dscale2, dshift2 = _rows_call(
        _bwd_resid_norm, "resid_norm2_bwd", [xs, mixed, dx2, dh2], [gate1, g_norm2, scale2, shift2],
        [(D, F32), (D, BF16)], [(1, D)] * 4, ts)
    (dw_out,) = _mm_tn(cat, [dmixed], "out_wgrad", 1024, 1024, tm)
    dcat, _ = _mm_nt([dmixed], wout, "out_dgrad", F32, tm, 512)
    dw_up_s = jnp.concatenate([_cols_by_dev(dw_up_v, 4), _cols_by_dev(dw_up_g, 4)], axis=0)
    dwconv_s = _cols_by_dev(jnp.concatenate([dwc_v[:3], dwc_g[:3]], axis=1))
    (dq, dk, dv), (r_up, r_out, r_conv) = _sb_bwd(
        proj, dcat, sb_tot, sb_first, comm=([dw_up_s, dw_out.reshape(N_DEV, 128, D), dwconv_s], False))
    dgq, dgk, dgv, dgg, dgf, dwfg_p, dbfg, dg_gla = _gla_bwd(proj, dcat, states, wfg_p, b_fg2, g_gla_out)
    dproj = jnp.concatenate([dq, dk, dv, dgq, dgk, dgv, dgg, dgf], axis=1).astype(BF16)
    (dw_in,) = _mm_tn(h, [dproj], "in_wgrad", 1024, IN_WP, tm)
    dh, (r_in, r_fg) = _mm_nt([dproj], win, "in_dgrad", F32, tm, 512,
                              comm=([_cols_by_dev(dw_in[:, :IN_W]).astype(BF16), _cols_by_dev(dwfg_p[:RANK])], False))
    grad_x, dg_norm1, dscale1, dshift1 = _rows_call(
        _bwd_norm1, "norm1_bwd", [xs, dh, dx1], [g_norm1, scale1, shift1], [(D, F32)], [(1, D)] * 3, ts)

    dmod = jnp.concatenate([dshift1, dscale1, dgate1, dshift2, dscale2, dgate2], axis=1)
    small_parts = [dmod, dg_norm1, dbfg, dg_gla, dg_norm2, dbc_v, dbc_g, dg_final, loss_part]
    n_small = sum(p.shape[1] for p in small_parts[:-1])
    (sg,) = _exchange([jnp.concatenate(small_parts, axis=1)], "gather_small_grads", True)
    sg = sg.reshape(N_DEV, n_small + 128)
    loss = jnp.sum(sg[:, n_small])
    small_w = [b_ada, g_norm1, b_fg2, g_gla_out, g_norm2, b_conv, g_final.reshape(1, D)]
    small_m = [m_b_ada, m_g_norm1, m_b_fg2, m_g_gla_out, m_g_norm2, m_b_conv, m_g_final.reshape(1, D)]
    small_v = [v_b_ada, v_g_norm1, v_b_fg2, v_g_gla_out, v_g_norm2, v_b_conv, v_g_final.reshape(1, D)]
    s_out = list(_adam_rows(sg, small_w, small_m, small_v, "adam_small"))
    s_out[24:] = [t.reshape(D) for t in s_out[24:]]
    s_g, s_d, s_m, s_v = [s_out[k::4] for k in range(4)]

    dmod_all = sg[:, :6 * D].reshape(N_DEV, N_DEV, 768)
    dmod_mine = lax.dynamic_index_in_dim(dmod_all, me, axis=1, keepdims=False)
    dw_ada = _ada_bwd(c16, jnp.pad(dmod_mine, ((0, 8), (0, 0))))
    a_g, a_d, a_m, a_v = [t[None] for t in _adam(dw_ada[None], w_ada[0], m_w_ada[0], v_w_ada[0], "adam_ada", 256)]

    recv = [r_in, r_fg, r_out, r_up, r_conv, r_down]
    big_w = [w_in, w_fg2, w_out, w_up, w_conv, w_down]
    big_m = [m_w_in, m_w_fg2, m_w_out, m_w_up, m_w_conv, m_w_down]
    big_v = [v_w_in, v_w_fg2, v_w_out, v_w_up, v_w_conv, v_w_down]
    big_name = ["adam_in", "adam_fg2", "adam_out", "adam_up", "adam_conv", "adam_down"]
    big_rows = [256, RANK, 128, 256, 3, 176]
    b_out = [_adam(r, w[0], m[0], v[0], name, tr)
             for r, w, m, v, name, tr in zip(recv, big_w, big_m, big_v, big_name, big_rows)]
    b_g, b_d, b_m, b_v = [[o[k][None] for o in b_out] for k in range(4)]

    def ordered(a, sm, bg):
        return [a, sm[0], sm[1], bg[0], bg[1], sm[2], sm[3], bg[2], sm[4], bg[3], bg[4], sm[5], bg[5], sm[6]]

    return (loss, grad_x[None], *ordered(a_g, s_g, b_g), *ordered(a_d, s_d, b_d),
            *ordered(a_m, s_m, b_m), *ordered(a_v, s_v, b_v))
```

```python
import numpy as np

import jax
import jax.numpy as jnp
from jax import lax
from jax.experimental import pallas as pl
from jax.experimental.pallas import tpu as pltpu

F32, BF16 = jnp.float32, jnp.bfloat16
N_DEV = 8
D = 1024
SB_W = 512
GLA_KW, GLA_VW = 256, 512
RANK = 16
IN_W = 3088
IN_WP = 3200
D_FF = 2816
FF_T = 256
N_FT = D_FF // FF_T
EPS = 1e-6
SB_B = 256
SB_DEAD = -110.0
CHUNK = 64
GLA_T = 512
VMEM_LIMIT = 56 * 1024 * 1024

LR, B1, B2, ADAM_EPS, WD, STEP = 0.001, 0.9, 0.999, 1e-08, 0.01, 10


def _params(dims=None, vmem=True):
    kw = {}
    if dims is not None:
        kw["dimension_semantics"] = dims
    if vmem:
        kw["vmem_limit_bytes"] = VMEM_LIMIT
    return pltpu.CompilerParams(**kw)


def _dot(a, b):
    return jnp.dot(a, b, preferred_element_type=F32)


def _dot_nt(a, b):
    return lax.dot_general(a, b, (((1,), (1,)), ((), ())), preferred_element_type=F32)


def _dot_tn(a, b):
    return lax.dot_general(a, b, (((0,), (0,)), ((), ())), preferred_element_type=F32)


def _hilo(x):
    hi = x.astype(BF16)
    lo = (x - hi.astype(F32)).astype(BF16)
    return hi, lo


def _sigmoid(x):
    return 1.0 / (1.0 + jnp.exp(-x))


def _log_sigmoid(x):
    return jnp.minimum(x, 0.0) - jnp.log(1.0 + jnp.exp(-jnp.abs(x)))


def _rms(x, g):
    n = x * lax.rsqrt(jnp.mean(x * x, axis=-1, keepdims=True) + EPS)
    return n * g


def _norm_mod(x, g, scale, shift):
    return _rms(x, g) * (1.0 + scale) + shift


N_PEER = N_DEV - 1


def _exchange_copies(x_refs, out_refs, send_sems, recv_sems, local_sems, gather):
    ix, iy, ic = lax.axis_index("x"), lax.axis_index("y"), lax.axis_index("c")
    me = 4 * ix + 2 * iy + ic
    peers = []
    for k in range(1, N_DEV):
        px = 1 - ix if k & 4 else ix
        py = 1 - iy if k & 2 else iy
        pc = 1 - ic if k & 1 else ic
        peers.append(((px, py, pc), 4 * px + 2 * py + pc))

    def copy(a, k, dev, src_slot, dst_slot):
        return pltpu.make_async_remote_copy(
            src_ref=x_refs[a] if gather else x_refs[a].at[src_slot],
            dst_ref=out_refs[a].at[dst_slot],
            send_sem=send_sems.at[a * N_PEER + k],
            recv_sem=recv_sems.at[a * N_PEER + k],
            device_id=dev,
            device_id_type=pl.DeviceIdType.MESH,
        )

    n = len(x_refs)
    mine = [pltpu.make_async_copy(x_refs[a] if gather else x_refs[a].at[me], out_refs[a].at[me], local_sems.at[a])
            for a in range(n)]
    sends = [copy(a, k, dev, pid, me) for a in range(n) for k, (dev, pid) in enumerate(peers)]
    recvs = [copy(a, k, dev, pid, pid) for a in range(n) for k, (dev, pid) in enumerate(peers)]
    return mine, sends, recvs


def _exchange_scratch(n):
    return [pltpu.SemaphoreType.DMA((n * N_PEER,)), pltpu.SemaphoreType.DMA((n * N_PEER,)),
            pltpu.SemaphoreType.DMA((n,))]


def _exchange_shapes(arrays, gather):
    return [jax.ShapeDtypeStruct((N_DEV,) + tuple(x.shape if gather else x.shape[1:]), x.dtype) for x in arrays]


def _exchange(arrays, name, gather):
    n = len(arrays)

    def body(*refs):
        mine, sends, recvs = _exchange_copies(refs[:n], refs[n:2 * n], *refs[2 * n:], gather)
        for cp in mine + sends:
            cp.start()
        for cp in recvs:
            cp.wait_recv()
        for cp in sends:
            cp.wait_send()
        for cp in mine:
            cp.wait()

    return pl.pallas_call(
        body,
        name=name,
        out_shape=_exchange_shapes(arrays, gather),
        in_specs=[pl.BlockSpec(memory_space=pl.ANY)] * n,
        out_specs=[pl.BlockSpec(memory_space=pl.ANY)] * n,
        scratch_shapes=_exchange_scratch(n),
    )(*arrays)


def _hosted_call(body, comm, *, name, grid, in_specs, out_specs, out_shape, scratch_shapes, dims, args):
    if comm is None:
        outs = pl.pallas_call(body, name=name, grid=grid, in_specs=in_specs, out_specs=out_specs, out_shape=out_shape,
                              scratch_shapes=scratch_shapes, compiler_params=_params(dims))(*args)
        return outs, []
    arrays, gather = comm
    n_in, n_out, n_scr, nc = len(in_specs), len(out_specs), len(scratch_shapes), len(arrays)

    def hosted(*refs):
        ins, c_in = refs[:n_in], refs[n_in:n_in + nc]
        outs, c_out = refs[n_in + nc:n_in + nc + n_out], refs[n_in + nc + n_out:n_in + 2 * nc + n_out]
        rest = refs[n_in + 2 * nc + n_out:]
        scratch, sems = rest[:n_scr], rest[n_scr:]
        mine, sends, recvs = _exchange_copies(c_in, c_out, *sems, gather)
        first = pl.program_id(0) == 0
        last = pl.program_id(0) == grid[0] - 1
        for axis in range(1, len(grid)):
            first = jnp.logical_and(first, pl.program_id(axis) == 0)
            last = jnp.logical_and(last, pl.program_id(axis) == grid[axis] - 1)

        @pl.when(first)
        def _():
            for cp in mine + sends:
                cp.start()

        body(*ins, *outs, *scratch)

        @pl.when(last)
        def _():
            for cp in recvs:
                cp.wait_recv()
            for cp in sends:
                cp.wait_send()
            for cp in mine:
                cp.wait()

    any_spec = pl.BlockSpec(memory_space=pl.ANY)
    outs = pl.pallas_call(
        hosted, name=name, grid=grid,
        in_specs=list(in_specs) + [any_spec] * nc,
        out_specs=list(out_specs) + [any_spec] * nc,
        out_shape=list(out_shape) + _exchange_shapes(arrays, gather),
        scratch_shapes=list(scratch_shapes) + _exchange_scratch(nc),
        compiler_params=_params(tuple("arbitrary" for _ in grid)),
    )(*args, *arrays)
    return outs[:n_out], outs[n_out:]


def _adam_math(g, w, m, v):
    m_new = B1 * m + (1.0 - B1) * g
    v_new = B2 * v + (1.0 - B2) * (g * g)
    m_hat = m_new / (1.0 - B1 ** STEP)
    v_hat = v_new / (1.0 - B2 ** STEP)
    return -LR * (m_hat / (jnp.sqrt(v_hat) + ADAM_EPS) + WD * w), m_new, v_new


def _adam(gparts, w, m, v, name, tr):
    n, rows, cols = gparts.shape

    def body(gp_ref, w_ref, m_ref, v_ref, g_ref, d_ref, nm_ref, nv_ref):
        g = gp_ref[0].astype(F32)
        for j in range(1, n):
            g = g + gp_ref[j].astype(F32)
        g_ref[...] = g
        d_ref[...], nm_ref[...], nv_ref[...] = _adam_math(g, w_ref[...], m_ref[...], v_ref[...])

    blk = pl.BlockSpec((tr, cols), lambda i: (i, 0))
    return pl.pallas_call(
        body,
        name=name,
        grid=(rows // tr,),
        out_shape=[jax.ShapeDtypeStruct((rows, cols), F32)] * 4,
        in_specs=[pl.BlockSpec((n, tr, cols), lambda i: (0, i, 0)), blk, blk, blk],
        out_specs=[blk] * 4,
        compiler_params=_params(("parallel",)),
    )(gparts, w, m, v)


def _adam_rows(parts, ws, ms, vs, name):
    n = parts.shape[0]
    k = len(ws)
    widths = [w.shape[1] for w in ws]

    def body(*refs):
        p_ref, w_refs, m_refs, v_refs = refs[0], refs[1:1 + k], refs[1 + k:1 + 2 * k], refs[1 + 2 * k:1 + 3 * k]
        outs = refs[1 + 3 * k:]
        total = p_ref[0:1, :]
        for j in range(1, n):
            total = total + p_ref[j:j + 1, :]
        off = 0
        for a, width in enumerate(widths):
            g = total[:, off:off + width]
            off += width
            outs[4 * a][...] = g
            outs[4 * a + 1][...], outs[4 * a + 2][...], outs[4 * a + 3][...] = _adam_math(
                g, w_refs[a][...], m_refs[a][...], v_refs[a][...])

    return pl.pallas_call(
        body, name=name,
        out_shape=[jax.ShapeDtypeStruct((1, width), F32) for width in widths for _ in range(4)],
        compiler_params=_params(),
    )(parts, *ws, *ms, *vs)


def _mm(a, b, name, out_dtype, tm, tn, n_out=1):
    m, k = a.shape
    n = b.shape[1] // n_out
    nt = n // tn

    def body(a_ref, *refs):
        a_blk = a_ref[...].astype(BF16)
        for b_ref, o_ref in zip(refs[:n_out], refs[n_out:]):
            o_ref[...] = _dot(a_blk, b_ref[...].astype(BF16)).astype(out_dtype)

    outs = pl.pallas_call(
        body,
        name=name,
        grid=(nt, m // tm),
        out_shape=[jax.ShapeDtypeStruct((m, n), out_dtype)] * n_out,
        in_specs=[pl.BlockSpec((tm, k), lambda j, i: (i, 0))]
        + [pl.BlockSpec((k, tn), lambda j, i, g=g: (0, g * nt + j)) for g in range(n_out)],
        out_specs=[pl.BlockSpec((tm, tn), lambda j, i: (i, j))] * n_out,
        compiler_params=_params(("parallel", "parallel")),
    )(a, *([b] * n_out))
    return outs[0] if n_out == 1 else outs


def _mm_nt(a_list, b, name, out_dtype, tm, tn, comm=None):
    n_a = len(a_list)
    m, k = a_list[0].shape
    n = b.shape[0]

    def body(*refs):
        o_ref = refs[2 * n_a]
        acc = _dot_nt(refs[0][...].astype(BF16), refs[n_a][...].astype(BF16))
        for g in range(1, n_a):
            acc = acc + _dot_nt(refs[g][...].astype(BF16), refs[n_a + g][...].astype(BF16))
        o_ref[...] = acc.astype(out_dtype)

    (out,), got = _hosted_call(
        body, comm, name=name, grid=(n // tn, m // tm),
        out_shape=[jax.ShapeDtypeStruct((m, n), out_dtype)],
        in_specs=[pl.BlockSpec((tm, k), lambda j, i: (i, 0))] * n_a
        + [pl.BlockSpec((tn, k), lambda j, i, g=g: (j, g)) for g in range(n_a)],
        out_specs=[pl.BlockSpec((tm, tn), lambda j, i: (i, j))],
        scratch_shapes=[], dims=("parallel", "parallel"), args=(*a_list, *([b] * n_a)))
    return out, got


def _mm_tn(a, b_list, name, tm, tn, tk):
    s, m = a.shape
    n = b_list[0].shape[1]
    n_b = len(b_list)

    def body(a_ref, *refs):
        b_refs, o_refs = refs[:n_b], refs[n_b:]

        @pl.when(pl.program_id(2) == 0)
        def _():
            for o_ref in o_refs:
                o_ref[...] = jnp.zeros_like(o_ref)

        a_blk = a_ref[...].astype(BF16)
        for b_ref, o_ref in zip(b_refs, o_refs):
            o_ref[...] += _dot_tn(a_blk, b_ref[...].astype(BF16))

    return pl.pallas_call(
        body,
        name=name,
        grid=(n // tn, m // tm, s // tk),
        out_shape=[jax.ShapeDtypeStruct((m, n), F32)] * n_b,
        in_specs=[pl.BlockSpec((tk, tm), lambda j, i, k: (k, i))] + [pl.BlockSpec((tk, tn), lambda j, i, k: (k, j))] * n_b,
        out_specs=[pl.BlockSpec((tm, tn), lambda j, i, k: (i, j))] * n_b,
        compiler_params=_params(("parallel", "parallel", "arbitrary")),
    )(a, *b_list)


def _rows_call(fn, name, rows, params, out_rows, out_accs, ts, comm=None):
    s = rows[0].shape[0]
    nr, npar, no = len(rows), len(params), len(out_rows)

    def body(*refs):
        r, p = refs[:nr], refs[nr:nr + npar]
        o, acc = refs[nr + npar:nr + npar + no], refs[nr + npar + no:]
        outs, sums = fn(*[t[...] for t in r], *[t[...] for t in p])
        for ref, val in zip(o, outs):
            ref[...] = val.astype(ref.dtype)
        if acc:
            @pl.when(pl.program_id(0) == 0)
            def _():
                for ref in acc:
                    ref[...] = jnp.zeros_like(ref)

            for ref, val in zip(acc, sums):
                ref[...] += val

    outs, got = _hosted_call(
        body, comm, name=name, grid=(s // ts,),
        out_shape=[jax.ShapeDtypeStruct((s, w), dt) for w, dt in out_rows]
        + [jax.ShapeDtypeStruct(shape, F32) for shape in out_accs],
        in_specs=[pl.BlockSpec((ts, t.shape[1]), lambda i: (i, 0)) for t in rows]
        + [pl.BlockSpec(t.shape, lambda i: (0, 0)) for t in params],
        out_specs=[pl.BlockSpec((ts, w), lambda i: (i, 0)) for w, _ in out_rows]
        + [pl.BlockSpec(shape, lambda i: (0, 0)) for shape in out_accs],
        scratch_shapes=[], dims=("arbitrary",), args=(*rows, *params))
    return outs if comm is None else (outs, got)


def _fwd_norm1(x, g, scale, shift):
    return (_norm_mod(x, g, scale, shift),), ()


def _resid_norm(x, mixed, gate, g, scale, shift):
    x1 = x + (1.0 + gate) * mixed
    return x1, _norm_mod(x1, g, scale, shift)


def _fwd_resid_norm(x, mixed, gate, g, scale, shift):
    return _resid_norm(x, mixed, gate, g, scale, shift), ()


def _final(x1, ffn, tgt, gate, g):
    def head(x1, ffn, gate, g):
        return _rms(x1 + (1.0 + gate) * ffn, g)

    y, vjp = jax.vjp(head, x1, ffn, gate, g)
    err = y - tgt
    dx2, dffn, dgate, dg = vjp(err * (1.0 / D))
    sq = jnp.sum(jnp.sum(err * err, axis=1, keepdims=True), axis=0, keepdims=True)
    loss = jnp.broadcast_to(sq * (0.5 / D), (1, 128))
    return (dx2, dffn), (dgate, dg, loss)


def _bwd_resid_norm(x, mixed, dx2, dh2, gate, g, scale, shift):
    _, vjp = jax.vjp(_resid_norm, x, mixed, gate, g, scale, shift)
    dx, dmixed, dgate, dg, dscale, dshift = vjp((dx2, dh2))
    return (dx, dmixed), (dgate, dg, dscale, dshift)


def _bwd_norm1(x, dh, dx1, g, scale, shift):
    _, vjp = jax.vjp(_norm_mod, x, g, scale, shift)
    dx, dg, dscale, dshift = vjp(dh)
    return (dx1 + dx,), (dg, dscale, dshift)


def _ada_fwd(c16, w):
    def body(c_ref, w_ref, o_ref):
        c = c_ref[...]
        o_ref[...] = _dot((c * _sigmoid(c)).astype(BF16), w_ref[...].astype(BF16))

    return pl.pallas_call(
        body, name="ada_fwd", out_shape=jax.ShapeDtypeStruct((c16.shape[0], w.shape[1]), F32),
        compiler_params=_params(),
    )(c16, w)


def _ada_bwd(c16, dmod16):
    def body(c_ref, d_ref, o_ref):
        c = c_ref[...]
        o_ref[...] = _dot_tn((c * _sigmoid(c)).astype(BF16), d_ref[...].astype(BF16))

    return pl.pallas_call(
        body, name="ada_bwd", out_shape=jax.ShapeDtypeStruct((c16.shape[1], dmod16.shape[1]), F32),
        compiler_params=_params(),
    )(c16, dmod16)


CONV_R = 256


def _conv_window(win, wc, bc):
    s1 = pltpu.roll(win, 1, 0)
    s2 = pltpu.roll(win, 2, 0)
    u = bc + wc[0:1] * s2 + wc[1:2] * s1 + wc[2:3] * win
    return u[8:], s1[8:], s2[8:], win[8:]


def _row_windows(ref, before, after, n_after, lanes):
    ts, r = ref.shape[0], CONV_R

    def middle(i):
        return ref[pl.ds(pl.multiple_of(i * r, r) - 8, r + 8 + n_after), lanes]

    first = jnp.concatenate([before, ref[0:r + n_after, lanes]], axis=0)
    last = ref[ts - r - 8:ts, lanes] if n_after == 0 else jnp.concatenate([ref[ts - r - 8:ts, lanes], after], axis=0)
    return first, middle, last


def _ffn_specs(ts, s):
    cur = pl.BlockSpec((ts, FF_T), lambda j, i: (i, j))
    halo = pl.BlockSpec((8, FF_T), lambda j, i: (jnp.maximum(i * (ts // 8) - 1, 0), j))
    nxt = pl.BlockSpec((8, FF_T), lambda j, i: (jnp.minimum((i + 1) * (ts // 8), s // 8 - 1), j))
    wc = [pl.BlockSpec((8, FF_T), lambda j, i, h=h: (0, h * N_FT + j)) for h in range(2)]
    bc = [pl.BlockSpec((1, FF_T), lambda j, i, h=h: (0, h * N_FT + j)) for h in range(2)]
    return cur, halo, nxt, wc, bc


def _conv_glu_fwd(up_v, up_g, wc, bc, ts):
    s = up_v.shape[0]
    cur, halo, _, wcs, bcs = _ffn_specs(ts, s)

    def body(v_ref, vh_ref, g_ref, gh_ref, wcv_ref, wcg_ref, bcv_ref, bcg_ref, a_ref):
        keep = jnp.where(pl.program_id(1) == 0, 0.0, 1.0)
        for lanes in (slice(0, 128), slice(128, 256)):
            wcv, wcg, bcv, bcg = wcv_ref[:, lanes], wcg_ref[:, lanes], bcv_ref[:, lanes], bcg_ref[:, lanes]
            first_v, mid_v, _ = _row_windows(v_ref, vh_ref[:, lanes] * keep, None, 0, lanes)
            first_g, mid_g, _ = _row_windows(g_ref, gh_ref[:, lanes] * keep, None, 0, lanes)

            def emit(win_v, win_g, start, lanes=lanes, wcv=wcv, wcg=wcg, bcv=bcv, bcg=bcg):
                val = _conv_window(win_v, wcv, bcv)[0]
                gte = _conv_window(win_g, wcg, bcg)[0]
                a_ref[pl.ds(start, CONV_R), lanes] = (val * (gte * _sigmoid(gte))).astype(BF16)

            emit(first_v, first_g, 0)

            def loop(i, carry, emit=emit, mid_v=mid_v, mid_g=mid_g):
                emit(mid_v(i), mid_g(i), pl.multiple_of(i * CONV_R, CONV_R))
                return carry

            lax.fori_loop(1, ts // CONV_R, loop, 0)

    return pl.pallas_call(
        body, name="conv_glu_fwd", grid=(N_FT, s // ts),
        out_shape=jax.ShapeDtypeStruct((s, D_FF), BF16),
        in_specs=[cur, halo, cur, halo, *wcs, *bcs], out_specs=cur,
        compiler_params=_params(("parallel", "arbitrary")),
    )(up_v, up_v, up_g, up_g, wc, wc, bc, bc)


def _glu_bwd(val, gte, da):
    sg = _sigmoid(gte)
    return da * (gte * sg), da * val * (sg * (1.0 + gte * (1.0 - sg)))


def _conv_glu_bwd(up_v, up_g, da, wc, bc, ts, comm=None):
    s = up_v.shape[0]
    nblk = s // ts
    cur, halo, nxt, wcs, bcs = _ffn_specs(ts, s)
    acc_w = pl.BlockSpec((8, FF_T), lambda j, i: (0, j))
    acc_b = pl.BlockSpec((1, FF_T), lambda j, i: (0, j))

    def body(v_ref, vh_ref, vn_ref, g_ref, gh_ref, gn_ref, da_ref, dan_ref, wcv_ref, wcg_ref, bcv_ref, bcg_ref,
             dupv_ref, dupg_ref, dwcv_ref, dwcg_ref, dbcv_ref, dbcg_ref):
        first = pl.program_id(1) == 0
        keep = jnp.where(first, 0.0, 1.0)
        keep_next = jnp.where(pl.program_id(1) == nblk - 1, 0.0, 1.0)
        r = CONV_R
        n = ts // r

        @pl.when(first)
        def _():
            for ref in (dwcv_ref, dwcg_ref, dbcv_ref, dbcg_ref):
                ref[...] = jnp.zeros_like(ref)

        for lanes in (slice(0, 128), slice(128, 256)):
            wcv, wcg, bcv, bcg = wcv_ref[:, lanes], wcg_ref[:, lanes], bcv_ref[:, lanes], bcg_ref[:, lanes]
            first_v, mid_v, last_v = _row_windows(v_ref, vh_ref[:, lanes] * keep, vn_ref[:, lanes], 8, lanes)
            first_g, mid_g, last_g = _row_windows(g_ref, gh_ref[:, lanes] * keep, gn_ref[:, lanes], 8, lanes)

            def emit(win_v, win_g, da_w, start, lanes=lanes, wcv=wcv, wcg=wcg, bcv=bcv, bcg=bcg):
                u_v, s1_v, s2_v, x_v = _conv_window(win_v, wcv, bcv)
                u_g, s1_g, s2_g, x_g = _conv_window(win_g, wcg, bcg)
                du_v, du_g = _glu_bwd(u_v, u_g, da_w)
                rows = pl.ds(start, r)
                for du, s1, s2, x, wc, dup_ref, dwc_ref, dbc_ref in (
                        (du_v, s1_v, s2_v, x_v, wcv, dupv_ref, dwcv_ref, dbcv_ref),
                        (du_g, s1_g, s2_g, x_g, wcg, dupg_ref, dwcg_ref, dbcg_ref)):
                    dup = wc[2:3] * du + wc[1:2] * pltpu.roll(du, r + 7, 0) + wc[0:1] * pltpu.roll(du, r + 6, 0)
                    dup_ref[rows, lanes] = dup[:r].astype(BF16)
                    du = du[:r]
                    dwc_ref[0:1, lanes] += jnp.sum(du * s2[:r], axis=0, keepdims=True)
                    dwc_ref[1:2, lanes] += jnp.sum(du * s1[:r], axis=0, keepdims=True)
                    dwc_ref[2:3, lanes] += jnp.sum(du * x[:r], axis=0, keepdims=True)
                    dbc_ref[:, lanes] += jnp.sum(du, axis=0, keepdims=True)

            emit(first_v, first_g, da_ref[0:r + 8, lanes], 0)

            def loop(i, carry, emit=emit, mid_v=mid_v, mid_g=mid_g, lanes=lanes):
                start = pl.multiple_of(i * r, r)
                emit(mid_v(i), mid_g(i), da_ref[pl.ds(start, r + 8), lanes], start)
                return carry

            lax.fori_loop(1, n - 1, loop, 0)
            da_last = jnp.concatenate([da_ref[ts - r:ts, lanes], dan_ref[:, lanes] * keep_next], axis=0)
            emit(last_v, last_g, da_last, ts - r)

    return _hosted_call(
        body, comm, name="conv_glu_bwd", grid=(N_FT, nblk),
        out_shape=[jax.ShapeDtypeStruct((s, D_FF), BF16)] * 2 + [jax.ShapeDtypeStruct((8, D_FF), F32)] * 2
        + [jax.ShapeDtypeStruct((1, D_FF), F32)] * 2,
        in_specs=[cur, halo, nxt, cur, halo, nxt, cur, nxt, *wcs, *bcs],
        out_specs=[cur, cur, acc_w, acc_w, acc_b, acc_b],
        scratch_shapes=[], dims=("parallel", "arbitrary"),
        args=(up_v, up_v, up_v, up_g, up_g, up_g, da, da, wc, wc, bc, bc))


def _tri(kind):
    b = SB_B
    m = {"lower_strict": np.tril(np.ones((b, b)), -1), "upper_incl": np.triu(np.ones((b, b)), 0),
         "upper_strict": np.triu(np.ones((b, b)), 1)}[kind]
    return jnp.asarray(np.concatenate([m, np.ones((b, 128))], axis=1), BF16)


def _key_sums(x, tri):
    hi, lo = _hilo(x)
    cb = _dot(hi, tri) + _dot(lo, tri)
    return cb[:, :SB_B], cb[:, SB_B:]


def _sb_fwd(proj, comm=None):
    s = proj.shape[0]
    b = SB_B

    def body(q_ref, k_ref, v_ref, tri_ref, o_ref, t_ref, first_ref, c_ref, a_ref):
        i = pl.program_id(1)
        lane = lax.broadcasted_iota(jnp.int32, (b, 128), 1)
        heads = (lane < 64, lane >= 64)
        causal = lax.broadcasted_iota(jnp.int32, (b, b), 1) < lax.broadcasted_iota(jnp.int32, (b, b), 0)
        q = q_ref[...] * 0.125
        qm = [jnp.where(h, q, 0.0).astype(BF16) for h in heads]
        c_ref[...] = jnp.zeros_like(c_ref)
        a_ref[...] = jnp.zeros_like(a_ref)

        def prepare(jj, masked):
            rows = pl.ds(pl.multiple_of(jj * b, b), b)
            kb = k_ref[rows, :].astype(BF16)
            vb = v_ref[rows, :]
            out = []
            for hh in range(2):
                z = _dot_nt(qm[hh], kb)
                lg = _log_sigmoid(-z)
                if masked:
                    lg = jnp.where(causal, lg, 0.0)
                after, total = _key_sums(lg, tri_ref[...])
                out.append((lg + z + after, total, jnp.where(heads[hh], vb, 0.0).astype(BF16)))
            return out

        def walk_blocks(blocks):
            for hh in range(2):
                c, a = c_ref[hh], a_ref[hh]
                for pre, masked in blocks:
                    logw, total, vm = pre[hh]
                    w = jnp.exp(logw + jnp.concatenate([c, c], axis=1))
                    if masked:
                        w = jnp.where(causal, w, 0.0)
                    a = a + _dot(w.astype(BF16), vm)
                    c = c + total
                c_ref[hh], a_ref[hh] = c, a

        def largest_sum():
            return jnp.max(jnp.maximum(c_ref[0], c_ref[1]))

        @pl.when(i == 0)
        def _():
            walk_blocks([(prepare(i, True), True)])

        @pl.when(i > 0)
        def _():
            walk_blocks([(prepare(i, True), True), (prepare(i - 1, False), False)])

        def more(state):
            jj, top = state
            return jnp.logical_and(jj >= 0, top > SB_DEAD)

        def walk(state):
            jj, _ = state
            walk_blocks([(prepare(jj, False), False)])
            return jj - 1, largest_sum()

        jj, _ = lax.while_loop(more, walk, (jnp.maximum(i - 2, -1), largest_sum()))
        o_ref[...] = (a_ref[0] + a_ref[1]).astype(BF16)
        t_ref[...] = jnp.concatenate([c_ref[0], c_ref[1]], axis=1)
        first_ref[pl.program_id(0), i] = (jj + 1).astype(F32)

    return _hosted_call(
        body, comm, name="sb_fwd", grid=(4, s // b),
        out_shape=[jax.ShapeDtypeStruct((s, SB_W), BF16), jax.ShapeDtypeStruct((s, 2 * SB_W), F32),
                   jax.ShapeDtypeStruct((4, s // b), F32)],
        in_specs=[pl.BlockSpec((b, 128), lambda p, i: (i, p)),
                  pl.BlockSpec((s, 128), lambda p, i: (0, 4 + p)),
                  pl.BlockSpec((s, 128), lambda p, i: (0, 8 + p)),
                  pl.BlockSpec((b, b + 128), lambda p, i: (0, 0))],
        out_specs=[pl.BlockSpec((b, 128), lambda p, i: (i, p)), pl.BlockSpec((b, 256), lambda p, i: (i, p)),
                   pl.BlockSpec(memory_space=pltpu.SMEM)],
        scratch_shapes=[pltpu.VMEM((2, b, 128), F32), pltpu.VMEM((2, b, 128), F32)],
        dims=("arbitrary", "arbitrary"), args=(proj, proj, proj, _tri("lower_strict")))


def _sb_bwd(proj, do, tot, first, comm=None):
    s = proj.shape[0]
    b = SB_B

    def body(q_ref, k_ref, v_ref, do_ref, t_ref, first_ref, ti_ref, ts_ref, dq_ref, dko_ref, dvo_ref, cl_ref, ce_ref,
             a_ref, dk_ref, dv_ref):
        i = pl.program_id(1)
        first = jnp.clip(first_ref[pl.program_id(0), i].astype(jnp.int32), 0, i)
        lane = lax.broadcasted_iota(jnp.int32, (b, 128), 1)
        heads = (lane < 64, lane >= 64)
        causal = lax.broadcasted_iota(jnp.int32, (b, b), 1) < lax.broadcasted_iota(jnp.int32, (b, b), 0)
        q = q_ref[...] * 0.125
        do = do_ref[...]
        qm = [jnp.where(h, q, 0.0).astype(BF16) for h in heads]
        dom = [jnp.where(h, do, 0.0).astype(BF16) for h in heads]
        cl_ref[...] = jnp.zeros_like(cl_ref)
        ce_ref[...] = jnp.zeros_like(ce_ref)
        a_ref[...] = jnp.zeros_like(a_ref)

        @pl.when(i == 0)
        def _():
            dk_ref[...] = jnp.zeros_like(dk_ref)
            dv_ref[...] = jnp.zeros_like(dv_ref)

        def prepare(jj, masked):
            rows = pl.ds(pl.multiple_of(jj * b, b), b)
            kf = k_ref[rows, :]
            kb = kf.astype(BF16)
            vb = v_ref[rows, :].astype(BF16)
            out = []
            for hh in range(2):
                z = _dot_nt(qm[hh], kb)
                lg = _log_sigmoid(-z)
                if masked:
                    lg = jnp.where(causal, lg, 0.0)
                upto, total = _key_sums(lg, ti_ref[...])
                lsz = lg + z
                out.append((lsz - upto, total, jnp.exp(lsz), _dot_nt(dom[hh], vb),
                            jnp.where(heads[hh], kf, 0.0).astype(BF16)))
            return rows, out

        def walk_blocks(blocks):
            grads = [[jnp.zeros((b, 128), F32), jnp.zeros((b, 128), F32)] for _ in blocks]
            for hh in range(2):
                cl, ce, a = cl_ref[hh], ce_ref[hh], a_ref[hh]
                t = t_ref[:, hh * 128:(hh + 1) * 128]
                for n, ((_, pre), masked) in enumerate(blocks):
                    logw, total, sig, dw, km = pre[hh]
                    w = jnp.exp(logw + jnp.concatenate([t - cl, t - cl], axis=1))
                    if masked:
                        w = jnp.where(causal, w, 0.0)
                    e = w * dw
                    sums = _dot(e.astype(BF16), ts_ref[...])
                    before, etot = sums[:, :SB_B], sums[:, SB_B:]
                    dz = e - sig * (e + before + jnp.concatenate([ce, ce], axis=1))
                    if masked:
                        dz = jnp.where(causal, dz, 0.0)
                    dzb = dz.astype(BF16)
                    a = a + _dot(dzb, km)
                    grads[n][0] = grads[n][0] + _dot_tn(dzb, qm[hh])
                    grads[n][1] = grads[n][1] + _dot_tn(w.astype(BF16), dom[hh])
                    cl = cl + total
                    ce = ce + etot
                cl_ref[hh], ce_ref[hh], a_ref[hh] = cl, ce, a
            for ((rows, _), _), (dk, dv) in zip(blocks, grads):
                dk_ref[rows, :] += dk
                dv_ref[rows, :] += dv

        def loop(jj, carry):
            walk_blocks([(prepare(jj, False), False)])
            return carry

        lax.fori_loop(first, i - 1, loop, 0)

        @pl.when(i == 0)
        def _():
            walk_blocks([(prepare(i, True), True)])

        @pl.when(i > 0)
        def _():
            walk_blocks([(prepare(i - 1, False), False), (prepare(i, True), True)])

        dq_ref[...] = ((a_ref[0] + a_ref[1]) * 0.125).astype(BF16)

        @pl.when(i == s // b - 1)
        def _():
            dko_ref[...] = dk_ref[...].astype(BF16)
            dvo_ref[...] = dv_ref[...].astype(BF16)

    blk = pl.BlockSpec((b, 128), lambda p, i: (i, p))
    full = pl.BlockSpec((s, 128), lambda p, i: (0, p))
    tri = pl.BlockSpec((b, b + 128), lambda p, i: (0, 0))
    return _hosted_call(
        body, comm, name="sb_bwd", grid=(4, s // b),
        out_shape=[jax.ShapeDtypeStruct((s, SB_W), BF16)] * 3,
        in_specs=[blk, pl.BlockSpec((s, 128), lambda p, i: (0, 4 + p)), pl.BlockSpec((s, 128), lambda p, i: (0, 8 + p)),
                  blk, pl.BlockSpec((b, 256), lambda p, i: (i, p)), pl.BlockSpec(memory_space=pltpu.SMEM), tri, tri],
        out_specs=[blk, full, full],
        scratch_shapes=[pltpu.VMEM((2, b, 128), F32)] * 3 + [pltpu.VMEM((s, 128), F32)] * 2,
        dims=("arbitrary", "arbitrary"),
        args=(proj, proj, proj, do, tot, first, _tri("upper_incl"), _tri("upper_strict")))


_COL_Q, _COL_K, _COL_V, _COL_G, _COL_F = 12, 14, 8, 10, 24


def _gla_consts():
    c = CHUNK
    incl = np.tril(np.ones((c, c)), 0)
    strict = np.tril(np.ones((c, c)), -1)
    bd = np.zeros((128, 256))
    bd[:64, :128] = 1.0
    bd[64:, 128:] = 1.0
    return jnp.asarray(incl, BF16), jnp.asarray(strict, BF16), jnp.asarray(bd, F32)


def _time_sums(tri, x):
    hi, lo = _hilo(x)
    return _dot(tri, hi) + _dot(tri, lo)


def _gla_gate(o, gg, g):
    parts = []
    for h in range(2):
        oh = o[:, h * 128:(h + 1) * 128]
        parts.append(oh * lax.rsqrt(jnp.mean(oh * oh, axis=-1, keepdims=True) + EPS))
    return (jnp.concatenate(parts, axis=1) * g) * (gg * _sigmoid(gg))


def _gla_chunk(la_c, k_c, incl, ones_cv):
    cum = _time_sums(incl, la_c)
    total = cum[CHUNK - 1:CHUNK]
    edec = jnp.exp(total - cum)
    kdec = k_c * edec
    hi, lo = _hilo(la_c)
    dec = jnp.exp(_dot_tn(hi, ones_cv) + _dot_tn(lo, ones_cv))
    return edec, kdec, dec


def _gla_fwd(proj, wfg, bfg, gain):
    s = proj.shape[0]
    t = GLA_T
    nch = t // CHUNK
    incl, _, bd = _gla_consts()

    def body(q_ref, k_ref, v_ref, gg_ref, f_ref, wf_ref, bf_ref, g_ref, incl_ref, bd_ref, o_ref, st_ref, state_ref):
        @pl.when(pl.program_id(0) == 0)
        def _():
            state_ref[...] = jnp.zeros_like(state_ref)

        la = _log_sigmoid(_dot(f_ref[...].astype(BF16), wf_ref[...].astype(BF16)) + bf_ref[...]) * (1.0 / 16.0)
        ones_cv = jnp.ones((CHUNK, 256), BF16)
        states = [state_ref[0], state_ref[1]]
        for cc in range(nch):
            rows = slice(cc * CHUNK, (cc + 1) * CHUNK)
            for p in range(2):
                kl, vl = slice(p * 128, (p + 1) * 128), slice(p * 256, (p + 1) * 256)
                _, kdec, dec = _gla_chunk(la[rows, kl], k_ref[rows, kl], incl_ref[...], ones_cv)
                kv = _dot_tn(kdec.astype(BF16), v_ref[rows, vl].astype(BF16))
                states[p] = dec * states[p] + bd_ref[...] * kv
                st_ref[p, cc] = states[p]
                o = _dot((q_ref[rows, kl] * 0.125).astype(BF16), states[p].astype(BF16))
                o_ref[rows, vl] = _gla_gate(o, gg_ref[rows, vl], g_ref[:, vl]).astype(BF16)
        state_ref[0] = states[0]
        state_ref[1] = states[1]

    def col(width, blk):
        return pl.BlockSpec((t, width), lambda i: (i, blk))

    def whole(shape):
        return pl.BlockSpec(shape, lambda i: tuple(0 for _ in shape))

    return pl.pallas_call(
        body, name="gla_fwd", grid=(s // t,),
        out_shape=[jax.ShapeDtypeStruct((s, GLA_VW), BF16), jax.ShapeDtypeStruct((2, s // CHUNK, 128, 256), F32)],
        in_specs=[col(256, _COL_Q // 2), col(256, _COL_K // 2), col(512, _COL_V // 2), col(512, _COL_G // 2),
                  col(128, _COL_F), whole((128, GLA_KW)), whole((1, GLA_KW)), whole((1, GLA_VW)),
                  whole((CHUNK, CHUNK)), whole((128, 256))],
        out_specs=[col(512, 0), pl.BlockSpec((2, nch, 128, 256), lambda i: (0, i, 0, 0))],
        scratch_shapes=[pltpu.VMEM((2, 128, 256), F32)],
        compiler_params=_params(("arbitrary",)),
    )(proj, proj, proj, proj, proj, wfg, bfg, gain, incl, bd)


def _gla_bwd(proj, do, states, wfg, bfg, gain):
    s = proj.shape[0]
    t = GLA_T
    nch = t // CHUNK
    nblk = s // t
    incl, strict, bd = _gla_consts()

    def body(q_ref, k_ref, v_ref, gg_ref, f_ref, do_ref, st_ref, sp_ref, wf_ref, bf_ref, g_ref, incl_ref, str_ref,
             bd_ref, dq_ref, dk_ref, dv_ref, dgg_ref, df_ref, dwf_ref, dbf_ref, dg_ref, carry_ref, dfs_ref):
        i = pl.program_id(0)

        @pl.when(i == 0)
        def _():
            carry_ref[...] = jnp.zeros_like(carry_ref)
            dwf_ref[...] = jnp.zeros_like(dwf_ref)
            dbf_ref[...] = jnp.zeros_like(dbf_ref)
            dg_ref[...] = jnp.zeros_like(dg_ref)

        fb = f_ref[...].astype(BF16)
        wf = wf_ref[...].astype(BF16)
        f = _dot(fb, wf) + bf_ref[...]
        la = _log_sigmoid(f) * (1.0 / 16.0)
        dla_df = _sigmoid(-f) * (1.0 / 16.0)
        ones_cv = jnp.ones((CHUNK, 256), BF16)
        ones_8v = jnp.ones((8, 256), BF16)
        first_block = jnp.where(i == nblk - 1, 0.0, 1.0)
        carries = [carry_ref[0], carry_ref[1]]
        dgains = [jnp.zeros((1, 256), F32), jnp.zeros((1, 256), F32)]
        for cc in reversed(range(nch)):
            rows = slice(cc * CHUNK, (cc + 1) * CHUNK)
            for p in range(2):
                kl, vl = slice(p * 128, (p + 1) * 128), slice(p * 256, (p + 1) * 256)
                edec, kdec, dec = _gla_chunk(la[rows, kl], k_ref[rows, kl], incl_ref[...], ones_cv)
                state = st_ref[p, cc]
                prev = st_ref[p, cc - 1] if cc > 0 else sp_ref[p, 0] * first_block
                qs = (q_ref[rows, kl] * 0.125).astype(BF16)
                sb16 = state.astype(BF16)
                o = _dot(qs, sb16)
                _, vjp = jax.vjp(_gla_gate, o, gg_ref[rows, vl], g_ref[:, vl])
                do_c, dgg_c, dg_c = vjp(do_ref[rows, vl])
                dgains[p] = dgains[p] + dg_c
                dgg_ref[rows, vl] = dgg_c.astype(BF16)
                do16 = do_c.astype(BF16)
                dq_ref[rows, kl] = (_dot_nt(do16, sb16) * 0.125).astype(BF16)
                grad = bd_ref[...] * _dot_tn(qs, do16) + carries[p]
                g16 = grad.astype(BF16)
                dv_ref[rows, vl] = _dot(kdec.astype(BF16), g16).astype(BF16)
                dkdec = _dot_nt(v_ref[rows, vl].astype(BF16), g16)
                hi, lo = _hilo(grad * prev * dec)
                ddec = (_dot_nt(ones_8v, hi) + _dot_nt(ones_8v, lo))[0:1]
                dk_ref[rows, kl] = (dkdec * edec).astype(BF16)
                dla = _time_sums(str_ref[...], dkdec * kdec) + ddec
                dfs_ref[rows, kl] = dla * dla_df[rows, kl]
                carries[p] = dec * grad
        carry_ref[0] = carries[0]
        carry_ref[1] = carries[1]
        df = dfs_ref[...]
        df16 = df.astype(BF16)
        df_ref[...] = _dot_nt(df16, wf).astype(BF16)
        dwf_ref[...] += _dot_tn(fb, df16)
        dbf_ref[...] += jnp.sum(df, axis=0, keepdims=True)
        dg_ref[...] += jnp.concatenate(dgains, axis=1)

    def col(width, blk):
        return pl.BlockSpec((t, width), lambda i: (nblk - 1 - i, blk))

    def whole(shape):
        return pl.BlockSpec(shape, lambda i: tuple(0 for _ in shape))

    return pl.pallas_call(
        body, name="gla_bwd", grid=(nblk,),
        out_shape=[jax.ShapeDtypeStruct((s, GLA_KW), BF16), jax.ShapeDtypeStruct((s, GLA_KW), BF16),
                   jax.ShapeDtypeStruct((s, GLA_VW), BF16), jax.ShapeDtypeStruct((s, GLA_VW), BF16),
                   jax.ShapeDtypeStruct((s, 128), BF16), jax.ShapeDtypeStruct((128, GLA_KW), F32),
                   jax.ShapeDtypeStruct((1, GLA_KW), F32), jax.ShapeDtypeStruct((1, GLA_VW), F32)],
        in_specs=[col(256, _COL_Q // 2), col(256, _COL_K // 2), col(512, _COL_V // 2), col(512, _COL_G // 2),
                  col(128, _COL_F), col(512, 1),
                  pl.BlockSpec((2, nch, 128, 256), lambda i: (0, nblk - 1 - i, 0, 0)),
                  pl.BlockSpec((2, 1, 128, 256), lambda i: (0, jnp.maximum((nblk - 1 - i) * nch - 1, 0), 0, 0)),
                  whole((128, GLA_KW)), whole((1, GLA_KW)), whole((1, GLA_VW)),
                  whole((CHUNK, CHUNK)), whole((CHUNK, CHUNK)), whole((128, 256))],
        out_specs=[col(256, 0), col(256, 0), col(512, 0), col(512, 0), col(128, 0),
                   whole((128, GLA_KW)), whole((1, GLA_KW)), whole((1, GLA_VW))],
        scratch_shapes=[pltpu.VMEM((2, 128, 256), F32), pltpu.VMEM((t, GLA_KW), F32)],
        compiler_params=_params(("arbitrary",)),
    )(proj, proj, proj, proj, proj, do, states, states, wfg, bfg, gain, incl, strict, bd)


def _cols_by_dev(a, n_dev=N_DEV):
    r, c = a.shape
    return a.reshape(r, n_dev, c // n_dev).transpose(1, 0, 2)


def _cols_from_dev(a):
    _, r, n = a.shape
    return a.transpose(1, 0, 2).reshape(r, N_DEV * n)


def kernel(x, c, w_ada, b_ada, g_norm1, w_in, w_fg2, b_fg2, g_gla_out, w_out, g_norm2, w_up, w_conv, b_conv, w_down, g_final, loss_target, m_w_ada, m_b_ada, m_g_norm1, m_w_in, m_w_fg2, m_b_fg2, m_g_gla_out, m_w_out, m_g_norm2, m_w_up, m_w_conv, m_b_conv, m_w_down, m_g_final, v_w_ada, v_b_ada, v_g_norm1, v_w_in, v_w_fg2, v_b_fg2, v_g_gla_out, v_w_out, v_g_norm2, v_w_up, v_w_conv, v_b_conv, v_w_down, v_g_final):
    s = x.shape[1]
    me = 4 * lax.axis_index("x") + 2 * lax.axis_index("y") + lax.axis_index("c")
    xs, tgt = x[0], loss_target[0]
    ts = min(512, s)
    tm = min(1024, s)
    tc = min(2048, s // 2)

    c_all, wconv_g, wfg_g = _exchange([c, w_conv[0], w_fg2[0]], "gather_first", True)
    wconv_p = jnp.pad(_cols_from_dev(wconv_g), ((0, 5), (0, 0)))
    wfg_p = jnp.pad(_cols_from_dev(wfg_g), ((0, 128 - RANK), (0, 0)))

    c16 = jnp.pad(c_all.reshape(N_DEV, D), ((0, 8), (0, 0)))
    modp = _ada_fwd(c16, w_ada[0])[:N_DEV]
    (mod_all,) = _exchange([modp], "gather_mod", True)
    mod = lax.dynamic_index_in_dim(mod_all, me, axis=1, keepdims=False).reshape(1, 6 * D) + b_ada
    shift1, scale1, gate1, shift2, scale2, gate2 = [mod[:, k * D:(k + 1) * D] for k in range(6)]

    (h,), (win_g,) = _rows_call(_fwd_norm1, "norm1", [xs], [g_norm1, scale1, shift1], [(D, BF16)], [], ts,
                                comm=([w_in[0].astype(BF16)], True))
    win = jnp.pad(_cols_from_dev(win_g), ((0, 0), (0, IN_WP - IN_W)))
    proj = _mm(h, win, "in_proj", F32, ts, IN_WP)
    (o_sb, sb_tot, sb_first), (wout_g, wup_g, wdown_g) = _sb_fwd(
        proj, comm=([w_out[0].astype(BF16), w_up[0].astype(BF16), w_down[0].astype(BF16)], True))
    wout = wout_g.reshape(D, D)
    wup = _cols_from_dev(wup_g)
    wdown = wdown_g.reshape(D_FF, D)
    o_gla, states = _gla_fwd(proj, wfg_p, b_fg2, g_gla_out)
    cat = jnp.concatenate([o_sb, o_gla], axis=1)
    mixed = _mm(cat, wout, "out_proj", F32, tm, 512)
    x1, h2 = _rows_call(_fwd_resid_norm, "resid_norm2", [xs, mixed], [gate1, g_norm2, scale2, shift2],
                        [(D, F32), (D, BF16)], [], ts)
    up_v, up_g = _mm(h2, wup, "up_proj", F32, tm, 1408, n_out=2)
    act = _conv_glu_fwd(up_v, up_g, wconv_p, b_conv, tc)
    ffn = _mm(act, wdown, "down_proj", F32, tm, 512)
    dx2, dffn, dgate2, dg_final, loss_part = _rows_call(
        _final, "final", [x1, ffn, tgt], [gate2, g_final.reshape(1, D)],
        [(D, F32), (D, BF16)], [(1, D), (1, D), (1, 128)], ts)

    (dw_down,) = _mm_tn(act, [dffn], "down_wgrad", 1408, 1024, tm)
    da, _ = _mm_nt([dffn], wdown, "down_dgrad", F32, tm, 1408)
    (dup_v, dup_g, dwc_v, dwc_g, dbc_v, dbc_g), (r_down,) = _conv_glu_bwd(
        up_v, up_g, da, wconv_p, b_conv, tc, comm=([dw_down.reshape(N_DEV, 352, D)], False))
    dw_up_v, dw_up_g = _mm_tn(h2, [dup_v, dup_g], "up_wgrad", 1024, 1408, tm)
    dh2, _ = _mm_nt([dup_v, dup_g], wup, "up_dgrad", F32, tm, 512)
    dx1, dmixed, dgate1, dg_norm2, dscale2, dshift2 = _rows_call(
        _bwd_resid_norm, "resid_norm2_bwd", [xs, mixed, dx2, dh2], [gate1, g_norm2, scale2, shift2],
        [(D, F32), (D, BF16)], [(1, D)] * 4, ts)
    (dw_out,) = _mm_tn(cat, [dmixed], "out_wgrad", 1024, 1024, tm)
    dcat, _ = _mm_nt([dmixed], wout, "out_dgrad", F32, tm, 512)
    dw_up_s = jnp.concatenate([_cols_by_dev(dw_up_v, 4), _cols_by_dev(dw_up_g, 4)], axis=0)
    dwconv_s = _cols_by_dev(jnp.concatenate([dwc_v[:3], dwc_g[:3]], axis=1))
    (dq, dk, dv), (r_up, r_out, r_conv) = _sb_bwd(
        proj, dcat, sb_tot, sb_first, comm=([dw_up_s, dw_out.reshape(N_DEV, 128, D), dwconv_s], False))
    dgq, dgk, dgv, dgg, dgf, dwfg_p, dbfg, dg_gla = _gla_bwd(proj, dcat, states, wfg_p, b_fg2, g_gla_out)
    dproj = jnp.concatenate([dq, dk, dv, dgq, dgk, dgv, dgg, dgf], axis=1)
    (dw_in,) = _mm_tn(h, [dproj], "in_wgrad", 1024, IN_WP, tm)
    dh, (r_in, r_fg) = _mm_nt([dproj], win, "in_dgrad", F32, tm, 512,
                              comm=([_cols_by_dev(dw_in[:, :IN_W]).astype(BF16), _cols_by_dev(dwfg_p[:RANK])], False))
    grad_x, dg_norm1, dscale1, dshift1 = _rows_call(
        _bwd_norm1, "norm1_bwd", [xs, dh, dx1], [g_norm1, scale1, shift1], [(D, F32)], [(1, D)] * 3, ts)

    dmod = jnp.concatenate([dshift1, dscale1, dgate1, dshift2, dscale2, dgate2], axis=1)
    small_parts = [dmod, dg_norm1, dbfg, dg_gla, dg_norm2, dbc_v, dbc_g, dg_final, loss_part]
    n_small = sum(p.shape[1] for p in small_parts[:-1])
    (sg,) = _exchange([jnp.concatenate(small_parts, axis=1)], "gather_small_grads", True)
    sg = sg.reshape(N_DEV, n_small + 128)
    loss = jnp.sum(sg[:, n_small])
    small_w = [b_ada, g_norm1, b_fg2, g_gla_out, g_norm2, b_conv, g_final.reshape(1, D)]
    small_m = [m_b_ada, m_g_norm1, m_b_fg2, m_g_gla_out, m_g_norm2, m_b_conv, m_g_final.reshape(1, D)]
    small_v = [v_b_ada, v_g_norm1, v_b_fg2, v_g_gla_out, v_g_norm2, v_b_conv, v_g_final.reshape(1, D)]
    s_out = list(_adam_rows(sg, small_w, small_m, small_v, "adam_small"))
    s_out[24:] = [t.reshape(D) for t in s_out[24:]]
    s_g, s_d, s_m, s_v = [s_out[k::4] for k in range(4)]

    dmod_all = sg[:, :6 * D].reshape(N_DEV, N_DEV, 768)
    dmod_mine = lax.dynamic_index_in_dim(dmod_all, me, axis=1, keepdims=False)
    dw_ada = _ada_bwd(c16, jnp.pad(dmod_mine, ((0, 8), (0, 0))))
    a_g, a_d, a_m, a_v = [t[None] for t in _adam(dw_ada[None], w_ada[0], m_w_ada[0], v_w_ada[0], "adam_ada", 256)]

    recv = [r_in, r_fg, r_out, r_up, r_conv, r_down]
    big_w = [w_in, w_fg2, w_out, w_up, w_conv, w_down]
    big_m = [m_w_in, m_w_fg2, m_w_out, m_w_up, m_w_conv, m_w_down]
    big_v = [v_w_in, v_w_fg2, v_w_out, v_w_up, v_w_conv, v_w_down]
    big_name = ["adam_in", "adam_fg2", "adam_out", "adam_up", "adam_conv", "adam_down"]
    big_rows = [256, RANK, 128, 256, 3, 176]
    b_out = [_adam(r, w[0], m[0], v[0], name, tr)
             for r, w, m, v, name, tr in zip(recv, big_w, big_m, big_v, big_name, big_rows)]
    b_g, b_d, b_m, b_v = [[o[k][None] for o in b_out] for k in range(4)]

    def ordered(a, sm, bg):
        return [a, sm[0], sm[1], bg[0], bg[1], sm[2], sm[3], bg[2], sm[4], bg[3], bg[4], sm[5], bg[5], sm[6]]

    return (loss, grad_x[None], *ordered(a_g, s_g, b_g), *ordered(a_d, s_d, b_d),
            *ordered(a_m, s_m, b_m), *ordered(a_v, s_v, b_v))
```

```python
import numpy as np

import jax
import jax.numpy as jnp
from jax import lax
from jax.experimental import pallas as pl
from jax.experimental.pallas import tpu as pltpu

F32, BF16 = jnp.float32, jnp.bfloat16
N_DEV = 8
D = 1024
SB_W = 512
GLA_KW, GLA_VW = 256, 512
RANK = 16
IN_W = 3088
IN_WP = 3200
D_FF = 2816
FF_T = 256
N_FT = D_FF // FF_T
EPS = 1e-6
SB_B = 256
SB_DEAD = -110.0
CHUNK = 64
GLA_T = 512
VMEM_LIMIT = 56 * 1024 * 1024

LR, B1, B2, ADAM_EPS, WD, STEP = 0.001, 0.9, 0.999, 1e-08, 0.01, 10


def _params(dims=None, vmem=True):
    kw = {}
    if dims is not None:
        kw["dimension_semantics"] = dims
    if vmem:
        kw["vmem_limit_bytes"] = VMEM_LIMIT
    return pltpu.CompilerParams(**kw)


def _dot(a, b):
    return jnp.dot(a, b, preferred_element_type=F32)


def _dot_nt(a, b):
    return lax.dot_general(a, b, (((1,), (1,)), ((), ())), preferred_element_type=F32)


def _dot_tn(a, b):
    return lax.dot_general(a, b, (((0,), (0,)), ((), ())), preferred_element_type=F32)


def _hilo(x):
    hi = x.astype(BF16)
    lo = (x - hi.astype(F32)).astype(BF16)
    return hi, lo


def _sigmoid(x):
    return 1.0 / (1.0 + jnp.exp(-x))


def _log_sigmoid(x):
    return jnp.minimum(x, 0.0) - jnp.log(1.0 + jnp.exp(-jnp.abs(x)))


def _rms(x, g):
    n = x * lax.rsqrt(jnp.mean(x * x, axis=-1, keepdims=True) + EPS)
    return n * g


def _norm_mod(x, g, scale, shift):
    return _rms(x, g) * (1.0 + scale) + shift


N_PEER = N_DEV - 1


def _exchange_copies(x_refs, out_refs, send_sems, recv_sems, local_sems, gather):
    ix, iy, ic = lax.axis_index("x"), lax.axis_index("y"), lax.axis_index("c")
    me = 4 * ix + 2 * iy + ic
    peers = []
    for k in range(1, N_DEV):
        px = 1 - ix if k & 4 else ix
        py = 1 - iy if k & 2 else iy
        pc = 1 - ic if k & 1 else ic
        peers.append(((px, py, pc), 4 * px + 2 * py + pc))

    def copy(a, k, dev, src_slot, dst_slot):
        return pltpu.make_async_remote_copy(
            src_ref=x_refs[a] if gather else x_refs[a].at[src_slot],
            dst_ref=out_refs[a].at[dst_slot],
            send_sem=send_sems.at[a * N_PEER + k],
            recv_sem=recv_sems.at[a * N_PEER + k],
            device_id=dev,
            device_id_type=pl.DeviceIdType.MESH,
        )

    n = len(x_refs)
    mine = [pltpu.make_async_copy(x_refs[a] if gather else x_refs[a].at[me], out_refs[a].at[me], local_sems.at[a])
            for a in range(n)]
    sends = [copy(a, k, dev, pid, me) for a in range(n) for k, (dev, pid) in enumerate(peers)]
    recvs = [copy(a, k, dev, pid, pid) for a in range(n) for k, (dev, pid) in enumerate(peers)]
    return mine, sends, recvs


def _exchange_scratch(n):
    return [pltpu.SemaphoreType.DMA((n * N_PEER,)), pltpu.SemaphoreType.DMA((n * N_PEER,)),
            pltpu.SemaphoreType.DMA((n,))]


def _exchange_shapes(arrays, gather):
    return [jax.ShapeDtypeStruct((N_DEV,) + tuple(x.shape if gather else x.shape[1:]), x.dtype) for x in arrays]


def _exchange(arrays, name, gather):
    n = len(arrays)

    def body(*refs):
        mine, sends, recvs = _exchange_copies(refs[:n], refs[n:2 * n], *refs[2 * n:], gather)
        for cp in mine + sends:
            cp.start()
        for cp in recvs:
            cp.wait_recv()
        for cp in sends:
            cp.wait_send()
        for cp in mine:
            cp.wait()

    return pl.pallas_call(
        body,
        name=name,
        out_shape=_exchange_shapes(arrays, gather),
        in_specs=[pl.BlockSpec(memory_space=pl.ANY)] * n,
        out_specs=[pl.BlockSpec(memory_space=pl.ANY)] * n,
        scratch_shapes=_exchange_scratch(n),
    )(*arrays)


def _hosted_call(body, comm, *, name, grid, in_specs, out_specs, out_shape, scratch_shapes, dims, args):
    if comm is None:
        outs = pl.pallas_call(body, name=name, grid=grid, in_specs=in_specs, out_specs=out_specs, out_shape=out_shape,
                              scratch_shapes=scratch_shapes, compiler_params=_params(dims))(*args)
        return outs, []
    arrays, gather = comm
    n_in, n_out, n_scr, nc = len(in_specs), len(out_specs), len(scratch_shapes), len(arrays)

    def hosted(*refs):
        ins, c_in = refs[:n_in], refs[n_in:n_in + nc]
        outs, c_out = refs[n_in + nc:n_in + nc + n_out], refs[n_in + nc + n_out:n_in + 2 * nc + n_out]
        rest = refs[n_in + 2 * nc + n_out:]
        scratch, sems = rest[:n_scr], rest[n_scr:]
        mine, sends, recvs = _exchange_copies(c_in, c_out, *sems, gather)
        first = pl.program_id(0) == 0
        last = pl.program_id(0) == grid[0] - 1
        for axis in range(1, len(grid)):
            first = jnp.logical_and(first, pl.program_id(axis) == 0)
            last = jnp.logical_and(last, pl.program_id(axis) == grid[axis] - 1)

        @pl.when(first)
        def _():
            for cp in mine + sends:
                cp.start()

        body(*ins, *outs, *scratch)

        @pl.when(last)
        def _():
            for cp in recvs:
                cp.wait_recv()
            for cp in sends:
                cp.wait_send()
            for cp in mine:
                cp.wait()

    any_spec = pl.BlockSpec(memory_space=pl.ANY)
    outs = pl.pallas_call(
        hosted, name=name, grid=grid,
        in_specs=list(in_specs) + [any_spec] * nc,
        out_specs=list(out_specs) + [any_spec] * nc,
        out_shape=list(out_shape) + _exchange_shapes(arrays, gather),
        scratch_shapes=list(scratch_shapes) + _exchange_scratch(nc),
        compiler_params=_params(tuple("arbitrary" for _ in grid)),
    )(*args, *arrays)
    return outs[:n_out], outs[n_out:]


def _adam_math(g, w, m, v):
    m_new = B1 * m + (1.0 - B1) * g
    v_new = B2 * v + (1.0 - B2) * (g * g)
    m_hat = m_new / (1.0 - B1 ** STEP)
    v_hat = v_new / (1.0 - B2 ** STEP)
    return -LR * (m_hat / (jnp.sqrt(v_hat) + ADAM_EPS) + WD * w), m_new, v_new


def _adam(gparts, w, m, v, name, tr):
    n, rows, cols = gparts.shape

    def body(gp_ref, w_ref, m_ref, v_ref, g_ref, d_ref, nm_ref, nv_ref):
        g = gp_ref[0].astype(F32)
        for j in range(1, n):
            g = g + gp_ref[j].astype(F32)
        g_ref[...] = g
        d_ref[...], nm_ref[...], nv_ref[...] = _adam_math(g, w_ref[...], m_ref[...], v_ref[...])

    blk = pl.BlockSpec((tr, cols), lambda i: (i, 0))
    return pl.pallas_call(
        body,
        name=name,
        grid=(rows // tr,),
        out_shape=[jax.ShapeDtypeStruct((rows, cols), F32)] * 4,
        in_specs=[pl.BlockSpec((n, tr, cols), lambda i: (0, i, 0)), blk, blk, blk],
        out_specs=[blk] * 4,
        compiler_params=_params(("parallel",)),
    )(gparts, w, m, v)


def _adam_rows(parts, ws, ms, vs, name):
    n = parts.shape[0]
    k = len(ws)
    widths = [w.shape[1] for w in ws]

    def body(*refs):
        p_ref, w_refs, m_refs, v_refs = refs[0], refs[1:1 + k], refs[1 + k:1 + 2 * k], refs[1 + 2 * k:1 + 3 * k]
        outs = refs[1 + 3 * k:]
        total = p_ref[0:1, :]
        for j in range(1, n):
            total = total + p_ref[j:j + 1, :]
        off = 0
        for a, width in enumerate(widths):
            g = total[:, off:off + width]
            off += width
            outs[4 * a][...] = g
            outs[4 * a + 1][...], outs[4 * a + 2][...], outs[4 * a + 3][...] = _adam_math(
                g, w_refs[a][...], m_refs[a][...], v_refs[a][...])

    return pl.pallas_call(
        body, name=name,
        out_shape=[jax.ShapeDtypeStruct((1, width), F32) for width in widths for _ in range(4)],
        compiler_params=_params(),
    )(parts, *ws, *ms, *vs)


def _mm(a_list, b, name, out_dtype, tm, tn, comm=None):
    n_a = len(a_list)
    m, k = a_list[0].shape
    n = b.shape[1]

    def body(*refs):
        o_ref = refs[2 * n_a]
        acc = _dot(refs[0][...].astype(BF16), refs[n_a][...].astype(BF16))
        for g in range(1, n_a):
            acc = acc + _dot(refs[g][...].astype(BF16), refs[n_a + g][...].astype(BF16))
        o_ref[...] = acc.astype(out_dtype)

    (out,), got = _hosted_call(
        body, comm, name=name, grid=(n // tn, m // tm),
        out_shape=[jax.ShapeDtypeStruct((m, n), out_dtype)],
        in_specs=[pl.BlockSpec((tm, k), lambda j, i: (i, 0))] * n_a
        + [pl.BlockSpec((k, tn), lambda j, i, g=g: (g, j)) for g in range(n_a)],
        out_specs=[pl.BlockSpec((tm, tn), lambda j, i: (i, j))],
        scratch_shapes=[], dims=("parallel", "parallel"), args=(*a_list, *([b] * n_a)))
    return out, got


def _mm_nt(a, b, name, out_dtype, tm, tn, n_out=1):
    m, k = a.shape
    n = b.shape[0] // n_out
    nt = n // tn

    def body(a_ref, *refs):
        a_blk = a_ref[...].astype(BF16)
        for b_ref, o_ref in zip(refs[:n_out], refs[n_out:]):
            o_ref[...] = _dot_nt(a_blk, b_ref[...].astype(BF16)).astype(out_dtype)

    outs = pl.pallas_call(
        body,
        name=name,
        grid=(nt, m // tm),
        out_shape=[jax.ShapeDtypeStruct((m, n), out_dtype)] * n_out,
        in_specs=[pl.BlockSpec((tm, k), lambda j, i: (i, 0))]
        + [pl.BlockSpec((tn, k), lambda j, i, o=o: (o * nt + j, 0)) for o in range(n_out)],
        out_specs=[pl.BlockSpec((tm, tn), lambda j, i: (i, j))] * n_out,
        compiler_params=_params(("parallel", "parallel")),
    )(a, *([b] * n_out))
    return outs[0] if n_out == 1 else outs


def _mm_tn(a, b_list, name, tm, tn, tk):
    s, m = a.shape
    n = b_list[0].shape[1]
    n_b = len(b_list)

    def body(a_ref, *refs):
        b_refs, o_refs = refs[:n_b], refs[n_b:]

        @pl.when(pl.program_id(2) == 0)
        def _():
            for o_ref in o_refs:
                o_ref[...] = jnp.zeros_like(o_ref)

        a_blk = a_ref[...].astype(BF16)
        for b_ref, o_ref in zip(b_refs, o_refs):
            o_ref[...] += _dot_tn(a_blk, b_ref[...].astype(BF16))

    return pl.pallas_call(
        body,
        name=name,
        grid=(n // tn, m // tm, s // tk),
        out_shape=[jax.ShapeDtypeStruct((m, n), F32)] * n_b,
        in_specs=[pl.BlockSpec((tk, tm), lambda j, i, k: (k, i))] + [pl.BlockSpec((tk, tn), lambda j, i, k: (k, j))] * n_b,
        out_specs=[pl.BlockSpec((tm, tn), lambda j, i, k: (i, j))] * n_b,
        compiler_params=_params(("parallel", "parallel", "arbitrary")),
    )(a, *b_list)


def _rows_call(fn, name, rows, params, out_rows, out_accs, ts, comm=None):
    s = rows[0].shape[0]
    nr, npar, no = len(rows), len(params), len(out_rows)

    def body(*refs):
        r, p = refs[:nr], refs[nr:nr + npar]
        o, acc = refs[nr + npar:nr + npar + no], refs[nr + npar + no:]
        outs, sums = fn(*[t[...] for t in r], *[t[...] for t in p])
        for ref, val in zip(o, outs):
            ref[...] = val.astype(ref.dtype)
        if acc:
            @pl.when(pl.program_id(0) == 0)
            def _():
                for ref in acc:
                    ref[...] = jnp.zeros_like(ref)

            for ref, val in zip(acc, sums):
                ref[...] += val

    outs, got = _hosted_call(
        body, comm, name=name, grid=(s // ts,),
        out_shape=[jax.ShapeDtypeStruct((s, w), dt) for w, dt in out_rows]
        + [jax.ShapeDtypeStruct(shape, F32) for shape in out_accs],
        in_specs=[pl.BlockSpec((ts, t.shape[1]), lambda i: (i, 0)) for t in rows]
        + [pl.BlockSpec(t.shape, lambda i: (0, 0)) for t in params],
        out_specs=[pl.BlockSpec((ts, w), lambda i: (i, 0)) for w, _ in out_rows]
        + [pl.BlockSpec(shape, lambda i: (0, 0)) for shape in out_accs],
        scratch_shapes=[], dims=("arbitrary",), args=(*rows, *params))
    return outs if comm is None else (outs, got)


def _fwd_norm1(x, g, scale, shift):
    return (_norm_mod(x, g, scale, shift),), ()


def _resid_norm(x, mixed, gate, g, scale, shift):
    x1 = x + (1.0 + gate) * mixed
    return x1, _norm_mod(x1, g, scale, shift)


def _fwd_resid_norm(x, mixed, gate, g, scale, shift):
    return _resid_norm(x, mixed, gate, g, scale, shift), ()


def _final(x1, ffn, tgt, gate, g):
    def head(x1, ffn, gate, g):
        return _rms(x1 + (1.0 + gate) * ffn, g)

    y, vjp = jax.vjp(head, x1, ffn, gate, g)
    err = y - tgt
    dx2, dffn, dgate, dg = vjp(err * (1.0 / D))
    sq = jnp.sum(jnp.sum(err * err, axis=1, keepdims=True), axis=0, keepdims=True)
    loss = jnp.broadcast_to(sq * (0.5 / D), (1, 128))
    return (dx2, dffn), (dgate, dg, loss)


def _bwd_resid_norm(x, mixed, dx2, dh2, gate, g, scale, shift):
    _, vjp = jax.vjp(_resid_norm, x, mixed, gate, g, scale, shift)
    dx, dmixed, dgate, dg, dscale, dshift = vjp((dx2, dh2))
    return (dx, dmixed), (dgate, dg, dscale, dshift)


def _bwd_norm1(x, dh, dx1, g, scale, shift):
    _, vjp = jax.vjp(_norm_mod, x, g, scale, shift)
    dx, dg, dscale, dshift = vjp(dh)
    return (dx1 + dx,), (dg, dscale, dshift)


def _ada_fwd(c16, w):
    def body(c_ref, w_ref, o_ref):
        c = c_ref[...]
        o_ref[...] = _dot((c * _sigmoid(c)).astype(BF16), w_ref[...].astype(BF16))

    return pl.pallas_call(
        body, name="ada_fwd", out_shape=jax.ShapeDtypeStruct((c16.shape[0], w.shape[1]), F32),
        compiler_params=_params(),
    )(c16, w)


def _ada_bwd(c16, dmod16):
    def body(c_ref, d_ref, o_ref):
        c = c_ref[...]
        o_ref[...] = _dot_tn((c * _sigmoid(c)).astype(BF16), d_ref[...].astype(BF16))

    return pl.pallas_call(
        body, name="ada_bwd", out_shape=jax.ShapeDtypeStruct((c16.shape[1], dmod16.shape[1]), F32),
        compiler_params=_params(),
    )(c16, dmod16)


CONV_R = 256


def _conv_window(win, wc, bc):
    s1 = pltpu.roll(win, 1, 0)
    s2 = pltpu.roll(win, 2, 0)
    u = bc + wc[0:1] * s2 + wc[1:2] * s1 + wc[2:3] * win
    return u[8:], s1[8:], s2[8:], win[8:]


def _row_windows(ref, before, after, n_after, lanes):
    ts, r = ref.shape[0], CONV_R

    def middle(i):
        return ref[pl.ds(pl.multiple_of(i * r, r) - 8, r + 8 + n_after), lanes]

    first = jnp.concatenate([before, ref[0:r + n_after, lanes]], axis=0)
    last = ref[ts - r - 8:ts, lanes] if n_after == 0 else jnp.concatenate([ref[ts - r - 8:ts, lanes], after], axis=0)
    return first, middle, last


def _ffn_specs(ts, s):
    cur = pl.BlockSpec((ts, FF_T), lambda j, i: (i, j))
    halo = pl.BlockSpec((8, FF_T), lambda j, i: (jnp.maximum(i * (ts // 8) - 1, 0), j))
    nxt = pl.BlockSpec((8, FF_T), lambda j, i: (jnp.minimum((i + 1) * (ts // 8), s // 8 - 1), j))
    wc = [pl.BlockSpec((8, FF_T), lambda j, i, h=h: (0, h * N_FT + j)) for h in range(2)]
    bc = [pl.BlockSpec((1, FF_T), lambda j, i, h=h: (0, h * N_FT + j)) for h in range(2)]
    return cur, halo, nxt, wc, bc


def _conv_glu_fwd(up_v, up_g, wc, bc, ts):
    s = up_v.shape[0]
    cur, halo, _, wcs, bcs = _ffn_specs(ts, s)

    def body(v_ref, vh_ref, g_ref, gh_ref, wcv_ref, wcg_ref, bcv_ref, bcg_ref, a_ref):
        keep = jnp.where(pl.program_id(1) == 0, 0.0, 1.0)
        for lanes in (slice(0, 128), slice(128, 256)):
            wcv, wcg, bcv, bcg = wcv_ref[:, lanes], wcg_ref[:, lanes], bcv_ref[:, lanes], bcg_ref[:, lanes]
            first_v, mid_v, _ = _row_windows(v_ref, vh_ref[:, lanes] * keep, None, 0, lanes)
            first_g, mid_g, _ = _row_windows(g_ref, gh_ref[:, lanes] * keep, None, 0, lanes)

            def emit(win_v, win_g, start, lanes=lanes, wcv=wcv, wcg=wcg, bcv=bcv, bcg=bcg):
                val = _conv_window(win_v, wcv, bcv)[0]
                gte = _conv_window(win_g, wcg, bcg)[0]
                a_ref[pl.ds(start, CONV_R), lanes] = (val * (gte * _sigmoid(gte))).astype(BF16)

            emit(first_v, first_g, 0)

            def loop(i, carry, emit=emit, mid_v=mid_v, mid_g=mid_g):
                emit(mid_v(i), mid_g(i), pl.multiple_of(i * CONV_R, CONV_R))
                return carry

            lax.fori_loop(1, ts // CONV_R, loop, 0)

    return pl.pallas_call(
        body, name="conv_glu_fwd", grid=(N_FT, s // ts),
        out_shape=jax.ShapeDtypeStruct((s, D_FF), BF16),
        in_specs=[cur, halo, cur, halo, *wcs, *bcs], out_specs=cur,
        compiler_params=_params(("parallel", "arbitrary")),
    )(up_v, up_v, up_g, up_g, wc, wc, bc, bc)


def _glu_bwd(val, gte, da):
    sg = _sigmoid(gte)
    return da * (gte * sg), da * val * (sg * (1.0 + gte * (1.0 - sg)))


def _conv_glu_bwd(up_v, up_g, da, wc, bc, ts, comm=None):
    s = up_v.shape[0]
    nblk = s // ts
    cur, halo, nxt, wcs, bcs = _ffn_specs(ts, s)
    acc_w = pl.BlockSpec((8, FF_T), lambda j, i: (0, j))
    acc_b = pl.BlockSpec((1, FF_T), lambda j, i: (0, j))

    def body(v_ref, vh_ref, vn_ref, g_ref, gh_ref, gn_ref, da_ref, dan_ref, wcv_ref, wcg_ref, bcv_ref, bcg_ref,
             dupv_ref, dupg_ref, dwcv_ref, dwcg_ref, dbcv_ref, dbcg_ref):
        first = pl.program_id(1) == 0
        keep = jnp.where(first, 0.0, 1.0)
        keep_next = jnp.where(pl.program_id(1) == nblk - 1, 0.0, 1.0)
        r = CONV_R
        n = ts // r

        @pl.when(first)
        def _():
            for ref in (dwcv_ref, dwcg_ref, dbcv_ref, dbcg_ref):
                ref[...] = jnp.zeros_like(ref)

        for lanes in (slice(0, 128), slice(128, 256)):
            wcv, wcg, bcv, bcg = wcv_ref[:, lanes], wcg_ref[:, lanes], bcv_ref[:, lanes], bcg_ref[:, lanes]
            first_v, mid_v, last_v = _row_windows(v_ref, vh_ref[:, lanes] * keep, vn_ref[:, lanes], 8, lanes)
            first_g, mid_g, last_g = _row_windows(g_ref, gh_ref[:, lanes] * keep, gn_ref[:, lanes], 8, lanes)

            def emit(win_v, win_g, da_w, start, lanes=lanes, wcv=wcv, wcg=wcg, bcv=bcv, bcg=bcg):
                u_v, s1_v, s2_v, x_v = _conv_window(win_v, wcv, bcv)
                u_g, s1_g, s2_g, x_g = _conv_window(win_g, wcg, bcg)
                du_v, du_g = _glu_bwd(u_v, u_g, da_w)
                rows = pl.ds(start, r)
                for du, s1, s2, x, wc, dup_ref, dwc_ref, dbc_ref in (
                        (du_v, s1_v, s2_v, x_v, wcv, dupv_ref, dwcv_ref, dbcv_ref),
                        (du_g, s1_g, s2_g, x_g, wcg, dupg_ref, dwcg_ref, dbcg_ref)):
                    dup = wc[2:3] * du + wc[1:2] * pltpu.roll(du, r + 7, 0) + wc[0:1] * pltpu.roll(du, r + 6, 0)
                    dup_ref[rows, lanes] = dup[:r].astype(BF16)
                    du = du[:r]
                    dwc_ref[0:1, lanes] += jnp.sum(du * s2[:r], axis=0, keepdims=True)
                    dwc_ref[1:2, lanes] += jnp.sum(du * s1[:r], axis=0, keepdims=True)
                    dwc_ref[2:3, lanes] += jnp.sum(du * x[:r], axis=0, keepdims=True)
                    dbc_ref[:, lanes] += jnp.sum(du, axis=0, keepdims=True)

            emit(first_v, first_g, da_ref[0:r + 8, lanes], 0)

            def loop(i, carry, emit=emit, mid_v=mid_v, mid_g=mid_g, lanes=lanes):
                start = pl.multiple_of(i * r, r)
                emit(mid_v(i), mid_g(i), da_ref[pl.ds(start, r + 8), lanes], start)
                return carry

            lax.fori_loop(1, n - 1, loop, 0)
            da_last = jnp.concatenate([da_ref[ts - r:ts, lanes], dan_ref[:, lanes] * keep_next], axis=0)
            emit(last_v, last_g, da_last, ts - r)

    return _hosted_call(
        body, comm, name="conv_glu_bwd", grid=(N_FT, nblk),
        out_shape=[jax.ShapeDtypeStruct((s, D_FF), BF16)] * 2 + [jax.ShapeDtypeStruct((8, D_FF), F32)] * 2
        + [jax.ShapeDtypeStruct((1, D_FF), F32)] * 2,
        in_specs=[cur, halo, nxt, cur, halo, nxt, cur, nxt, *wcs, *bcs],
        out_specs=[cur, cur, acc_w, acc_w, acc_b, acc_b],
        scratch_shapes=[], dims=("parallel", "arbitrary"),
        args=(up_v, up_v, up_v, up_g, up_g, up_g, da, da, wc, wc, bc, bc))


def _tri(kind):
    b = SB_B
    m = {"lower_strict": np.tril(np.ones((b, b)), -1), "upper_incl": np.triu(np.ones((b, b)), 0),
         "upper_strict": np.triu(np.ones((b, b)), 1)}[kind]
    return jnp.asarray(np.concatenate([m, np.ones((b, 128))], axis=1), BF16)


def _key_sums(x, tri):
    hi, lo = _hilo(x)
    cb = _dot(hi, tri) + _dot(lo, tri)
    return cb[:, :SB_B], cb[:, SB_B:]


def _sb_fwd(proj, comm=None):
    s = proj.shape[0]
    b = SB_B

    def body(q_ref, k_ref, v_ref, tri_ref, o_ref, t_ref, first_ref, c_ref, a_ref):
        i = pl.program_id(1)
        lane = lax.broadcasted_iota(jnp.int32, (b, 128), 1)
        heads = (lane < 64, lane >= 64)
        causal = lax.broadcasted_iota(jnp.int32, (b, b), 1) < lax.broadcasted_iota(jnp.int32, (b, b), 0)
        q = q_ref[...] * 0.125
        qm = [jnp.where(h, q, 0.0).astype(BF16) for h in heads]
        c_ref[...] = jnp.zeros_like(c_ref)
        a_ref[...] = jnp.zeros_like(a_ref)

        def prepare(jj, masked):
            rows = pl.ds(pl.multiple_of(jj * b, b), b)
            kb = k_ref[rows, :].astype(BF16)
            vb = v_ref[rows, :]
            out = []
            for hh in range(2):
                z = _dot_nt(qm[hh], kb)
                lg = _log_sigmoid(-z)
                if masked:
                    lg = jnp.where(causal, lg, 0.0)
                after, total = _key_sums(lg, tri_ref[...])
                out.append((lg + z + after, total, jnp.where(heads[hh], vb, 0.0).astype(BF16)))
            return out

        def walk_blocks(blocks):
            for hh in range(2):
                c, a = c_ref[hh], a_ref[hh]
                for pre, masked in blocks:
                    logw, total, vm = pre[hh]
                    w = jnp.exp(logw + jnp.concatenate([c, c], axis=1))
                    if masked:
                        w = jnp.where(causal, w, 0.0)
                    a = a + _dot(w.astype(BF16), vm)
                    c = c + total
                c_ref[hh], a_ref[hh] = c, a

        def largest_sum():
            return jnp.max(jnp.maximum(c_ref[0], c_ref[1]))

        @pl.when(i == 0)
        def _():
            walk_blocks([(prepare(i, True), True)])

        @pl.when(i > 0)
        def _():
            walk_blocks([(prepare(i, True), True), (prepare(i - 1, False), False)])

        def more(state):
            jj, top = state
            return jnp.logical_and(jj >= 0, top > SB_DEAD)

        def walk(state):
            jj, _ = state
            walk_blocks([(prepare(jj, False), False)])
            return jj - 1, largest_sum()

        jj, _ = lax.while_loop(more, walk, (jnp.maximum(i - 2, -1), largest_sum()))
        o_ref[...] = (a_ref[0] + a_ref[1]).astype(BF16)
        t_ref[...] = jnp.concatenate([c_ref[0], c_ref[1]], axis=1)
        first_ref[pl.program_id(0), i] = (jj + 1).astype(F32)

    return _hosted_call(
        body, comm, name="sb_fwd", grid=(4, s // b),
        out_shape=[jax.ShapeDtypeStruct((s, SB_W), BF16), jax.ShapeDtypeStruct((s, 2 * SB_W), F32),
                   jax.ShapeDtypeStruct((4, s // b), F32)],
        in_specs=[pl.BlockSpec((b, 128), lambda p, i: (i, p)),
                  pl.BlockSpec((s, 128), lambda p, i: (0, 4 + p)),
                  pl.BlockSpec((s, 128), lambda p, i: (0, 8 + p)),
                  pl.BlockSpec((b, b + 128), lambda p, i: (0, 0))],
        out_specs=[pl.BlockSpec((b, 128), lambda p, i: (i, p)), pl.BlockSpec((b, 256), lambda p, i: (i, p)),
                   pl.BlockSpec(memory_space=pltpu.SMEM)],
        scratch_shapes=[pltpu.VMEM((2, b, 128), F32), pltpu.VMEM((2, b, 128), F32)],
        dims=("arbitrary", "arbitrary"), args=(proj, proj, proj, _tri("lower_strict")))


def _sb_bwd(proj, do, tot, first, comm=None):
    s = proj.shape[0]
    b = SB_B

    def body(q_ref, k_ref, v_ref, do_ref, t_ref, first_ref, ti_ref, ts_ref, dq_ref, dko_ref, dvo_ref, cl_ref, ce_ref,
             a_ref, dk_ref, dv_ref):
        i = pl.program_id(1)
        first = jnp.clip(first_ref[pl.program_id(0), i].astype(jnp.int32), 0, i)
        lane = lax.broadcasted_iota(jnp.int32, (b, 128), 1)
        heads = (lane < 64, lane >= 64)
        causal = lax.broadcasted_iota(jnp.int32, (b, b), 1) < lax.broadcasted_iota(jnp.int32, (b, b), 0)
        q = q_ref[...] * 0.125
        do = do_ref[...]
        qm = [jnp.where(h, q, 0.0).astype(BF16) for h in heads]
        dom = [jnp.where(h, do, 0.0).astype(BF16) for h in heads]
        cl_ref[...] = jnp.zeros_like(cl_ref)
        ce_ref[...] = jnp.zeros_like(ce_ref)
        a_ref[...] = jnp.zeros_like(a_ref)

        @pl.when(i == 0)
        def _():
            dk_ref[...] = jnp.zeros_like(dk_ref)
            dv_ref[...] = jnp.zeros_like(dv_ref)

        def prepare(jj, masked):
            rows = pl.ds(pl.multiple_of(jj * b, b), b)
            kf = k_ref[rows, :]
            kb = kf.astype(BF16)
            vb = v_ref[rows, :].astype(BF16)
            out = []
            for hh in range(2):
                z = _dot_nt(qm[hh], kb)
                lg = _log_sigmoid(-z)
                if masked:
                    lg = jnp.where(causal, lg, 0.0)
                upto, total = _key_sums(lg, ti_ref[...])
                lsz = lg + z
                out.append((lsz - upto, total, jnp.exp(lsz), _dot_nt(dom[hh], vb),
                            jnp.where(heads[hh], kf, 0.0).astype(BF16)))
            return rows, out

        def walk_blocks(blocks):
            grads = [[jnp.zeros((b, 128), F32), jnp.zeros((b, 128), F32)] for _ in blocks]
            for hh in range(2):
                cl, ce, a = cl_ref[hh], ce_ref[hh], a_ref[hh]
                t = t_ref[:, hh * 128:(hh + 1) * 128]
                for n, ((_, pre), masked) in enumerate(blocks):
                    logw, total, sig, dw, km = pre[hh]
                    w = jnp.exp(logw + jnp.concatenate([t - cl, t - cl], axis=1))
                    if masked:
                        w = jnp.where(causal, w, 0.0)
                    e = w * dw
                    sums = _dot(e.astype(BF16), ts_ref[...])
                    before, etot = sums[:, :SB_B], sums[:, SB_B:]
                    dz = e - sig * (e + before + jnp.concatenate([ce, ce], axis=1))
                    if masked:
                        dz = jnp.where(causal, dz, 0.0)
                    dzb = dz.astype(BF16)
                    a = a + _dot(dzb, km)
                    grads[n][0] = grads[n][0] + _dot_tn(dzb, qm[hh])
                    grads[n][1] = grads[n][1] + _dot_tn(w.astype(BF16), dom[hh])
                    cl = cl + total
                    ce = ce + etot
                cl_ref[hh], ce_ref[hh], a_ref[hh] = cl, ce, a
            for ((rows, _), _), (dk, dv) in zip(blocks, grads):
                dk_ref[rows, :] += dk
                dv_ref[rows, :] += dv

        def loop(jj, carry):
            walk_blocks([(prepare(jj, False), False)])
            return carry

        lax.fori_loop(first, i - 1, loop, 0)

        @pl.when(i == 0)
        def _():
            walk_blocks([(prepare(i, True), True)])

        @pl.when(i > 0)
        def _():
            walk_blocks([(prepare(i - 1, False), False), (prepare(i, True), True)])

        dq_ref[...] = ((a_ref[0] + a_ref[1]) * 0.125).astype(BF16)

        @pl.when(i == s // b - 1)
        def _():
            dko_ref[...] = dk_ref[...].astype(BF16)
            dvo_ref[...] = dv_ref[...].astype(BF16)

    blk = pl.BlockSpec((b, 128), lambda p, i: (i, p))
    full = pl.BlockSpec((s, 128), lambda p, i: (0, p))
    tri = pl.BlockSpec((b, b + 128), lambda p, i: (0, 0))
    return _hosted_call(
        body, comm, name="sb_bwd", grid=(4, s // b),
        out_shape=[jax.ShapeDtypeStruct((s, SB_W), BF16)] * 3,
        in_specs=[blk, pl.BlockSpec((s, 128), lambda p, i: (0, 4 + p)), pl.BlockSpec((s, 128), lambda p, i: (0, 8 + p)),
                  blk, pl.BlockSpec((b, 256), lambda p, i: (i, p)), pl.BlockSpec(memory_space=pltpu.SMEM), tri, tri],
        out_specs=[blk, full, full],
        scratch_shapes=[pltpu.VMEM((2, b, 128), F32)] * 3 + [pltpu.VMEM((s, 128), F32)] * 2,
        dims=("arbitrary", "arbitrary"),
        args=(proj, proj, proj, do, tot, first, _tri("upper_incl"), _tri("upper_strict")))


_COL_Q, _COL_K, _COL_V, _COL_G, _COL_F = 12, 14, 8, 10, 24


def _gla_consts():
    c = CHUNK
    incl = np.tril(np.ones((c, c)), 0)
    strict = np.tril(np.ones((c, c)), -1)
    bd = np.zeros((128, 256))
    bd[:64, :128] = 1.0
    bd[64:, 128:] = 1.0
    return jnp.asarray(incl, BF16), jnp.asarray(strict, BF16), jnp.asarray(bd, F32)


def _time_sums(tri, x):
    hi, lo = _hilo(x)
    return _dot(tri, hi) + _dot(tri, lo)


def _gla_gate(o, gg, g):
    parts = []
    for h in range(2):
        oh = o[:, h * 128:(h + 1) * 128]
        parts.append(oh * lax.rsqrt(jnp.mean(oh * oh, axis=-1, keepdims=True) + EPS))
    return (jnp.concatenate(parts, axis=1) * g) * (gg * _sigmoid(gg))


def _gla_chunk(la_c, k_c, incl, ones_cv):
    cum = _time_sums(incl, la_c)
    total = cum[CHUNK - 1:CHUNK]
    edec = jnp.exp(total - cum)
    kdec = k_c * edec
    hi, lo = _hilo(la_c)
    dec = jnp.exp(_dot_tn(hi, ones_cv) + _dot_tn(lo, ones_cv))
    return edec, kdec, dec


def _gla_fwd(proj, wfg, bfg, gain):
    s = proj.shape[0]
    t = GLA_T
    nch = t // CHUNK
    incl, _, bd = _gla_consts()

    def body(q_ref, k_ref, v_ref, gg_ref, f_ref, wf_ref, bf_ref, g_ref, incl_ref, bd_ref, o_ref, st_ref, state_ref):
        @pl.when(pl.program_id(0) == 0)
        def _():
            state_ref[...] = jnp.zeros_like(state_ref)

        la = _log_sigmoid(_dot(f_ref[...].astype(BF16), wf_ref[...].astype(BF16)) + bf_ref[...]) * (1.0 / 16.0)
        ones_cv = jnp.ones((CHUNK, 256), BF16)
        states = [state_ref[0], state_ref[1]]
        for cc in range(nch):
            rows = slice(cc * CHUNK, (cc + 1) * CHUNK)
            for p in range(2):
                kl, vl = slice(p * 128, (p + 1) * 128), slice(p * 256, (p + 1) * 256)
                _, kdec, dec = _gla_chunk(la[rows, kl], k_ref[rows, kl], incl_ref[...], ones_cv)
                kv = _dot_tn(kdec.astype(BF16), v_ref[rows, vl].astype(BF16))
                states[p] = dec * states[p] + bd_ref[...] * kv
                st_ref[p, cc] = states[p]
                o = _dot((q_ref[rows, kl] * 0.125).astype(BF16), states[p].astype(BF16))
                o_ref[rows, vl] = _gla_gate(o, gg_ref[rows, vl], g_ref[:, vl]).astype(BF16)
        state_ref[0] = states[0]
        state_ref[1] = states[1]

    def col(width, blk):
        return pl.BlockSpec((t, width), lambda i: (i, blk))

    def whole(shape):
        return pl.BlockSpec(shape, lambda i: tuple(0 for _ in shape))

    return pl.pallas_call(
        body, name="gla_fwd", grid=(s // t,),
        out_shape=[jax.ShapeDtypeStruct((s, GLA_VW), BF16), jax.ShapeDtypeStruct((2, s // CHUNK, 128, 256), F32)],
        in_specs=[col(256, _COL_Q // 2), col(256, _COL_K // 2), col(512, _COL_V // 2), col(512, _COL_G // 2),
                  col(128, _COL_F), whole((128, GLA_KW)), whole((1, GLA_KW)), whole((1, GLA_VW)),
                  whole((CHUNK, CHUNK)), whole((128, 256))],
        out_specs=[col(512, 0), pl.BlockSpec((2, nch, 128, 256), lambda i: (0, i, 0, 0))],
        scratch_shapes=[pltpu.VMEM((2, 128, 256), F32)],
        compiler_params=_params(("arbitrary",)),
    )(proj, proj, proj, proj, proj, wfg, bfg, gain, incl, bd)


def _gla_bwd(proj, do, states, wfg, bfg, gain):
    s = proj.shape[0]
    t = GLA_T
    nch = t // CHUNK
    nblk = s // t
    incl, strict, bd = _gla_consts()

    def body(q_ref, k_ref, v_ref, gg_ref, f_ref, do_ref, st_ref, sp_ref, wf_ref, bf_ref, g_ref, incl_ref, str_ref,
             bd_ref, dq_ref, dk_ref, dv_ref, dgg_ref, df_ref, dwf_ref, dbf_ref, dg_ref, carry_ref, dfs_ref):
        i = pl.program_id(0)

        @pl.when(i == 0)
        def _():
            carry_ref[...] = jnp.zeros_like(carry_ref)
            dwf_ref[...] = jnp.zeros_like(dwf_ref)
            dbf_ref[...] = jnp.zeros_like(dbf_ref)
            dg_ref[...] = jnp.zeros_like(dg_ref)

        fb = f_ref[...].astype(BF16)
        wf = wf_ref[...].astype(BF16)
        f = _dot(fb, wf) + bf_ref[...]
        la = _log_sigmoid(f) * (1.0 / 16.0)
        dla_df = _sigmoid(-f) * (1.0 / 16.0)
        ones_cv = jnp.ones((CHUNK, 256), BF16)
        ones_8v = jnp.ones((8, 256), BF16)
        first_block = jnp.where(i == nblk - 1, 0.0, 1.0)
        carries = [carry_ref[0], carry_ref[1]]
        dgains = [jnp.zeros((1, 256), F32), jnp.zeros((1, 256), F32)]
        for cc in reversed(range(nch)):
            rows = slice(cc * CHUNK, (cc + 1) * CHUNK)
            for p in range(2):
                kl, vl = slice(p * 128, (p + 1) * 128), slice(p * 256, (p + 1) * 256)
                edec, kdec, dec = _gla_chunk(la[rows, kl], k_ref[rows, kl], incl_ref[...], ones_cv)
                state = st_ref[p, cc]
                prev = st_ref[p, cc - 1] if cc > 0 else sp_ref[p, 0] * first_block
                qs = (q_ref[rows, kl] * 0.125).astype(BF16)
                sb16 = state.astype(BF16)
                o = _dot(qs, sb16)
                _, vjp = jax.vjp(_gla_gate, o, gg_ref[rows, vl], g_ref[:, vl])
                do_c, dgg_c, dg_c = vjp(do_ref[rows, vl])
                dgains[p] = dgains[p] + dg_c
                dgg_ref[rows, vl] = dgg_c.astype(BF16)
                do16 = do_c.astype(BF16)
                dq_ref[rows, kl] = (_dot_nt(do16, sb16) * 0.125).astype(BF16)
                grad = bd_ref[...] * _dot_tn(qs, do16) + carries[p]
                g16 = grad.astype(BF16)
                dv_ref[rows, vl] = _dot(kdec.astype(BF16), g16).astype(BF16)
                dkdec = _dot_nt(v_ref[rows, vl].astype(BF16), g16)
                hi, lo = _hilo(grad * prev * dec)
                ddec = (_dot_nt(ones_8v, hi) + _dot_nt(ones_8v, lo))[0:1]
                dk_ref[rows, kl] = (dkdec * edec).astype(BF16)
                dla = _time_sums(str_ref[...], dkdec * kdec) + ddec
                dfs_ref[rows, kl] = dla * dla_df[rows, kl]
                carries[p] = dec * grad
        carry_ref[0] = carries[0]
        carry_ref[1] = carries[1]
        df = dfs_ref[...]
        df16 = df.astype(BF16)
        df_ref[...] = _dot_nt(df16, wf).astype(BF16)
        dwf_ref[...] += _dot_tn(fb, df16)
        dbf_ref[...] += jnp.sum(df, axis=0, keepdims=True)
        dg_ref[...] += jnp.concatenate(dgains, axis=1)

    def col(width, blk):
        return pl.BlockSpec((t, width), lambda i: (nblk - 1 - i, blk))

    def whole(shape):
        return pl.BlockSpec(shape, lambda i: tuple(0 for _ in shape))

    return pl.pallas_call(
        body, name="gla_bwd", grid=(nblk,),
        out_shape=[jax.ShapeDtypeStruct((s, GLA_KW), BF16), jax.ShapeDtypeStruct((s, GLA_KW), BF16),
                   jax.ShapeDtypeStruct((s, GLA_VW), BF16), jax.ShapeDtypeStruct((s, GLA_VW), BF16),
                   jax.ShapeDtypeStruct((s, 128), BF16), jax.ShapeDtypeStruct((128, GLA_KW), F32),
                   jax.ShapeDtypeStruct((1, GLA_KW), F32), jax.ShapeDtypeStruct((1, GLA_VW), F32)],
        in_specs=[col(256, _COL_Q // 2), col(256, _COL_K // 2), col(512, _COL_V // 2), col(512, _COL_G // 2),
                  col(128, _COL_F), col(512, 1),
                  pl.BlockSpec((2, nch, 128, 256), lambda i: (0, nblk - 1 - i, 0, 0)),
                  pl.BlockSpec((2, 1, 128, 256), lambda i: (0, jnp.maximum((nblk - 1 - i) * nch - 1, 0), 0, 0)),
                  whole((128, GLA_KW)), whole((1, GLA_KW)), whole((1, GLA_VW)),
                  whole((CHUNK, CHUNK)), whole((CHUNK, CHUNK)), whole((128, 256))],
        out_specs=[col(256, 0), col(256, 0), col(512, 0), col(512, 0), col(128, 0),
                   whole((128, GLA_KW)), whole((1, GLA_KW)), whole((1, GLA_VW))],
        scratch_shapes=[pltpu.VMEM((2, 128, 256), F32), pltpu.VMEM((t, GLA_KW), F32)],
        compiler_params=_params(("arbitrary",)),
    )(proj, proj, proj, proj, proj, do, states, states, wfg, bfg, gain, incl, strict, bd)


def _cols_by_dev(a, n_dev=N_DEV):
    r, c = a.shape
    return a.reshape(r, n_dev, c // n_dev).transpose(1, 0, 2)


def _cols_from_dev(a):
    _, r, n = a.shape
    return a.transpose(1, 0, 2).reshape(r, N_DEV * n)


def kernel(x, c, w_ada, b_ada, g_norm1, w_in, w_fg2, b_fg2, g_gla_out, w_out, g_norm2, w_up, w_conv, b_conv, w_down, g_final, loss_target, m_w_ada, m_b_ada, m_g_norm1, m_w_in, m_w_fg2, m_b_fg2, m_g_gla_out, m_w_out, m_g_norm2, m_w_up, m_w_conv, m_b_conv, m_w_down, m_g_final, v_w_ada, v_b_ada, v_g_norm1, v_w_in, v_w_fg2, v_b_fg2, v_g_gla_out, v_w_out, v_g_norm2, v_w_up, v_w_conv, v_b_conv, v_w_down, v_g_final):
    s = x.shape[1]
    me = 4 * lax.axis_index("x") + 2 * lax.axis_index("y") + lax.axis_index("c")
    xs, tgt = x[0], loss_target[0]
    ts = min(512, s)
    tm = min(1024, s)
    tc = min(2048, s // 2)

    c_all, wconv_g, wfg_g = _exchange([c, w_conv[0], w_fg2[0]], "gather_first", True)
    wconv_p = jnp.pad(_cols_from_dev(wconv_g), ((0, 5), (0, 0)))
    wfg_p = jnp.pad(_cols_from_dev(wfg_g), ((0, 128 - RANK), (0, 0)))

    c16 = jnp.pad(c_all.reshape(N_DEV, D), ((0, 8), (0, 0)))
    modp = _ada_fwd(c16, w_ada[0])[:N_DEV]
    (mod_all,) = _exchange([modp], "gather_mod", True)
    mod = lax.dynamic_index_in_dim(mod_all, me, axis=1, keepdims=False).reshape(1, 6 * D) + b_ada
    shift1, scale1, gate1, shift2, scale2, gate2 = [mod[:, k * D:(k + 1) * D] for k in range(6)]

    w_in_t, m_in_t, v_in_t = [jnp.swapaxes(t[0], 0, 1) for t in (w_in, m_w_in, v_w_in)]
    w_up_t, m_up_t, v_up_t = [jnp.swapaxes(t[0], 0, 1) for t in (w_up, m_w_up, v_w_up)]
    (h,), (win_g,) = _rows_call(_fwd_norm1, "norm1", [xs], [g_norm1, scale1, shift1], [(D, BF16)], [], ts,
                                comm=([w_in_t.astype(BF16)], True))
    win_t = jnp.pad(win_g.reshape(IN_W, D), ((0, IN_WP - IN_W), (0, 0)))
    proj = _mm_nt(h, win_t, "in_proj", F32, ts, IN_WP)
    (o_sb, sb_tot, sb_first), (wout_g, wup_g, wdown_g) = _sb_fwd(
        proj, comm=([w_out[0].astype(BF16), w_up_t.astype(BF16), w_down[0].astype(BF16)], True))
    wout = wout_g.reshape(D, D)
    wup_t = wup_g.reshape(2 * D_FF, D)
    wdown = wdown_g.reshape(D_FF, D)
    o_gla, states = _gla_fwd(proj, wfg_p, b_fg2, g_gla_out)
    cat = jnp.concatenate([o_sb, o_gla], axis=1)
    mixed, _ = _mm([cat], wout, "out_proj", F32, tm, 512)
    x1, h2 = _rows_call(_fwd_resid_norm, "resid_norm2", [xs, mixed], [gate1, g_norm2, scale2, shift2],
                        [(D, F32), (D, BF16)], [], ts)
    up_v, up_g = _mm_nt(h2, wup_t, "up_proj", F32, tm, 1408, n_out=2)
    act = _conv_glu_fwd(up_v, up_g, wconv_p, b_conv, tc)
    ffn, _ = _mm([act], wdown, "down_proj", F32, tm, 512)
    dx2, dffn, dgate2, dg_final, loss_part = _rows_call(
        _final, "final", [x1, ffn, tgt], [gate2, g_final.reshape(1, D)],
        [(D, F32), (D, BF16)], [(1, D), (1, D), (1, 128)], ts)

    (dw_down,) = _mm_tn(act, [dffn], "down_wgrad", 1408, 1024, tm)
    da = _mm_nt(dffn, wdown, "down_dgrad", F32, tm, 1408)
    (dup_v, dup_g, dwc_v, dwc_g, dbc_v, dbc_g), (r_down,) = _conv_glu_bwd(
        up_v, up_g, da, wconv_p, b_conv, tc, comm=([dw_down.reshape(N_DEV, 352, D)], False))
    (dw_up_tv,) = _mm_tn(dup_v, [h2], "up_wgrad_v", 1408, 1024, tm)
    (dw_up_tg,) = _mm_tn(dup_g, [h2], "up_wgrad_g", 1408, 1024, tm)
    dh2, _ = _mm([dup_v, dup_g], wup_t, "up_dgrad", F32, tm, 512)
    dx1, dmixed, dgate1, dg_norm2, dscale2, dshift2 = _rows_call(
        _bwd_resid_norm, "resid_norm2_bwd", [xs, mixed, dx2, dh2], [gate1, g_norm2, scale2, shift2],
        [(D, F32), (D, BF16)], [(1, D)] * 4, ts)
    (dw_out,) = _mm_tn(cat, [dmixed], "out_wgrad", 1024, 1024, tm)
    dcat = _mm_nt(dmixed, wout, "out_dgrad", F32, tm, 512)
    dw_up_s = jnp.concatenate([dw_up_tv, dw_up_tg], axis=0).reshape(N_DEV, 704, D)
    dwconv_s = _cols_by_dev(jnp.concatenate([dwc_v[:3], dwc_g[:3]], axis=1))
    (dq, dk, dv), (r_up, r_out, r_conv) = _sb_bwd(
        proj, dcat, sb_tot, sb_first, comm=([dw_up_s, dw_out.reshape(N_DEV, 128, D), dwconv_s], False))
    dgq, dgk, dgv, dgg, dgf, dwfg_p, dbfg, dg_gla = _gla_bwd(proj, dcat, states, wfg_p, b_fg2, g_gla_out)
    dproj = jnp.concatenate([dq, dk, dv, dgq, dgk, dgv, dgg, dgf], axis=1)
    (dw_in_t,) = _mm_tn(dproj, [h], "in_wgrad", IN_WP, 1024, tm)
    dh, (r_in, r_fg) = _mm([dproj], win_t, "in_dgrad", F32, tm, 512,
                           comm=([dw_in_t[:IN_W].reshape(N_DEV, 386, D).astype(BF16), _cols_by_dev(dwfg_p[:RANK])],
                                 False))
    grad_x, dg_norm1, dscale1, dshift1 = _rows_call(
        _bwd_norm1, "norm1_bwd", [xs, dh, dx1], [g_norm1, scale1, shift1], [(D, F32)], [(1, D)] * 3, ts)

    dmod = jnp.concatenate([dshift1, dscale1, dgate1, dshift2, dscale2, dgate2], axis=1)
    small_parts = [dmod, dg_norm1, dbfg, dg_gla, dg_norm2, dbc_v, dbc_g, dg_final, loss_part]
    n_small = sum(p.shape[1] for p in small_parts[:-1])
    (sg,) = _exchange([jnp.concatenate(small_parts, axis=1)], "gather_small_grads", True)
    sg = sg.reshape(N_DEV, n_small + 128)
    loss = jnp.sum(sg[:, n_small])
    small_w = [b_ada, g_norm1, b_fg2, g_gla_out, g_norm2, b_conv, g_final.reshape(1, D)]
    small_m = [m_b_ada, m_g_norm1, m_b_fg2, m_g_gla_out, m_g_norm2, m_b_conv, m_g_final.reshape(1, D)]
    small_v = [v_b_ada, v_g_norm1, v_b_fg2, v_g_gla_out, v_g_norm2, v_b_conv, v_g_final.reshape(1, D)]
    s_out = list(_adam_rows(sg, small_w, small_m, small_v, "adam_small"))
    s_out[24:] = [t.reshape(D) for t in s_out[24:]]
    s_g, s_d, s_m, s_v = [s_out[k::4] for k in range(4)]

    dmod_all = sg[:, :6 * D].reshape(N_DEV, N_DEV, 768)
    dmod_mine = lax.dynamic_index_in_dim(dmod_all, me, axis=1, keepdims=False)
    dw_ada = _ada_bwd(c16, jnp.pad(dmod_mine, ((0, 8), (0, 0))))
    a_g, a_d, a_m, a_v = [t[None] for t in _adam(dw_ada[None], w_ada[0], m_w_ada[0], v_w_ada[0], "adam_ada", 256)]

    recv = [r_in, r_fg, r_out, r_up, r_conv, r_down]
    big_w = [w_in_t, w_fg2[0], w_out[0], w_up_t, w_conv[0], w_down[0]]
    big_m = [m_in_t, m_w_fg2[0], m_w_out[0], m_up_t, m_w_conv[0], m_w_down[0]]
    big_v = [v_in_t, v_w_fg2[0], v_w_out[0], v_up_t, v_w_conv[0], v_w_down[0]]
    big_name = ["adam_in", "adam_fg2", "adam_out", "adam_up", "adam_conv", "adam_down"]
    big_rows = [386, RANK, 128, 176, 3, 176]
    b_out = [_adam(r, w, m, v, name, tr)
             for r, w, m, v, name, tr in zip(recv, big_w, big_m, big_v, big_name, big_rows)]
    for k in (0, 3):
        b_out[k] = [jnp.swapaxes(t, 0, 1) for t in b_out[k]]
    b_g, b_d, b_m, b_v = [[o[k][None] for o in b_out] for k in range(4)]

    def ordered(a, sm, bg):
        return [a, sm[0], sm[1], bg[0], bg[1], sm[2], sm[3], bg[2], sm[4], bg[3], bg[4], sm[5], bg[5], sm[6]]

    return (loss, grad_x[None], *ordered(a_g, s_g, b_g), *ordered(a_d, s_d, b_d),
            *ordered(a_m, s_m, b_m), *ordered(a_v, s_v, b_v))
```

```python
import numpy as np

import jax
import jax.numpy as jnp
from jax import lax
from jax.experimental import pallas as pl
from jax.experimental.pallas import tpu as pltpu

F32, BF16 = jnp.float32, jnp.bfloat16
N_DEV = 8
D = 1024
SB_W = 512
GLA_KW, GLA_VW = 256, 512
RANK = 16
IN_W = 3088
IN_WP = 3200
D_FF = 2816
FF_T = 256
N_FT = D_FF // FF_T
EPS = 1e-6
SB_B = 256
SB_DEAD = -110.0
CHUNK = 64
GLA_T = 512
VMEM_LIMIT = 56 * 1024 * 1024

LR, B1, B2, ADAM_EPS, WD, STEP = 0.001, 0.9, 0.999, 1e-08, 0.01, 10


def _params(dims=None, vmem=True):
    kw = {}
    if dims is not None:
        kw["dimension_semantics"] = dims
    if vmem:
        kw["vmem_limit_bytes"] = VMEM_LIMIT
    return pltpu.CompilerParams(**kw)


def _dot(a, b):
    return jnp.dot(a, b, preferred_element_type=F32)


def _dot_nt(a, b):
    return lax.dot_general(a, b, (((1,), (1,)), ((), ())), preferred_element_type=F32)


def _dot_tn(a, b):
    return lax.dot_general(a, b, (((0,), (0,)), ((), ())), preferred_element_type=F32)


def _hilo(x):
    hi = x.astype(BF16)
    lo = (x - hi.astype(F32)).astype(BF16)
    return hi, lo


def _sigmoid(x):
    return 1.0 / (1.0 + jnp.exp(-x))


def _log_sigmoid(x):
    return jnp.minimum(x, 0.0) - jnp.log(1.0 + jnp.exp(-jnp.abs(x)))


def _rms(x, g):
    n = x * lax.rsqrt(jnp.mean(x * x, axis=-1, keepdims=True) + EPS)
    return n * g


def _norm_mod(x, g, scale, shift):
    return _rms(x, g) * (1.0 + scale) + shift


N_PEER = N_DEV - 1


def _exchange_copies(x_refs, out_refs, send_sems, recv_sems, local_sems, gather):
    ix, iy, ic = lax.axis_index("x"), lax.axis_index("y"), lax.axis_index("c")
    me = 4 * ix + 2 * iy + ic
    peers = []
    for k in range(1, N_DEV):
        px = 1 - ix if k & 4 else ix
        py = 1 - iy if k & 2 else iy
        pc = 1 - ic if k & 1 else ic
        peers.append(((px, py, pc), 4 * px + 2 * py + pc))

    def copy(a, k, dev, src_slot, dst_slot):
        return pltpu.make_async_remote_copy(
            src_ref=x_refs[a] if gather else x_refs[a].at[src_slot],
            dst_ref=out_refs[a].at[dst_slot],
            send_sem=send_sems.at[a * N_PEER + k],
            recv_sem=recv_sems.at[a * N_PEER + k],
            device_id=dev,
            device_id_type=pl.DeviceIdType.MESH,
        )

    n = len(x_refs)
    mine = [pltpu.make_async_copy(x_refs[a] if gather else x_refs[a].at[me], out_refs[a].at[me], local_sems.at[a])
            for a in range(n)]
    sends = [copy(a, k, dev, pid, me) for a in range(n) for k, (dev, pid) in enumerate(peers)]
    recvs = [copy(a, k, dev, pid, pid) for a in range(n) for k, (dev, pid) in enumerate(peers)]
    return mine, sends, recvs


def _exchange_scratch(n):
    return [pltpu.SemaphoreType.DMA((n * N_PEER,)), pltpu.SemaphoreType.DMA((n * N_PEER,)),
            pltpu.SemaphoreType.DMA((n,))]


def _exchange_shapes(arrays, gather):
    return [jax.ShapeDtypeStruct((N_DEV,) + tuple(x.shape if gather else x.shape[1:]), x.dtype) for x in arrays]


def _exchange(arrays, name, gather):
    n = len(arrays)

    def body(*refs):
        mine, sends, recvs = _exchange_copies(refs[:n], refs[n:2 * n], *refs[2 * n:], gather)
        for cp in mine + sends:
            cp.start()
        for cp in recvs:
            cp.wait_recv()
        for cp in sends:
            cp.wait_send()
        for cp in mine:
            cp.wait()

    return pl.pallas_call(
        body,
        name=name,
        out_shape=_exchange_shapes(arrays, gather),
        in_specs=[pl.BlockSpec(memory_space=pl.ANY)] * n,
        out_specs=[pl.BlockSpec(memory_space=pl.ANY)] * n,
        scratch_shapes=_exchange_scratch(n),
    )(*arrays)


def _hosted_call(body, comm, *, name, grid, in_specs, out_specs, out_shape, scratch_shapes, dims, args):
    if comm is None:
        outs = pl.pallas_call(body, name=name, grid=grid, in_specs=in_specs, out_specs=out_specs, out_shape=out_shape,
                              scratch_shapes=scratch_shapes, compiler_params=_params(dims))(*args)
        return outs, []
    arrays, gather = comm
    n_in, n_out, n_scr, nc = len(in_specs), len(out_specs), len(scratch_shapes), len(arrays)

    def hosted(*refs):
        ins, c_in = refs[:n_in], refs[n_in:n_in + nc]
        outs, c_out = refs[n_in + nc:n_in + nc + n_out], refs[n_in + nc + n_out:n_in + 2 * nc + n_out]
        rest = refs[n_in + 2 * nc + n_out:]
        scratch, sems = rest[:n_scr], rest[n_scr:]
        mine, sends, recvs = _exchange_copies(c_in, c_out, *sems, gather)
        first = pl.program_id(0) == 0
        last = pl.program_id(0) == grid[0] - 1
        for axis in range(1, len(grid)):
            first = jnp.logical_and(first, pl.program_id(axis) == 0)
            last = jnp.logical_and(last, pl.program_id(axis) == grid[axis] - 1)

        @pl.when(first)
        def _():
            for cp in mine + sends:
                cp.start()

        body(*ins, *outs, *scratch)

        @pl.when(last)
        def _():
            for cp in recvs:
                cp.wait_recv()
            for cp in sends:
                cp.wait_send()
            for cp in mine:
                cp.wait()

    any_spec = pl.BlockSpec(memory_space=pl.ANY)
    outs = pl.pallas_call(
        hosted, name=name, grid=grid,
        in_specs=list(in_specs) + [any_spec] * nc,
        out_specs=list(out_specs) + [any_spec] * nc,
        out_shape=list(out_shape) + _exchange_shapes(arrays, gather),
        scratch_shapes=list(scratch_shapes) + _exchange_scratch(nc),
        compiler_params=_params(tuple("arbitrary" for _ in grid)),
    )(*args, *arrays)
    return outs[:n_out], outs[n_out:]


def _adam_math(g, w, m, v):
    m_new = B1 * m + (1.0 - B1) * g
    v_new = B2 * v + (1.0 - B2) * (g * g)
    m_hat = m_new / (1.0 - B1 ** STEP)
    v_hat = v_new / (1.0 - B2 ** STEP)
    return -LR * (m_hat / (jnp.sqrt(v_hat) + ADAM_EPS) + WD * w), m_new, v_new


def _adam(gparts, w, m, v, name, tr):
    n, rows, cols = gparts.shape

    def body(gp_ref, w_ref, m_ref, v_ref, g_ref, d_ref, nm_ref, nv_ref):
        g = gp_ref[0].astype(F32)
        for j in range(1, n):
            g = g + gp_ref[j].astype(F32)
        g_ref[...] = g
        d_ref[...], nm_ref[...], nv_ref[...] = _adam_math(g, w_ref[...], m_ref[...], v_ref[...])

    blk = pl.BlockSpec((tr, cols), lambda i: (i, 0))
    return pl.pallas_call(
        body,
        name=name,
        grid=(rows // tr,),
        out_shape=[jax.ShapeDtypeStruct((rows, cols), F32)] * 4,
        in_specs=[pl.BlockSpec((n, tr, cols), lambda i: (0, i, 0)), blk, blk, blk],
        out_specs=[blk] * 4,
        compiler_params=_params(("parallel",)),
    )(gparts, w, m, v)


def _adam_rows(parts, ws, ms, vs, name):
    n = parts.shape[0]
    k = len(ws)
    widths = [w.shape[1] for w in ws]

    def body(*refs):
        p_ref, w_refs, m_refs, v_refs = refs[0], refs[1:1 + k], refs[1 + k:1 + 2 * k], refs[1 + 2 * k:1 + 3 * k]
        outs = refs[1 + 3 * k:]
        total = p_ref[0:1, :]
        for j in range(1, n):
            total = total + p_ref[j:j + 1, :]
        off = 0
        for a, width in enumerate(widths):
            g = total[:, off:off + width]
            off += width
            outs[4 * a][...] = g
            outs[4 * a + 1][...], outs[4 * a + 2][...], outs[4 * a + 3][...] = _adam_math(
                g, w_refs[a][...], m_refs[a][...], v_refs[a][...])

    return pl.pallas_call(
        body, name=name,
        out_shape=[jax.ShapeDtypeStruct((1, width), F32) for width in widths for _ in range(4)],
        compiler_params=_params(),
    )(parts, *ws, *ms, *vs)


def _mm(a_list, b, name, out_dtype, tm, tn, comm=None):
    n_a = len(a_list)
    m, k = a_list[0].shape
    n = b.shape[1]

    def body(*refs):
        o_ref = refs[2 * n_a]
        acc = _dot(refs[0][...].astype(BF16), refs[n_a][...].astype(BF16))
        for g in range(1, n_a):
            acc = acc + _dot(refs[g][...].astype(BF16), refs[n_a + g][...].astype(BF16))
        o_ref[...] = acc.astype(out_dtype)

    (out,), got = _hosted_call(
        body, comm, name=name, grid=(n // tn, m // tm),
        out_shape=[jax.ShapeDtypeStruct((m, n), out_dtype)],
        in_specs=[pl.BlockSpec((tm, k), lambda j, i: (i, 0))] * n_a
        + [pl.BlockSpec((k, tn), lambda j, i, g=g: (g, j)) for g in range(n_a)],
        out_specs=[pl.BlockSpec((tm, tn), lambda j, i: (i, j))],
        scratch_shapes=[], dims=("parallel", "parallel"), args=(*a_list, *([b] * n_a)))
    return out, got


def _mm_nt(a, b, name, out_dtype, tm, tn, n_out=1):
    m, k = a.shape
    n = b.shape[0] // n_out
    nt = n // tn

    def body(a_ref, *refs):
        a_blk = a_ref[...].astype(BF16)
        for b_ref, o_ref in zip(refs[:n_out], refs[n_out:]):
            o_ref[...] = _dot_nt(a_blk, b_ref[...].astype(BF16)).astype(out_dtype)

    outs = pl.pallas_call(
        body,
        name=name,
        grid=(nt, m // tm),
        out_shape=[jax.ShapeDtypeStruct((m, n), out_dtype)] * n_out,
        in_specs=[pl.BlockSpec((tm, k), lambda j, i: (i, 0))]
        + [pl.BlockSpec((tn, k), lambda j, i, o=o: (o * nt + j, 0)) for o in range(n_out)],
        out_specs=[pl.BlockSpec((tm, tn), lambda j, i: (i, j))] * n_out,
        compiler_params=_params(("parallel", "parallel")),
    )(a, *([b] * n_out))
    return outs[0] if n_out == 1 else outs


def _mm_tn(a, b_list, name, tm, tn, tk):
    s, m = a.shape
    n = b_list[0].shape[1]
    n_b = len(b_list)

    def body(a_ref, *refs):
        b_refs, o_refs = refs[:n_b], refs[n_b:]

        @pl.when(pl.program_id(2) == 0)
        def _():
            for o_ref in o_refs:
                o_ref[...] = jnp.zeros_like(o_ref)

        a_blk = a_ref[...].astype(BF16)
        for b_ref, o_ref in zip(b_refs, o_refs):
            o_ref[...] += _dot_tn(a_blk, b_ref[...].astype(BF16))

    return pl.pallas_call(
        body,
        name=name,
        grid=(n // tn, m // tm, s // tk),
        out_shape=[jax.ShapeDtypeStruct((m, n), F32)] * n_b,
        in_specs=[pl.BlockSpec((tk, tm), lambda j, i, k: (k, i))] + [pl.BlockSpec((tk, tn), lambda j, i, k: (k, j))] * n_b,
        out_specs=[pl.BlockSpec((tm, tn), lambda j, i, k: (i, j))] * n_b,
        compiler_params=_params(("parallel", "parallel", "arbitrary")),
    )(a, *b_list)


def _rows_call(fn, name, rows, params, out_rows, out_accs, ts, comm=None):
    s = rows[0].shape[0]
    nr, npar, no = len(rows), len(params), len(out_rows)

    def body(*refs):
        r, p = refs[:nr], refs[nr:nr + npar]
        o, acc = refs[nr + npar:nr + npar + no], refs[nr + npar + no:]
        outs, sums = fn(*[t[...] for t in r], *[t[...] for t in p])
        for ref, val in zip(o, outs):
            ref[...] = val.astype(ref.dtype)
        if acc:
            @pl.when(pl.program_id(0) == 0)
            def _():
                for ref in acc:
                    ref[...] = jnp.zeros_like(ref)

            for ref, val in zip(acc, sums):
                ref[...] += val

    outs, got = _hosted_call(
        body, comm, name=name, grid=(s // ts,),
        out_shape=[jax.ShapeDtypeStruct((s, w), dt) for w, dt in out_rows]
        + [jax.ShapeDtypeStruct(shape, F32) for shape in out_accs],
        in_specs=[pl.BlockSpec((ts, t.shape[1]), lambda i: (i, 0)) for t in rows]
        + [pl.BlockSpec(t.shape, lambda i: (0, 0)) for t in params],
        out_specs=[pl.BlockSpec((ts, w), lambda i: (i, 0)) for w, _ in out_rows]
        + [pl.BlockSpec(shape, lambda i: (0, 0)) for shape in out_accs],
        scratch_shapes=[], dims=("arbitrary",), args=(*rows, *params))
    return outs if comm is None else (outs, got)


def _fwd_norm1(x, g, scale, shift):
    return (_norm_mod(x, g, scale, shift),), ()


def _resid_norm(x, mixed, gate, g, scale, shift):
    x1 = x + (1.0 + gate) * mixed
    return x1, _norm_mod(x1, g, scale, shift)


def _fwd_resid_norm(x, mixed, gate, g, scale, shift):
    return _resid_norm(x, mixed, gate, g, scale, shift), ()


def _final(x1, ffn, tgt, gate, g):
    def head(x1, ffn, gate, g):
        return _rms(x1 + (1.0 + gate) * ffn, g)

    y, vjp = jax.vjp(head, x1, ffn, gate, g)
    err = y - tgt
    dx2, dffn, dgate, dg = vjp(err * (1.0 / D))
    sq = jnp.sum(jnp.sum(err * err, axis=1, keepdims=True), axis=0, keepdims=True)
    loss = jnp.broadcast_to(sq * (0.5 / D), (1, 128))
    return (dx2, dffn), (dgate, dg, loss)


def _bwd_resid_norm(x, mixed, dx2, dh2, gate, g, scale, shift):
    _, vjp = jax.vjp(_resid_norm, x, mixed, gate, g, scale, shift)
    dx, dmixed, dgate, dg, dscale, dshift = vjp((dx2, dh2))
    return (dx, dmixed), (dgate, dg, dscale, dshift)


def _bwd_norm1(x, dh, dx1, g, scale, shift):
    _, vjp = jax.vjp(_norm_mod, x, g, scale, shift)
    dx, dg, dscale, dshift = vjp(dh)
    return (dx1 + dx,), (dg, dscale, dshift)


def _ada_fwd(c16, w):
    def body(c_ref, w_ref, o_ref):
        c = c_ref[...]
        o_ref[...] = _dot((c * _sigmoid(c)).astype(BF16), w_ref[...].astype(BF16))

    return pl.pallas_call(
        body, name="ada_fwd", out_shape=jax.ShapeDtypeStruct((c16.shape[0], w.shape[1]), F32),
        compiler_params=_params(),
    )(c16, w)


def _ada_bwd(c16, dmod16):
    def body(c_ref, d_ref, o_ref):
        c = c_ref[...]
        o_ref[...] = _dot_tn((c * _sigmoid(c)).astype(BF16), d_ref[...].astype(BF16))

    return pl.pallas_call(
        body, name="ada_bwd", out_shape=jax.ShapeDtypeStruct((c16.shape[1], dmod16.shape[1]), F32),
        compiler_params=_params(),
    )(c16, dmod16)


CONV_R = 256


def _conv_window(win, wc, bc):
    s1 = pltpu.roll(win, 1, 0)
    s2 = pltpu.roll(win, 2, 0)
    u = bc + wc[0:1] * s2 + wc[1:2] * s1 + wc[2:3] * win
    return u[8:], s1[8:], s2[8:], win[8:]


def _row_windows(ref, before, after, n_after, lanes):
    ts, r = ref.shape[0], CONV_R

    def middle(i):
        return ref[pl.ds(pl.multiple_of(i * r, r) - 8, r + 8 + n_after), lanes]

    first = jnp.concatenate([before, ref[0:r + n_after, lanes]], axis=0)
    last = ref[ts - r - 8:ts, lanes] if n_after == 0 else jnp.concatenate([ref[ts - r - 8:ts, lanes], after], axis=0)
    return first, middle, last


def _ffn_specs(ts, s):
    cur = pl.BlockSpec((ts, FF_T), lambda j, i: (i, j))
    halo = pl.BlockSpec((8, FF_T), lambda j, i: (jnp.maximum(i * (ts // 8) - 1, 0), j))
    nxt = pl.BlockSpec((8, FF_T), lambda j, i: (jnp.minimum((i + 1) * (ts // 8), s // 8 - 1), j))
    wc = [pl.BlockSpec((8, FF_T), lambda j, i, h=h: (0, h * N_FT + j)) for h in range(2)]
    bc = [pl.BlockSpec((1, FF_T), lambda j, i, h=h: (0, h * N_FT + j)) for h in range(2)]
    return cur, halo, nxt, wc, bc


def _conv_glu_fwd(up_v, up_g, wc, bc, ts):
    s = up_v.shape[0]
    cur, halo, _, wcs, bcs = _ffn_specs(ts, s)

    def body(v_ref, vh_ref, g_ref, gh_ref, wcv_ref, wcg_ref, bcv_ref, bcg_ref, a_ref):
        keep = jnp.where(pl.program_id(1) == 0, 0.0, 1.0)
        for lanes in (slice(0, 128), slice(128, 256)):
            wcv, wcg, bcv, bcg = wcv_ref[:, lanes], wcg_ref[:, lanes], bcv_ref[:, lanes], bcg_ref[:, lanes]
            first_v, mid_v, _ = _row_windows(v_ref, vh_ref[:, lanes] * keep, None, 0, lanes)
            first_g, mid_g, _ = _row_windows(g_ref, gh_ref[:, lanes] * keep, None, 0, lanes)

            def emit(win_v, win_g, start, lanes=lanes, wcv=wcv, wcg=wcg, bcv=bcv, bcg=bcg):
                val = _conv_window(win_v, wcv, bcv)[0]
                gte = _conv_window(win_g, wcg, bcg)[0]
                a_ref[pl.ds(start, CONV_R), lanes] = (val * (gte * _sigmoid(gte))).astype(BF16)

            emit(first_v, first_g, 0)

            def loop(i, carry, emit=emit, mid_v=mid_v, mid_g=mid_g):
                emit(mid_v(i), mid_g(i), pl.multiple_of(i * CONV_R, CONV_R))
                return carry

            lax.fori_loop(1, ts // CONV_R, loop, 0)

    return pl.pallas_call(
        body, name="conv_glu_fwd", grid=(N_FT, s // ts),
        out_shape=jax.ShapeDtypeStruct((s, D_FF), BF16),
        in_specs=[cur, halo, cur, halo, *wcs, *bcs], out_specs=cur,
        compiler_params=_params(("parallel", "arbitrary")),
    )(up_v, up_v, up_g, up_g, wc, wc, bc, bc)


def _glu_bwd(val, gte, da):
    sg = _sigmoid(gte)
    return da * (gte * sg), da * val * (sg * (1.0 + gte * (1.0 - sg)))


def _conv_glu_bwd(up_v, up_g, da, wc, bc, ts, comm=None):
    s = up_v.shape[0]
    nblk = s // ts
    cur, halo, nxt, wcs, bcs = _ffn_specs(ts, s)
    acc_w = pl.BlockSpec((8, FF_T), lambda j, i: (0, j))
    acc_b = pl.BlockSpec((1, FF_T), lambda j, i: (0, j))

    def body(v_ref, vh_ref, vn_ref, g_ref, gh_ref, gn_ref, da_ref, dan_ref, wcv_ref, wcg_ref, bcv_ref, bcg_ref,
             dupv_ref, dupg_ref, dwcv_ref, dwcg_ref, dbcv_ref, dbcg_ref):
        first = pl.program_id(1) == 0
        keep = jnp.where(first, 0.0, 1.0)
        keep_next = jnp.where(pl.program_id(1) == nblk - 1, 0.0, 1.0)
        r = CONV_R
        n = ts // r

        @pl.when(first)
        def _():
            for ref in (dwcv_ref, dwcg_ref, dbcv_ref, dbcg_ref):
                ref[...] = jnp.zeros_like(ref)

        for lanes in (slice(0, 128), slice(128, 256)):
            wcv, wcg, bcv, bcg = wcv_ref[:, lanes], wcg_ref[:, lanes], bcv_ref[:, lanes], bcg_ref[:, lanes]
            first_v, mid_v, last_v = _row_windows(v_ref, vh_ref[:, lanes] * keep, vn_ref[:, lanes], 8, lanes)
            first_g, mid_g, last_g = _row_windows(g_ref, gh_ref[:, lanes] * keep, gn_ref[:, lanes], 8, lanes)

            def emit(win_v, win_g, da_w, start, lanes=lanes, wcv=wcv, wcg=wcg, bcv=bcv, bcg=bcg):
                u_v, s1_v, s2_v, x_v = _conv_window(win_v, wcv, bcv)
                u_g, s1_g, s2_g, x_g = _conv_window(win_g, wcg, bcg)
                du_v, du_g = _glu_bwd(u_v, u_g, da_w)
                rows = pl.ds(start, r)
                for du, s1, s2, x, wc, dup_ref, dwc_ref, dbc_ref in (
                        (du_v, s1_v, s2_v, x_v, wcv, dupv_ref, dwcv_ref, dbcv_ref),
                        (du_g, s1_g, s2_g, x_g, wcg, dupg_ref, dwcg_ref, dbcg_ref)):
                    dup = wc[2:3] * du + wc[1:2] * pltpu.roll(du, r + 7, 0) + wc[0:1] * pltpu.roll(du, r + 6, 0)
                    dup_ref[rows, lanes] = dup[:r].astype(BF16)
                    du = du[:r]
                    dwc_ref[0:1, lanes] += jnp.sum(du * s2[:r], axis=0, keepdims=True)
                    dwc_ref[1:2, lanes] += jnp.sum(du * s1[:r], axis=0, keepdims=True)
                    dwc_ref[2:3, lanes] += jnp.sum(du * x[:r], axis=0, keepdims=True)
                    dbc_ref[:, lanes] += jnp.sum(du, axis=0, keepdims=True)

            emit(first_v, first_g, da_ref[0:r + 8, lanes], 0)

            def loop(i, carry, emit=emit, mid_v=mid_v, mid_g=mid_g, lanes=lanes):
                start = pl.multiple_of(i * r, r)
                emit(mid_v(i), mid_g(i), da_ref[pl.ds(start, r + 8), lanes], start)
                return carry

            lax.fori_loop(1, n - 1, loop, 0)
            da_last = jnp.concatenate([da_ref[ts - r:ts, lanes], dan_ref[:, lanes] * keep_next], axis=0)
            emit(last_v, last_g, da_last, ts - r)

    return _hosted_call(
        body, comm, name="conv_glu_bwd", grid=(N_FT, nblk),
        out_shape=[jax.ShapeDtypeStruct((s, D_FF), BF16)] * 2 + [jax.ShapeDtypeStruct((8, D_FF), F32)] * 2
        + [jax.ShapeDtypeStruct((1, D_FF), F32)] * 2,
        in_specs=[cur, halo, nxt, cur, halo, nxt, cur, nxt, *wcs, *bcs],
        out_specs=[cur, cur, acc_w, acc_w, acc_b, acc_b],
        scratch_shapes=[], dims=("parallel", "arbitrary"),
        args=(up_v, up_v, up_v, up_g, up_g, up_g, da, da, wc, wc, bc, bc))


def _tri(kind):
    b = SB_B
    m = {"lower_strict": np.tril(np.ones((b, b)), -1), "upper_incl": np.triu(np.ones((b, b)), 0),
         "upper_strict": np.triu(np.ones((b, b)), 1)}[kind]
    return jnp.asarray(np.concatenate([m, np.ones((b, 128))], axis=1), BF16)


def _key_sums(x, tri):
    hi, lo = _hilo(x)
    cb = _dot(hi, tri) + _dot(lo, tri)
    return cb[:, :SB_B], cb[:, SB_B:]


def _sb_fwd(proj, comm=None):
    s = proj.shape[0]
    b = SB_B

    def body(q_ref, k_ref, v_ref, tri_ref, o_ref, t_ref, first_ref, c_ref, a_ref):
        i = pl.program_id(1)
        lane = lax.broadcasted_iota(jnp.int32, (b, 128), 1)
        heads = (lane < 64, lane >= 64)
        causal = lax.broadcasted_iota(jnp.int32, (b, b), 1) < lax.broadcasted_iota(jnp.int32, (b, b), 0)
        q = q_ref[...] * 0.125
        qm = [jnp.where(h, q, 0.0).astype(BF16) for h in heads]
        c_ref[...] = jnp.zeros_like(c_ref)
        a_ref[...] = jnp.zeros_like(a_ref)

        def prepare(jj, masked):
            rows = pl.ds(pl.multiple_of(jj * b, b), b)
            kb = k_ref[rows, :].astype(BF16)
            vb = v_ref[rows, :]
            out = []
            for hh in range(2):
                z = _dot_nt(qm[hh], kb)
                lg = _log_sigmoid(-z)
                if masked:
                    lg = jnp.where(causal, lg, 0.0)
                after, total = _key_sums(lg, tri_ref[...])
                out.append((lg + z + after, total, jnp.where(heads[hh], vb, 0.0).astype(BF16)))
            return out

        def walk_blocks(blocks):
            for hh in range(2):
                c, a = c_ref[hh], a_ref[hh]
                for pre, masked in blocks:
                    logw, total, vm = pre[hh]
                    w = jnp.exp(logw + jnp.concatenate([c, c], axis=1))
                    if masked:
                        w = jnp.where(causal, w, 0.0)
                    a = a + _dot(w.astype(BF16), vm)
                    c = c + total
                c_ref[hh], a_ref[hh] = c, a

        def largest_sum():
            return jnp.max(jnp.maximum(c_ref[0], c_ref[1]))

        @pl.when(i == 0)
        def _():
            walk_blocks([(prepare(i, True), True)])

        @pl.when(i > 0)
        def _():
            walk_blocks([(prepare(i, True), True), (prepare(i - 1, False), False)])

        def more(state):
            jj, top = state
            return jnp.logical_and(jj >= 0, top > SB_DEAD)

        def walk(state):
            jj, _ = state
            walk_blocks([(prepare(jj, False), False)])
            return jj - 1, largest_sum()

        jj, _ = lax.while_loop(more, walk, (jnp.maximum(i - 2, -1), largest_sum()))
        o_ref[...] = (a_ref[0] + a_ref[1]).astype(BF16)
        t_ref[...] = jnp.concatenate([c_ref[0], c_ref[1]], axis=1)
        first_ref[pl.program_id(0), i] = (jj + 1).astype(F32)

    return _hosted_call(
        body, comm, name="sb_fwd", grid=(4, s // b),
        out_shape=[jax.ShapeDtypeStruct((s, SB_W), BF16), jax.ShapeDtypeStruct((s, 2 * SB_W), F32),
                   jax.ShapeDtypeStruct((4, s // b), F32)],
        in_specs=[pl.BlockSpec((b, 128), lambda p, i: (i, p)),
                  pl.BlockSpec((s, 128), lambda p, i: (0, 4 + p)),
                  pl.BlockSpec((s, 128), lambda p, i: (0, 8 + p)),
                  pl.BlockSpec((b, b + 128), lambda p, i: (0, 0))],
        out_specs=[pl.BlockSpec((b, 128), lambda p, i: (i, p)), pl.BlockSpec((b, 256), lambda p, i: (i, p)),
                   pl.BlockSpec(memory_space=pltpu.SMEM)],
        scratch_shapes=[pltpu.VMEM((2, b, 128), F32), pltpu.VMEM((2, b, 128), F32)],
        dims=("arbitrary", "arbitrary"), args=(proj, proj, proj, _tri("lower_strict")))


def _sb_bwd(proj, do, tot, first, comm=None):
    s = proj.shape[0]
    b = SB_B

    def body(q_ref, k_ref, v_ref, do_ref, t_ref, first_ref, ti_ref, ts_ref, dq_ref, dko_ref, dvo_ref, cl_ref, ce_ref,
             a_ref, dk_ref, dv_ref):
        i = pl.program_id(1)
        first = jnp.clip(first_ref[pl.program_id(0), i].astype(jnp.int32), 0, i)
        lane = lax.broadcasted_iota(jnp.int32, (b, 128), 1)
        heads = (lane < 64, lane >= 64)
        causal = lax.broadcasted_iota(jnp.int32, (b, b), 1) < lax.broadcasted_iota(jnp.int32, (b, b), 0)
        q = q_ref[...] * 0.125
        do = do_ref[...]
        qm = [jnp.where(h, q, 0.0).astype(BF16) for h in heads]
        dom = [jnp.where(h, do, 0.0).astype(BF16) for h in heads]
        cl_ref[...] = jnp.zeros_like(cl_ref)
        ce_ref[...] = jnp.zeros_like(ce_ref)
        a_ref[...] = jnp.zeros_like(a_ref)

        @pl.when(i == 0)
        def _():
            dk_ref[...] = jnp.zeros_like(dk_ref)
            dv_ref[...] = jnp.zeros_like(dv_ref)

        def prepare(jj, masked):
            rows = pl.ds(pl.multiple_of(jj * b, b), b)
            kf = k_ref[rows, :]
            kb = kf.astype(BF16)
            vb = v_ref[rows, :].astype(BF16)
            out = []
            for hh in range(2):
                z = _dot_nt(qm[hh], kb)
                lg = _log_sigmoid(-z)
                if masked:
                    lg = jnp.where(causal, lg, 0.0)
                upto, total = _key_sums(lg, ti_ref[...])
                lsz = lg + z
                out.append((lsz - upto, total, jnp.exp(lsz), _dot_nt(dom[hh], vb),
                            jnp.where(heads[hh], kf, 0.0).astype(BF16)))
            return rows, out

        def walk_blocks(blocks):
            grads = [[jnp.zeros((b, 128), F32), jnp.zeros((b, 128), F32)] for _ in blocks]
            for hh in range(2):
                cl, ce, a = cl_ref[hh], ce_ref[hh], a_ref[hh]
                t = t_ref[:, hh * 128:(hh + 1) * 128]
                for n, ((_, pre), masked) in enumerate(blocks):
                    logw, total, sig, dw, km = pre[hh]
                    w = jnp.exp(logw + jnp.concatenate([t - cl, t - cl], axis=1))
                    if masked:
                        w = jnp.where(causal, w, 0.0)
                    e = w * dw
                    sums = _dot(e.astype(BF16), ts_ref[...])
                    before, etot = sums[:, :SB_B], sums[:, SB_B:]
                    dz = e - sig * (e + before + jnp.concatenate([ce, ce], axis=1))
                    if masked:
                        dz = jnp.where(causal, dz, 0.0)
                    dzb = dz.astype(BF16)
                    a = a + _dot(dzb, km)
                    grads[n][0] = grads[n][0] + _dot_tn(dzb, qm[hh])
                    grads[n][1] = grads[n][1] + _dot_tn(w.astype(BF16), dom[hh])
                    cl = cl + total
                    ce = ce + etot
                cl_ref[hh], ce_ref[hh], a_ref[hh] = cl, ce, a
            for ((rows, _), _), (dk, dv) in zip(blocks, grads):
                dk_ref[rows, :] += dk
                dv_ref[rows, :] += dv

        def loop(jj, carry):
            walk_blocks([(prepare(jj, False), False)])
            return carry

        lax.fori_loop(first, i - 1, loop, 0)

        @pl.when(i == 0)
        def _():
            walk_blocks([(prepare(i, True), True)])

        @pl.when(i > 0)
        def _():
            walk_blocks([(prepare(i - 1, False), False), (prepare(i, True), True)])

        dq_ref[...] = ((a_ref[0] + a_ref[1]) * 0.125).astype(BF16)

        @pl.when(i == s // b - 1)
        def _():
            dko_ref[...] = dk_ref[...].astype(BF16)
            dvo_ref[...] = dv_ref[...].astype(BF16)

    blk = pl.BlockSpec((b, 128), lambda p, i: (i, p))
    full = pl.BlockSpec((s, 128), lambda p, i: (0, p))
    tri = pl.BlockSpec((b, b + 128), lambda p, i: (0, 0))
    return _hosted_call(
        body, comm, name="sb_bwd", grid=(4, s // b),
        out_shape=[jax.ShapeDtypeStruct((s, SB_W), BF16)] * 3,
        in_specs=[blk, pl.BlockSpec((s, 128), lambda p, i: (0, 4 + p)), pl.BlockSpec((s, 128), lambda p, i: (0, 8 + p)),
                  blk, pl.BlockSpec((b, 256), lambda p, i: (i, p)), pl.BlockSpec(memory_space=pltpu.SMEM), tri, tri],
        out_specs=[blk, full, full],
        scratch_shapes=[pltpu.VMEM((2, b, 128), F32)] * 3 + [pltpu.VMEM((s, 128), F32)] * 2,
        dims=("arbitrary", "arbitrary"),
        args=(proj, proj, proj, do, tot, first, _tri("upper_incl"), _tri("upper_strict")))


_COL_Q, _COL_K, _COL_V, _COL_G, _COL_F = 12, 14, 8, 10, 24


def _gla_consts():
    c = CHUNK
    incl = np.tril(np.ones((c, c)), 0)
    strict = np.tril(np.ones((c, c)), -1)
    bd = np.zeros((256, 128))
    bd[:128, :64] = 1.0
    bd[128:, 64:] = 1.0
    return jnp.asarray(incl, BF16), jnp.asarray(strict, BF16), jnp.asarray(bd, F32)


def _time_sums(tri, x):
    hi, lo = _hilo(x)
    return _dot(tri, hi) + _dot(tri, lo)


def _gla_gate(o, gg, g):
    parts = []
    for h in range(2):
        oh = o[:, h * 128:(h + 1) * 128]
        parts.append(oh * lax.rsqrt(jnp.mean(oh * oh, axis=-1, keepdims=True) + EPS))
    return (jnp.concatenate(parts, axis=1) * g) * (gg * _sigmoid(gg))


def _gla_chunk(la_c, k_c, incl):
    cum = _time_sums(incl, la_c)
    total = cum[CHUNK - 1:CHUNK]
    edec = jnp.exp(total - cum)
    return edec, k_c * edec, jnp.exp(total)


def _gla_fwd(proj, wfg, bfg, gain):
    s = proj.shape[0]
    t = GLA_T
    nch = t // CHUNK
    incl, _, bd = _gla_consts()

    def body(q_ref, k_ref, v_ref, gg_ref, f_ref, wf_ref, bf_ref, g_ref, incl_ref, bd_ref, o_ref, st_ref, state_ref):
        @pl.when(pl.program_id(0) == 0)
        def _():
            state_ref[...] = jnp.zeros_like(state_ref)

        la = _log_sigmoid(_dot(f_ref[...].astype(BF16), wf_ref[...].astype(BF16)) + bf_ref[...]) * (1.0 / 16.0)
        states = [state_ref[0], state_ref[1]]
        for cc in range(nch):
            rows = slice(cc * CHUNK, (cc + 1) * CHUNK)
            for p in range(2):
                kl, vl = slice(p * 128, (p + 1) * 128), slice(p * 256, (p + 1) * 256)
                _, kdec, dec = _gla_chunk(la[rows, kl], k_ref[rows, kl], incl_ref[...])
                kv = _dot_tn(v_ref[rows, vl].astype(BF16), kdec.astype(BF16))
                states[p] = states[p] * dec + bd_ref[...] * kv
                st_ref[p, cc] = states[p]
                o = _dot_nt((q_ref[rows, kl] * 0.125).astype(BF16), states[p].astype(BF16))
                o_ref[rows, vl] = _gla_gate(o, gg_ref[rows, vl], g_ref[:, vl]).astype(BF16)
        state_ref[0] = states[0]
        state_ref[1] = states[1]

    def col(width, blk):
        return pl.BlockSpec((t, width), lambda i: (i, blk))

    def whole(shape):
        return pl.BlockSpec(shape, lambda i: tuple(0 for _ in shape))

    return pl.pallas_call(
        body, name="gla_fwd", grid=(s // t,),
        out_shape=[jax.ShapeDtypeStruct((s, GLA_VW), BF16), jax.ShapeDtypeStruct((2, s // CHUNK, 256, 128), F32)],
        in_specs=[col(256, _COL_Q // 2), col(256, _COL_K // 2), col(512, _COL_V // 2), col(512, _COL_G // 2),
                  col(128, _COL_F), whole((128, GLA_KW)), whole((1, GLA_KW)), whole((1, GLA_VW)),
                  whole((CHUNK, CHUNK)), whole((256, 128))],
        out_specs=[col(512, 0), pl.BlockSpec((2, nch, 256, 128), lambda i: (0, i, 0, 0))],
        scratch_shapes=[pltpu.VMEM((2, 256, 128), F32)],
        compiler_params=_params(("arbitrary",)),
    )(proj, proj, proj, proj, proj, wfg, bfg, gain, incl, bd)


def _gla_bwd(proj, do, states, wfg, bfg, gain):
    s = proj.shape[0]
    t = GLA_T
    nch = t // CHUNK
    nblk = s // t
    incl, strict, bd = _gla_consts()

    def body(q_ref, k_ref, v_ref, gg_ref, f_ref, do_ref, st_ref, sp_ref, wf_ref, bf_ref, g_ref, incl_ref, str_ref,
             bd_ref, dq_ref, dk_ref, dv_ref, dgg_ref, df_ref, dwf_ref, dbf_ref, dg_ref, carry_ref, dfs_ref):
        i = pl.program_id(0)

        @pl.when(i == 0)
        def _():
            carry_ref[...] = jnp.zeros_like(carry_ref)
            dwf_ref[...] = jnp.zeros_like(dwf_ref)
            dbf_ref[...] = jnp.zeros_like(dbf_ref)
            dg_ref[...] = jnp.zeros_like(dg_ref)

        fb = f_ref[...].astype(BF16)
        wf = wf_ref[...].astype(BF16)
        f = _dot(fb, wf) + bf_ref[...]
        la = _log_sigmoid(f) * (1.0 / 16.0)
        dla_df = _sigmoid(-f) * (1.0 / 16.0)
        first_block = jnp.where(i == nblk - 1, 0.0, 1.0)
        carries = [carry_ref[0], carry_ref[1]]
        dgains = [jnp.zeros((1, 256), F32), jnp.zeros((1, 256), F32)]
        local = {}
        for cc in reversed(range(nch)):
            rows = slice(cc * CHUNK, (cc + 1) * CHUNK)
            for p in range(2):
                kl, vl = slice(p * 128, (p + 1) * 128), slice(p * 256, (p + 1) * 256)
                edec, kdec, dec = _gla_chunk(la[rows, kl], k_ref[rows, kl], incl_ref[...])
                qs = (q_ref[rows, kl] * 0.125).astype(BF16)
                sb16 = st_ref[p, cc].astype(BF16)
                o = _dot_nt(qs, sb16)
                _, vjp = jax.vjp(_gla_gate, o, gg_ref[rows, vl], g_ref[:, vl])
                do_c, dgg_c, dg_c = vjp(do_ref[rows, vl])
                dgains[p] = dgains[p] + dg_c
                dgg_ref[rows, vl] = dgg_c.astype(BF16)
                do16 = do_c.astype(BF16)
                dq_ref[rows, kl] = (_dot(do16, sb16) * 0.125).astype(BF16)
                local[cc, p] = (edec, kdec, dec, bd_ref[...] * _dot_tn(do16, qs))
        for cc in reversed(range(nch)):
            rows = slice(cc * CHUNK, (cc + 1) * CHUNK)
            for p in range(2):
                kl, vl = slice(p * 128, (p + 1) * 128), slice(p * 256, (p + 1) * 256)
                edec, kdec, dec, readers = local[cc, p]
                prev = st_ref[p, cc - 1] if cc > 0 else sp_ref[p, 0] * first_block
                grad = readers + carries[p]
                g16 = grad.astype(BF16)
                dv_ref[rows, vl] = _dot_nt(kdec.astype(BF16), g16).astype(BF16)
                dkdec = _dot(v_ref[rows, vl].astype(BF16), g16)
                ddec = jnp.sum(grad * prev, axis=0, keepdims=True) * dec
                dk_ref[rows, kl] = (dkdec * edec).astype(BF16)
                dla = _time_sums(str_ref[...], dkdec * kdec) + ddec
                dfs_ref[rows, kl] = dla * dla_df[rows, kl]
                carries[p] = grad * dec
        carry_ref[0] = carries[0]
        carry_ref[1] = carries[1]
        df = dfs_ref[...]
        df16 = df.astype(BF16)
        df_ref[...] = _dot_nt(df16, wf).astype(BF16)
        dwf_ref[...] += _dot_tn(fb, df16)
        dbf_ref[...] += jnp.sum(df, axis=0, keepdims=True)
        dg_ref[...] += jnp.concatenate(dgains, axis=1)

    def col(width, blk):
        return pl.BlockSpec((t, width), lambda i: (nblk - 1 - i, blk))

    def whole(shape):
        return pl.BlockSpec(shape, lambda i: tuple(0 for _ in shape))

    return pl.pallas_call(
        body, name="gla_bwd", grid=(nblk,),
        out_shape=[jax.ShapeDtypeStruct((s, GLA_KW), BF16), jax.ShapeDtypeStruct((s, GLA_KW), BF16),
                   jax.ShapeDtypeStruct((s, GLA_VW), BF16), jax.ShapeDtypeStruct((s, GLA_VW), BF16),
                   jax.ShapeDtypeStruct((s, 128), BF16), jax.ShapeDtypeStruct((128, GLA_KW), F32),
                   jax.ShapeDtypeStruct((1, GLA_KW), F32), jax.ShapeDtypeStruct((1, GLA_VW), F32)],
        in_specs=[col(256, _COL_Q // 2), col(256, _COL_K // 2), col(512, _COL_V // 2), col(512, _COL_G // 2),
                  col(128, _COL_F), col(512, 1),
                  pl.BlockSpec((2, nch, 256, 128), lambda i: (0, nblk - 1 - i, 0, 0)),
                  pl.BlockSpec((2, 1, 256, 128), lambda i: (0, jnp.maximum((nblk - 1 - i) * nch - 1, 0), 0, 0)),
                  whole((128, GLA_KW)), whole((1, GLA_KW)), whole((1, GLA_VW)),
                  whole((CHUNK, CHUNK)), whole((CHUNK, CHUNK)), whole((256, 128))],
        out_specs=[col(256, 0), col(256, 0), col(512, 0), col(512, 0), col(128, 0),
                   whole((128, GLA_KW)), whole((1, GLA_KW)), whole((1, GLA_VW))],
        scratch_shapes=[pltpu.VMEM((2, 256, 128), F32), pltpu.VMEM((t, GLA_KW), F32)],
        compiler_params=_params(("arbitrary",)),
    )(proj, proj, proj, proj, proj, do, states, states, wfg, bfg, gain, incl, strict, bd)


def _cols_by_dev(a, n_dev=N_DEV):
    r, c = a.shape
    return a.reshape(r, n_dev, c // n_dev).transpose(1, 0, 2)


def _cols_from_dev(a):
    _, r, n = a.shape
    return a.transpose(1, 0, 2).reshape(r, N_DEV * n)


def kernel(x, c, w_ada, b_ada, g_norm1, w_in, w_fg2, b_fg2, g_gla_out, w_out, g_norm2, w_up, w_conv, b_conv, w_down, g_final, loss_target, m_w_ada, m_b_ada, m_g_norm1, m_w_in, m_w_fg2, m_b_fg2, m_g_gla_out, m_w_out, m_g_norm2, m_w_up, m_w_conv, m_b_conv, m_w_down, m_g_final, v_w_ada, v_b_ada, v_g_norm1, v_w_in, v_w_fg2, v_b_fg2, v_g_gla_out, v_w_out, v_g_norm2, v_w_up, v_w_conv, v_b_conv, v_w_down, v_g_final):
    s = x.shape[1]
    me = 4 * lax.axis_index("x") + 2 * lax.axis_index("y") + lax.axis_index("c")
    xs, tgt = x[0], loss_target[0]
    ts = min(512, s)
    tm = min(1024, s)
    tc = min(2048, s // 2)

    c_all, wconv_g, wfg_g = _exchange([c, w_conv[0], w_fg2[0]], "gather_first", True)
    wconv_p = jnp.pad(_cols_from_dev(wconv_g), ((0, 5), (0, 0)))
    wfg_p = jnp.pad(_cols_from_dev(wfg_g), ((0, 128 - RANK), (0, 0)))

    c16 = jnp.pad(c_all.reshape(N_DEV, D), ((0, 8), (0, 0)))
    modp = _ada_fwd(c16, w_ada[0])[:N_DEV]
    (mod_all,) = _exchange([modp], "gather_mod", True)
    mod = lax.dynamic_index_in_dim(mod_all, me, axis=1, keepdims=False).reshape(1, 6 * D) + b_ada
    shift1, scale1, gate1, shift2, scale2, gate2 = [mod[:, k * D:(k + 1) * D] for k in range(6)]

    w_in_t, m_in_t, v_in_t = [jnp.swapaxes(t[0], 0, 1) for t in (w_in, m_w_in, v_w_in)]
    w_up_t, m_up_t, v_up_t = [jnp.swapaxes(t[0], 0, 1) for t in (w_up, m_w_up, v_w_up)]
    (h,), (win_g,) = _rows_call(_fwd_norm1, "norm1", [xs], [g_norm1, scale1, shift1], [(D, BF16)], [], ts,
                                comm=([w_in_t.astype(BF16)], True))
    win_t = jnp.pad(win_g.reshape(IN_W, D), ((0, IN_WP - IN_W), (0, 0)))
    proj = _mm_nt(h, win_t, "in_proj", F32, ts, IN_WP)
    (o_sb, sb_tot, sb_first), (wout_g, wup_g, wdown_g) = _sb_fwd(
        proj, comm=([w_out[0].astype(BF16), w_up_t.astype(BF16), w_down[0].astype(BF16)], True))
    wout = wout_g.reshape(D, D)
    wup_t = wup_g.reshape(2 * D_FF, D)
    wdown = wdown_g.reshape(D_FF, D)
    o_gla, states = _gla_fwd(proj, wfg_p, b_fg2, g_gla_out)
    cat = jnp.concatenate([o_sb, o_gla], axis=1)
    mixed, _ = _mm([cat], wout, "out_proj", F32, tm, 512)
    x1, h2 = _rows_call(_fwd_resid_norm, "resid_norm2", [xs, mixed], [gate1, g_norm2, scale2, shift2],
                        [(D, F32), (D, BF16)], [], ts)
    up_v, up_g = _mm_nt(h2, wup_t, "up_proj", F32, tm, 1408, n_out=2)
    act = _conv_glu_fwd(up_v, up_g, wconv_p, b_conv, tc)
    ffn, _ = _mm([act], wdown, "down_proj", F32, tm, 512)
    dx2, dffn, dgate2, dg_final, loss_part = _rows_call(
        _final, "final", [x1, ffn, tgt], [gate2, g_final.reshape(1, D)],
        [(D, F32), (D, BF16)], [(1, D), (1, D), (1, 128)], ts)

    (dw_down,) = _mm_tn(act, [dffn], "down_wgrad", 1408, 1024, tm)
    da = _mm_nt(dffn, wdown, "down_dgrad", F32, tm, 1408)
    (dup_v, dup_g, dwc_v, dwc_g, dbc_v, dbc_g), (r_down,) = _conv_glu_bwd(
        up_v, up_g, da, wconv_p, b_conv, tc, comm=([dw_down.reshape(N_DEV, 352, D)], False))
    (dw_up_tv,) = _mm_tn(dup_v, [h2], "up_wgrad_v", 1408, 1024, tm)
    (dw_up_tg,) = _mm_tn(dup_g, [h2], "up_wgrad_g", 1408, 1024, tm)
    dh2, _ = _mm([dup_v, dup_g], wup_t, "up_dgrad", F32, tm, 512)
    dx1, dmixed, dgate1, dg_norm2, dscale2, dshift2 = _rows_call(
        _bwd_resid_norm, "resid_norm2_bwd", [xs, mixed, dx2, dh2], [gate1, g_norm2, scale2, shift2],
        [(D, F32), (D, BF16)], [(1, D)] * 4, ts)
    (dw_out,) = _mm_tn(cat, [dmixed], "out_wgrad", 1024, 1024, tm)
    dcat = _mm_nt(dmixed, wout, "out_dgrad", F32, tm, 512)
    dw_up_s = jnp.concatenate([dw_up_tv, dw_up_tg], axis=0).reshape(N_DEV, 704, D)
    dwconv_s = _cols_by_dev(jnp.concatenate([dwc_v[:3], dwc_g[:3]], axis=1))
    (dq, dk, dv), (r_up, r_out, r_conv) = _sb_bwd(
        proj, dcat, sb_tot, sb_first, comm=([dw_up_s, dw_out.reshape(N_DEV, 128, D), dwconv_s], False))
    dgq, dgk, dgv, dgg, dgf, dwfg_p, dbfg, dg_gla = _gla_bwd(proj, dcat, states, wfg_p, b_fg2, g_gla_out)
    dproj = jnp.concatenate([dq, dk, dv, dgq, dgk, dgv, dgg, dgf], axis=1)
    (dw_in_t,) = _mm_tn(dproj, [h], "in_wgrad", IN_WP, 1024, tm)
    dh, (r_in, r_fg) = _mm([dproj], win_t, "in_dgrad", F32, tm, 512,
                           comm=([dw_in_t[:IN_W].reshape(N_DEV, 386, D).astype(BF16), _cols_by_dev(dwfg_p[:RANK])],
                                 False))
    grad_x, dg_norm1, dscale1, dshift1 = _rows_call(
        _bwd_norm1, "norm1_bwd", [xs, dh, dx1], [g_norm1, scale1, shift1], [(D, F32)], [(1, D)] * 3, ts)

    dmod = jnp.concatenate([dshift1, dscale1, dgate1, dshift2, dscale2, dgate2], axis=1)
    small_parts = [dmod, dg_norm1, dbfg, dg_gla, dg_norm2, dbc_v, dbc_g, dg_final, loss_part]
    n_small = sum(p.shape[1] for p in small_parts[:-1])
    (sg,) = _exchange([jnp.concatenate(small_parts, axis=1)], "gather_small_grads", True)
    sg = sg.reshape(N_DEV, n_small + 128)
    loss = jnp.sum(sg[:, n_small])
    small_w = [b_ada, g_norm1, b_fg2, g_gla_out, g_norm2, b_conv, g_final.reshape(1, D)]
    small_m = [m_b_ada, m_g_norm1, m_b_fg2, m_g_gla_out, m_g_norm2, m_b_conv, m_g_final.reshape(1, D)]
    small_v = [v_b_ada, v_g_norm1, v_b_fg2, v_g_gla_out, v_g_norm2, v_b_conv, v_g_final.reshape(1, D)]
    s_out = list(_adam_rows(sg, small_w, small_m, small_v, "adam_small"))
    s_out[24:] = [t.reshape(D) for t in s_out[24:]]
    s_g, s_d, s_m, s_v = [s_out[k::4] for k in range(4)]

    dmod_all = sg[:, :6 * D].reshape(N_DEV, N_DEV, 768)
    dmod_mine = lax.dynamic_index_in_dim(dmod_all, me, axis=1, keepdims=False)
    dw_ada = _ada_bwd(c16, jnp.pad(dmod_mine, ((0, 8), (0, 0))))
    a_g, a_d, a_m, a_v = [t[None] for t in _adam(dw_ada[None], w_ada[0], m_w_ada[0], v_w_ada[0], "adam_ada", 256)]

    recv = [r_in, r_fg, r_out, r_up, r_conv, r_down]
    big_w = [w_in_t, w_fg2[0], w_out[0], w_up_t, w_conv[0], w_down[0]]
    big_m = [m_in_t, m_w_fg2[0], m_w_out[0], m_up_t, m_w_conv[0], m_w_down[0]]
    big_v = [v_in_t, v_w_fg2[0], v_w_out[0], v_up_t, v_w_conv[0], v_w_down[0]]
    big_name = ["adam_in", "adam_fg2", "adam_out", "adam_up", "adam_conv", "adam_down"]
    big_rows = [386, RANK, 128, 176, 3, 176]
    b_out = [_adam(r, w, m, v, name, tr)
             for r, w, m, v, name, tr in zip(recv, big_w, big_m, big_v, big_name, big_rows)]
    for k in (0, 3):
        b_out[k] = [jnp.swapaxes(t, 0, 1) for t in b_out[k]]
    b_g, b_d, b_m, b_v = [[o[k][None] for o in b_out] for k in range(4)]

    def ordered(a, sm, bg):
        return [a, sm[0], sm[1], bg[0], bg[1], sm[2], sm[3], bg[2], sm[4], bg[3], bg[4], sm[5], bg[5], sm[6]]

    return (loss, grad_x[None], *ordered(a_g, s_g, b_g), *ordered(a_d, s_d, b_d),
            *ordered(a_m, s_m, b_m), *ordered(a_v, s_v, b_v))
```

```python
import numpy as np

import jax
import jax.numpy as jnp
from jax import lax
from jax.experimental import pallas as pl
from jax.experimental.pallas import tpu as pltpu

F32, BF16 = jnp.float32, jnp.bfloat16
N_DEV = 8
D = 1024
SB_W = 512
GLA_KW, GLA_VW = 256, 512
RANK = 16
IN_W = 3088
IN_WP = 3200
D_FF = 2816
FF_T = 256
N_FT = D_FF // FF_T
EPS = 1e-6
SB_B = 256
SB_DEAD = -110.0
CHUNK = 64
GLA_T = 512
VMEM_LIMIT = 56 * 1024 * 1024

LR, B1, B2, ADAM_EPS, WD, STEP = 0.001, 0.9, 0.999, 1e-08, 0.01, 10


def _params(dims=None, vmem=True):
    kw = {}
    if dims is not None:
        kw["dimension_semantics"] = dims
    if vmem:
        kw["vmem_limit_bytes"] = VMEM_LIMIT
    return pltpu.CompilerParams(**kw)


def _dot(a, b):
    return jnp.dot(a, b, preferred_element_type=F32)


def _dot_nt(a, b):
    return lax.dot_general(a, b, (((1,), (1,)), ((), ())), preferred_element_type=F32)


def _dot_tn(a, b):
    return lax.dot_general(a, b, (((0,), (0,)), ((), ())), preferred_element_type=F32)


def _hilo(x):
    hi = x.astype(BF16)
    lo = (x - hi.astype(F32)).astype(BF16)
    return hi, lo


def _sigmoid(x):
    return 1.0 / (1.0 + jnp.exp(-x))


def _log_sigmoid(x):
    return jnp.minimum(x, 0.0) - jnp.log(1.0 + jnp.exp(-jnp.abs(x)))


def _rms(x, g):
    n = x * lax.rsqrt(jnp.mean(x * x, axis=-1, keepdims=True) + EPS)
    return n * g


def _norm_mod(x, g, scale, shift):
    return _rms(x, g) * (1.0 + scale) + shift


N_PEER = N_DEV - 1


def _exchange_copies(x_refs, out_refs, send_sems, recv_sems, local_sems, gather):
    ix, iy, ic = lax.axis_index("x"), lax.axis_index("y"), lax.axis_index("c")
    me = 4 * ix + 2 * iy + ic
    peers = []
    for k in range(1, N_DEV):
        px = 1 - ix if k & 4 else ix
        py = 1 - iy if k & 2 else iy
        pc = 1 - ic if k & 1 else ic
        peers.append(((px, py, pc), 4 * px + 2 * py + pc))

    def copy(a, k, dev, src_slot, dst_slot):
        return pltpu.make_async_remote_copy(
            src_ref=x_refs[a] if gather else x_refs[a].at[src_slot],
            dst_ref=out_refs[a].at[dst_slot],
            send_sem=send_sems.at[a * N_PEER + k],
            recv_sem=recv_sems.at[a * N_PEER + k],
            device_id=dev,
            device_id_type=pl.DeviceIdType.MESH,
        )

    n = len(x_refs)
    mine = [pltpu.make_async_copy(x_refs[a] if gather else x_refs[a].at[me], out_refs[a].at[me], local_sems.at[a])
            for a in range(n)]
    sends = [copy(a, k, dev, pid, me) for a in range(n) for k, (dev, pid) in enumerate(peers)]
    recvs = [copy(a, k, dev, pid, pid) for a in range(n) for k, (dev, pid) in enumerate(peers)]
    return mine, sends, recvs


def _exchange_scratch(n):
    return [pltpu.SemaphoreType.DMA((n * N_PEER,)), pltpu.SemaphoreType.DMA((n * N_PEER,)),
            pltpu.SemaphoreType.DMA((n,))]


def _exchange_shapes(arrays, gather):
    return [jax.ShapeDtypeStruct((N_DEV,) + tuple(x.shape if gather else x.shape[1:]), x.dtype) for x in arrays]


def _exchange(arrays, name, gather):
    n = len(arrays)

    def body(*refs):
        mine, sends, recvs = _exchange_copies(refs[:n], refs[n:2 * n], *refs[2 * n:], gather)
        for cp in mine + sends:
            cp.start()
        for cp in recvs:
            cp.wait_recv()
        for cp in sends:
            cp.wait_send()
        for cp in mine:
            cp.wait()

    return pl.pallas_call(
        body,
        name=name,
        out_shape=_exchange_shapes(arrays, gather),
        in_specs=[pl.BlockSpec(memory_space=pl.ANY)] * n,
        out_specs=[pl.BlockSpec(memory_space=pl.ANY)] * n,
        scratch_shapes=_exchange_scratch(n),
    )(*arrays)


def _hosted_call(body, comm, *, name, grid, in_specs, out_specs, out_shape, scratch_shapes, dims, args):
    if comm is None:
        outs = pl.pallas_call(body, name=name, grid=grid, in_specs=in_specs, out_specs=out_specs, out_shape=out_shape,
                              scratch_shapes=scratch_shapes, compiler_params=_params(dims))(*args)
        return outs, []
    arrays, gather = comm
    n_in, n_out, n_scr, nc = len(in_specs), len(out_specs), len(scratch_shapes), len(arrays)

    def hosted(*refs):
        ins, c_in = refs[:n_in], refs[n_in:n_in + nc]
        outs, c_out = refs[n_in + nc:n_in + nc + n_out], refs[n_in + nc + n_out:n_in + 2 * nc + n_out]
        rest = refs[n_in + 2 * nc + n_out:]
        scratch, sems = rest[:n_scr], rest[n_scr:]
        mine, sends, recvs = _exchange_copies(c_in, c_out, *sems, gather)
        first = pl.program_id(0) == 0
        last = pl.program_id(0) == grid[0] - 1
        for axis in range(1, len(grid)):
            first = jnp.logical_and(first, pl.program_id(axis) == 0)
            last = jnp.logical_and(last, pl.program_id(axis) == grid[axis] - 1)

        @pl.when(first)
        def _():
            for cp in mine + sends:
                cp.start()

        body(*ins, *outs, *scratch)

        @pl.when(last)
        def _():
            for cp in recvs:
                cp.wait_recv()
            for cp in sends:
                cp.wait_send()
            for cp in mine:
                cp.wait()

    any_spec = pl.BlockSpec(memory_space=pl.ANY)
    outs = pl.pallas_call(
        hosted, name=name, grid=grid,
        in_specs=list(in_specs) + [any_spec] * nc,
        out_specs=list(out_specs) + [any_spec] * nc,
        out_shape=list(out_shape) + _exchange_shapes(arrays, gather),
        scratch_shapes=list(scratch_shapes) + _exchange_scratch(nc),
        compiler_params=_params(tuple("arbitrary" for _ in grid)),
    )(*args, *arrays)
    return outs[:n_out], outs[n_out:]


def _adam_math(g, w, m, v):
    m_new = B1 * m + (1.0 - B1) * g
    v_new = B2 * v + (1.0 - B2) * (g * g)
    m_hat = m_new / (1.0 - B1 ** STEP)
    v_hat = v_new / (1.0 - B2 ** STEP)
    return -LR * (m_hat / (jnp.sqrt(v_hat) + ADAM_EPS) + WD * w), m_new, v_new


def _adam(gparts, w, m, v, name, tr):
    n, rows, cols = gparts.shape

    def body(gp_ref, w_ref, m_ref, v_ref, g_ref, d_ref, nm_ref, nv_ref):
        g = gp_ref[0].astype(F32)
        for j in range(1, n):
            g = g + gp_ref[j].astype(F32)
        g_ref[...] = g
        d_ref[...], nm_ref[...], nv_ref[...] = _adam_math(g, w_ref[...], m_ref[...], v_ref[...])

    blk = pl.BlockSpec((tr, cols), lambda i: (i, 0))
    return pl.pallas_call(
        body,
        name=name,
        grid=(rows // tr,),
        out_shape=[jax.ShapeDtypeStruct((rows, cols), F32)] * 4,
        in_specs=[pl.BlockSpec((n, tr, cols), lambda i: (0, i, 0)), blk, blk, blk],
        out_specs=[blk] * 4,
        compiler_params=_params(("parallel",)),
    )(gparts, w, m, v)


def _adam_rows(parts, ws, ms, vs, name):
    n = parts.shape[0]
    k = len(ws)
    widths = [w.shape[1] for w in ws]

    def body(*refs):
        p_ref, w_refs, m_refs, v_refs = refs[0], refs[1:1 + k], refs[1 + k:1 + 2 * k], refs[1 + 2 * k:1 + 3 * k]
        outs = refs[1 + 3 * k:]
        total = p_ref[0:1, :]
        for j in range(1, n):
            total = total + p_ref[j:j + 1, :]
        off = 0
        for a, width in enumerate(widths):
            g = total[:, off:off + width]
            off += width
            outs[4 * a][...] = g
            outs[4 * a + 1][...], outs[4 * a + 2][...], outs[4 * a + 3][...] = _adam_math(
                g, w_refs[a][...], m_refs[a][...], v_refs[a][...])

    return pl.pallas_call(
        body, name=name,
        out_shape=[jax.ShapeDtypeStruct((1, width), F32) for width in widths for _ in range(4)],
        compiler_params=_params(),
    )(parts, *ws, *ms, *vs)


def _mm(a_list, b, name, out_dtype, tm, tn, comm=None):
    n_a = len(a_list)
    m, k = a_list[0].shape
    n = b.shape[1]

    def body(*refs):
        o_ref = refs[2 * n_a]
        acc = _dot(refs[0][...].astype(BF16), refs[n_a][...].astype(BF16))
        for g in range(1, n_a):
            acc = acc + _dot(refs[g][...].astype(BF16), refs[n_a + g][...].astype(BF16))
        o_ref[...] = acc.astype(out_dtype)

    (out,), got = _hosted_call(
        body, comm, name=name, grid=(n // tn, m // tm),
        out_shape=[jax.ShapeDtypeStruct((m, n), out_dtype)],
        in_specs=[pl.BlockSpec((tm, k), lambda j, i: (i, 0))] * n_a
        + [pl.BlockSpec((k, tn), lambda j, i, g=g: (g, j)) for g in range(n_a)],
        out_specs=[pl.BlockSpec((tm, tn), lambda j, i: (i, j))],
        scratch_shapes=[], dims=("parallel", "parallel"), args=(*a_list, *([b] * n_a)))
    return out, got


def _mm_nt(a, b, name, out_dtype, tm, tn, n_out=1):
    m, k = a.shape
    n = b.shape[0] // n_out
    nt = n // tn

    def body(a_ref, *refs):
        a_blk = a_ref[...].astype(BF16)
        for b_ref, o_ref in zip(refs[:n_out], refs[n_out:]):
            o_ref[...] = _dot_nt(a_blk, b_ref[...].astype(BF16)).astype(out_dtype)

    outs = pl.pallas_call(
        body,
        name=name,
        grid=(nt, m // tm),
        out_shape=[jax.ShapeDtypeStruct((m, n), out_dtype)] * n_out,
        in_specs=[pl.BlockSpec((tm, k), lambda j, i: (i, 0))]
        + [pl.BlockSpec((tn, k), lambda j, i, o=o: (o * nt + j, 0)) for o in range(n_out)],
        out_specs=[pl.BlockSpec((tm, tn), lambda j, i: (i, j))] * n_out,
        compiler_params=_params(("parallel", "parallel")),
    )(a, *([b] * n_out))
    return outs[0] if n_out == 1 else outs


def _mm_tn(a, b_list, name, tm, tn, tk):
    s, m = a.shape
    n = b_list[0].shape[1]
    n_b = len(b_list)

    def body(a_ref, *refs):
        b_refs, o_refs = refs[:n_b], refs[n_b:]

        @pl.when(pl.program_id(2) == 0)
        def _():
            for o_ref in o_refs:
                o_ref[...] = jnp.zeros_like(o_ref)

        a_blk = a_ref[...].astype(BF16)
        for b_ref, o_ref in zip(b_refs, o_refs):
            o_ref[...] += _dot_tn(a_blk, b_ref[...].astype(BF16))

    return pl.pallas_call(
        body,
        name=name,
        grid=(n // tn, m // tm, s // tk),
        out_shape=[jax.ShapeDtypeStruct((m, n), F32)] * n_b,
        in_specs=[pl.BlockSpec((tk, tm), lambda j, i, k: (k, i))] + [pl.BlockSpec((tk, tn), lambda j, i, k: (k, j))] * n_b,
        out_specs=[pl.BlockSpec((tm, tn), lambda j, i, k: (i, j))] * n_b,
        compiler_params=_params(("parallel", "parallel", "arbitrary")),
    )(a, *b_list)


def _rows_call(fn, name, rows, params, out_rows, out_accs, ts, comm=None):
    s = rows[0].shape[0]
    nr, npar, no = len(rows), len(params), len(out_rows)

    def body(*refs):
        r, p = refs[:nr], refs[nr:nr + npar]
        o, acc = refs[nr + npar:nr + npar + no], refs[nr + npar + no:]
        outs, sums = fn(*[t[...] for t in r], *[t[...] for t in p])
        for ref, val in zip(o, outs):
            ref[...] = val.astype(ref.dtype)
        if acc:
            @pl.when(pl.program_id(0) == 0)
            def _():
                for ref in acc:
                    ref[...] = jnp.zeros_like(ref)

            for ref, val in zip(acc, sums):
                ref[...] += val

    outs, got = _hosted_call(
        body, comm, name=name, grid=(s // ts,),
        out_shape=[jax.ShapeDtypeStruct((s, w), dt) for w, dt in out_rows]
        + [jax.ShapeDtypeStruct(shape, F32) for shape in out_accs],
        in_specs=[pl.BlockSpec((ts, t.shape[1]), lambda i: (i, 0)) for t in rows]
        + [pl.BlockSpec(t.shape, lambda i: (0, 0)) for t in params],
        out_specs=[pl.BlockSpec((ts, w), lambda i: (i, 0)) for w, _ in out_rows]
        + [pl.BlockSpec(shape, lambda i: (0, 0)) for shape in out_accs],
        scratch_shapes=[], dims=("arbitrary",), args=(*rows, *params))
    return outs if comm is None else (outs, got)


def _fwd_norm1(x, g, scale, shift):
    return (_norm_mod(x, g, scale, shift),), ()


def _resid_norm(x, mixed, gate, g, scale, shift):
    x1 = x + (1.0 + gate) * mixed
    return x1, _norm_mod(x1, g, scale, shift)


def _fwd_resid_norm(x, mixed, gate, g, scale, shift):
    return _resid_norm(x, mixed, gate, g, scale, shift), ()


def _final(x1, ffn, tgt, gate, g):
    def head(x1, ffn, gate, g):
        return _rms(x1 + (1.0 + gate) * ffn, g)

    y, vjp = jax.vjp(head, x1, ffn, gate, g)
    err = y - tgt
    dx2, dffn, dgate, dg = vjp(err * (1.0 / D))
    sq = jnp.sum(jnp.sum(err * err, axis=1, keepdims=True), axis=0, keepdims=True)
    loss = jnp.broadcast_to(sq * (0.5 / D), (1, 128))
    return (dx2, dffn), (dgate, dg, loss)


def _bwd_resid_norm(x, mixed, dx2, dh2, gate, g, scale, shift):
    _, vjp = jax.vjp(_resid_norm, x, mixed, gate, g, scale, shift)
    dx, dmixed, dgate, dg, dscale, dshift = vjp((dx2, dh2))
    return (dx, dmixed), (dgate, dg, dscale, dshift)


def _bwd_norm1(x, dh, dx1, g, scale, shift):
    _, vjp = jax.vjp(_norm_mod, x, g, scale, shift)
    dx, dg, dscale, dshift = vjp(dh)
    return (dx1 + dx,), (dg, dscale, dshift)


def _ada_fwd(c16, w):
    def body(c_ref, w_ref, o_ref):
        c = c_ref[...]
        o_ref[...] = _dot((c * _sigmoid(c)).astype(BF16), w_ref[...].astype(BF16))

    return pl.pallas_call(
        body, name="ada_fwd", out_shape=jax.ShapeDtypeStruct((c16.shape[0], w.shape[1]), F32),
        compiler_params=_params(),
    )(c16, w)


def _ada_bwd(c16, dmod16):
    def body(c_ref, d_ref, o_ref):
        c = c_ref[...]
        o_ref[...] = _dot_tn((c * _sigmoid(c)).astype(BF16), d_ref[...].astype(BF16))

    return pl.pallas_call(
        body, name="ada_bwd", out_shape=jax.ShapeDtypeStruct((c16.shape[1], dmod16.shape[1]), F32),
        compiler_params=_params(),
    )(c16, dmod16)


CONV_R = 256


def _conv_window(win, wc, bc):
    s1 = pltpu.roll(win, 1, 0)
    s2 = pltpu.roll(win, 2, 0)
    u = bc + wc[0:1] * s2 + wc[1:2] * s1 + wc[2:3] * win
    return u[8:], s1[8:], s2[8:], win[8:]


def _row_windows(ref, before, after, n_after, lanes):
    ts, r = ref.shape[0], CONV_R

    def middle(i):
        return ref[pl.ds(pl.multiple_of(i * r, r) - 8, r + 8 + n_after), lanes]

    first = jnp.concatenate([before, ref[0:r + n_after, lanes]], axis=0)
    last = ref[ts - r - 8:ts, lanes] if n_after == 0 else jnp.concatenate([ref[ts - r - 8:ts, lanes], after], axis=0)
    return first, middle, last


def _ffn_specs(ts, s):
    cur = pl.BlockSpec((ts, FF_T), lambda j, i: (i, j))
    halo = pl.BlockSpec((8, FF_T), lambda j, i: (jnp.maximum(i * (ts // 8) - 1, 0), j))
    nxt = pl.BlockSpec((8, FF_T), lambda j, i: (jnp.minimum((i + 1) * (ts // 8), s // 8 - 1), j))
    wc = [pl.BlockSpec((8, FF_T), lambda j, i, h=h: (0, h * N_FT + j)) for h in range(2)]
    bc = [pl.BlockSpec((1, FF_T), lambda j, i, h=h: (0, h * N_FT + j)) for h in range(2)]
    return cur, halo, nxt, wc, bc


def _conv_glu_fwd(up_v, up_g, wc, bc, ts):
    s = up_v.shape[0]
    cur, halo, _, wcs, bcs = _ffn_specs(ts, s)

    def body(v_ref, vh_ref, g_ref, gh_ref, wcv_ref, wcg_ref, bcv_ref, bcg_ref, a_ref):
        keep = jnp.where(pl.program_id(1) == 0, 0.0, 1.0)
        for lanes in (slice(0, 128), slice(128, 256)):
            wcv, wcg, bcv, bcg = wcv_ref[:, lanes], wcg_ref[:, lanes], bcv_ref[:, lanes], bcg_ref[:, lanes]
            first_v, mid_v, _ = _row_windows(v_ref, vh_ref[:, lanes] * keep, None, 0, lanes)
            first_g, mid_g, _ = _row_windows(g_ref, gh_ref[:, lanes] * keep, None, 0, lanes)

            def emit(win_v, win_g, start, lanes=lanes, wcv=wcv, wcg=wcg, bcv=bcv, bcg=bcg):
                val = _conv_window(win_v, wcv, bcv)[0]
                gte = _conv_window(win_g, wcg, bcg)[0]
                a_ref[pl.ds(start, CONV_R), lanes] = (val * (gte * _sigmoid(gte))).astype(BF16)

            emit(first_v, first_g, 0)

            def loop(i, carry, emit=emit, mid_v=mid_v, mid_g=mid_g):
                emit(mid_v(i), mid_g(i), pl.multiple_of(i * CONV_R, CONV_R))
                return carry

            lax.fori_loop(1, ts // CONV_R, loop, 0)

    return pl.pallas_call(
        body, name="conv_glu_fwd", grid=(N_FT, s // ts),
        out_shape=jax.ShapeDtypeStruct((s, D_FF), BF16),
        in_specs=[cur, halo, cur, halo, *wcs, *bcs], out_specs=cur,
        compiler_params=_params(("parallel", "arbitrary")),
    )(up_v, up_v, up_g, up_g, wc, wc, bc, bc)


def _glu_bwd(val, gte, da):
    sg = _sigmoid(gte)
    return da * (gte * sg), da * val * (sg * (1.0 + gte * (1.0 - sg)))


def _conv_glu_bwd(up_v, up_g, da, wc, bc, ts, comm=None):
    s = up_v.shape[0]
    nblk = s // ts
    cur, halo, nxt, wcs, bcs = _ffn_specs(ts, s)
    acc_w = pl.BlockSpec((8, FF_T), lambda j, i: (0, j))
    acc_b = pl.BlockSpec((1, FF_T), lambda j, i: (0, j))

    def body(v_ref, vh_ref, vn_ref, g_ref, gh_ref, gn_ref, da_ref, dan_ref, wcv_ref, wcg_ref, bcv_ref, bcg_ref,
             dupv_ref, dupg_ref, dwcv_ref, dwcg_ref, dbcv_ref, dbcg_ref):
        first = pl.program_id(1) == 0
        keep = jnp.where(first, 0.0, 1.0)
        keep_next = jnp.where(pl.program_id(1) == nblk - 1, 0.0, 1.0)
        r = CONV_R
        n = ts // r

        @pl.when(first)
        def _():
            for ref in (dwcv_ref, dwcg_ref, dbcv_ref, dbcg_ref):
                ref[...] = jnp.zeros_like(ref)

        for lanes in (slice(0, 128), slice(128, 256)):
            wcv, wcg, bcv, bcg = wcv_ref[:, lanes], wcg_ref[:, lanes], bcv_ref[:, lanes], bcg_ref[:, lanes]
            first_v, mid_v, last_v = _row_windows(v_ref, vh_ref[:, lanes] * keep, vn_ref[:, lanes], 8, lanes)
            first_g, mid_g, last_g = _row_windows(g_ref, gh_ref[:, lanes] * keep, gn_ref[:, lanes], 8, lanes)

            def emit(win_v, win_g, da_w, start, lanes=lanes, wcv=wcv, wcg=wcg, bcv=bcv, bcg=bcg):
                u_v, s1_v, s2_v, x_v = _conv_window(win_v, wcv, bcv)
                u_g, s1_g, s2_g, x_g = _conv_window(win_g, wcg, bcg)
                du_v, du_g = _glu_bwd(u_v, u_g, da_w)
                rows = pl.ds(start, r)
                for du, s1, s2, x, wc, dup_ref, dwc_ref, dbc_ref in (
                        (du_v, s1_v, s2_v, x_v, wcv, dupv_ref, dwcv_ref, dbcv_ref),
                        (du_g, s1_g, s2_g, x_g, wcg, dupg_ref, dwcg_ref, dbcg_ref)):
                    dup = wc[2:3] * du + wc[1:2] * pltpu.roll(du, r + 7, 0) + wc[0:1] * pltpu.roll(du, r + 6, 0)
                    dup_ref[rows, lanes] = dup[:r].astype(BF16)
                    du = du[:r]
                    dwc_ref[0:1, lanes] += jnp.sum(du * s2[:r], axis=0, keepdims=True)
                    dwc_ref[1:2, lanes] += jnp.sum(du * s1[:r], axis=0, keepdims=True)
                    dwc_ref[2:3, lanes] += jnp.sum(du * x[:r], axis=0, keepdims=True)
                    dbc_ref[:, lanes] += jnp.sum(du, axis=0, keepdims=True)

            emit(first_v, first_g, da_ref[0:r + 8, lanes], 0)

            def loop(i, carry, emit=emit, mid_v=mid_v, mid_g=mid_g, lanes=lanes):
                start = pl.multiple_of(i * r, r)
                emit(mid_v(i), mid_g(i), da_ref[pl.ds(start, r + 8), lanes], start)
                return carry

            lax.fori_loop(1, n - 1, loop, 0)
            da_last = jnp.concatenate([da_ref[ts - r:ts, lanes], dan_ref[:, lanes] * keep_next], axis=0)
            emit(last_v, last_g, da_last, ts - r)

    return _hosted_call(
        body, comm, name="conv_glu_bwd", grid=(N_FT, nblk),
        out_shape=[jax.ShapeDtypeStruct((s, D_FF), BF16)] * 2 + [jax.ShapeDtypeStruct((8, D_FF), F32)] * 2
        + [jax.ShapeDtypeStruct((1, D_FF), F32)] * 2,
        in_specs=[cur, halo, nxt, cur, halo, nxt, cur, nxt, *wcs, *bcs],
        out_specs=[cur, cur, acc_w, acc_w, acc_b, acc_b],
        scratch_shapes=[], dims=("parallel", "arbitrary"),
        args=(up_v, up_v, up_v, up_g, up_g, up_g, da, da, wc, wc, bc, bc))


def _tri(kind):
    b = SB_B
    m = {"lower_strict": np.tril(np.ones((b, b)), -1), "upper_incl": np.triu(np.ones((b, b)), 0),
         "upper_strict": np.triu(np.ones((b, b)), 1)}[kind]
    return jnp.asarray(np.concatenate([m, np.ones((b, 128))], axis=1), BF16)


def _key_sums(x, tri):
    cb = _dot(x.astype(BF16), tri)
    return cb[:, :SB_B], cb[:, SB_B:]


def _sb_fwd(proj, comm=None):
    s = proj.shape[0]
    b = SB_B

    def body(q_ref, k_ref, v_ref, tri_ref, o_ref, t_ref, first_ref, c_ref, a_ref):
        i = pl.program_id(1)
        lane = lax.broadcasted_iota(jnp.int32, (b, 128), 1)
        heads = (lane < 64, lane >= 64)
        causal = lax.broadcasted_iota(jnp.int32, (b, b), 1) < lax.broadcasted_iota(jnp.int32, (b, b), 0)
        q = q_ref[...] * 0.125
        qm = [jnp.where(h, q, 0.0).astype(BF16) for h in heads]
        c_ref[...] = jnp.zeros_like(c_ref)
        a_ref[...] = jnp.zeros_like(a_ref)

        def prepare(jj, masked):
            rows = pl.ds(pl.multiple_of(jj * b, b), b)
            kb = k_ref[rows, :].astype(BF16)
            vb = v_ref[rows, :]
            out = []
            for hh in range(2):
                z = _dot_nt(qm[hh], kb)
                lg = _log_sigmoid(-z)
                if masked:
                    lg = jnp.where(causal, lg, 0.0)
                after, total = _key_sums(lg, tri_ref[...])
                out.append((lg + z + after, total, jnp.where(heads[hh], vb, 0.0).astype(BF16)))
            return out

        def walk_blocks(blocks):
            for hh in range(2):
                c, a = c_ref[hh], a_ref[hh]
                for pre, masked in blocks:
                    logw, total, vm = pre[hh]
                    w = jnp.exp(logw + jnp.concatenate([c, c], axis=1))
                    if masked:
                        w = jnp.where(causal, w, 0.0)
                    a = a + _dot(w.astype(BF16), vm)
                    c = c + total
                c_ref[hh], a_ref[hh] = c, a

        def largest_sum():
            return jnp.max(jnp.maximum(c_ref[0], c_ref[1]))

        @pl.when(i == 0)
        def _():
            walk_blocks([(prepare(i, True), True)])

        @pl.when(i > 0)
        def _():
            walk_blocks([(prepare(i, True), True), (prepare(i - 1, False), False)])

        def more(state):
            jj, top = state
            return jnp.logical_and(jj >= 0, top > SB_DEAD)

        def walk(state):
            jj, _ = state
            walk_blocks([(prepare(jj, False), False)])
            return jj - 1, largest_sum()

        jj, _ = lax.while_loop(more, walk, (jnp.maximum(i - 2, -1), largest_sum()))
        o_ref[...] = (a_ref[0] + a_ref[1]).astype(BF16)
        t_ref[...] = jnp.concatenate([c_ref[0], c_ref[1]], axis=1)
        first_ref[pl.program_id(0), i] = (jj + 1).astype(F32)

    return _hosted_call(
        body, comm, name="sb_fwd", grid=(4, s // b),
        out_shape=[jax.ShapeDtypeStruct((s, SB_W), BF16), jax.ShapeDtypeStruct((s, 2 * SB_W), F32),
                   jax.ShapeDtypeStruct((4, s // b), F32)],
        in_specs=[pl.BlockSpec((b, 128), lambda p, i: (i, p)),
                  pl.BlockSpec((s, 128), lambda p, i: (0, 4 + p)),
                  pl.BlockSpec((s, 128), lambda p, i: (0, 8 + p)),
                  pl.BlockSpec((b, b + 128), lambda p, i: (0, 0))],
        out_specs=[pl.BlockSpec((b, 128), lambda p, i: (i, p)), pl.BlockSpec((b, 256), lambda p, i: (i, p)),
                   pl.BlockSpec(memory_space=pltpu.SMEM)],
        scratch_shapes=[pltpu.VMEM((2, b, 128), F32), pltpu.VMEM((2, b, 128), F32)],
        dims=("arbitrary", "arbitrary"), args=(proj, proj, proj, _tri("lower_strict")))


def _sb_bwd(proj, do, tot, first, comm=None):
    s = proj.shape[0]
    b = SB_B

    def body(q_ref, k_ref, v_ref, do_ref, t_ref, first_ref, ti_ref, ts_ref, dq_ref, dko_ref, dvo_ref, cl_ref, ce_ref,
             a_ref, dk_ref, dv_ref):
        i = pl.program_id(1)
        first = jnp.clip(first_ref[pl.program_id(0), i].astype(jnp.int32), 0, i)
        lane = lax.broadcasted_iota(jnp.int32, (b, 128), 1)
        heads = (lane < 64, lane >= 64)
        causal = lax.broadcasted_iota(jnp.int32, (b, b), 1) < lax.broadcasted_iota(jnp.int32, (b, b), 0)
        q = q_ref[...] * 0.125
        do = do_ref[...]
        qm = [jnp.where(h, q, 0.0).astype(BF16) for h in heads]
        dom = [jnp.where(h, do, 0.0).astype(BF16) for h in heads]
        cl_ref[...] = jnp.zeros_like(cl_ref)
        ce_ref[...] = jnp.zeros_like(ce_ref)
        a_ref[...] = jnp.zeros_like(a_ref)

        @pl.when(i == 0)
        def _():
            dk_ref[...] = jnp.zeros_like(dk_ref)
            dv_ref[...] = jnp.zeros_like(dv_ref)

        def prepare(jj, masked):
            rows = pl.ds(pl.multiple_of(jj * b, b), b)
            kf = k_ref[rows, :]
            kb = kf.astype(BF16)
            vb = v_ref[rows, :].astype(BF16)
            out = []
            for hh in range(2):
                z = _dot_nt(qm[hh], kb)
                lg = _log_sigmoid(-z)
                if masked:
                    lg = jnp.where(causal, lg, 0.0)
                upto, total = _key_sums(lg, ti_ref[...])
                lsz = lg + z
                out.append((lsz - upto, total, jnp.exp(lsz), _dot_nt(dom[hh], vb),
                            jnp.where(heads[hh], kf, 0.0).astype(BF16)))
            return rows, out

        def walk_blocks(blocks):
            grads = [[jnp.zeros((b, 128), F32), jnp.zeros((b, 128), F32)] for _ in blocks]
            cl = [cl_ref[0], cl_ref[1]]
            ce = [ce_ref[0], ce_ref[1]]
            a = [a_ref[0], a_ref[1]]
            for n, ((_, pre), masked) in enumerate(blocks):
                for hh in range(2):
                    t = t_ref[:, hh * 128:(hh + 1) * 128]
                    logw, total, sig, dw, km = pre[hh]
                    w = jnp.exp(logw + jnp.concatenate([t - cl[hh], t - cl[hh]], axis=1))
                    if masked:
                        w = jnp.where(causal, w, 0.0)
                    e = w * dw
                    sums = _dot(e.astype(BF16), ts_ref[...])
                    before, etot = sums[:, :SB_B], sums[:, SB_B:]
                    dz = e - sig * (e + before + jnp.concatenate([ce[hh], ce[hh]], axis=1))
                    if masked:
                        dz = jnp.where(causal, dz, 0.0)
                    dzb = dz.astype(BF16)
                    a[hh] = a[hh] + _dot(dzb, km)
                    grads[n][0] = grads[n][0] + _dot_tn(dzb, qm[hh])
                    grads[n][1] = grads[n][1] + _dot_tn(w.astype(BF16), dom[hh])
                    cl[hh] = cl[hh] + total
                    ce[hh] = ce[hh] + etot
            for hh in range(2):
                cl_ref[hh], ce_ref[hh], a_ref[hh] = cl[hh], ce[hh], a[hh]
            for ((rows, _), _), (dk, dv) in zip(blocks, grads):
                dk_ref[rows, :] += dk
                dv_ref[rows, :] += dv

        def loop(jj, carry):
            walk_blocks([(prepare(jj, False), False)])
            return carry

        lax.fori_loop(first, i - 1, loop, 0)

        @pl.when(i == 0)
        def _():
            walk_blocks([(prepare(i, True), True)])

        @pl.when(i > 0)
        def _():
            walk_blocks([(prepare(i - 1, False), False), (prepare(i, True), True)])

        dq_ref[...] = ((a_ref[0] + a_ref[1]) * 0.125).astype(BF16)

        @pl.when(i == s // b - 1)
        def _():
            dko_ref[...] = dk_ref[...].astype(BF16)
            dvo_ref[...] = dv_ref[...].astype(BF16)

    blk = pl.BlockSpec((b, 128), lambda p, i: (i, p))
    full = pl.BlockSpec((s, 128), lambda p, i: (0, p))
    tri = pl.BlockSpec((b, b + 128), lambda p, i: (0, 0))
    return _hosted_call(
        body, comm, name="sb_bwd", grid=(4, s // b),
        out_shape=[jax.ShapeDtypeStruct((s, SB_W), BF16)] * 3,
        in_specs=[blk, pl.BlockSpec((s, 128), lambda p, i: (0, 4 + p)), pl.BlockSpec((s, 128), lambda p, i: (0, 8 + p)),
                  blk, pl.BlockSpec((b, 256), lambda p, i: (i, p)), pl.BlockSpec(memory_space=pltpu.SMEM), tri, tri],
        out_specs=[blk, full, full],
        scratch_shapes=[pltpu.VMEM((2, b, 128), F32)] * 3 + [pltpu.VMEM((s, 128), F32)] * 2,
        dims=("arbitrary", "arbitrary"),
        args=(proj, proj, proj, do, tot, first, _tri("upper_incl"), _tri("upper_strict")))


_COL_Q, _COL_K, _COL_V, _COL_G, _COL_F = 12, 14, 8, 10, 24


def _gla_consts():
    c = CHUNK
    incl = np.tril(np.ones((c, c)), 0)
    strict = np.tril(np.ones((c, c)), -1)
    bd = np.zeros((256, 128))
    bd[:128, :64] = 1.0
    bd[128:, 64:] = 1.0
    return jnp.asarray(incl, BF16), jnp.asarray(strict, BF16), jnp.asarray(bd, F32)


def _time_sums(tri, x):
    hi, lo = _hilo(x)
    return _dot(tri, hi) + _dot(tri, lo)


def _gla_gate(o, gg, g):
    parts = []
    for h in range(2):
        oh = o[:, h * 128:(h + 1) * 128]
        parts.append(oh * lax.rsqrt(jnp.mean(oh * oh, axis=-1, keepdims=True) + EPS))
    return (jnp.concatenate(parts, axis=1) * g) * (gg * _sigmoid(gg))


def _gla_chunk(la_c, k_c, incl):
    cum = _time_sums(incl, la_c)
    total = cum[CHUNK - 1:CHUNK]
    edec = jnp.exp(total - cum)
    return edec, k_c * edec, jnp.exp(total)


def _gla_fwd(proj, wfg, bfg, gain):
    s = proj.shape[0]
    t = GLA_T
    nch = t // CHUNK
    incl, _, bd = _gla_consts()

    def body(q_ref, k_ref, v_ref, gg_ref, f_ref, wf_ref, bf_ref, g_ref, incl_ref, bd_ref, o_ref, st_ref, state_ref):
        @pl.when(pl.program_id(0) == 0)
        def _():
            state_ref[...] = jnp.zeros_like(state_ref)

        la = _log_sigmoid(_dot(f_ref[...].astype(BF16), wf_ref[...].astype(BF16)) + bf_ref[...]) * (1.0 / 16.0)
        states = [state_ref[0], state_ref[1]]
        for cc in range(nch):
            rows = slice(cc * CHUNK, (cc + 1) * CHUNK)
            for p in range(2):
                kl, vl = slice(p * 128, (p + 1) * 128), slice(p * 256, (p + 1) * 256)
                _, kdec, dec = _gla_chunk(la[rows, kl], k_ref[rows, kl], incl_ref[...])
                kv = _dot_tn(v_ref[rows, vl].astype(BF16), kdec.astype(BF16))
                states[p] = states[p] * dec + bd_ref[...] * kv
                st_ref[p, cc] = states[p]
                o = _dot_nt((q_ref[rows, kl] * 0.125).astype(BF16), states[p].astype(BF16))
                o_ref[rows, vl] = _gla_gate(o, gg_ref[rows, vl], g_ref[:, vl]).astype(BF16)
        state_ref[0] = states[0]
        state_ref[1] = states[1]

    def col(width, blk):
        return pl.BlockSpec((t, width), lambda i: (i, blk))

    def whole(shape):
        return pl.BlockSpec(shape, lambda i: tuple(0 for _ in shape))

    return pl.pallas_call(
        body, name="gla_fwd", grid=(s // t,),
        out_shape=[jax.ShapeDtypeStruct((s, GLA_VW), BF16), jax.ShapeDtypeStruct((2, s // CHUNK, 256, 128), F32)],
        in_specs=[col(256, _COL_Q // 2), col(256, _COL_K // 2), col(512, _COL_V // 2), col(512, _COL_G // 2),
                  col(128, _COL_F), whole((128, GLA_KW)), whole((1, GLA_KW)), whole((1, GLA_VW)),
                  whole((CHUNK, CHUNK)), whole((256, 128))],
        out_specs=[col(512, 0), pl.BlockSpec((2, nch, 256, 128), lambda i: (0, i, 0, 0))],
        scratch_shapes=[pltpu.VMEM((2, 256, 128), F32)],
        compiler_params=_params(("arbitrary",)),
    )(proj, proj, proj, proj, proj, wfg, bfg, gain, incl, bd)


def _gla_bwd(proj, do, states, wfg, bfg, gain):
    s = proj.shape[0]
    t = GLA_T
    nch = t // CHUNK
    nblk = s // t
    incl, strict, bd = _gla_consts()

    def body(q_ref, k_ref, v_ref, gg_ref, f_ref, do_ref, st_ref, sp_ref, wf_ref, bf_ref, g_ref, incl_ref, str_ref,
             bd_ref, dq_ref, dk_ref, dv_ref, dgg_ref, df_ref, dwf_ref, dbf_ref, dg_ref, carry_ref, dfs_ref):
        i = pl.program_id(0)

        @pl.when(i == 0)
        def _():
            carry_ref[...] = jnp.zeros_like(carry_ref)
            dwf_ref[...] = jnp.zeros_like(dwf_ref)
            dbf_ref[...] = jnp.zeros_like(dbf_ref)
            dg_ref[...] = jnp.zeros_like(dg_ref)

        fb = f_ref[...].astype(BF16)
        wf = wf_ref[...].astype(BF16)
        f = _dot(fb, wf) + bf_ref[...]
        la = _log_sigmoid(f) * (1.0 / 16.0)
        dla_df = _sigmoid(-f) * (1.0 / 16.0)
        first_block = jnp.where(i == nblk - 1, 0.0, 1.0)
        carries = [carry_ref[0], carry_ref[1]]
        dgains = [jnp.zeros((1, 256), F32), jnp.zeros((1, 256), F32)]
        local = {}
        for cc in reversed(range(nch)):
            rows = slice(cc * CHUNK, (cc + 1) * CHUNK)
            for p in range(2):
                kl, vl = slice(p * 128, (p + 1) * 128), slice(p * 256, (p + 1) * 256)
                edec, kdec, dec = _gla_chunk(la[rows, kl], k_ref[rows, kl], incl_ref[...])
                qs = (q_ref[rows, kl] * 0.125).astype(BF16)
                sb16 = st_ref[p, cc].astype(BF16)
                o = _dot_nt(qs, sb16)
                _, vjp = jax.vjp(_gla_gate, o, gg_ref[rows, vl], g_ref[:, vl])
                do_c, dgg_c, dg_c = vjp(do_ref[rows, vl])
                dgains[p] = dgains[p] + dg_c
                dgg_ref[rows, vl] = dgg_c.astype(BF16)
                do16 = do_c.astype(BF16)
                dq_ref[rows, kl] = (_dot(do16, sb16) * 0.125).astype(BF16)
                local[cc, p] = (edec, kdec, dec, bd_ref[...] * _dot_tn(do16, qs))
        for cc in reversed(range(nch)):
            rows = slice(cc * CHUNK, (cc + 1) * CHUNK)
            for p in range(2):
                kl, vl = slice(p * 128, (p + 1) * 128), slice(p * 256, (p + 1) * 256)
                edec, kdec, dec, readers = local[cc, p]
                prev = st_ref[p, cc - 1] if cc > 0 else sp_ref[p, 0] * first_block
                grad = readers + carries[p]
                g16 = grad.astype(BF16)
                dv_ref[rows, vl] = _dot_nt(kdec.astype(BF16), g16).astype(BF16)
                dkdec = _dot(v_ref[rows, vl].astype(BF16), g16)
                ddec = jnp.sum(grad * prev, axis=0, keepdims=True) * dec
                dk_ref[rows, kl] = (dkdec * edec).astype(BF16)
                dla = _time_sums(str_ref[...], dkdec * kdec) + ddec
                dfs_ref[rows, kl] = dla * dla_df[rows, kl]
                carries[p] = grad * dec
        carry_ref[0] = carries[0]
        carry_ref[1] = carries[1]
        df = dfs_ref[...]
        df16 = df.astype(BF16)
        df_ref[...] = _dot_nt(df16, wf).astype(BF16)
        dwf_ref[...] += _dot_tn(fb, df16)
        dbf_ref[...] += jnp.sum(df, axis=0, keepdims=True)
        dg_ref[...] += jnp.concatenate(dgains, axis=1)

    def col(width, blk):
        return pl.BlockSpec((t, width), lambda i: (nblk - 1 - i, blk))

    def whole(shape):
        return pl.BlockSpec(shape, lambda i: tuple(0 for _ in shape))

    return pl.pallas_call(
        body, name="gla_bwd", grid=(nblk,),
        out_shape=[jax.ShapeDtypeStruct((s, GLA_KW), BF16), jax.ShapeDtypeStruct((s, GLA_KW), BF16),
                   jax.ShapeDtypeStruct((s, GLA_VW), BF16), jax.ShapeDtypeStruct((s, GLA_VW), BF16),
                   jax.ShapeDtypeStruct((s, 128), BF16), jax.ShapeDtypeStruct((128, GLA_KW), F32),
                   jax.ShapeDtypeStruct((1, GLA_KW), F32), jax.ShapeDtypeStruct((1, GLA_VW), F32)],
        in_specs=[col(256, _COL_Q // 2), col(256, _COL_K // 2), col(512, _COL_V // 2), col(512, _COL_G // 2),
                  col(128, _COL_F), col(512, 1),
                  pl.BlockSpec((2, nch, 256, 128), lambda i: (0, nblk - 1 - i, 0, 0)),
                  pl.BlockSpec((2, 1, 256, 128), lambda i: (0, jnp.maximum((nblk - 1 - i) * nch - 1, 0), 0, 0)),
                  whole((128, GLA_KW)), whole((1, GLA_KW)), whole((1, GLA_VW)),
                  whole((CHUNK, CHUNK)), whole((CHUNK, CHUNK)), whole((256, 128))],
        out_specs=[col(256, 0), col(256, 0), col(512, 0), col(512, 0), col(128, 0),
                   whole((128, GLA_KW)), whole((1, GLA_KW)), whole((1, GLA_VW))],
        scratch_shapes=[pltpu.VMEM((2, 256, 128), F32), pltpu.VMEM((t, GLA_KW), F32)],
        compiler_params=_params(("arbitrary",)),
    )(proj, proj, proj, proj, proj, do, states, states, wfg, bfg, gain, incl, strict, bd)


def _cols_by_dev(a, n_dev=N_DEV):
    r, c = a.shape
    return a.reshape(r, n_dev, c // n_dev).transpose(1, 0, 2)


def _cols_from_dev(a):
    _, r, n = a.shape
    return a.transpose(1, 0, 2).reshape(r, N_DEV * n)


def kernel(x, c, w_ada, b_ada, g_norm1, w_in, w_fg2, b_fg2, g_gla_out, w_out, g_norm2, w_up, w_conv, b_conv, w_down, g_final, loss_target, m_w_ada, m_b_ada, m_g_norm1, m_w_in, m_w_fg2, m_b_fg2, m_g_gla_out, m_w_out, m_g_norm2, m_w_up, m_w_conv, m_b_conv, m_w_down, m_g_final, v_w_ada, v_b_ada, v_g_norm1, v_w_in, v_w_fg2, v_b_fg2, v_g_gla_out, v_w_out, v_g_norm2, v_w_up, v_w_conv, v_b_conv, v_w_down, v_g_final):
    s = x.shape[1]
    me = 4 * lax.axis_index("x") + 2 * lax.axis_index("y") + lax.axis_index("c")
    xs, tgt = x[0], loss_target[0]
    ts = min(512, s)
    tm = min(1024, s)
    tc = min(2048, s // 2)

    c_all, wconv_g, wfg_g = _exchange([c, w_conv[0], w_fg2[0]], "gather_first", True)
    wconv_p = jnp.pad(_cols_from_dev(wconv_g), ((0, 5), (0, 0)))
    wfg_p = jnp.pad(_cols_from_dev(wfg_g), ((0, 128 - RANK), (0, 0)))

    c16 = jnp.pad(c_all.reshape(N_DEV, D), ((0, 8), (0, 0)))
    modp = _ada_fwd(c16, w_ada[0])[:N_DEV]
    (mod_all,) = _exchange([modp], "gather_mod", True)
    mod = lax.dynamic_index_in_dim(mod_all, me, axis=1, keepdims=False).reshape(1, 6 * D) + b_ada
    shift1, scale1, gate1, shift2, scale2, gate2 = [mod[:, k * D:(k + 1) * D] for k in range(6)]

    w_in_t, m_in_t, v_in_t = [jnp.swapaxes(t[0], 0, 1) for t in (w_in, m_w_in, v_w_in)]
    w_up_t, m_up_t, v_up_t = [jnp.swapaxes(t[0], 0, 1) for t in (w_up, m_w_up, v_w_up)]
    (h,), (win_g,) = _rows_call(_fwd_norm1, "norm1", [xs], [g_norm1, scale1, shift1], [(D, BF16)], [], ts,
                                comm=([w_in_t.astype(BF16)], True))
    win_t = jnp.pad(win_g.reshape(IN_W, D), ((0, IN_WP - IN_W), (0, 0)))
    proj = _mm_nt(h, win_t, "in_proj", F32, ts, IN_WP)
    (o_sb, sb_tot, sb_first), (wout_g, wup_g, wdown_g) = _sb_fwd(
        proj, comm=([w_out[0].astype(BF16), w_up_t.astype(BF16), w_down[0].astype(BF16)], True))
    wout = wout_g.reshape(D, D)
    wup_t = wup_g.reshape(2 * D_FF, D)
    wdown = wdown_g.reshape(D_FF, D)
    o_gla, states = _gla_fwd(proj, wfg_p, b_fg2, g_gla_out)
    cat = jnp.concatenate([o_sb, o_gla], axis=1)
    mixed, _ = _mm([cat], wout, "out_proj", F32, tm, 512)
    x1, h2 = _rows_call(_fwd_resid_norm, "resid_norm2", [xs, mixed], [gate1, g_norm2, scale2, shift2],
                        [(D, F32), (D, BF16)], [], ts)
    up_v, up_g = _mm_nt(h2, wup_t, "up_proj", F32, tm, 1408, n_out=2)
    act = _conv_glu_fwd(up_v, up_g, wconv_p, b_conv, tc)
    ffn, _ = _mm([act], wdown, "down_proj", F32, tm, 512)
    dx2, dffn, dgate2, dg_final, loss_part = _rows_call(
        _final, "final", [x1, ffn, tgt], [gate2, g_final.reshape(1, D)],
        [(D, F32), (D, BF16)], [(1, D), (1, D), (1, 128)], ts)

    (dw_down,) = _mm_tn(act, [dffn], "down_wgrad", 1408, 1024, tm)
    da = _mm_nt(dffn, wdown, "down_dgrad", F32, tm, 1408)
    (dup_v, dup_g, dwc_v, dwc_g, dbc_v, dbc_g), (r_down,) = _conv_glu_bwd(
        up_v, up_g, da, wconv_p, b_conv, tc, comm=([dw_down.reshape(N_DEV, 352, D)], False))
    (dw_up_tv,) = _mm_tn(dup_v, [h2], "up_wgrad_v", 1408, 1024, tm)
    (dw_up_tg,) = _mm_tn(dup_g, [h2], "up_wgrad_g", 1408, 1024, tm)
    dh2, _ = _mm([dup_v, dup_g], wup_t, "up_dgrad", F32, tm, 512)
    dx1, dmixed, dgate1, dg_norm2, dscale2, dshift2 = _rows_call(
        _bwd_resid_norm, "resid_norm2_bwd", [xs, mixed, dx2, dh2], [gate1, g_norm2, scale2, shift2],
        [(D, F32), (D, BF16)], [(1, D)] * 4, ts)
    (dw_out,) = _mm_tn(cat, [dmixed], "out_wgrad", 1024, 1024, tm)
    dcat = _mm_nt(dmixed, wout, "out_dgrad", F32, tm, 512)
    dw_up_s = jnp.concatenate([dw_up_tv, dw_up_tg], axis=0).reshape(N_DEV, 704, D)
    dwconv_s = _cols_by_dev(jnp.concatenate([dwc_v[:3], dwc_g[:3]], axis=1))
    (dq, dk, dv), (r_up, r_out, r_conv) = _sb_bwd(
        proj, dcat, sb_tot, sb_first, comm=([dw_up_s, dw_out.reshape(N_DEV, 128, D), dwconv_s], False))
    dgq, dgk, dgv, dgg, dgf, dwfg_p, dbfg, dg_gla = _gla_bwd(proj, dcat, states, wfg_p, b_fg2, g_gla_out)
    dproj = jnp.concatenate([dq, dk, dv, dgq, dgk, dgv, dgg, dgf], axis=1)
    (dw_in_t,) = _mm_tn(dproj, [h], "in_wgrad", IN_WP, 1024, tm)
    dh, (r_in, r_fg) = _mm([dproj], win_t, "in_dgrad", F32, tm, 512,
                           comm=([dw_in_t[:IN_W].reshape(N_DEV, 386, D).astype(BF16), _cols_by_dev(dwfg_p[:RANK])],
                                 False))
    grad_x, dg_norm1, dscale1, dshift1 = _rows_call(
        _bwd_norm1, "norm1_bwd", [xs, dh, dx1], [g_norm1, scale1, shift1], [(D, F32)], [(1, D)] * 3, ts)

    dmod = jnp.concatenate([dshift1, dscale1, dgate1, dshift2, dscale2, dgate2], axis=1)
    small_parts = [dmod, dg_norm1, dbfg, dg_gla, dg_norm2, dbc_v, dbc_g, dg_final, loss_part]
    n_small = sum(p.shape[1] for p in small_parts[:-1])
    (sg,) = _exchange([jnp.concatenate(small_parts, axis=1)], "gather_small_grads", True)
    sg = sg.reshape(N_DEV, n_small + 128)
    loss = jnp.sum(sg[:, n_small])
    small_w = [b_ada, g_norm1, b_fg2, g_gla_out, g_norm2, b_conv, g_final.reshape(1, D)]
    small_m = [m_b_ada, m_g_norm1, m_b_fg2, m_g_gla_out, m_g_norm2, m_b_conv, m_g_final.reshape(1, D)]
    small_v = [v_b_ada, v_g_norm1, v_b_fg2, v_g_gla_out, v_g_norm2, v_b_conv, v_g_final.reshape(1, D)]
    s_out = list(_adam_rows(sg, small_w, small_m, small_v, "adam_small"))
    s_out[24:] = [t.reshape(D) for t in s_out[24:]]
    s_g, s_d, s_m, s_v = [s_out[k::4] for k in range(4)]

    dmod_all = sg[:, :6 * D].reshape(N_DEV, N_DEV, 768)
    dmod_mine = lax.dynamic_index_in_dim(dmod_all, me, axis=1, keepdims=False)
    dw_ada = _ada_bwd(c16, jnp.pad(dmod_mine, ((0, 8), (0, 0))))
    a_g, a_d, a_m, a_v = [t[None] for t in _adam(dw_ada[None], w_ada[0], m_w_ada[0], v_w_ada[0], "adam_ada", 256)]

    recv = [r_in, r_fg, r_out, r_up, r_conv, r_down]
    big_w = [w_in_t, w_fg2[0], w_out[0], w_up_t, w_conv[0], w_down[0]]
    big_m = [m_in_t, m_w_fg2[0], m_w_out[0], m_up_t, m_w_conv[0], m_w_down[0]]
    big_v = [v_in_t, v_w_fg2[0], v_w_out[0], v_up_t, v_w_conv[0], v_w_down[0]]
    big_name = ["adam_in", "adam_fg2", "adam_out", "adam_up", "adam_conv", "adam_down"]
    big_rows = [386, RANK, 128, 176, 3, 176]
    b_out = [_adam(r, w, m, v, name, tr)
             for r, w, m, v, name, tr in zip(recv, big_w, big_m, big_v, big_name, big_rows)]
    for k in (0, 3):
        b_out[k] = [jnp.swapaxes(t, 0, 1) for t in b_out[k]]
    b_g, b_d, b_m, b_v = [[o[k][None] for o in b_out] for k in range(4)]

    def ordered(a, sm, bg):
        return [a, sm[0], sm[1], bg[0], bg[1], sm[2], sm[3], bg[2], sm[4], bg[3], bg[4], sm[5], bg[5], sm[6]]

    return (loss, grad_x[None], *ordered(a_g, s_g, b_g), *ordered(a_d, s_d, b_d),
            *ordered(a_m, s_m, b_m), *ordered(a_v, s_v, b_v))
```

```python
import numpy as np

import jax
import jax.numpy as jnp
from jax import lax
from jax.experimental import pallas as pl
from jax.experimental.pallas import tpu as pltpu

F32, BF16 = jnp.float32, jnp.bfloat16
N_DEV = 8
D = 1024
SB_W = 512
GLA_KW, GLA_VW = 256, 512
RANK = 16
IN_W = 3088
IN_WP = 3200
D_FF = 2816
FF_T = 256
N_FT = D_FF // FF_T
EPS = 1e-6
SB_B = 256
SB_DEAD = -110.0
CHUNK = 64
GLA_T = 512
VMEM_LIMIT = 56 * 1024 * 1024

LR, B1, B2, ADAM_EPS, WD, STEP = 0.001, 0.9, 0.999, 1e-08, 0.01, 10


def _params(dims=None, vmem=True):
    kw = {}
    if dims is not None:
        kw["dimension_semantics"] = dims
    if vmem:
        kw["vmem_limit_bytes"] = VMEM_LIMIT
    return pltpu.CompilerParams(**kw)


def _dot(a, b):
    return jnp.dot(a, b, preferred_element_type=F32)


def _dot_nt(a, b):
    return lax.dot_general(a, b, (((1,), (1,)), ((), ())), preferred_element_type=F32)


def _dot_tn(a, b):
    return lax.dot_general(a, b, (((0,), (0,)), ((), ())), preferred_element_type=F32)


def _hilo(x):
    hi = x.astype(BF16)
    lo = (x - hi.astype(F32)).astype(BF16)
    return hi, lo


def _sigmoid(x):
    return 1.0 / (1.0 + jnp.exp(-x))


def _log_sigmoid(x):
    return jnp.minimum(x, 0.0) - jnp.log(1.0 + jnp.exp(-jnp.abs(x)))


def _rms(x, g):
    n = x * lax.rsqrt(jnp.mean(x * x, axis=-1, keepdims=True) + EPS)
    return n * g


def _norm_mod(x, g, scale, shift):
    return _rms(x, g) * (1.0 + scale) + shift


N_PEER = N_DEV - 1


def _exchange_copies(x_refs, out_refs, send_sems, recv_sems, local_sems, gather):
    ix, iy, ic = lax.axis_index("x"), lax.axis_index("y"), lax.axis_index("c")
    me = 4 * ix + 2 * iy + ic
    peers = []
    for k in range(1, N_DEV):
        px = 1 - ix if k & 4 else ix
        py = 1 - iy if k & 2 else iy
        pc = 1 - ic if k & 1 else ic
        peers.append(((px, py, pc), 4 * px + 2 * py + pc))

    def copy(a, k, dev, src_slot, dst_slot):
        return pltpu.make_async_remote_copy(
            src_ref=x_refs[a] if gather else x_refs[a].at[src_slot],
            dst_ref=out_refs[a].at[dst_slot],
            send_sem=send_sems.at[a * N_PEER + k],
            recv_sem=recv_sems.at[a * N_PEER + k],
            device_id=dev,
            device_id_type=pl.DeviceIdType.MESH,
        )

    n = len(x_refs)
    mine = [pltpu.make_async_copy(x_refs[a] if gather else x_refs[a].at[me], out_refs[a].at[me], local_sems.at[a])
            for a in range(n)]
    sends = [copy(a, k, dev, pid, me) for a in range(n) for k, (dev, pid) in enumerate(peers)]
    recvs = [copy(a, k, dev, pid, pid) for a in range(n) for k, (dev, pid) in enumerate(peers)]
    return mine, sends, recvs


def _exchange_scratch(n):
    return [pltpu.SemaphoreType.DMA((n * N_PEER,)), pltpu.SemaphoreType.DMA((n * N_PEER,)),
            pltpu.SemaphoreType.DMA((n,))]


def _exchange_shapes(arrays, gather):
    return [jax.ShapeDtypeStruct((N_DEV,) + tuple(x.shape if gather else x.shape[1:]), x.dtype) for x in arrays]


def _exchange(arrays, name, gather):
    n = len(arrays)

    def body(*refs):
        mine, sends, recvs = _exchange_copies(refs[:n], refs[n:2 * n], *refs[2 * n:], gather)
        for cp in mine + sends:
            cp.start()
        for cp in recvs:
            cp.wait_recv()
        for cp in sends:
            cp.wait_send()
        for cp in mine:
            cp.wait()

    return pl.pallas_call(
        body,
        name=name,
        out_shape=_exchange_shapes(arrays, gather),
        in_specs=[pl.BlockSpec(memory_space=pl.ANY)] * n,
        out_specs=[pl.BlockSpec(memory_space=pl.ANY)] * n,
        scratch_shapes=_exchange_scratch(n),
    )(*arrays)


def _hosted_call(body, comm, *, name, grid, in_specs, out_specs, out_shape, scratch_shapes, dims, args):
    if comm is None:
        outs = pl.pallas_call(body, name=name, grid=grid, in_specs=in_specs, out_specs=out_specs, out_shape=out_shape,
                              scratch_shapes=scratch_shapes, compiler_params=_params(dims))(*args)
        return outs, []
    arrays, gather = comm
    n_in, n_out, n_scr, nc = len(in_specs), len(out_specs), len(scratch_shapes), len(arrays)

    def hosted(*refs):
        ins, c_in = refs[:n_in], refs[n_in:n_in + nc]
        outs, c_out = refs[n_in + nc:n_in + nc + n_out], refs[n_in + nc + n_out:n_in + 2 * nc + n_out]
        rest = refs[n_in + 2 * nc + n_out:]
        scratch, sems = rest[:n_scr], rest[n_scr:]
        mine, sends, recvs = _exchange_copies(c_in, c_out, *sems, gather)
        first = pl.program_id(0) == 0
        last = pl.program_id(0) == grid[0] - 1
        for axis in range(1, len(grid)):
            first = jnp.logical_and(first, pl.program_id(axis) == 0)
            last = jnp.logical_and(last, pl.program_id(axis) == grid[axis] - 1)

        @pl.when(first)
        def _():
            for cp in mine + sends:
                cp.start()

        body(*ins, *outs, *scratch)

        @pl.when(last)
        def _():
            for cp in recvs:
                cp.wait_recv()
            for cp in sends:
                cp.wait_send()
            for cp in mine:
                cp.wait()

    any_spec = pl.BlockSpec(memory_space=pl.ANY)
    outs = pl.pallas_call(
        hosted, name=name, grid=grid,
        in_specs=list(in_specs) + [any_spec] * nc,
        out_specs=list(out_specs) + [any_spec] * nc,
        out_shape=list(out_shape) + _exchange_shapes(arrays, gather),
        scratch_shapes=list(scratch_shapes) + _exchange_scratch(nc),
        compiler_params=_params(tuple("arbitrary" for _ in grid)),
    )(*args, *arrays)
    return outs[:n_out], outs[n_out:]


def _adam_math(g, w, m, v):
    m_new = B1 * m + (1.0 - B1) * g
    v_new = B2 * v + (1.0 - B2) * (g * g)
    m_hat = m_new / (1.0 - B1 ** STEP)
    v_hat = v_new / (1.0 - B2 ** STEP)
    return -LR * (m_hat / (jnp.sqrt(v_hat) + ADAM_EPS) + WD * w), m_new, v_new


def _adam(gparts, w, m, v, name, tr):
    n, rows, cols = gparts.shape

    def body(gp_ref, w_ref, m_ref, v_ref, g_ref, d_ref, nm_ref, nv_ref):
        g = gp_ref[0].astype(F32)
        for j in range(1, n):
            g = g + gp_ref[j].astype(F32)
        g_ref[...] = g
        d_ref[...], nm_ref[...], nv_ref[...] = _adam_math(g, w_ref[...], m_ref[...], v_ref[...])

    blk = pl.BlockSpec((tr, cols), lambda i: (i, 0))
    return pl.pallas_call(
        body,
        name=name,
        grid=(rows // tr,),
        out_shape=[jax.ShapeDtypeStruct((rows, cols), F32)] * 4,
        in_specs=[pl.BlockSpec((n, tr, cols), lambda i: (0, i, 0)), blk, blk, blk],
        out_specs=[blk] * 4,
        compiler_params=_params(("parallel",)),
    )(gparts, w, m, v)


def _adam_rows(parts, ws, ms, vs, name):
    n = parts.shape[0]
    k = len(ws)
    widths = [w.shape[1] for w in ws]

    def body(*refs):
        p_ref, w_refs, m_refs, v_refs = refs[0], refs[1:1 + k], refs[1 + k:1 + 2 * k], refs[1 + 2 * k:1 + 3 * k]
        outs = refs[1 + 3 * k:]
        total = p_ref[0:1, :]
        for j in range(1, n):
            total = total + p_ref[j:j + 1, :]
        off = 0
        for a, width in enumerate(widths):
            g = total[:, off:off + width]
            off += width
            outs[4 * a][...] = g
            outs[4 * a + 1][...], outs[4 * a + 2][...], outs[4 * a + 3][...] = _adam_math(
                g, w_refs[a][...], m_refs[a][...], v_refs[a][...])

    return pl.pallas_call(
        body, name=name,
        out_shape=[jax.ShapeDtypeStruct((1, width), F32) for width in widths for _ in range(4)],
        compiler_params=_params(),
    )(parts, *ws, *ms, *vs)


def _mm(a_list, b, name, out_dtype, tm, tn, comm=None):
    n_a = len(a_list)
    m = a_list[0].shape[0]
    n = b.shape[1]
    widths = [a.shape[1] for a in a_list]
    starts = [sum(widths[:g]) for g in range(n_a)]
    assert all(s0 % w == 0 for s0, w in zip(starts, widths))

    def body(*refs):
        o_ref = refs[2 * n_a]
        acc = _dot(refs[0][...].astype(BF16), refs[n_a][...].astype(BF16))
        for g in range(1, n_a):
            acc = acc + _dot(refs[g][...].astype(BF16), refs[n_a + g][...].astype(BF16))
        o_ref[...] = acc.astype(out_dtype)

    (out,), got = _hosted_call(
        body, comm, name=name, grid=(n // tn, m // tm),
        out_shape=[jax.ShapeDtypeStruct((m, n), out_dtype)],
        in_specs=[pl.BlockSpec((tm, w), lambda j, i: (i, 0)) for w in widths]
        + [pl.BlockSpec((w, tn), lambda j, i, blk=s0 // w: (blk, j)) for s0, w in zip(starts, widths)],
        out_specs=[pl.BlockSpec((tm, tn), lambda j, i: (i, j))],
        scratch_shapes=[], dims=("parallel", "parallel"), args=(*a_list, *([b] * n_a)))
    return out, got


def _mm_tn_rows(a_list, b, name, tk):
    s, n = b.shape
    n_a = len(a_list)

    def body(*refs):
        a_refs, b_ref, o_refs = refs[:n_a], refs[n_a], refs[n_a + 1:]

        @pl.when(pl.program_id(0) == 0)
        def _():
            for o_ref in o_refs:
                o_ref[...] = jnp.zeros_like(o_ref)

        b_blk = b_ref[...].astype(BF16)
        for a_ref, o_ref in zip(a_refs, o_refs):
            o_ref[...] += _dot_tn(a_ref[...].astype(BF16), b_blk)

    return pl.pallas_call(
        body,
        name=name,
        grid=(s // tk,),
        out_shape=[jax.ShapeDtypeStruct((a.shape[1], n), F32) for a in a_list],
        in_specs=[pl.BlockSpec((tk, a.shape[1]), lambda k: (k, 0)) for a in a_list] + [pl.BlockSpec((tk, n), lambda k: (k, 0))],
        out_specs=[pl.BlockSpec((a.shape[1], n), lambda k: (0, 0)) for a in a_list],
        compiler_params=_params(("arbitrary",)),
    )(*a_list, b)


def _mm_nt(a, b, name, out_dtype, tm, tn, n_out=1):
    m, k = a.shape
    n = b.shape[0] // n_out
    nt = n // tn

    def body(a_ref, *refs):
        a_blk = a_ref[...].astype(BF16)
        for b_ref, o_ref in zip(refs[:n_out], refs[n_out:]):
            o_ref[...] = _dot_nt(a_blk, b_ref[...].astype(BF16)).astype(out_dtype)

    outs = pl.pallas_call(
        body,
        name=name,
        grid=(nt, m // tm),
        out_shape=[jax.ShapeDtypeStruct((m, n), out_dtype)] * n_out,
        in_specs=[pl.BlockSpec((tm, k), lambda j, i: (i, 0))]
        + [pl.BlockSpec((tn, k), lambda j, i, o=o: (o * nt + j, 0)) for o in range(n_out)],
        out_specs=[pl.BlockSpec((tm, tn), lambda j, i: (i, j))] * n_out,
        compiler_params=_params(("parallel", "parallel")),
    )(a, *([b] * n_out))
    return outs[0] if n_out == 1 else outs


def _mm_tn(a, b_list, name, tm, tn, tk):
    s, m = a.shape
    n = b_list[0].shape[1]
    n_b = len(b_list)

    def body(a_ref, *refs):
        b_refs, o_refs = refs[:n_b], refs[n_b:]

        @pl.when(pl.program_id(2) == 0)
        def _():
            for o_ref in o_refs:
                o_ref[...] = jnp.zeros_like(o_ref)

        a_blk = a_ref[...].astype(BF16)
        for b_ref, o_ref in zip(b_refs, o_refs):
            o_ref[...] += _dot_tn(a_blk, b_ref[...].astype(BF16))

    return pl.pallas_call(
        body,
        name=name,
        grid=(n // tn, m // tm, s // tk),
        out_shape=[jax.ShapeDtypeStruct((m, n), F32)] * n_b,
        in_specs=[pl.BlockSpec((tk, tm), lambda j, i, k: (k, i))] + [pl.BlockSpec((tk, tn), lambda j, i, k: (k, j))] * n_b,
        out_specs=[pl.BlockSpec((tm, tn), lambda j, i, k: (i, j))] * n_b,
        compiler_params=_params(("parallel", "parallel", "arbitrary")),
    )(a, *b_list)


def _rows_call(fn, name, rows, params, out_rows, out_accs, ts, comm=None):
    s = rows[0].shape[0]
    nr, npar, no = len(rows), len(params), len(out_rows)

    def body(*refs):
        r, p = refs[:nr], refs[nr:nr + npar]
        o, acc = refs[nr + npar:nr + npar + no], refs[nr + npar + no:]
        outs, sums = fn(*[t[...] for t in r], *[t[...] for t in p])
        for ref, val in zip(o, outs):
            ref[...] = val.astype(ref.dtype)
        if acc:
            @pl.when(pl.program_id(0) == 0)
            def _():
                for ref in acc:
                    ref[...] = jnp.zeros_like(ref)

            for ref, val in zip(acc, sums):
                ref[...] += val

    outs, got = _hosted_call(
        body, comm, name=name, grid=(s // ts,),
        out_shape=[jax.ShapeDtypeStruct((s, w), dt) for w, dt in out_rows]
        + [jax.ShapeDtypeStruct(shape, F32) for shape in out_accs],
        in_specs=[pl.BlockSpec((ts, t.shape[1]), lambda i: (i, 0)) for t in rows]
        + [pl.BlockSpec(t.shape, lambda i: (0, 0)) for t in params],
        out_specs=[pl.BlockSpec((ts, w), lambda i: (i, 0)) for w, _ in out_rows]
        + [pl.BlockSpec(shape, lambda i: (0, 0)) for shape in out_accs],
        scratch_shapes=[], dims=("arbitrary",), args=(*rows, *params))
    return outs if comm is None else (outs, got)


def _fwd_norm1(x, g, scale, shift):
    return (_norm_mod(x, g, scale, shift),), ()


def _resid_norm(x, mixed, gate, g, scale, shift):
    x1 = x + (1.0 + gate) * mixed
    return x1, _norm_mod(x1, g, scale, shift)


def _fwd_resid_norm(x, mixed, gate, g, scale, shift):
    return _resid_norm(x, mixed, gate, g, scale, shift), ()


def _final(x1, ffn, tgt, gate, g):
    def head(x1, ffn, gate, g):
        return _rms(x1 + (1.0 + gate) * ffn, g)

    y, vjp = jax.vjp(head, x1, ffn, gate, g)
    err = y - tgt
    dx2, dffn, dgate, dg = vjp(err * (1.0 / D))
    sq = jnp.sum(jnp.sum(err * err, axis=1, keepdims=True), axis=0, keepdims=True)
    loss = jnp.broadcast_to(sq * (0.5 / D), (1, 128))
    return (dx2, dffn), (dgate, dg, loss)


def _bwd_resid_norm(x, mixed, dx2, dh2, gate, g, scale, shift):
    _, vjp = jax.vjp(_resid_norm, x, mixed, gate, g, scale, shift)
    dx, dmixed, dgate, dg, dscale, dshift = vjp((dx2, dh2))
    return (dx, dmixed), (dgate, dg, dscale, dshift)


def _bwd_norm1(x, dh, dx1, g, scale, shift):
    _, vjp = jax.vjp(_norm_mod, x, g, scale, shift)
    dx, dg, dscale, dshift = vjp(dh)
    return (dx1 + dx,), (dg, dscale, dshift)


def _ada_fwd(c16, w):
    def body(c_ref, w_ref, o_ref):
        c = c_ref[...]
        o_ref[...] = _dot((c * _sigmoid(c)).astype(BF16), w_ref[...].astype(BF16))

    return pl.pallas_call(
        body, name="ada_fwd", out_shape=jax.ShapeDtypeStruct((c16.shape[0], w.shape[1]), F32),
        compiler_params=_params(),
    )(c16, w)


def _ada_bwd(c16, dmod16):
    def body(c_ref, d_ref, o_ref):
        c = c_ref[...]
        o_ref[...] = _dot_tn((c * _sigmoid(c)).astype(BF16), d_ref[...].astype(BF16))

    return pl.pallas_call(
        body, name="ada_bwd", out_shape=jax.ShapeDtypeStruct((c16.shape[1], dmod16.shape[1]), F32),
        compiler_params=_params(),
    )(c16, dmod16)


CONV_R = 256


def _conv_window(win, wc, bc):
    s1 = pltpu.roll(win, 1, 0)
    s2 = pltpu.roll(win, 2, 0)
    u = bc + wc[0:1] * s2 + wc[1:2] * s1 + wc[2:3] * win
    return u[8:], s1[8:], s2[8:], win[8:]


def _row_windows(ref, before, after, n_after, lanes):
    ts, r = ref.shape[0], CONV_R

    def middle(i):
        return ref[pl.ds(pl.multiple_of(i * r, r) - 8, r + 8 + n_after), lanes]

    first = jnp.concatenate([before, ref[0:r + n_after, lanes]], axis=0)
    last = ref[ts - r - 8:ts, lanes] if n_after == 0 else jnp.concatenate([ref[ts - r - 8:ts, lanes], after], axis=0)
    return first, middle, last


def _ffn_specs(ts, s):
    cur = pl.BlockSpec((ts, FF_T), lambda j, i: (i, j))
    halo = pl.BlockSpec((8, FF_T), lambda j, i: (jnp.maximum(i * (ts // 8) - 1, 0), j))
    nxt = pl.BlockSpec((8, FF_T), lambda j, i: (jnp.minimum((i + 1) * (ts // 8), s // 8 - 1), j))
    wc = [pl.BlockSpec((8, FF_T), lambda j, i, h=h: (0, h * N_FT + j)) for h in range(2)]
    bc = [pl.BlockSpec((1, FF_T), lambda j, i, h=h: (0, h * N_FT + j)) for h in range(2)]
    return cur, halo, nxt, wc, bc


def _conv_glu_fwd(up_v, up_g, wc, bc, ts):
    s = up_v.shape[0]
    cur, halo, _, wcs, bcs = _ffn_specs(ts, s)

    def body(v_ref, vh_ref, g_ref, gh_ref, wcv_ref, wcg_ref, bcv_ref, bcg_ref, a_ref):
        keep = jnp.where(pl.program_id(1) == 0, 0.0, 1.0)
        for lanes in (slice(0, 128), slice(128, 256)):
            wcv, wcg, bcv, bcg = wcv_ref[:, lanes], wcg_ref[:, lanes], bcv_ref[:, lanes], bcg_ref[:, lanes]
            first_v, mid_v, _ = _row_windows(v_ref, vh_ref[:, lanes] * keep, None, 0, lanes)
            first_g, mid_g, _ = _row_windows(g_ref, gh_ref[:, lanes] * keep, None, 0, lanes)

            def emit(win_v, win_g, start, lanes=lanes, wcv=wcv, wcg=wcg, bcv=bcv, bcg=bcg):
                val = _conv_window(win_v, wcv, bcv)[0]
                gte = _conv_window(win_g, wcg, bcg)[0]
                a_ref[pl.ds(start, CONV_R), lanes] = (val * (gte * _sigmoid(gte))).astype(BF16)

            emit(first_v, first_g, 0)

            def loop(i, carry, emit=emit, mid_v=mid_v, mid_g=mid_g):
                emit(mid_v(i), mid_g(i), pl.multiple_of(i * CONV_R, CONV_R))
                return carry

            lax.fori_loop(1, ts // CONV_R, loop, 0)

    return pl.pallas_call(
        body, name="conv_glu_fwd", grid=(N_FT, s // ts),
        out_shape=jax.ShapeDtypeStruct((s, D_FF), BF16),
        in_specs=[cur, halo, cur, halo, *wcs, *bcs], out_specs=cur,
        compiler_params=_params(("parallel", "arbitrary")),
    )(up_v, up_v, up_g, up_g, wc, wc, bc, bc)


def _glu_bwd(val, gte, da):
    sg = _sigmoid(gte)
    return da * (gte * sg), da * val * (sg * (1.0 + gte * (1.0 - sg)))


def _conv_glu_bwd(up_v, up_g, da, wc, bc, ts, comm=None):
    s = up_v.shape[0]
    nblk = s // ts
    cur, halo, nxt, wcs, bcs = _ffn_specs(ts, s)
    acc_w = pl.BlockSpec((8, FF_T), lambda j, i: (0, j))
    acc_b = pl.BlockSpec((1, FF_T), lambda j, i: (0, j))

    def body(v_ref, vh_ref, vn_ref, g_ref, gh_ref, gn_ref, da_ref, dan_ref, wcv_ref, wcg_ref, bcv_ref, bcg_ref,
             dupv_ref, dupg_ref, dwcv_ref, dwcg_ref, dbcv_ref, dbcg_ref):
        first = pl.program_id(1) == 0
        keep = jnp.where(first, 0.0, 1.0)
        keep_next = jnp.where(pl.program_id(1) == nblk - 1, 0.0, 1.0)
        r = CONV_R
        n = ts // r

        @pl.when(first)
        def _():
            for ref in (dwcv_ref, dwcg_ref, dbcv_ref, dbcg_ref):
                ref[...] = jnp.zeros_like(ref)

        for lanes in (slice(0, 128), slice(128, 256)):
            wcv, wcg, bcv, bcg = wcv_ref[:, lanes], wcg_ref[:, lanes], bcv_ref[:, lanes], bcg_ref[:, lanes]
            first_v, mid_v, last_v = _row_windows(v_ref, vh_ref[:, lanes] * keep, vn_ref[:, lanes], 8, lanes)
            first_g, mid_g, last_g = _row_windows(g_ref, gh_ref[:, lanes] * keep, gn_ref[:, lanes], 8, lanes)

            def emit(win_v, win_g, da_w, start, lanes=lanes, wcv=wcv, wcg=wcg, bcv=bcv, bcg=bcg):
                u_v, s1_v, s2_v, x_v = _conv_window(win_v, wcv, bcv)
                u_g, s1_g, s2_g, x_g = _conv_window(win_g, wcg, bcg)
                du_v, du_g = _glu_bwd(u_v, u_g, da_w)
                rows = pl.ds(start, r)
                for du, s1, s2, x, wc, dup_ref, dwc_ref, dbc_ref in (
                        (du_v, s1_v, s2_v, x_v, wcv, dupv_ref, dwcv_ref, dbcv_ref),
                        (du_g, s1_g, s2_g, x_g, wcg, dupg_ref, dwcg_ref, dbcg_ref)):
                    dup = wc[2:3] * du + wc[1:2] * pltpu.roll(du, r + 7, 0) + wc[0:1] * pltpu.roll(du, r + 6, 0)
                    dup_ref[rows, lanes] = dup[:r].astype(BF16)
                    du = du[:r]
                    dwc_ref[0:1, lanes] += jnp.sum(du * s2[:r], axis=0, keepdims=True)
                    dwc_ref[1:2, lanes] += jnp.sum(du * s1[:r], axis=0, keepdims=True)
                    dwc_ref[2:3, lanes] += jnp.sum(du * x[:r], axis=0, keepdims=True)
                    dbc_ref[:, lanes] += jnp.sum(du, axis=0, keepdims=True)

            emit(first_v, first_g, da_ref[0:r + 8, lanes], 0)

            def loop(i, carry, emit=emit, mid_v=mid_v, mid_g=mid_g, lanes=lanes):
                start = pl.multiple_of(i * r, r)
                emit(mid_v(i), mid_g(i), da_ref[pl.ds(start, r + 8), lanes], start)
                return carry

            lax.fori_loop(1, n - 1, loop, 0)
            da_last = jnp.concatenate([da_ref[ts - r:ts, lanes], dan_ref[:, lanes] * keep_next], axis=0)
            emit(last_v, last_g, da_last, ts - r)

    return _hosted_call(
        body, comm, name="conv_glu_bwd", grid=(N_FT, nblk),
        out_shape=[jax.ShapeDtypeStruct((s, D_FF), BF16)] * 2 + [jax.ShapeDtypeStruct((8, D_FF), F32)] * 2
        + [jax.ShapeDtypeStruct((1, D_FF), F32)] * 2,
        in_specs=[cur, halo, nxt, cur, halo, nxt, cur, nxt, *wcs, *bcs],
        out_specs=[cur, cur, acc_w, acc_w, acc_b, acc_b],
        scratch_shapes=[], dims=("parallel", "arbitrary"),
        args=(up_v, up_v, up_v, up_g, up_g, up_g, da, da, wc, wc, bc, bc))


def _tri(kind):
    b = SB_B
    m = {"lower_strict": np.tril(np.ones((b, b)), -1), "upper_incl": np.triu(np.ones((b, b)), 0),
         "upper_strict": np.triu(np.ones((b, b)), 1)}[kind]
    return jnp.asarray(np.concatenate([m, np.ones((b, 128))], axis=1), BF16)


def _key_sums(x, tri):
    cb = _dot(x.astype(BF16), tri)
    return cb[:, :SB_B], cb[:, SB_B:]


def _sb_fwd(proj, comm=None):
    s = proj.shape[0]
    b = SB_B

    def body(q_ref, k_ref, v_ref, tri_ref, o_ref, t_ref, first_ref, c_ref, a_ref):
        i = pl.program_id(1)
        lane = lax.broadcasted_iota(jnp.int32, (b, 128), 1)
        heads = (lane < 64, lane >= 64)
        causal = lax.broadcasted_iota(jnp.int32, (b, b), 1) < lax.broadcasted_iota(jnp.int32, (b, b), 0)
        q = q_ref[...] * 0.125
        qm = [jnp.where(h, q, 0.0).astype(BF16) for h in heads]
        c_ref[...] = jnp.zeros_like(c_ref)
        a_ref[...] = jnp.zeros_like(a_ref)

        def prepare(jj, masked):
            rows = pl.ds(pl.multiple_of(jj * b, b), b)
            kb = k_ref[rows, :].astype(BF16)
            vb = v_ref[rows, :]
            out = []
            for hh in range(2):
                z = _dot_nt(qm[hh], kb)
                lg = _log_sigmoid(-z)
                if masked:
                    lg = jnp.where(causal, lg, 0.0)
                after, total = _key_sums(lg, tri_ref[...])
                out.append((lg + z + after, total, jnp.where(heads[hh], vb, 0.0).astype(BF16)))
            return out

        def walk_blocks(blocks):
            for hh in range(2):
                c, a = c_ref[hh], a_ref[hh]
                for pre, masked in blocks:
                    logw, total, vm = pre[hh]
                    w = jnp.exp(logw + jnp.concatenate([c, c], axis=1))
                    if masked:
                        w = jnp.where(causal, w, 0.0)
                    a = a + _dot(w.astype(BF16), vm)
                    c = c + total
                c_ref[hh], a_ref[hh] = c, a

        def largest_sum():
            return jnp.max(jnp.maximum(c_ref[0], c_ref[1]))

        @pl.when(i == 0)
        def _():
            walk_blocks([(prepare(i, True), True)])

        @pl.when(i > 0)
        def _():
            walk_blocks([(prepare(i, True), True), (prepare(i - 1, False), False)])

        def more(state):
            jj, top = state
            return jnp.logical_and(jj >= 0, top > SB_DEAD)

        def walk(state):
            jj, _ = state
            walk_blocks([(prepare(jj, False), False)])
            return jj - 1, largest_sum()

        jj, _ = lax.while_loop(more, walk, (jnp.maximum(i - 2, -1), largest_sum()))
        o_ref[...] = (a_ref[0] + a_ref[1]).astype(BF16)
        t_ref[...] = jnp.concatenate([c_ref[0], c_ref[1]], axis=1)
        first_ref[pl.program_id(0), i] = (jj + 1).astype(F32)

    return _hosted_call(
        body, comm, name="sb_fwd", grid=(4, s // b),
        out_shape=[jax.ShapeDtypeStruct((s, SB_W), BF16), jax.ShapeDtypeStruct((s, 2 * SB_W), F32),
                   jax.ShapeDtypeStruct((4, s // b), F32)],
        in_specs=[pl.BlockSpec((b, 128), lambda p, i: (i, p)),
                  pl.BlockSpec((s, 128), lambda p, i: (0, 4 + p)),
                  pl.BlockSpec((s, 128), lambda p, i: (0, 8 + p)),
                  pl.BlockSpec((b, b + 128), lambda p, i: (0, 0))],
        out_specs=[pl.BlockSpec((b, 128), lambda p, i: (i, p)), pl.BlockSpec((b, 256), lambda p, i: (i, p)),
                   pl.BlockSpec(memory_space=pltpu.SMEM)],
        scratch_shapes=[pltpu.VMEM((2, b, 128), F32), pltpu.VMEM((2, b, 128), F32)],
        dims=("arbitrary", "arbitrary"), args=(proj, proj, proj, _tri("lower_strict")))


def _sb_bwd(proj, do, tot, first, comm=None):
    s = proj.shape[0]
    b = SB_B

    def body(q_ref, k_ref, v_ref, do_ref, t_ref, first_ref, ti_ref, ts_ref, dq_ref, dko_ref, dvo_ref, cl_ref, ce_ref,
             a_ref, dk_ref, dv_ref):
        i = pl.program_id(1)
        first = jnp.clip(first_ref[pl.program_id(0), i].astype(jnp.int32), 0, i)
        lane = lax.broadcasted_iota(jnp.int32, (b, 128), 1)
        heads = (lane < 64, lane >= 64)
        causal = lax.broadcasted_iota(jnp.int32, (b, b), 1) < lax.broadcasted_iota(jnp.int32, (b, b), 0)
        q = q_ref[...] * 0.125
        do = do_ref[...]
        qm = [jnp.where(h, q, 0.0).astype(BF16) for h in heads]
        dom = [jnp.where(h, do, 0.0).astype(BF16) for h in heads]
        cl_ref[...] = jnp.zeros_like(cl_ref)
        ce_ref[...] = jnp.zeros_like(ce_ref)
        a_ref[...] = jnp.zeros_like(a_ref)

        @pl.when(i == 0)
        def _():
            dk_ref[...] = jnp.zeros_like(dk_ref)
            dv_ref[...] = jnp.zeros_like(dv_ref)

        def prepare(jj, masked):
            rows = pl.ds(pl.multiple_of(jj * b, b), b)
            kf = k_ref[rows, :]
            kb = kf.astype(BF16)
            vb = v_ref[rows, :].astype(BF16)
            out = []
            for hh in range(2):
                z = _dot_nt(qm[hh], kb)
                lg = _log_sigmoid(-z)
                if masked:
                    lg = jnp.where(causal, lg, 0.0)
                upto, total = _key_sums(lg, ti_ref[...])
                lsz = lg + z
                out.append((lsz - upto, total, jnp.exp(lsz), _dot_nt(dom[hh], vb),
                            jnp.where(heads[hh], kf, 0.0).astype(BF16)))
            return rows, out

        def walk_blocks(blocks):
            grads = [[jnp.zeros((b, 128), F32), jnp.zeros((b, 128), F32)] for _ in blocks]
            cl = [cl_ref[0], cl_ref[1]]
            ce = [ce_ref[0], ce_ref[1]]
            a = [a_ref[0], a_ref[1]]
            for n, ((_, pre), masked) in enumerate(blocks):
                for hh in range(2):
                    t = t_ref[:, hh * 128:(hh + 1) * 128]
                    logw, total, sig, dw, km = pre[hh]
                    w = jnp.exp(logw + jnp.concatenate([t - cl[hh], t - cl[hh]], axis=1))
                    if masked:
                        w = jnp.where(causal, w, 0.0)
                    e = w * dw
                    sums = _dot(e.astype(BF16), ts_ref[...])
                    before, etot = sums[:, :SB_B], sums[:, SB_B:]
                    dz = e - sig * (e + before + jnp.concatenate([ce[hh], ce[hh]], axis=1))
                    if masked:
                        dz = jnp.where(causal, dz, 0.0)
                    dzb = dz.astype(BF16)
                    a[hh] = a[hh] + _dot(dzb, km)
                    grads[n][0] = grads[n][0] + _dot_tn(dzb, qm[hh])
                    grads[n][1] = grads[n][1] + _dot_tn(w.astype(BF16), dom[hh])
                    cl[hh] = cl[hh] + total
                    ce[hh] = ce[hh] + etot
            for hh in range(2):
                cl_ref[hh], ce_ref[hh], a_ref[hh] = cl[hh], ce[hh], a[hh]
            for ((rows, _), _), (dk, dv) in zip(blocks, grads):
                dk_ref[rows, :] += dk
                dv_ref[rows, :] += dv

        def loop(jj, carry):
            walk_blocks([(prepare(jj, False), False)])
            return carry

        lax.fori_loop(first, i - 1, loop, 0)

        @pl.when(i == 0)
        def _():
            walk_blocks([(prepare(i, True), True)])

        @pl.when(i > 0)
        def _():
            walk_blocks([(prepare(i - 1, False), False), (prepare(i, True), True)])

        dq_ref[...] = ((a_ref[0] + a_ref[1]) * 0.125).astype(BF16)

        @pl.when(i == s // b - 1)
        def _():
            dko_ref[...] = dk_ref[...].astype(BF16)
            dvo_ref[...] = dv_ref[...].astype(BF16)

    blk = pl.BlockSpec((b, 128), lambda p, i: (i, p))
    full = pl.BlockSpec((s, 128), lambda p, i: (0, p))
    tri = pl.BlockSpec((b, b + 128), lambda p, i: (0, 0))
    return _hosted_call(
        body, comm, name="sb_bwd", grid=(4, s // b),
        out_shape=[jax.ShapeDtypeStruct((s, SB_W), BF16)] * 3,
        in_specs=[blk, pl.BlockSpec((s, 128), lambda p, i: (0, 4 + p)), pl.BlockSpec((s, 128), lambda p, i: (0, 8 + p)),
                  blk, pl.BlockSpec((b, 256), lambda p, i: (i, p)), pl.BlockSpec(memory_space=pltpu.SMEM), tri, tri],
        out_specs=[blk, full, full],
        scratch_shapes=[pltpu.VMEM((2, b, 128), F32)] * 3 + [pltpu.VMEM((s, 128), F32)] * 2,
        dims=("arbitrary", "arbitrary"),
        args=(proj, proj, proj, do, tot, first, _tri("upper_incl"), _tri("upper_strict")))


_COL_Q, _COL_K, _COL_V, _COL_G, _COL_F = 12, 14, 8, 10, 24


def _gla_consts():
    c = CHUNK
    incl = np.tril(np.ones((c, c)), 0)
    strict = np.tril(np.ones((c, c)), -1)
    bd = np.zeros((256, 128))
    bd[:128, :64] = 1.0
    bd[128:, 64:] = 1.0
    return jnp.asarray(incl, BF16), jnp.asarray(strict, BF16), jnp.asarray(bd, F32)


def _time_sums(tri, x):
    hi, lo = _hilo(x)
    return _dot(tri, hi) + _dot(tri, lo)


def _gla_gate(o, gg, g):
    parts = []
    for h in range(2):
        oh = o[:, h * 128:(h + 1) * 128]
        parts.append(oh * lax.rsqrt(jnp.mean(oh * oh, axis=-1, keepdims=True) + EPS))
    return (jnp.concatenate(parts, axis=1) * g) * (gg * _sigmoid(gg))


def _gla_chunk(la_c, k_c, incl):
    cum = _time_sums(incl, la_c)
    total = cum[CHUNK - 1:CHUNK]
    edec = jnp.exp(total - cum)
    return edec, k_c * edec, jnp.exp(total)


def _gla_fwd(proj, wfg, bfg, gain):
    s = proj.shape[0]
    t = GLA_T
    nch = t // CHUNK
    incl, _, bd = _gla_consts()

    def body(q_ref, k_ref, v_ref, gg_ref, f_ref, wf_ref, bf_ref, g_ref, incl_ref, bd_ref, o_ref, st_ref, state_ref):
        @pl.when(pl.program_id(0) == 0)
        def _():
            state_ref[...] = jnp.zeros_like(state_ref)

        la = _log_sigmoid(_dot(f_ref[...].astype(BF16), wf_ref[...].astype(BF16)) + bf_ref[...]) * (1.0 / 16.0)
        states = [state_ref[0], state_ref[1]]
        for cc in range(nch):
            rows = slice(cc * CHUNK, (cc + 1) * CHUNK)
            for p in range(2):
                kl, vl = slice(p * 128, (p + 1) * 128), slice(p * 256, (p + 1) * 256)
                _, kdec, dec = _gla_chunk(la[rows, kl], k_ref[rows, kl], incl_ref[...])
                kv = _dot_tn(v_ref[rows, vl].astype(BF16), kdec.astype(BF16))
                states[p] = states[p] * dec + bd_ref[...] * kv
                st_ref[p, cc] = states[p]
                o = _dot_nt((q_ref[rows, kl] * 0.125).astype(BF16), states[p].astype(BF16))
                o_ref[rows, vl] = _gla_gate(o, gg_ref[rows, vl], g_ref[:, vl]).astype(BF16)
        state_ref[0] = states[0]
        state_ref[1] = states[1]

    def col(width, blk):
        return pl.BlockSpec((t, width), lambda i: (i, blk))

    def whole(shape):
        return pl.BlockSpec(shape, lambda i: tuple(0 for _ in shape))

    return pl.pallas_call(
        body, name="gla_fwd", grid=(s // t,),
        out_shape=[jax.ShapeDtypeStruct((s, GLA_VW), BF16), jax.ShapeDtypeStruct((2, s // CHUNK, 256, 128), F32)],
        in_specs=[col(256, _COL_Q // 2), col(256, _COL_K // 2), col(512, _COL_V // 2), col(512, _COL_G // 2),
                  col(128, _COL_F), whole((128, GLA_KW)), whole((1, GLA_KW)), whole((1, GLA_VW)),
                  whole((CHUNK, CHUNK)), whole((256, 128))],
        out_specs=[col(512, 0), pl.BlockSpec((2, nch, 256, 128), lambda i: (0, i, 0, 0))],
        scratch_shapes=[pltpu.VMEM((2, 256, 128), F32)],
        compiler_params=_params(("arbitrary",)),
    )(proj, proj, proj, proj, proj, wfg, bfg, gain, incl, bd)


def _gla_bwd(proj, do, states, wfg, bfg, gain):
    s = proj.shape[0]
    t = GLA_T
    nch = t // CHUNK
    nblk = s // t
    incl, strict, bd = _gla_consts()

    def body(q_ref, k_ref, v_ref, gg_ref, f_ref, do_ref, st_ref, sp_ref, wf_ref, bf_ref, g_ref, incl_ref, str_ref,
             bd_ref, dq_ref, dk_ref, dv_ref, dgg_ref, df_ref, dwf_ref, dbf_ref, dg_ref, carry_ref, dfs_ref):
        i = pl.program_id(0)

        @pl.when(i == 0)
        def _():
            carry_ref[...] = jnp.zeros_like(carry_ref)
            dwf_ref[...] = jnp.zeros_like(dwf_ref)
            dbf_ref[...] = jnp.zeros_like(dbf_ref)
            dg_ref[...] = jnp.zeros_like(dg_ref)

        fb = f_ref[...].astype(BF16)
        wf = wf_ref[...].astype(BF16)
        f = _dot(fb, wf) + bf_ref[...]
        la = _log_sigmoid(f) * (1.0 / 16.0)
        dla_df = _sigmoid(-f) * (1.0 / 16.0)
        first_block = jnp.where(i == nblk - 1, 0.0, 1.0)
        carries = [carry_ref[0], carry_ref[1]]
        dgains = [jnp.zeros((1, 256), F32), jnp.zeros((1, 256), F32)]
        local = {}
        for cc in reversed(range(nch)):
            rows = slice(cc * CHUNK, (cc + 1) * CHUNK)
            for p in range(2):
                kl, vl = slice(p * 128, (p + 1) * 128), slice(p * 256, (p + 1) * 256)
                edec, kdec, dec = _gla_chunk(la[rows, kl], k_ref[rows, kl], incl_ref[...])
                qs = (q_ref[rows, kl] * 0.125).astype(BF16)
                sb16 = st_ref[p, cc].astype(BF16)
                o = _dot_nt(qs, sb16)
                _, vjp = jax.vjp(_gla_gate, o, gg_ref[rows, vl], g_ref[:, vl])
                do_c, dgg_c, dg_c = vjp(do_ref[rows, vl])
                dgains[p] = dgains[p] + dg_c
                dgg_ref[rows, vl] = dgg_c.astype(BF16)
                do16 = do_c.astype(BF16)
                dq_ref[rows, kl] = (_dot(do16, sb16) * 0.125).astype(BF16)
                local[cc, p] = (edec, kdec, dec, bd_ref[...] * _dot_tn(do16, qs))
        for cc in reversed(range(nch)):
            rows = slice(cc * CHUNK, (cc + 1) * CHUNK)
            for p in range(2):
                kl, vl = slice(p * 128, (p + 1) * 128), slice(p * 256, (p + 1) * 256)
                edec, kdec, dec, readers = local[cc, p]
                prev = st_ref[p, cc - 1] if cc > 0 else sp_ref[p, 0] * first_block
                grad = readers + carries[p]
                g16 = grad.astype(BF16)
                dv_ref[rows, vl] = _dot_nt(kdec.astype(BF16), g16).astype(BF16)
                dkdec = _dot(v_ref[rows, vl].astype(BF16), g16)
                ddec = jnp.sum(grad * prev, axis=0, keepdims=True) * dec
                dk_ref[rows, kl] = (dkdec * edec).astype(BF16)
                dla = _time_sums(str_ref[...], dkdec * kdec) + ddec
                dfs_ref[rows, kl] = dla * dla_df[rows, kl]
                carries[p] = grad * dec
        carry_ref[0] = carries[0]
        carry_ref[1] = carries[1]
        df = dfs_ref[...]
        df16 = df.astype(BF16)
        df_ref[...] = _dot_nt(df16, wf).astype(BF16)
        dwf_ref[...] += _dot_tn(fb, df16)
        dbf_ref[...] += jnp.sum(df, axis=0, keepdims=True)
        dg_ref[...] += jnp.concatenate(dgains, axis=1)

    def col(width, blk):
        return pl.BlockSpec((t, width), lambda i: (nblk - 1 - i, blk))

    def whole(shape):
        return pl.BlockSpec(shape, lambda i: tuple(0 for _ in shape))

    return pl.pallas_call(
        body, name="gla_bwd", grid=(nblk,),
        out_shape=[jax.ShapeDtypeStruct((s, GLA_KW), BF16), jax.ShapeDtypeStruct((s, GLA_KW), BF16),
                   jax.ShapeDtypeStruct((s, GLA_VW), BF16), jax.ShapeDtypeStruct((s, GLA_VW), BF16),
                   jax.ShapeDtypeStruct((s, 128), BF16), jax.ShapeDtypeStruct((128, GLA_KW), F32),
                   jax.ShapeDtypeStruct((1, GLA_KW), F32), jax.ShapeDtypeStruct((1, GLA_VW), F32)],
        in_specs=[col(256, _COL_Q // 2), col(256, _COL_K // 2), col(512, _COL_V // 2), col(512, _COL_G // 2),
                  col(128, _COL_F), col(512, 1),
                  pl.BlockSpec((2, nch, 256, 128), lambda i: (0, nblk - 1 - i, 0, 0)),
                  pl.BlockSpec((2, 1, 256, 128), lambda i: (0, jnp.maximum((nblk - 1 - i) * nch - 1, 0), 0, 0)),
                  whole((128, GLA_KW)), whole((1, GLA_KW)), whole((1, GLA_VW)),
                  whole((CHUNK, CHUNK)), whole((CHUNK, CHUNK)), whole((256, 128))],
        out_specs=[col(256, 0), col(256, 0), col(512, 0), col(512, 0), col(128, 0),
                   whole((128, GLA_KW)), whole((1, GLA_KW)), whole((1, GLA_VW))],
        scratch_shapes=[pltpu.VMEM((2, 256, 128), F32), pltpu.VMEM((t, GLA_KW), F32)],
        compiler_params=_params(("arbitrary",)),
    )(proj, proj, proj, proj, proj, do, states, states, wfg, bfg, gain, incl, strict, bd)


def _cols_by_dev(a, n_dev=N_DEV):
    r, c = a.shape
    return a.reshape(r, n_dev, c // n_dev).transpose(1, 0, 2)


def _cols_from_dev(a):
    _, r, n = a.shape
    return a.transpose(1, 0, 2).reshape(r, N_DEV * n)


def kernel(x, c, w_ada, b_ada, g_norm1, w_in, w_fg2, b_fg2, g_gla_out, w_out, g_norm2, w_up, w_conv, b_conv, w_down, g_final, loss_target, m_w_ada, m_b_ada, m_g_norm1, m_w_in, m_w_fg2, m_b_fg2, m_g_gla_out, m_w_out, m_g_norm2, m_w_up, m_w_conv, m_b_conv, m_w_down, m_g_final, v_w_ada, v_b_ada, v_g_norm1, v_w_in, v_w_fg2, v_b_fg2, v_g_gla_out, v_w_out, v_g_norm2, v_w_up, v_w_conv, v_b_conv, v_w_down, v_g_final):
    s = x.shape[1]
    me = 4 * lax.axis_index("x") + 2 * lax.axis_index("y") + lax.axis_index("c")
    xs, tgt = x[0], loss_target[0]
    ts = min(512, s)
    tm = min(1024, s)
    tc = min(2048, s // 2)

    c_all, wconv_g, wfg_g = _exchange([c, w_conv[0], w_fg2[0]], "gather_first", True)
    wconv_p = jnp.pad(_cols_from_dev(wconv_g), ((0, 5), (0, 0)))
    wfg_p = jnp.pad(_cols_from_dev(wfg_g), ((0, 128 - RANK), (0, 0)))

    c16 = jnp.pad(c_all.reshape(N_DEV, D), ((0, 8), (0, 0)))
    modp = _ada_fwd(c16, w_ada[0])[:N_DEV]
    (mod_all,) = _exchange([modp], "gather_mod", True)
    mod = lax.dynamic_index_in_dim(mod_all, me, axis=1, keepdims=False).reshape(1, 6 * D) + b_ada
    shift1, scale1, gate1, shift2, scale2, gate2 = [mod[:, k * D:(k + 1) * D] for k in range(6)]

    w_in_t, m_in_t, v_in_t = [jnp.swapaxes(t[0], 0, 1) for t in (w_in, m_w_in, v_w_in)]
    w_up_t, m_up_t, v_up_t = [jnp.swapaxes(t[0], 0, 1) for t in (w_up, m_w_up, v_w_up)]
    (h,), (win_g,) = _rows_call(_fwd_norm1, "norm1", [xs], [g_norm1, scale1, shift1], [(D, BF16)], [], ts,
                                comm=([w_in_t.astype(BF16)], True))
    win_t = jnp.pad(win_g.reshape(IN_W, D), ((0, IN_WP - IN_W), (0, 0)))
    proj = _mm_nt(h, win_t, "in_proj", F32, ts, IN_WP)
    (o_sb, sb_tot, sb_first), (wout_g, wup_g, wdown_g) = _sb_fwd(
        proj, comm=([w_out[0].astype(BF16), w_up_t.astype(BF16), w_down[0].astype(BF16)], True))
    wout = wout_g.reshape(D, D)
    wup_t = wup_g.reshape(2 * D_FF, D)
    wdown = wdown_g.reshape(D_FF, D)
    o_gla, states = _gla_fwd(proj, wfg_p, b_fg2, g_gla_out)
    cat = jnp.concatenate([o_sb, o_gla], axis=1)
    mixed, _ = _mm([cat], wout, "out_proj", F32, tm, 512)
    x1, h2 = _rows_call(_fwd_resid_norm, "resid_norm2", [xs, mixed], [gate1, g_norm2, scale2, shift2],
                        [(D, F32), (D, BF16)], [], ts)
    up_v, up_g = _mm_nt(h2, wup_t, "up_proj", F32, tm, 1408, n_out=2)
    act = _conv_glu_fwd(up_v, up_g, wconv_p, b_conv, tc)
    ffn, _ = _mm([act], wdown, "down_proj", F32, tm, 512)
    dx2, dffn, dgate2, dg_final, loss_part = _rows_call(
        _final, "final", [x1, ffn, tgt], [gate2, g_final.reshape(1, D)],
        [(D, F32), (D, BF16)], [(1, D), (1, D), (1, 128)], ts)

    (dw_down,) = _mm_tn(act, [dffn], "down_wgrad", 1408, 1024, tm)
    da = _mm_nt(dffn, wdown, "down_dgrad", F32, tm, 1408)
    (dup_v, dup_g, dwc_v, dwc_g, dbc_v, dbc_g), (r_down,) = _conv_glu_bwd(
        up_v, up_g, da, wconv_p, b_conv, tc, comm=([dw_down.reshape(N_DEV, 352, D)], False))
    (dw_up_tv,) = _mm_tn(dup_v, [h2], "up_wgrad_v", 1408, 1024, tm)
    (dw_up_tg,) = _mm_tn(dup_g, [h2], "up_wgrad_g", 1408, 1024, tm)
    dh2, _ = _mm([dup_v, dup_g], wup_t, "up_dgrad", F32, tm, 512)
    dx1, dmixed, dgate1, dg_norm2, dscale2, dshift2 = _rows_call(
        _bwd_resid_norm, "resid_norm2_bwd", [xs, mixed, dx2, dh2], [gate1, g_norm2, scale2, shift2],
        [(D, F32), (D, BF16)], [(1, D)] * 4, ts)
    (dw_out,) = _mm_tn(cat, [dmixed], "out_wgrad", 1024, 1024, tm)
    dcat = _mm_nt(dmixed, wout, "out_dgrad", F32, tm, 512)
    dw_up_s = jnp.concatenate([dw_up_tv, dw_up_tg], axis=0).reshape(N_DEV, 704, D)
    dwconv_s = _cols_by_dev(jnp.concatenate([dwc_v[:3], dwc_g[:3]], axis=1))
    (dq, dk, dv), (r_up, r_out, r_conv) = _sb_bwd(
        proj, dcat, sb_tot, sb_first, comm=([dw_up_s, dw_out.reshape(N_DEV, 128, D), dwconv_s], False))
    dgq, dgk, dgv, dgg, dgf, dwfg_p, dbfg, dg_gla = _gla_bwd(proj, dcat, states, wfg_p, b_fg2, g_gla_out)
    dproj = [dq, dk, dv, dgq, dgk, dgv, dgg, dgf]
    dw_in_t = jnp.concatenate(_mm_tn_rows(dproj, h, "in_wgrad", tm), axis=0)
    dh, (r_in, r_fg) = _mm(dproj, win_t, "in_dgrad", F32, tm, 512,
                           comm=([dw_in_t[:IN_W].reshape(N_DEV, 386, D).astype(BF16), _cols_by_dev(dwfg_p[:RANK])],
                                 False))
    grad_x, dg_norm1, dscale1, dshift1 = _rows_call(
        _bwd_norm1, "norm1_bwd", [xs, dh, dx1], [g_norm1, scale1, shift1], [(D, F32)], [(1, D)] * 3, ts)

    dmod = jnp.concatenate([dshift1, dscale1, dgate1, dshift2, dscale2, dgate2], axis=1)
    small_parts = [dmod, dg_norm1, dbfg, dg_gla, dg_norm2, dbc_v, dbc_g, dg_final, loss_part]
    n_small = sum(p.shape[1] for p in small_parts[:-1])
    (sg,) = _exchange([jnp.concatenate(small_parts, axis=1)], "gather_small_grads", True)
    sg = sg.reshape(N_DEV, n_small + 128)
    loss = jnp.sum(sg[:, n_small])
    small_w = [b_ada, g_norm1, b_fg2, g_gla_out, g_norm2, b_conv, g_final.reshape(1, D)]
    small_m = [m_b_ada, m_g_norm1, m_b_fg2, m_g_gla_out, m_g_norm2, m_b_conv, m_g_final.reshape(1, D)]
    small_v = [v_b_ada, v_g_norm1, v_b_fg2, v_g_gla_out, v_g_norm2, v_b_conv, v_g_final.reshape(1, D)]
    s_out = list(_adam_rows(sg, small_w, small_m, small_v, "adam_small"))
    s_out[24:] = [t.reshape(D) for t in s_out[24:]]
    s_g, s_d, s_m, s_v = [s_out[k::4] for k in range(4)]

    dmod_all = sg[:, :6 * D].reshape(N_DEV, N_DEV, 768)
    dmod_mine = lax.dynamic_index_in_dim(dmod_all, me, axis=1, keepdims=False)
    dw_ada = _ada_bwd(c16, jnp.pad(dmod_mine, ((0, 8), (0, 0))))
    a_g, a_d, a_m, a_v = [t[None] for t in _adam(dw_ada[None], w_ada[0], m_w_ada[0], v_w_ada[0], "adam_ada", 256)]

    recv = [r_in, r_fg, r_out, r_up, r_conv, r_down]
    big_w = [w_in_t, w_fg2[0], w_out[0], w_up_t, w_conv[0], w_down[0]]
    big_m = [m_in_t, m_w_fg2[0], m_w_out[0], m_up_t, m_w_conv[0], m_w_down[0]]
    big_v = [v_in_t, v_w_fg2[0], v_w_out[0], v_up_t, v_w_conv[0], v_w_down[0]]
    big_name = ["adam_in", "adam_fg2", "adam_out", "adam_up", "adam_conv", "adam_down"]
    big_rows = [386, RANK, 128, 176, 3, 176]
    b_out = [_adam(r, w, m, v, name, tr)
             for r, w, m, v, name, tr in zip(recv, big_w, big_m, big_v, big_name, big_rows)]
    for k in (0, 3):
        b_out[k] = [jnp.swapaxes(t, 0, 1) for t in b_out[k]]
    b_g, b_d, b_m, b_v = [[o[k][None] for o in b_out] for k in range(4)]

    def ordered(a, sm, bg):
        return [a, sm[0], sm[1], bg[0], bg[1], sm[2], sm[3], bg[2], sm[4], bg[3], bg[4], sm[5], bg[5], sm[6]]

    return (loss, grad_x[None], *ordered(a_g, s_g, b_g), *ordered(a_d, s_d, b_d),
            *ordered(a_m, s_m, b_m), *ordered(a_v, s_v, b_v))
```

```python
import numpy as np

import jax
import jax.numpy as jnp
from jax import lax
from jax.experimental import pallas as pl
from jax.experimental.pallas import tpu as pltpu

F32, BF16 = jnp.float32, jnp.bfloat16
N_DEV = 8
D = 1024
SB_W = 512
GLA_KW, GLA_VW = 256, 512
RANK = 16
IN_W = 3088
IN_WP = 3200
D_FF = 2816
FF_T = 256
N_FT = D_FF // FF_T
EPS = 1e-6
SB_B = 256
SB_DEAD = -110.0
CHUNK = 64
GLA_T = 512
VMEM_LIMIT = 56 * 1024 * 1024

LR, B1, B2, ADAM_EPS, WD, STEP = 0.001, 0.9, 0.999, 1e-08, 0.01, 10


def _params(dims=None, vmem=True):
    kw = {}
    if dims is not None:
        kw["dimension_semantics"] = dims
    if vmem:
        kw["vmem_limit_bytes"] = VMEM_LIMIT
    return pltpu.CompilerParams(**kw)


def _dot(a, b):
    return jnp.dot(a, b, preferred_element_type=F32)


def _dot_nt(a, b):
    return lax.dot_general(a, b, (((1,), (1,)), ((), ())), preferred_element_type=F32)


def _dot_tn(a, b):
    return lax.dot_general(a, b, (((0,), (0,)), ((), ())), preferred_element_type=F32)


def _hilo(x):
    hi = x.astype(BF16)
    lo = (x - hi.astype(F32)).astype(BF16)
    return hi, lo


def _sigmoid(x):
    return 1.0 / (1.0 + jnp.exp(-x))


def _log_sigmoid(x):
    return jnp.minimum(x, 0.0) - jnp.log(1.0 + jnp.exp(-jnp.abs(x)))


def _rms(x, g):
    n = x * lax.rsqrt(jnp.mean(x * x, axis=-1, keepdims=True) + EPS)
    return n * g


def _norm_mod(x, g, scale, shift):
    return _rms(x, g) * (1.0 + scale) + shift


N_PEER = N_DEV - 1


def _exchange_copies(x_refs, out_refs, send_sems, recv_sems, local_sems, gather):
    ix, iy, ic = lax.axis_index("x"), lax.axis_index("y"), lax.axis_index("c")
    me = 4 * ix + 2 * iy + ic
    peers = []
    for k in range(1, N_DEV):
        px = 1 - ix if k & 4 else ix
        py = 1 - iy if k & 2 else iy
        pc = 1 - ic if k & 1 else ic
        peers.append(((px, py, pc), 4 * px + 2 * py + pc))

    def copy(a, k, dev, src_slot, dst_slot):
        return pltpu.make_async_remote_copy(
            src_ref=x_refs[a] if gather else x_refs[a].at[src_slot],
            dst_ref=out_refs[a].at[dst_slot],
            send_sem=send_sems.at[a * N_PEER + k],
            recv_sem=recv_sems.at[a * N_PEER + k],
            device_id=dev,
            device_id_type=pl.DeviceIdType.MESH,
        )

    n = len(x_refs)
    mine = [pltpu.make_async_copy(x_refs[a] if gather else x_refs[a].at[me], out_refs[a].at[me], local_sems.at[a])
            for a in range(n)]
    sends = [copy(a, k, dev, pid, me) for a in range(n) for k, (dev, pid) in enumerate(peers)]
    recvs = [copy(a, k, dev, pid, pid) for a in range(n) for k, (dev, pid) in enumerate(peers)]
    return mine, sends, recvs


def _exchange_scratch(n):
    return [pltpu.SemaphoreType.DMA((n * N_PEER,)), pltpu.SemaphoreType.DMA((n * N_PEER,)),
            pltpu.SemaphoreType.DMA((n,))]


def _exchange_shapes(arrays, gather):
    return [jax.ShapeDtypeStruct((N_DEV,) + tuple(x.shape if gather else x.shape[1:]), x.dtype) for x in arrays]


def _exchange(arrays, name, gather):
    n = len(arrays)

    def body(*refs):
        mine, sends, recvs = _exchange_copies(refs[:n], refs[n:2 * n], *refs[2 * n:], gather)
        for cp in mine + sends:
            cp.start()
        for cp in recvs:
            cp.wait_recv()
        for cp in sends:
            cp.wait_send()
        for cp in mine:
            cp.wait()

    return pl.pallas_call(
        body,
        name=name,
        out_shape=_exchange_shapes(arrays, gather),
        in_specs=[pl.BlockSpec(memory_space=pl.ANY)] * n,
        out_specs=[pl.BlockSpec(memory_space=pl.ANY)] * n,
        scratch_shapes=_exchange_scratch(n),
    )(*arrays)


def _hosted_call(body, comm, *, name, grid, in_specs, out_specs, out_shape, scratch_shapes, dims, args):
    if comm is None:
        outs = pl.pallas_call(body, name=name, grid=grid, in_specs=in_specs, out_specs=out_specs, out_shape=out_shape,
                              scratch_shapes=scratch_shapes, compiler_params=_params(dims))(*args)
        return outs, []
    arrays, gather = comm
    n_in, n_out, n_scr, nc = len(in_specs), len(out_specs), len(scratch_shapes), len(arrays)

    def hosted(*refs):
        ins, c_in = refs[:n_in], refs[n_in:n_in + nc]
        outs, c_out = refs[n_in + nc:n_in + nc + n_out], refs[n_in + nc + n_out:n_in + 2 * nc + n_out]
        rest = refs[n_in + 2 * nc + n_out:]
        scratch, sems = rest[:n_scr], rest[n_scr:]
        mine, sends, recvs = _exchange_copies(c_in, c_out, *sems, gather)
        first = pl.program_id(0) == 0
        last = pl.program_id(0) == grid[0] - 1
        for axis in range(1, len(grid)):
            first = jnp.logical_and(first, pl.program_id(axis) == 0)
            last = jnp.logical_and(last, pl.program_id(axis) == grid[axis] - 1)

        @pl.when(first)
        def _():
            for cp in mine + sends:
                cp.start()

        body(*ins, *outs, *scratch)

        @pl.when(last)
        def _():
            for cp in recvs:
                cp.wait_recv()
            for cp in sends:
                cp.wait_send()
            for cp in mine:
                cp.wait()

    any_spec = pl.BlockSpec(memory_space=pl.ANY)
    outs = pl.pallas_call(
        hosted, name=name, grid=grid,
        in_specs=list(in_specs) + [any_spec] * nc,
        out_specs=list(out_specs) + [any_spec] * nc,
        out_shape=list(out_shape) + _exchange_shapes(arrays, gather),
        scratch_shapes=list(scratch_shapes) + _exchange_scratch(nc),
        compiler_params=_params(tuple("arbitrary" for _ in grid)),
    )(*args, *arrays)
    return outs[:n_out], outs[n_out:]


def _adam_math(g, w, m, v):
    m_new = B1 * m + (1.0 - B1) * g
    v_new = B2 * v + (1.0 - B2) * (g * g)
    m_hat = m_new / (1.0 - B1 ** STEP)
    v_hat = v_new / (1.0 - B2 ** STEP)
    return -LR * (m_hat / (jnp.sqrt(v_hat) + ADAM_EPS) + WD * w), m_new, v_new


def _adam(gparts, w, m, v, name, tr):
    n, rows, cols = gparts.shape

    def body(gp_ref, w_ref, m_ref, v_ref, g_ref, d_ref, nm_ref, nv_ref):
        g = gp_ref[0].astype(F32)
        for j in range(1, n):
            g = g + gp_ref[j].astype(F32)
        g_ref[...] = g
        d_ref[...], nm_ref[...], nv_ref[...] = _adam_math(g, w_ref[...], m_ref[...], v_ref[...])

    blk = pl.BlockSpec((tr, cols), lambda i: (i, 0))
    return pl.pallas_call(
        body,
        name=name,
        grid=(rows // tr,),
        out_shape=[jax.ShapeDtypeStruct((rows, cols), F32)] * 4,
        in_specs=[pl.BlockSpec((n, tr, cols), lambda i: (0, i, 0)), blk, blk, blk],
        out_specs=[blk] * 4,
        compiler_params=_params(("parallel",)),
    )(gparts, w, m, v)


def _adam_rows(parts, ws, ms, vs, name):
    n = parts.shape[0]
    k = len(ws)
    widths = [w.shape[1] for w in ws]

    def body(*refs):
        p_ref, w_refs, m_refs, v_refs = refs[0], refs[1:1 + k], refs[1 + k:1 + 2 * k], refs[1 + 2 * k:1 + 3 * k]
        outs = refs[1 + 3 * k:]
        total = p_ref[0:1, :]
        for j in range(1, n):
            total = total + p_ref[j:j + 1, :]
        off = 0
        for a, width in enumerate(widths):
            g = total[:, off:off + width]
            off += width
            outs[4 * a][...] = g
            outs[4 * a + 1][...], outs[4 * a + 2][...], outs[4 * a + 3][...] = _adam_math(
                g, w_refs[a][...], m_refs[a][...], v_refs[a][...])

    return pl.pallas_call(
        body, name=name,
        out_shape=[jax.ShapeDtypeStruct((1, width), F32) for width in widths for _ in range(4)],
        compiler_params=_params(),
    )(parts, *ws, *ms, *vs)


def _mm(a_list, b, name, out_dtype, tm, tn, comm=None):
    n_a = len(a_list)
    m = a_list[0].shape[0]
    n = b.shape[1]
    widths = [a.shape[1] for a in a_list]
    starts = [sum(widths[:g]) for g in range(n_a)]
    assert all(s0 % w == 0 for s0, w in zip(starts, widths))

    def body(*refs):
        o_ref = refs[2 * n_a]
        acc = _dot(refs[0][...].astype(BF16), refs[n_a][...].astype(BF16))
        for g in range(1, n_a):
            acc = acc + _dot(refs[g][...].astype(BF16), refs[n_a + g][...].astype(BF16))
        o_ref[...] = acc.astype(out_dtype)

    (out,), got = _hosted_call(
        body, comm, name=name, grid=(n // tn, m // tm),
        out_shape=[jax.ShapeDtypeStruct((m, n), out_dtype)],
        in_specs=[pl.BlockSpec((tm, w), lambda j, i: (i, 0)) for w in widths]
        + [pl.BlockSpec((w, tn), lambda j, i, blk=s0 // w: (blk, j)) for s0, w in zip(starts, widths)],
        out_specs=[pl.BlockSpec((tm, tn), lambda j, i: (i, j))],
        scratch_shapes=[], dims=("parallel", "parallel"), args=(*a_list, *([b] * n_a)))
    return out, got


def _mm_tn_rows(a_list, b, name, tk):
    s, n = b.shape
    n_a = len(a_list)

    def body(*refs):
        a_refs, b_ref, o_refs = refs[:n_a], refs[n_a], refs[n_a + 1:]

        @pl.when(pl.program_id(0) == 0)
        def _():
            for o_ref in o_refs:
                o_ref[...] = jnp.zeros_like(o_ref)

        b_blk = b_ref[...].astype(BF16)
        for a_ref, o_ref in zip(a_refs, o_refs):
            o_ref[...] += _dot_tn(a_ref[...].astype(BF16), b_blk)

    return pl.pallas_call(
        body,
        name=name,
        grid=(s // tk,),
        out_shape=[jax.ShapeDtypeStruct((a.shape[1], n), F32) for a in a_list],
        in_specs=[pl.BlockSpec((tk, a.shape[1]), lambda k: (k, 0)) for a in a_list] + [pl.BlockSpec((tk, n), lambda k: (k, 0))],
        out_specs=[pl.BlockSpec((a.shape[1], n), lambda k: (0, 0)) for a in a_list],
        compiler_params=_params(("arbitrary",)),
    )(*a_list, b)


def _mm_nt(a, b, name, out_dtype, tm, tn, n_out=1):
    m, k = a.shape
    n = b.shape[0] // n_out
    nt = n // tn

    def body(a_ref, *refs):
        a_blk = a_ref[...].astype(BF16)
        for b_ref, o_ref in zip(refs[:n_out], refs[n_out:]):
            o_ref[...] = _dot_nt(a_blk, b_ref[...].astype(BF16)).astype(out_dtype)

    outs = pl.pallas_call(
        body,
        name=name,
        grid=(nt, m // tm),
        out_shape=[jax.ShapeDtypeStruct((m, n), out_dtype)] * n_out,
        in_specs=[pl.BlockSpec((tm, k), lambda j, i: (i, 0))]
        + [pl.BlockSpec((tn, k), lambda j, i, o=o: (o * nt + j, 0)) for o in range(n_out)],
        out_specs=[pl.BlockSpec((tm, tn), lambda j, i: (i, j))] * n_out,
        compiler_params=_params(("parallel", "parallel")),
    )(a, *([b] * n_out))
    return outs[0] if n_out == 1 else outs


def _mm_tn(a, b_list, name, tm, tn, tk):
    s, m = a.shape
    n = b_list[0].shape[1]
    n_b = len(b_list)

    def body(a_ref, *refs):
        b_refs, o_refs = refs[:n_b], refs[n_b:]

        @pl.when(pl.program_id(2) == 0)
        def _():
            for o_ref in o_refs:
                o_ref[...] = jnp.zeros_like(o_ref)

        a_blk = a_ref[...].astype(BF16)
        for b_ref, o_ref in zip(b_refs, o_refs):
            o_ref[...] += _dot_tn(a_blk, b_ref[...].astype(BF16))

    return pl.pallas_call(
        body,
        name=name,
        grid=(n // tn, m // tm, s // tk),
        out_shape=[jax.ShapeDtypeStruct((m, n), F32)] * n_b,
        in_specs=[pl.BlockSpec((tk, tm), lambda j, i, k: (k, i))] + [pl.BlockSpec((tk, tn), lambda j, i, k: (k, j))] * n_b,
        out_specs=[pl.BlockSpec((tm, tn), lambda j, i, k: (i, j))] * n_b,
        compiler_params=_params(("parallel", "parallel", "arbitrary")),
    )(a, *b_list)


def _rows_call(fn, name, rows, params, out_rows, out_accs, ts, comm=None):
    s = rows[0].shape[0]
    nr, npar, no = len(rows), len(params), len(out_rows)

    def body(*refs):
        r, p = refs[:nr], refs[nr:nr + npar]
        o, acc = refs[nr + npar:nr + npar + no], refs[nr + npar + no:]
        outs, sums = fn(*[t[...] for t in r], *[t[...] for t in p])
        for ref, val in zip(o, outs):
            ref[...] = val.astype(ref.dtype)
        if acc:
            @pl.when(pl.program_id(0) == 0)
            def _():
                for ref in acc:
                    ref[...] = jnp.zeros_like(ref)

            for ref, val in zip(acc, sums):
                ref[...] += val

    outs, got = _hosted_call(
        body, comm, name=name, grid=(s // ts,),
        out_shape=[jax.ShapeDtypeStruct((s, w), dt) for w, dt in out_rows]
        + [jax.ShapeDtypeStruct(shape, F32) for shape in out_accs],
        in_specs=[pl.BlockSpec((ts, t.shape[1]), lambda i: (i, 0)) for t in rows]
        + [pl.BlockSpec(t.shape, lambda i: (0, 0)) for t in params],
        out_specs=[pl.BlockSpec((ts, w), lambda i: (i, 0)) for w, _ in out_rows]
        + [pl.BlockSpec(shape, lambda i: (0, 0)) for shape in out_accs],
        scratch_shapes=[], dims=("arbitrary",), args=(*rows, *params))
    return outs if comm is None else (outs, got)


def _fwd_norm1(x, g, scale, shift):
    return (_norm_mod(x, g, scale, shift),), ()


def _resid_norm(x, mixed, gate, g, scale, shift):
    x1 = x + (1.0 + gate) * mixed
    return x1, _norm_mod(x1, g, scale, shift)


def _fwd_resid_norm(x, mixed, gate, g, scale, shift):
    return _resid_norm(x, mixed, gate, g, scale, shift), ()


def _final(x1, ffn, tgt, gate, g):
    def head(x1, ffn, gate, g):
        return _rms(x1 + (1.0 + gate) * ffn, g)

    y, vjp = jax.vjp(head, x1, ffn, gate, g)
    err = y - tgt
    dx2, dffn, dgate, dg = vjp(err * (1.0 / D))
    sq = jnp.sum(jnp.sum(err * err, axis=1, keepdims=True), axis=0, keepdims=True)
    loss = jnp.broadcast_to(sq * (0.5 / D), (1, 128))
    return (dx2, dffn), (dgate, dg, loss)


def _bwd_resid_norm(x, mixed, dx2, dh2, gate, g, scale, shift):
    _, vjp = jax.vjp(_resid_norm, x, mixed, gate, g, scale, shift)
    dx, dmixed, dgate, dg, dscale, dshift = vjp((dx2, dh2))
    return (dx, dmixed), (dgate, dg, dscale, dshift)


def _bwd_norm1(x, dh, dx1, g, scale, shift):
    _, vjp = jax.vjp(_norm_mod, x, g, scale, shift)
    dx, dg, dscale, dshift = vjp(dh)
    return (dx1 + dx,), (dg, dscale, dshift)


def _ada_fwd(c16, w):
    def body(c_ref, w_ref, o_ref):
        c = c_ref[...]
        o_ref[...] = _dot((c * _sigmoid(c)).astype(BF16), w_ref[...].astype(BF16))

    return pl.pallas_call(
        body, name="ada_fwd", out_shape=jax.ShapeDtypeStruct((c16.shape[0], w.shape[1]), F32),
        compiler_params=_params(),
    )(c16, w)


def _ada_bwd(c16, dmod16):
    def body(c_ref, d_ref, o_ref):
        c = c_ref[...]
        o_ref[...] = _dot_tn((c * _sigmoid(c)).astype(BF16), d_ref[...].astype(BF16))

    return pl.pallas_call(
        body, name="ada_bwd", out_shape=jax.ShapeDtypeStruct((c16.shape[1], dmod16.shape[1]), F32),
        compiler_params=_params(),
    )(c16, dmod16)


CONV_R = 256


def _conv_window(win, wc, bc):
    s1 = pltpu.roll(win, 1, 0)
    s2 = pltpu.roll(win, 2, 0)
    u = bc + wc[0:1] * s2 + wc[1:2] * s1 + wc[2:3] * win
    return u[8:], s1[8:], s2[8:], win[8:]


def _row_windows(ref, before, after, n_after, lanes):
    ts, r = ref.shape[0], CONV_R

    def middle(i):
        return ref[pl.ds(pl.multiple_of(i * r, r) - 8, r + 8 + n_after), lanes]

    first = jnp.concatenate([before, ref[0:r + n_after, lanes]], axis=0)
    last = ref[ts - r - 8:ts, lanes] if n_after == 0 else jnp.concatenate([ref[ts - r - 8:ts, lanes], after], axis=0)
    return first, middle, last


def _ffn_specs(ts, s):
    cur = pl.BlockSpec((ts, FF_T), lambda j, i: (i, j))
    halo = pl.BlockSpec((8, FF_T), lambda j, i: (jnp.maximum(i * (ts // 8) - 1, 0), j))
    nxt = pl.BlockSpec((8, FF_T), lambda j, i: (jnp.minimum((i + 1) * (ts // 8), s // 8 - 1), j))
    wc = [pl.BlockSpec((8, FF_T), lambda j, i, h=h: (0, h * N_FT + j)) for h in range(2)]
    bc = [pl.BlockSpec((1, FF_T), lambda j, i, h=h: (0, h * N_FT + j)) for h in range(2)]
    return cur, halo, nxt, wc, bc


def _conv_glu_fwd(up_v, up_g, wc, bc, ts):
    s = up_v.shape[0]
    cur, halo, _, wcs, bcs = _ffn_specs(ts, s)

    def body(v_ref, vh_ref, g_ref, gh_ref, wcv_ref, wcg_ref, bcv_ref, bcg_ref, a_ref):
        keep = jnp.where(pl.program_id(1) == 0, 0.0, 1.0)
        for lanes in (slice(0, 128), slice(128, 256)):
            wcv, wcg, bcv, bcg = wcv_ref[:, lanes], wcg_ref[:, lanes], bcv_ref[:, lanes], bcg_ref[:, lanes]
            first_v, mid_v, _ = _row_windows(v_ref, vh_ref[:, lanes] * keep, None, 0, lanes)
            first_g, mid_g, _ = _row_windows(g_ref, gh_ref[:, lanes] * keep, None, 0, lanes)

            def emit(win_v, win_g, start, lanes=lanes, wcv=wcv, wcg=wcg, bcv=bcv, bcg=bcg):
                val = _conv_window(win_v, wcv, bcv)[0]
                gte = _conv_window(win_g, wcg, bcg)[0]
                a_ref[pl.ds(start, CONV_R), lanes] = (val * (gte * _sigmoid(gte))).astype(BF16)

            emit(first_v, first_g, 0)

            def loop(i, carry, emit=emit, mid_v=mid_v, mid_g=mid_g):
                emit(mid_v(i), mid_g(i), pl.multiple_of(i * CONV_R, CONV_R))
                return carry

            lax.fori_loop(1, ts // CONV_R, loop, 0)

    return pl.pallas_call(
        body, name="conv_glu_fwd", grid=(N_FT, s // ts),
        out_shape=jax.ShapeDtypeStruct((s, D_FF), BF16),
        in_specs=[cur, halo, cur, halo, *wcs, *bcs], out_specs=cur,
        compiler_params=_params(("parallel", "arbitrary")),
    )(up_v, up_v, up_g, up_g, wc, wc, bc, bc)


def _glu_bwd(val, gte, da):
    sg = _sigmoid(gte)
    return da * (gte * sg), da * val * (sg * (1.0 + gte * (1.0 - sg)))


def _conv_glu_bwd(up_v, up_g, da, wc, bc, ts, comm=None):
    s = up_v.shape[0]
    nblk = s // ts
    cur, halo, nxt, wcs, bcs = _ffn_specs(ts, s)
    acc_w = pl.BlockSpec((8, FF_T), lambda j, i: (0, j))
    acc_b = pl.BlockSpec((1, FF_T), lambda j, i: (0, j))

    def body(v_ref, vh_ref, vn_ref, g_ref, gh_ref, gn_ref, da_ref, dan_ref, wcv_ref, wcg_ref, bcv_ref, bcg_ref,
             dupv_ref, dupg_ref, dwcv_ref, dwcg_ref, dbcv_ref, dbcg_ref):
        first = pl.program_id(1) == 0
        keep = jnp.where(first, 0.0, 1.0)
        keep_next = jnp.where(pl.program_id(1) == nblk - 1, 0.0, 1.0)
        r = CONV_R
        n = ts // r

        @pl.when(first)
        def _():
            for ref in (dwcv_ref, dwcg_ref, dbcv_ref, dbcg_ref):
                ref[...] = jnp.zeros_like(ref)

        for lanes in (slice(0, 128), slice(128, 256)):
            wcv, wcg, bcv, bcg = wcv_ref[:, lanes], wcg_ref[:, lanes], bcv_ref[:, lanes], bcg_ref[:, lanes]
            first_v, mid_v, last_v = _row_windows(v_ref, vh_ref[:, lanes] * keep, vn_ref[:, lanes], 8, lanes)
            first_g, mid_g, last_g = _row_windows(g_ref, gh_ref[:, lanes] * keep, gn_ref[:, lanes], 8, lanes)

            def emit(win_v, win_g, da_w, start, lanes=lanes, wcv=wcv, wcg=wcg, bcv=bcv, bcg=bcg):
                u_v, s1_v, s2_v, x_v = _conv_window(win_v, wcv, bcv)
                u_g, s1_g, s2_g, x_g = _conv_window(win_g, wcg, bcg)
                du_v, du_g = _glu_bwd(u_v, u_g, da_w)
                rows = pl.ds(start, r)
                for du, s1, s2, x, wc, dup_ref, dwc_ref, dbc_ref in (
                        (du_v, s1_v, s2_v, x_v, wcv, dupv_ref, dwcv_ref, dbcv_ref),
                        (du_g, s1_g, s2_g, x_g, wcg, dupg_ref, dwcg_ref, dbcg_ref)):
                    dup = wc[2:3] * du + wc[1:2] * pltpu.roll(du, r + 7, 0) + wc[0:1] * pltpu.roll(du, r + 6, 0)
                    dup_ref[rows, lanes] = dup[:r].astype(BF16)
                    du = du[:r]
                    dwc_ref[0:1, lanes] += jnp.sum(du * s2[:r], axis=0, keepdims=True)
                    dwc_ref[1:2, lanes] += jnp.sum(du * s1[:r], axis=0, keepdims=True)
                    dwc_ref[2:3, lanes] += jnp.sum(du * x[:r], axis=0, keepdims=True)
                    dbc_ref[:, lanes] += jnp.sum(du, axis=0, keepdims=True)

            emit(first_v, first_g, da_ref[0:r + 8, lanes], 0)

            def loop(i, carry, emit=emit, mid_v=mid_v, mid_g=mid_g, lanes=lanes):
                start = pl.multiple_of(i * r, r)
                emit(mid_v(i), mid_g(i), da_ref[pl.ds(start, r + 8), lanes], start)
                return carry

            lax.fori_loop(1, n - 1, loop, 0)
            da_last = jnp.concatenate([da_ref[ts - r:ts, lanes], dan_ref[:, lanes] * keep_next], axis=0)
            emit(last_v, last_g, da_last, ts - r)

    return _hosted_call(
        body, comm, name="conv_glu_bwd", grid=(N_FT, nblk),
        out_shape=[jax.ShapeDtypeStruct((s, D_FF), BF16)] * 2 + [jax.ShapeDtypeStruct((8, D_FF), F32)] * 2
        + [jax.ShapeDtypeStruct((1, D_FF), F32)] * 2,
        in_specs=[cur, halo, nxt, cur, halo, nxt, cur, nxt, *wcs, *bcs],
        out_specs=[cur, cur, acc_w, acc_w, acc_b, acc_b],
        scratch_shapes=[], dims=("parallel", "arbitrary"),
        args=(up_v, up_v, up_v, up_g, up_g, up_g, da, da, wc, wc, bc, bc))


def _tri(kind):
    b = SB_B
    m = {"lower_strict": np.tril(np.ones((b, b)), -1), "upper_incl": np.triu(np.ones((b, b)), 0),
         "upper_strict": np.triu(np.ones((b, b)), 1)}[kind]
    return jnp.asarray(np.concatenate([m, np.ones((b, 128))], axis=1), BF16)


def _key_sums(x, tri):
    cb = _dot(x.astype(BF16), tri)
    return cb[:, :SB_B], cb[:, SB_B:]


def _sb_fwd(proj, comm=None):
    s = proj.shape[0]
    b = SB_B

    def body(q_ref, k_ref, v_ref, tri_ref, o_ref, t_ref, first_ref, c_ref, a_ref):
        i = pl.program_id(1)
        lane = lax.broadcasted_iota(jnp.int32, (b, 128), 1)
        heads = (lane < 64, lane >= 64)
        causal = lax.broadcasted_iota(jnp.int32, (b, b), 1) < lax.broadcasted_iota(jnp.int32, (b, b), 0)
        q = q_ref[...] * 0.125
        qm = [jnp.where(h, q, 0.0).astype(BF16) for h in heads]
        c_ref[...] = jnp.zeros_like(c_ref)
        a_ref[...] = jnp.zeros_like(a_ref)

        def prepare(jj, masked):
            rows = pl.ds(pl.multiple_of(jj * b, b), b)
            kb = k_ref[rows, :].astype(BF16)
            vb = v_ref[rows, :]
            out = []
            for hh in range(2):
                z = _dot_nt(qm[hh], kb)
                lg = _log_sigmoid(-z)
                if masked:
                    lg = jnp.where(causal, lg, 0.0)
                after, total = _key_sums(lg, tri_ref[...])
                out.append((lg + z + after, total, jnp.where(heads[hh], vb, 0.0).astype(BF16)))
            return out

        def walk_blocks(blocks):
            for hh in range(2):
                c, a = c_ref[hh], a_ref[hh]
                for pre, masked in blocks:
                    logw, total, vm = pre[hh]
                    w = jnp.exp(logw + jnp.concatenate([c, c], axis=1))
                    if masked:
                        w = jnp.where(causal, w, 0.0)
                    a = a + _dot(w.astype(BF16), vm)
                    c = c + total
                c_ref[hh], a_ref[hh] = c, a

        def largest_sum():
            return jnp.max(jnp.maximum(c_ref[0], c_ref[1]))

        @pl.when(i == 0)
        def _():
            walk_blocks([(prepare(i, True), True)])

        @pl.when(i > 0)
        def _():
            walk_blocks([(prepare(i, True), True), (prepare(i - 1, False), False)])

        def more(state):
            jj, top = state
            return jnp.logical_and(jj >= 0, top > SB_DEAD)

        def walk(state):
            jj, _ = state
            walk_blocks([(prepare(jj, False), False)])
            return jj - 1, largest_sum()

        jj, _ = lax.while_loop(more, walk, (jnp.maximum(i - 2, -1), largest_sum()))
        o_ref[...] = (a_ref[0] + a_ref[1]).astype(BF16)
        t_ref[...] = jnp.concatenate([c_ref[0], c_ref[1]], axis=1)
        first_ref[pl.program_id(0), i] = (jj + 1).astype(F32)

    return _hosted_call(
        body, comm, name="sb_fwd", grid=(4, s // b),
        out_shape=[jax.ShapeDtypeStruct((s, SB_W), BF16), jax.ShapeDtypeStruct((s, 2 * SB_W), F32),
                   jax.ShapeDtypeStruct((4, s // b), F32)],
        in_specs=[pl.BlockSpec((b, 128), lambda p, i: (i, p)),
                  pl.BlockSpec((s, 128), lambda p, i: (0, 4 + p)),
                  pl.BlockSpec((s, 128), lambda p, i: (0, 8 + p)),
                  pl.BlockSpec((b, b + 128), lambda p, i: (0, 0))],
        out_specs=[pl.BlockSpec((b, 128), lambda p, i: (i, p)), pl.BlockSpec((b, 256), lambda p, i: (i, p)),
                   pl.BlockSpec(memory_space=pltpu.SMEM)],
        scratch_shapes=[pltpu.VMEM((2, b, 128), F32), pltpu.VMEM((2, b, 128), F32)],
        dims=("arbitrary", "arbitrary"), args=(proj, proj, proj, _tri("lower_strict")))


def _sb_bwd(proj, do, tot, first, comm=None):
    s = proj.shape[0]
    b = SB_B

    def body(q_ref, k_ref, v_ref, do_ref, t_ref, first_ref, ti_ref, ts_ref, dq_ref, dko_ref, dvo_ref, cl_ref, ce_ref,
             a_ref, dk_ref, dv_ref):
        i = pl.program_id(1)
        first = jnp.clip(first_ref[pl.program_id(0), i].astype(jnp.int32), 0, i)
        lane = lax.broadcasted_iota(jnp.int32, (b, 128), 1)
        heads = (lane < 64, lane >= 64)
        causal = lax.broadcasted_iota(jnp.int32, (b, b), 1) < lax.broadcasted_iota(jnp.int32, (b, b), 0)
        q = q_ref[...] * 0.125
        do = do_ref[...]
        qm = [jnp.where(h, q, 0.0).astype(BF16) for h in heads]
        dom = [jnp.where(h, do, 0.0).astype(BF16) for h in heads]
        cl_ref[...] = jnp.zeros_like(cl_ref)
        ce_ref[...] = jnp.zeros_like(ce_ref)
        a_ref[...] = jnp.zeros_like(a_ref)

        @pl.when(i == 0)
        def _():
            dk_ref[...] = jnp.zeros_like(dk_ref)
            dv_ref[...] = jnp.zeros_like(dv_ref)

        def prepare(jj, masked):
            rows = pl.ds(pl.multiple_of(jj * b, b), b)
            kf = k_ref[rows, :]
            kb = kf.astype(BF16)
            vb = v_ref[rows, :].astype(BF16)
            out = []
            for hh in range(2):
                z = _dot_nt(qm[hh], kb)
                lg = _log_sigmoid(-z)
                if masked:
                    lg = jnp.where(causal, lg, 0.0)
                upto, total = _key_sums(lg, ti_ref[...])
                lsz = lg + z
                out.append((lsz - upto, total, jnp.exp(lsz), _dot_nt(dom[hh], vb),
                            jnp.where(heads[hh], kf, 0.0).astype(BF16)))
            return rows, out

        def walk_blocks(blocks):
            grads = [[jnp.zeros((b, 128), F32), jnp.zeros((b, 128), F32)] for _ in blocks]
            cl = [cl_ref[0], cl_ref[1]]
            ce = [ce_ref[0], ce_ref[1]]
            a = [a_ref[0], a_ref[1]]
            for n, ((_, pre), masked) in enumerate(blocks):
                for hh in range(2):
                    t = t_ref[:, hh * 128:(hh + 1) * 128]
                    logw, total, sig, dw, km = pre[hh]
                    w = jnp.exp(logw + jnp.concatenate([t - cl[hh], t - cl[hh]], axis=1))
                    if masked:
                        w = jnp.where(causal, w, 0.0)
                    e = w * dw
                    sums = _dot(e.astype(BF16), ts_ref[...])
                    before, etot = sums[:, :SB_B], sums[:, SB_B:]
                    dz = e - sig * (e + before + jnp.concatenate([ce[hh], ce[hh]], axis=1))
                    if masked:
                        dz = jnp.where(causal, dz, 0.0)
                    dzb = dz.astype(BF16)
                    a[hh] = a[hh] + _dot(dzb, km)
                    grads[n][0] = grads[n][0] + _dot_tn(dzb, qm[hh])
                    grads[n][1] = grads[n][1] + _dot_tn(w.astype(BF16), dom[hh])
                    cl[hh] = cl[hh] + total
                    ce[hh] = ce[hh] + etot
            for hh in range(2):
                cl_ref[hh], ce_ref[hh], a_ref[hh] = cl[hh], ce[hh], a[hh]
            for ((rows, _), _), (dk, dv) in zip(blocks, grads):
                dk_ref[rows, :] += dk
                dv_ref[rows, :] += dv

        def loop(jj, carry):
            walk_blocks([(prepare(jj, False), False)])
            return carry

        lax.fori_loop(first, i - 1, loop, 0)

        @pl.when(i == 0)
        def _():
            walk_blocks([(prepare(i, True), True)])

        @pl.when(i > 0)
        def _():
            walk_blocks([(prepare(i - 1, False), False), (prepare(i, True), True)])

        dq_ref[...] = ((a_ref[0] + a_ref[1]) * 0.125).astype(BF16)

        @pl.when(i == s // b - 1)
        def _():
            dko_ref[...] = dk_ref[...].astype(BF16)
            dvo_ref[...] = dv_ref[...].astype(BF16)

    blk = pl.BlockSpec((b, 128), lambda p, i: (i, p))
    full = pl.BlockSpec((s, 128), lambda p, i: (0, p))
    tri = pl.BlockSpec((b, b + 128), lambda p, i: (0, 0))
    return _hosted_call(
        body, comm, name="sb_bwd", grid=(4, s // b),
        out_shape=[jax.ShapeDtypeStruct((s, SB_W), BF16)] * 3,
        in_specs=[blk, pl.BlockSpec((s, 128), lambda p, i: (0, 4 + p)), pl.BlockSpec((s, 128), lambda p, i: (0, 8 + p)),
                  blk, pl.BlockSpec((b, 256), lambda p, i: (i, p)), pl.BlockSpec(memory_space=pltpu.SMEM), tri, tri],
        out_specs=[blk, full, full],
        scratch_shapes=[pltpu.VMEM((2, b, 128), F32)] * 3 + [pltpu.VMEM((s, 128), F32)] * 2,
        dims=("arbitrary", "arbitrary"),
        args=(proj, proj, proj, do, tot, first, _tri("upper_incl"), _tri("upper_strict")))


_COL_Q, _COL_K, _COL_V, _COL_G, _COL_F = 12, 14, 8, 10, 24


def _gla_consts():
    c = CHUNK
    incl = np.tril(np.ones((c, c)), 0)
    strict = np.tril(np.ones((c, c)), -1)
    bd = np.zeros((256, 128))
    bd[:128, :64] = 1.0
    bd[128:, 64:] = 1.0
    return jnp.asarray(incl, BF16), jnp.asarray(strict, BF16), jnp.asarray(bd, F32)


def _time_sums(tri, x):
    hi, lo = _hilo(x)
    return _dot(tri, hi) + _dot(tri, lo)


def _gla_gate(o, gg, g):
    parts = []
    for h in range(2):
        oh = o[:, h * 128:(h + 1) * 128]
        parts.append(oh * lax.rsqrt(jnp.mean(oh * oh, axis=-1, keepdims=True) + EPS))
    return (jnp.concatenate(parts, axis=1) * g) * (gg * _sigmoid(gg))


def _gla_chunk(la_c, k_c, incl):
    cum = _time_sums(incl, la_c)
    total = cum[CHUNK - 1:CHUNK]
    edec = jnp.exp(total - cum)
    return edec, k_c * edec, jnp.exp(total)


def _gla_fwd(proj, wfg, bfg, gain):
    s = proj.shape[0]
    t = GLA_T
    nch = t // CHUNK
    incl, _, bd = _gla_consts()

    def body(q_ref, k_ref, v_ref, gg_ref, f_ref, wf_ref, bf_ref, g_ref, incl_ref, bd_ref, o_ref, st_ref, state_ref):
        @pl.when(pl.program_id(0) == 0)
        def _():
            state_ref[...] = jnp.zeros_like(state_ref)

        la = _log_sigmoid(_dot(f_ref[...].astype(BF16), wf_ref[...].astype(BF16)) + bf_ref[...]) * (1.0 / 16.0)
        states = [state_ref[0], state_ref[1]]
        local = {}
        for cc in range(nch):
            rows = slice(cc * CHUNK, (cc + 1) * CHUNK)
            for p in range(2):
                kl, vl = slice(p * 128, (p + 1) * 128), slice(p * 256, (p + 1) * 256)
                _, kdec, dec = _gla_chunk(la[rows, kl], k_ref[rows, kl], incl_ref[...])
                local[cc, p] = (dec, bd_ref[...] * _dot_tn(v_ref[rows, vl].astype(BF16), kdec.astype(BF16)))
        for cc in range(nch):
            rows = slice(cc * CHUNK, (cc + 1) * CHUNK)
            for p in range(2):
                kl, vl = slice(p * 128, (p + 1) * 128), slice(p * 256, (p + 1) * 256)
                dec, kv = local[cc, p]
                states[p] = states[p] * dec + kv
                st_ref[p, cc] = states[p]
                o = _dot_nt((q_ref[rows, kl] * 0.125).astype(BF16), states[p].astype(BF16))
                o_ref[rows, vl] = _gla_gate(o, gg_ref[rows, vl], g_ref[:, vl]).astype(BF16)
        state_ref[0] = states[0]
        state_ref[1] = states[1]

    def col(width, blk):
        return pl.BlockSpec((t, width), lambda i: (i, blk))

    def whole(shape):
        return pl.BlockSpec(shape, lambda i: tuple(0 for _ in shape))

    return pl.pallas_call(
        body, name="gla_fwd", grid=(s // t,),
        out_shape=[jax.ShapeDtypeStruct((s, GLA_VW), BF16), jax.ShapeDtypeStruct((2, s // CHUNK, 256, 128), F32)],
        in_specs=[col(256, _COL_Q // 2), col(256, _COL_K // 2), col(512, _COL_V // 2), col(512, _COL_G // 2),
                  col(128, _COL_F), whole((128, GLA_KW)), whole((1, GLA_KW)), whole((1, GLA_VW)),
                  whole((CHUNK, CHUNK)), whole((256, 128))],
        out_specs=[col(512, 0), pl.BlockSpec((2, nch, 256, 128), lambda i: (0, i, 0, 0))],
        scratch_shapes=[pltpu.VMEM((2, 256, 128), F32)],
        compiler_params=_params(("arbitrary",)),
    )(proj, proj, proj, proj, proj, wfg, bfg, gain, incl, bd)


def _gla_bwd(proj, do, states, wfg, bfg, gain):
    s = proj.shape[0]
    t = GLA_T
    nch = t // CHUNK
    nblk = s // t
    incl, strict, bd = _gla_consts()

    def body(q_ref, k_ref, v_ref, gg_ref, f_ref, do_ref, st_ref, sp_ref, wf_ref, bf_ref, g_ref, incl_ref, str_ref,
             bd_ref, dq_ref, dk_ref, dv_ref, dgg_ref, df_ref, dwf_ref, dbf_ref, dg_ref, carry_ref, dfs_ref):
        i = pl.program_id(0)

        @pl.when(i == 0)
        def _():
            carry_ref[...] = jnp.zeros_like(carry_ref)
            dwf_ref[...] = jnp.zeros_like(dwf_ref)
            dbf_ref[...] = jnp.zeros_like(dbf_ref)
            dg_ref[...] = jnp.zeros_like(dg_ref)

        fb = f_ref[...].astype(BF16)
        wf = wf_ref[...].astype(BF16)
        f = _dot(fb, wf) + bf_ref[...]
        la = _log_sigmoid(f) * (1.0 / 16.0)
        dla_df = _sigmoid(-f) * (1.0 / 16.0)
        first_block = jnp.where(i == nblk - 1, 0.0, 1.0)
        carries = [carry_ref[0], carry_ref[1]]
        dgains = [jnp.zeros((1, 256), F32), jnp.zeros((1, 256), F32)]
        local = {}
        for cc in reversed(range(nch)):
            rows = slice(cc * CHUNK, (cc + 1) * CHUNK)
            for p in range(2):
                kl, vl = slice(p * 128, (p + 1) * 128), slice(p * 256, (p + 1) * 256)
                edec, kdec, dec = _gla_chunk(la[rows, kl], k_ref[rows, kl], incl_ref[...])
                qs = (q_ref[rows, kl] * 0.125).astype(BF16)
                sb16 = st_ref[p, cc].astype(BF16)
                o = _dot_nt(qs, sb16)
                _, vjp = jax.vjp(_gla_gate, o, gg_ref[rows, vl], g_ref[:, vl])
                do_c, dgg_c, dg_c = vjp(do_ref[rows, vl])
                dgains[p] = dgains[p] + dg_c
                dgg_ref[rows, vl] = dgg_c.astype(BF16)
                do16 = do_c.astype(BF16)
                dq_ref[rows, kl] = (_dot(do16, sb16) * 0.125).astype(BF16)
                local[cc, p] = (edec, kdec, dec, bd_ref[...] * _dot_tn(do16, qs))
        for cc in reversed(range(nch)):
            rows = slice(cc * CHUNK, (cc + 1) * CHUNK)
            for p in range(2):
                kl, vl = slice(p * 128, (p + 1) * 128), slice(p * 256, (p + 1) * 256)
                edec, kdec, dec, readers = local[cc, p]
                prev = st_ref[p, cc - 1] if cc > 0 else sp_ref[p, 0] * first_block
                grad = readers + carries[p]
                g16 = grad.astype(BF16)
                dv_ref[rows, vl] = _dot_nt(kdec.astype(BF16), g16).astype(BF16)
                dkdec = _dot(v_ref[rows, vl].astype(BF16), g16)
                ddec = jnp.sum(grad * prev, axis=0, keepdims=True) * dec
                dk_ref[rows, kl] = (dkdec * edec).astype(BF16)
                dla = _time_sums(str_ref[...], dkdec * kdec) + ddec
                dfs_ref[rows, kl] = dla * dla_df[rows, kl]
                carries[p] = grad * dec
        carry_ref[0] = carries[0]
        carry_ref[1] = carries[1]
        df = dfs_ref[...]
        df16 = df.astype(BF16)
        df_ref[...] = _dot_nt(df16, wf).astype(BF16)
        dwf_ref[...] += _dot_tn(fb, df16)
        dbf_ref[...] += jnp.sum(df, axis=0, keepdims=True)
        dg_ref[...] += jnp.concatenate(dgains, axis=1)

    def col(width, blk):
        return pl.BlockSpec((t, width), lambda i: (nblk - 1 - i, blk))

    def whole(shape):
        return pl.BlockSpec(shape, lambda i: tuple(0 for _ in shape))

    return pl.pallas_call(
        body, name="gla_bwd", grid=(nblk,),
        out_shape=[jax.ShapeDtypeStruct((s, GLA_KW), BF16), jax.ShapeDtypeStruct((s, GLA_KW), BF16),
                   jax.ShapeDtypeStruct((s, GLA_VW), BF16), jax.ShapeDtypeStruct((s, GLA_VW), BF16),
                   jax.ShapeDtypeStruct((s, 128), BF16), jax.ShapeDtypeStruct((128, GLA_KW), F32),
                   jax.ShapeDtypeStruct((1, GLA_KW), F32), jax.ShapeDtypeStruct((1, GLA_VW), F32)],
        in_specs=[col(256, _COL_Q // 2), col(256, _COL_K // 2), col(512, _COL_V // 2), col(512, _COL_G // 2),
                  col(128, _COL_F), col(512, 1),
                  pl.BlockSpec((2, nch, 256, 128), lambda i: (0, nblk - 1 - i, 0, 0)),
                  pl.BlockSpec((2, 1, 256, 128), lambda i: (0, jnp.maximum((nblk - 1 - i) * nch - 1, 0), 0, 0)),
                  whole((128, GLA_KW)), whole((1, GLA_KW)), whole((1, GLA_VW)),
                  whole((CHUNK, CHUNK)), whole((CHUNK, CHUNK)), whole((256, 128))],
        out_specs=[col(256, 0), col(256, 0), col(512, 0), col(512, 0), col(128, 0),
                   whole((128, GLA_KW)), whole((1, GLA_KW)), whole((1, GLA_VW))],
        scratch_shapes=[pltpu.VMEM((2, 256, 128), F32), pltpu.VMEM((t, GLA_KW), F32)],
        compiler_params=_params(("arbitrary",)),
    )(proj, proj, proj, proj, proj, do, states, states, wfg, bfg, gain, incl, strict, bd)


def _cols_by_dev(a, n_dev=N_DEV):
    r, c = a.shape
    return a.reshape(r, n_dev, c // n_dev).transpose(1, 0, 2)


def _cols_from_dev(a):
    _, r, n = a.shape
    return a.transpose(1, 0, 2).reshape(r, N_DEV * n)


def kernel(x, c, w_ada, b_ada, g_norm1, w_in, w_fg2, b_fg2, g_gla_out, w_out, g_norm2, w_up, w_conv, b_conv, w_down, g_final, loss_target, m_w_ada, m_b_ada, m_g_norm1, m_w_in, m_w_fg2, m_b_fg2, m_g_gla_out, m_w_out, m_g_norm2, m_w_up, m_w_conv, m_b_conv, m_w_down, m_g_final, v_w_ada, v_b_ada, v_g_norm1, v_w_in, v_w_fg2, v_b_fg2, v_g_gla_out, v_w_out, v_g_norm2, v_w_up, v_w_conv, v_b_conv, v_w_down, v_g_final):
    s = x.shape[1]
    me = 4 * lax.axis_index("x") + 2 * lax.axis_index("y") + lax.axis_index("c")
    xs, tgt = x[0], loss_target[0]
    ts = min(512, s)
    tm = min(1024, s)
    tc = min(2048, s // 2)

    c_all, wconv_g, wfg_g = _exchange([c, w_conv[0], w_fg2[0]], "gather_first", True)
    wconv_p = jnp.pad(_cols_from_dev(wconv_g), ((0, 5), (0, 0)))
    wfg_p = jnp.pad(_cols_from_dev(wfg_g), ((0, 128 - RANK), (0, 0)))

    c16 = jnp.pad(c_all.reshape(N_DEV, D), ((0, 8), (0, 0)))
    modp = _ada_fwd(c16, w_ada[0])[:N_DEV]
    (mod_all,) = _exchange([modp], "gather_mod", True)
    mod = lax.dynamic_index_in_dim(mod_all, me, axis=1, keepdims=False).reshape(1, 6 * D) + b_ada
    shift1, scale1, gate1, shift2, scale2, gate2 = [mod[:, k * D:(k + 1) * D] for k in range(6)]

    w_in_t, m_in_t, v_in_t = [jnp.swapaxes(t[0], 0, 1) for t in (w_in, m_w_in, v_w_in)]
    w_up_t, m_up_t, v_up_t = [jnp.swapaxes(t[0], 0, 1) for t in (w_up, m_w_up, v_w_up)]
    (h,), (win_g,) = _rows_call(_fwd_norm1, "norm1", [xs], [g_norm1, scale1, shift1], [(D, BF16)], [], ts,
                                comm=([w_in_t.astype(BF16)], True))
    win_t = jnp.pad(win_g.reshape(IN_W, D), ((0, IN_WP - IN_W), (0, 0)))
    proj = _mm_nt(h, win_t, "in_proj", F32, ts, IN_WP)
    (o_sb, sb_tot, sb_first), (wout_g, wup_g, wdown_g) = _sb_fwd(
        proj, comm=([w_out[0].astype(BF16), w_up_t.astype(BF16), w_down[0].astype(BF16)], True))
    wout = wout_g.reshape(D, D)
    wup_t = wup_g.reshape(2 * D_FF, D)
    wdown = wdown_g.reshape(D_FF, D)
    o_gla, states = _gla_fwd(proj, wfg_p, b_fg2, g_gla_out)
    mixed, _ = _mm([o_sb, o_gla], wout, "out_proj", F32, tm, 512)
    x1, h2 = _rows_call(_fwd_resid_norm, "resid_norm2", [xs, mixed], [gate1, g_norm2, scale2, shift2],
                        [(D, F32), (D, BF16)], [], ts)
    up_v, up_g = _mm_nt(h2, wup_t, "up_proj", F32, tm, 1408, n_out=2)
    act = _conv_glu_fwd(up_v, up_g, wconv_p, b_conv, tc)
    ffn, _ = _mm([act], wdown, "down_proj", F32, tm, 512)
    dx2, dffn, dgate2, dg_final, loss_part = _rows_call(
        _final, "final", [x1, ffn, tgt], [gate2, g_final.reshape(1, D)],
        [(D, F32), (D, BF16)], [(1, D), (1, D), (1, 128)], ts)

    (dw_down,) = _mm_tn(act, [dffn], "down_wgrad", 1408, 1024, tm)
    da = _mm_nt(dffn, wdown, "down_dgrad", F32, tm, 1408)
    (dup_v, dup_g, dwc_v, dwc_g, dbc_v, dbc_g), (r_down,) = _conv_glu_bwd(
        up_v, up_g, da, wconv_p, b_conv, tc, comm=([dw_down.reshape(N_DEV, 352, D)], False))
    (dw_up_tv,) = _mm_tn(dup_v, [h2], "up_wgrad_v", 1408, 1024, tm)
    (dw_up_tg,) = _mm_tn(dup_g, [h2], "up_wgrad_g", 1408, 1024, tm)
    dh2, _ = _mm([dup_v, dup_g], wup_t, "up_dgrad", F32, tm, 512)
    dx1, dmixed, dgate1, dg_norm2, dscale2, dshift2 = _rows_call(
        _bwd_resid_norm, "resid_norm2_bwd", [xs, mixed, dx2, dh2], [gate1, g_norm2, scale2, shift2],
        [(D, F32), (D, BF16)], [(1, D)] * 4, ts)
    dw_out = jnp.concatenate(_mm_tn_rows([o_sb, o_gla], dmixed, "out_wgrad", tm), axis=0)
    dcat = _mm_nt(dmixed, wout, "out_dgrad", F32, tm, 512)
    dw_up_s = jnp.concatenate([dw_up_tv, dw_up_tg], axis=0).reshape(N_DEV, 704, D)
    dwconv_s = _cols_by_dev(jnp.concatenate([dwc_v[:3], dwc_g[:3]], axis=1))
    (dq, dk, dv), (r_up, r_out, r_conv) = _sb_bwd(
        proj, dcat, sb_tot, sb_first, comm=([dw_up_s, dw_out.reshape(N_DEV, 128, D), dwconv_s], False))
    dgq, dgk, dgv, dgg, dgf, dwfg_p, dbfg, dg_gla = _gla_bwd(proj, dcat, states, wfg_p, b_fg2, g_gla_out)
    dproj = [dq, dk, dv, dgq, dgk, dgv, dgg, dgf]
    dw_in_t = jnp.concatenate(_mm_tn_rows(dproj, h, "in_wgrad", tm), axis=0)
    dh, (r_in, r_fg) = _mm(dproj, win_t, "in_dgrad", F32, tm, 512,
                           comm=([dw_in_t[:IN_W].reshape(N_DEV, 386, D).astype(BF16), _cols_by_dev(dwfg_p[:RANK])],
                                 False))
    grad_x, dg_norm1, dscale1, dshift1 = _rows_call(
        _bwd_norm1, "norm1_bwd", [xs, dh, dx1], [g_norm1, scale1, shift1], [(D, F32)], [(1, D)] * 3, ts)

    dmod = jnp.concatenate([dshift1, dscale1, dgate1, dshift2, dscale2, dgate2], axis=1)
    small_parts = [dmod, dg_norm1, dbfg, dg_gla, dg_norm2, dbc_v, dbc_g, dg_final, loss_part]
    n_small = sum(p.shape[1] for p in small_parts[:-1])
    (sg,) = _exchange([jnp.concatenate(small_parts, axis=1)], "gather_small_grads", True)
    sg = sg.reshape(N_DEV, n_small + 128)
    loss = jnp.sum(sg[:, n_small])
    small_w = [b_ada, g_norm1, b_fg2, g_gla_out, g_norm2, b_conv, g_final.reshape(1, D)]
    small_m = [m_b_ada, m_g_norm1, m_b_fg2, m_g_gla_out, m_g_norm2, m_b_conv, m_g_final.reshape(1, D)]
    small_v = [v_b_ada, v_g_norm1, v_b_fg2, v_g_gla_out, v_g_norm2, v_b_conv, v_g_final.reshape(1, D)]
    s_out = list(_adam_rows(sg, small_w, small_m, small_v, "adam_small"))
    s_out[24:] = [t.reshape(D) for t in s_out[24:]]
    s_g, s_d, s_m, s_v = [s_out[k::4] for k in range(4)]

    dmod_all = sg[:, :6 * D].reshape(N_DEV, N_DEV, 768)
    dmod_mine = lax.dynamic_index_in_dim(dmod_all, me, axis=1, keepdims=False)
    dw_ada = _ada_bwd(c16, jnp.pad(dmod_mine, ((0, 8), (0, 0))))
    a_g, a_d, a_m, a_v = [t[None] for t in _adam(dw_ada[None], w_ada[0], m_w_ada[0], v_w_ada[0], "adam_ada", 256)]

    recv = [r_in, r_fg, r_out, r_up, r_conv, r_down]
    big_w = [w_in_t, w_fg2[0], w_out[0], w_up_t, w_conv[0], w_down[0]]
    big_m = [m_in_t, m_w_fg2[0], m_w_out[0], m_up_t, m_w_conv[0], m_w_down[0]]
    big_v = [v_in_t, v_w_fg2[0], v_w_out[0], v_up_t, v_w_conv[0], v_w_down[0]]
    big_name = ["adam_in", "adam_fg2", "adam_out", "adam_up", "adam_conv", "adam_down"]
    big_rows = [386, RANK, 128, 176, 3, 176]
    b_out = [_adam(r, w, m, v, name, tr)
             for r, w, m, v, name, tr in zip(recv, big_w, big_m, big_v, big_name, big_rows)]
    for k in (0, 3):
        b_out[k] = [jnp.swapaxes(t, 0, 1) for t in b_out[k]]
    b_g, b_d, b_m, b_v = [[o[k][None] for o in b_out] for k in range(4)]

    def ordered(a, sm, bg):
        return [a, sm[0], sm[1], bg[0], bg[1], sm[2], sm[3], bg[2], sm[4], bg[3], bg[4], sm[5], bg[5], sm[6]]

    return (loss, grad_x[None], *ordered(a_g, s_g, b_g), *ordered(a_d, s_d, b_d),
            *ordered(a_m, s_m, b_m), *ordered(a_v, s_v, b_v))
```

```python
import numpy as np

import jax
import jax.numpy as jnp
from jax import lax
from jax.experimental import pallas as pl
from jax.experimental.pallas import tpu as pltpu

F32, BF16 = jnp.float32, jnp.bfloat16
N_DEV = 8
D = 1024
SB_W = 512
GLA_KW, GLA_VW = 256, 512
RANK = 16
IN_W = 3088
IN_WP = 3200
D_FF = 2816
FF_T = 256
N_FT = D_FF // FF_T
EPS = 1e-6
SB_B = 256
SB_DEAD = -110.0
CHUNK = 64
GLA_T = 512
VMEM_LIMIT = 56 * 1024 * 1024

LR, B1, B2, ADAM_EPS, WD, STEP = 0.001, 0.9, 0.999, 1e-08, 0.01, 10


def _params(dims=None, vmem=True):
    kw = {}
    if dims is not None:
        kw["dimension_semantics"] = dims
    if vmem:
        kw["vmem_limit_bytes"] = VMEM_LIMIT
    return pltpu.CompilerParams(**kw)


def _dot(a, b):
    return jnp.dot(a, b, preferred_element_type=F32)


def _dot_nt(a, b):
    return lax.dot_general(a, b, (((1,), (1,)), ((), ())), preferred_element_type=F32)


def _dot_tn(a, b):
    return lax.dot_general(a, b, (((0,), (0,)), ((), ())), preferred_element_type=F32)


def _hilo(x):
    hi = x.astype(BF16)
    lo = (x - hi.astype(F32)).astype(BF16)
    return hi, lo


def _sigmoid(x):
    return 1.0 / (1.0 + jnp.exp(-x))


def _log_sigmoid(x):
    return jnp.minimum(x, 0.0) - jnp.log(1.0 + jnp.exp(-jnp.abs(x)))


def _rms(x, g):
    n = x * lax.rsqrt(jnp.mean(x * x, axis=-1, keepdims=True) + EPS)
    return n * g


def _norm_mod(x, g, scale, shift):
    return _rms(x, g) * (1.0 + scale) + shift


N_PEER = N_DEV - 1


def _exchange_copies(x_refs, out_refs, send_sems, recv_sems, local_sems, gather):
    ix, iy, ic = lax.axis_index("x"), lax.axis_index("y"), lax.axis_index("c")
    me = 4 * ix + 2 * iy + ic
    peers = []
    for k in range(1, N_DEV):
        px = 1 - ix if k & 4 else ix
        py = 1 - iy if k & 2 else iy
        pc = 1 - ic if k & 1 else ic
        peers.append(((px, py, pc), 4 * px + 2 * py + pc))

    def copy(a, k, dev, src_slot, dst_slot):
        return pltpu.make_async_remote_copy(
            src_ref=x_refs[a] if gather else x_refs[a].at[src_slot],
            dst_ref=out_refs[a].at[dst_slot],
            send_sem=send_sems.at[a * N_PEER + k],
            recv_sem=recv_sems.at[a * N_PEER + k],
            device_id=dev,
            device_id_type=pl.DeviceIdType.MESH,
        )

    n = len(x_refs)
    mine = [pltpu.make_async_copy(x_refs[a] if gather else x_refs[a].at[me], out_refs[a].at[me], local_sems.at[a])
            for a in range(n)]
    sends = [copy(a, k, dev, pid, me) for a in range(n) for k, (dev, pid) in enumerate(peers)]
    recvs = [copy(a, k, dev, pid, pid) for a in range(n) for k, (dev, pid) in enumerate(peers)]
    return mine, sends, recvs


def _exchange_scratch(n):
    return [pltpu.SemaphoreType.DMA((n * N_PEER,)), pltpu.SemaphoreType.DMA((n * N_PEER,)),
            pltpu.SemaphoreType.DMA((n,))]


def _exchange_shapes(arrays, gather):
    return [jax.ShapeDtypeStruct((N_DEV,) + tuple(x.shape if gather else x.shape[1:]), x.dtype) for x in arrays]


def _exchange(arrays, name, gather):
    n = len(arrays)

    def body(*refs):
        mine, sends, recvs = _exchange_copies(refs[:n], refs[n:2 * n], *refs[2 * n:], gather)
        for cp in mine + sends:
            cp.start()
        for cp in recvs:
            cp.wait_recv()
        for cp in sends:
            cp.wait_send()
        for cp in mine:
            cp.wait()

    return pl.pallas_call(
        body,
        name=name,
        out_shape=_exchange_shapes(arrays, gather),
        in_specs=[pl.BlockSpec(memory_space=pl.ANY)] * n,
        out_specs=[pl.BlockSpec(memory_space=pl.ANY)] * n,
        scratch_shapes=_exchange_scratch(n),
    )(*arrays)


def _hosted_call(body, comm, *, name, grid, in_specs, out_specs, out_shape, scratch_shapes, dims, args):
    if comm is None:
        outs = pl.pallas_call(body, name=name, grid=grid, in_specs=in_specs, out_specs=out_specs, out_shape=out_shape,
                              scratch_shapes=scratch_shapes, compiler_params=_params(dims))(*args)
        return outs, []
    arrays, gather = comm
    n_in, n_out, n_scr, nc = len(in_specs), len(out_specs), len(scratch_shapes), len(arrays)

    def hosted(*refs):
        ins, c_in = refs[:n_in], refs[n_in:n_in + nc]
        outs, c_out = refs[n_in + nc:n_in + nc + n_out], refs[n_in + nc + n_out:n_in + 2 * nc + n_out]
        rest = refs[n_in + 2 * nc + n_out:]
        scratch, sems = rest[:n_scr], rest[n_scr:]
        mine, sends, recvs = _exchange_copies(c_in, c_out, *sems, gather)
        first = pl.program_id(0) == 0
        last = pl.program_id(0) == grid[0] - 1
        for axis in range(1, len(grid)):
            first = jnp.logical_and(first, pl.program_id(axis) == 0)
            last = jnp.logical_and(last, pl.program_id(axis) == grid[axis] - 1)

        @pl.when(first)
        def _():
            for cp in mine + sends:
                cp.start()

        body(*ins, *outs, *scratch)

        @pl.when(last)
        def _():
            for cp in recvs:
                cp.wait_recv()
            for cp in sends:
                cp.wait_send()
            for cp in mine:
                cp.wait()

    any_spec = pl.BlockSpec(memory_space=pl.ANY)
    outs = pl.pallas_call(
        hosted, name=name, grid=grid,
        in_specs=list(in_specs) + [any_spec] * nc,
        out_specs=list(out_specs) + [any_spec] * nc,
        out_shape=list(out_shape) + _exchange_shapes(arrays, gather),
        scratch_shapes=list(scratch_shapes) + _exchange_scratch(nc),
        compiler_params=_params(tuple("arbitrary" for _ in grid)),
    )(*args, *arrays)
    return outs[:n_out], outs[n_out:]


def _adam_math(g, w, m, v):
    m_new = B1 * m + (1.0 - B1) * g
    v_new = B2 * v + (1.0 - B2) * (g * g)
    m_hat = m_new / (1.0 - B1 ** STEP)
    v_hat = v_new / (1.0 - B2 ** STEP)
    return -LR * (m_hat / (jnp.sqrt(v_hat) + ADAM_EPS) + WD * w), m_new, v_new


def _adam(gparts, w, m, v, name, tr):
    n, rows, cols = gparts.shape

    def body(gp_ref, w_ref, m_ref, v_ref, g_ref, d_ref, nm_ref, nv_ref):
        g = gp_ref[0].astype(F32)
        for j in range(1, n):
            g = g + gp_ref[j].astype(F32)
        g_ref[...] = g
        d_ref[...], nm_ref[...], nv_ref[...] = _adam_math(g, w_ref[...], m_ref[...], v_ref[...])

    blk = pl.BlockSpec((tr, cols), lambda i: (i, 0))
    return pl.pallas_call(
        body,
        name=name,
        grid=(rows // tr,),
        out_shape=[jax.ShapeDtypeStruct((rows, cols), F32)] * 4,
        in_specs=[pl.BlockSpec((n, tr, cols), lambda i: (0, i, 0)), blk, blk, blk],
        out_specs=[blk] * 4,
        compiler_params=_params(("parallel",)),
    )(gparts, w, m, v)


def _adam_rows(parts, ws, ms, vs, name):
    n = parts.shape[0]
    k = len(ws)
    widths = [w.shape[1] for w in ws]

    def body(*refs):
        p_ref, w_refs, m_refs, v_refs = refs[0], refs[1:1 + k], refs[1 + k:1 + 2 * k], refs[1 + 2 * k:1 + 3 * k]
        outs = refs[1 + 3 * k:]
        total = p_ref[0:1, :]
        for j in range(1, n):
            total = total + p_ref[j:j + 1, :]
        off = 0
        for a, width in enumerate(widths):
            g = total[:, off:off + width]
            off += width
            outs[4 * a][...] = g
            outs[4 * a + 1][...], outs[4 * a + 2][...], outs[4 * a + 3][...] = _adam_math(
                g, w_refs[a][...], m_refs[a][...], v_refs[a][...])

    return pl.pallas_call(
        body, name=name,
        out_shape=[jax.ShapeDtypeStruct((1, width), F32) for width in widths for _ in range(4)],
        compiler_params=_params(),
    )(parts, *ws, *ms, *vs)


def _mm(a_list, b, name, out_dtype, tm, tn, comm=None):
    n_a = len(a_list)
    m = a_list[0].shape[0]
    n = b.shape[1]
    widths = [a.shape[1] for a in a_list]
    starts = [sum(widths[:g]) for g in range(n_a)]
    assert all(s0 % w == 0 for s0, w in zip(starts, widths))

    def body(*refs):
        o_ref = refs[2 * n_a]
        acc = _dot(refs[0][...].astype(BF16), refs[n_a][...].astype(BF16))
        for g in range(1, n_a):
            acc = acc + _dot(refs[g][...].astype(BF16), refs[n_a + g][...].astype(BF16))
        o_ref[...] = acc.astype(out_dtype)

    (out,), got = _hosted_call(
        body, comm, name=name, grid=(n // tn, m // tm),
        out_shape=[jax.ShapeDtypeStruct((m, n), out_dtype)],
        in_specs=[pl.BlockSpec((tm, w), lambda j, i: (i, 0)) for w in widths]
        + [pl.BlockSpec((w, tn), lambda j, i, blk=s0 // w: (blk, j)) for s0, w in zip(starts, widths)],
        out_specs=[pl.BlockSpec((tm, tn), lambda j, i: (i, j))],
        scratch_shapes=[], dims=("parallel", "parallel"), args=(*a_list, *([b] * n_a)))
    return out, got


def _mm_tn_rows(a_list, b, name, tk):
    s, n = b.shape
    n_a = len(a_list)

    def body(*refs):
        a_refs, b_ref, o_refs = refs[:n_a], refs[n_a], refs[n_a + 1:]

        @pl.when(pl.program_id(0) == 0)
        def _():
            for o_ref in o_refs:
                o_ref[...] = jnp.zeros_like(o_ref)

        b_blk = b_ref[...].astype(BF16)
        for a_ref, o_ref in zip(a_refs, o_refs):
            o_ref[...] += _dot_tn(a_ref[...].astype(BF16), b_blk)

    return pl.pallas_call(
        body,
        name=name,
        grid=(s // tk,),
        out_shape=[jax.ShapeDtypeStruct((a.shape[1], n), F32) for a in a_list],
        in_specs=[pl.BlockSpec((tk, a.shape[1]), lambda k: (k, 0)) for a in a_list] + [pl.BlockSpec((tk, n), lambda k: (k, 0))],
        out_specs=[pl.BlockSpec((a.shape[1], n), lambda k: (0, 0)) for a in a_list],
        compiler_params=_params(("arbitrary",)),
    )(*a_list, b)


def _mm_nt(a, b, name, out_dtype, tm, tn, n_out=1):
    m, k = a.shape
    n = b.shape[0] // n_out
    nt = n // tn

    def body(a_ref, *refs):
        a_blk = a_ref[...].astype(BF16)
        for b_ref, o_ref in zip(refs[:n_out], refs[n_out:]):
            o_ref[...] = _dot_nt(a_blk, b_ref[...].astype(BF16)).astype(out_dtype)

    outs = pl.pallas_call(
        body,
        name=name,
        grid=(nt, m // tm),
        out_shape=[jax.ShapeDtypeStruct((m, n), out_dtype)] * n_out,
        in_specs=[pl.BlockSpec((tm, k), lambda j, i: (i, 0))]
        + [pl.BlockSpec((tn, k), lambda j, i, o=o: (o * nt + j, 0)) for o in range(n_out)],
        out_specs=[pl.BlockSpec((tm, tn), lambda j, i: (i, j))] * n_out,
        compiler_params=_params(("parallel", "parallel")),
    )(a, *([b] * n_out))
    return outs[0] if n_out == 1 else outs


def _mm_tn(a, b_list, name, tm, tn, tk):
    s, m = a.shape
    n = b_list[0].shape[1]
    n_b = len(b_list)

    def body(a_ref, *refs):
        b_refs, o_refs = refs[:n_b], refs[n_b:]

        @pl.when(pl.program_id(2) == 0)
        def _():
            for o_ref in o_refs:
                o_ref[...] = jnp.zeros_like(o_ref)

        a_blk = a_ref[...].astype(BF16)
        for b_ref, o_ref in zip(b_refs, o_refs):
            o_ref[...] += _dot_tn(a_blk, b_ref[...].astype(BF16))

    return pl.pallas_call(
        body,
        name=name,
        grid=(n // tn, m // tm, s // tk),
        out_shape=[jax.ShapeDtypeStruct((m, n), F32)] * n_b,
        in_specs=[pl.BlockSpec((tk, tm), lambda j, i, k: (k, i))] + [pl.BlockSpec((tk, tn), lambda j, i, k: (k, j))] * n_b,
        out_specs=[pl.BlockSpec((tm, tn), lambda j, i, k: (i, j))] * n_b,
        compiler_params=_params(("parallel", "parallel", "arbitrary")),
    )(a, *b_list)


def _rows_call(fn, name, rows, params, out_rows, out_accs, ts, comm=None):
    s = rows[0].shape[0]
    nr, npar, no = len(rows), len(params), len(out_rows)

    def body(*refs):
        r, p = refs[:nr], refs[nr:nr + npar]
        o, acc = refs[nr + npar:nr + npar + no], refs[nr + npar + no:]
        outs, sums = fn(*[t[...] for t in r], *[t[...] for t in p])
        for ref, val in zip(o, outs):
            ref[...] = val.astype(ref.dtype)
        if acc:
            @pl.when(pl.program_id(0) == 0)
            def _():
                for ref in acc:
                    ref[...] = jnp.zeros_like(ref)

            for ref, val in zip(acc, sums):
                ref[...] += val

    outs, got = _hosted_call(
        body, comm, name=name, grid=(s // ts,),
        out_shape=[jax.ShapeDtypeStruct((s, w), dt) for w, dt in out_rows]
        + [jax.ShapeDtypeStruct(shape, F32) for shape in out_accs],
        in_specs=[pl.BlockSpec((ts, t.shape[1]), lambda i: (i, 0)) for t in rows]
        + [pl.BlockSpec(t.shape, lambda i: (0, 0)) for t in params],
        out_specs=[pl.BlockSpec((ts, w), lambda i: (i, 0)) for w, _ in out_rows]
        + [pl.BlockSpec(shape, lambda i: (0, 0)) for shape in out_accs],
        scratch_shapes=[], dims=("arbitrary",), args=(*rows, *params))
    return outs if comm is None else (outs, got)


def _fwd_norm1(x, g, scale, shift):
    return (_norm_mod(x, g, scale, shift),), ()


def _resid_norm(x, mixed, gate, g, scale, shift):
    x1 = x + (1.0 + gate) * mixed
    return x1, _norm_mod(x1, g, scale, shift)


def _fwd_resid_norm(x, mixed, gate, g, scale, shift):
    return _resid_norm(x, mixed, gate, g, scale, shift), ()


def _final(x1, ffn, tgt, gate, g):
    def head(x1, ffn, gate, g):
        return _rms(x1 + (1.0 + gate) * ffn, g)

    y, vjp = jax.vjp(head, x1, ffn, gate, g)
    err = y - tgt
    dx2, dffn, dgate, dg = vjp(err * (1.0 / D))
    sq = jnp.sum(jnp.sum(err * err, axis=1, keepdims=True), axis=0, keepdims=True)
    loss = jnp.broadcast_to(sq * (0.5 / D), (1, 128))
    return (dx2, dffn), (dgate, dg, loss)


def _bwd_resid_norm(x, mixed, dx2, dh2, gate, g, scale, shift):
    _, vjp = jax.vjp(_resid_norm, x, mixed, gate, g, scale, shift)
    dx, dmixed, dgate, dg, dscale, dshift = vjp((dx2, dh2))
    return (dx, dmixed), (dgate, dg, dscale, dshift)


def _bwd_norm1(x, dh, dx1, g, scale, shift):
    _, vjp = jax.vjp(_norm_mod, x, g, scale, shift)
    dx, dg, dscale, dshift = vjp(dh)
    return (dx1 + dx,), (dg, dscale, dshift)


def _ada_fwd(c16, w):
    def body(c_ref, w_ref, o_ref):
        c = c_ref[...]
        o_ref[...] = _dot((c * _sigmoid(c)).astype(BF16), w_ref[...].astype(BF16))

    return pl.pallas_call(
        body, name="ada_fwd", out_shape=jax.ShapeDtypeStruct((c16.shape[0], w.shape[1]), F32),
        compiler_params=_params(),
    )(c16, w)


def _ada_bwd(c16, dmod16):
    def body(c_ref, d_ref, o_ref):
        c = c_ref[...]
        o_ref[...] = _dot_tn((c * _sigmoid(c)).astype(BF16), d_ref[...].astype(BF16))

    return pl.pallas_call(
        body, name="ada_bwd", out_shape=jax.ShapeDtypeStruct((c16.shape[1], dmod16.shape[1]), F32),
        compiler_params=_params(),
    )(c16, dmod16)


CONV_R = 256


def _conv_window(win, wc, bc):
    s1 = pltpu.roll(win, 1, 0)
    s2 = pltpu.roll(win, 2, 0)
    u = bc + wc[0:1] * s2 + wc[1:2] * s1 + wc[2:3] * win
    return u[8:], s1[8:], s2[8:], win[8:]


def _row_windows(ref, before, after, n_after, lanes):
    ts, r = ref.shape[0], CONV_R

    def middle(i):
        return ref[pl.ds(pl.multiple_of(i * r, r) - 8, r + 8 + n_after), lanes]

    first = jnp.concatenate([before, ref[0:r + n_after, lanes]], axis=0)
    last = ref[ts - r - 8:ts, lanes] if n_after == 0 else jnp.concatenate([ref[ts - r - 8:ts, lanes], after], axis=0)
    return first, middle, last


def _ffn_specs(ts, s):
    cur = pl.BlockSpec((ts, FF_T), lambda j, i: (i, j))
    halo = pl.BlockSpec((8, FF_T), lambda j, i: (jnp.maximum(i * (ts // 8) - 1, 0), j))
    nxt = pl.BlockSpec((8, FF_T), lambda j, i: (jnp.minimum((i + 1) * (ts // 8), s // 8 - 1), j))
    wc = [pl.BlockSpec((8, FF_T), lambda j, i, h=h: (0, h * N_FT + j)) for h in range(2)]
    bc = [pl.BlockSpec((1, FF_T), lambda j, i, h=h: (0, h * N_FT + j)) for h in range(2)]
    return cur, halo, nxt, wc, bc


def _conv_glu_fwd(up_v, up_g, wc, bc, ts):
    s = up_v.shape[0]
    cur, halo, _, wcs, bcs = _ffn_specs(ts, s)

    def body(v_ref, vh_ref, g_ref, gh_ref, wcv_ref, wcg_ref, bcv_ref, bcg_ref, a_ref):
        keep = jnp.where(pl.program_id(1) == 0, 0.0, 1.0)
        for lanes in (slice(0, 128), slice(128, 256)):
            wcv, wcg, bcv, bcg = wcv_ref[:, lanes], wcg_ref[:, lanes], bcv_ref[:, lanes], bcg_ref[:, lanes]
            first_v, mid_v, _ = _row_windows(v_ref, vh_ref[:, lanes] * keep, None, 0, lanes)
            first_g, mid_g, _ = _row_windows(g_ref, gh_ref[:, lanes] * keep, None, 0, lanes)

            def emit(win_v, win_g, start, lanes=lanes, wcv=wcv, wcg=wcg, bcv=bcv, bcg=bcg):
                val = _conv_window(win_v, wcv, bcv)[0]
                gte = _conv_window(win_g, wcg, bcg)[0]
                a_ref[pl.ds(start, CONV_R), lanes] = (val * (gte * _sigmoid(gte))).astype(BF16)

            emit(first_v, first_g, 0)

            def loop(i, carry, emit=emit, mid_v=mid_v, mid_g=mid_g):
                emit(mid_v(i), mid_g(i), pl.multiple_of(i * CONV_R, CONV_R))
                return carry

            lax.fori_loop(1, ts // CONV_R, loop, 0)

    return pl.pallas_call(
        body, name="conv_glu_fwd", grid=(N_FT, s // ts),
        out_shape=jax.ShapeDtypeStruct((s, D_FF), BF16),
        in_specs=[cur, halo, cur, halo, *wcs, *bcs], out_specs=cur,
        compiler_params=_params(("parallel", "arbitrary")),
    )(up_v, up_v, up_g, up_g, wc, wc, bc, bc)


def _glu_bwd(val, gte, da):
    sg = _sigmoid(gte)
    return da * (gte * sg), da * val * (sg * (1.0 + gte * (1.0 - sg)))


def _conv_glu_bwd(up_v, up_g, da, wc, bc, ts, comm=None):
    s = up_v.shape[0]
    nblk = s // ts
    cur, halo, nxt, wcs, bcs = _ffn_specs(ts, s)
    acc_w = pl.BlockSpec((8, FF_T), lambda j, i: (0, j))
    acc_b = pl.BlockSpec((1, FF_T), lambda j, i: (0, j))

    def body(v_ref, vh_ref, vn_ref, g_ref, gh_ref, gn_ref, da_ref, dan_ref, wcv_ref, wcg_ref, bcv_ref, bcg_ref,
             dupv_ref, dupg_ref, dwcv_ref, dwcg_ref, dbcv_ref, dbcg_ref):
        first = pl.program_id(1) == 0
        keep = jnp.where(first, 0.0, 1.0)
        keep_next = jnp.where(pl.program_id(1) == nblk - 1, 0.0, 1.0)
        r = CONV_R
        n = ts // r

        @pl.when(first)
        def _():
            for ref in (dwcv_ref, dwcg_ref, dbcv_ref, dbcg_ref):
                ref[...] = jnp.zeros_like(ref)

        for lanes in (slice(0, 128), slice(128, 256)):
            wcv, wcg, bcv, bcg = wcv_ref[:, lanes], wcg_ref[:, lanes], bcv_ref[:, lanes], bcg_ref[:, lanes]
            first_v, mid_v, last_v = _row_windows(v_ref, vh_ref[:, lanes] * keep, vn_ref[:, lanes], 8, lanes)
            first_g, mid_g, last_g = _row_windows(g_ref, gh_ref[:, lanes] * keep, gn_ref[:, lanes], 8, lanes)

            def emit(win_v, win_g, da_w, start, lanes=lanes, wcv=wcv, wcg=wcg, bcv=bcv, bcg=bcg):
                u_v, s1_v, s2_v, x_v = _conv_window(win_v, wcv, bcv)
                u_g, s1_g, s2_g, x_g = _conv_window(win_g, wcg, bcg)
                du_v, du_g = _glu_bwd(u_v, u_g, da_w)
                rows = pl.ds(start, r)
                for du, s1, s2, x, wc, dup_ref, dwc_ref, dbc_ref in (
                        (du_v, s1_v, s2_v, x_v, wcv, dupv_ref, dwcv_ref, dbcv_ref),
                        (du_g, s1_g, s2_g, x_g, wcg, dupg_ref, dwcg_ref, dbcg_ref)):
                    dup = wc[2:3] * du + wc[1:2] * pltpu.roll(du, r + 7, 0) + wc[0:1] * pltpu.roll(du, r + 6, 0)
                    dup_ref[rows, lanes] = dup[:r].astype(BF16)
                    du = du[:r]
                    dwc_ref[0:1, lanes] += jnp.sum(du * s2[:r], axis=0, keepdims=True)
                    dwc_ref[1:2, lanes] += jnp.sum(du * s1[:r], axis=0, keepdims=True)
                    dwc_ref[2:3, lanes] += jnp.sum(du * x[:r], axis=0, keepdims=True)
                    dbc_ref[:, lanes] += jnp.sum(du, axis=0, keepdims=True)

            emit(first_v, first_g, da_ref[0:r + 8, lanes], 0)

            def loop(i, carry, emit=emit, mid_v=mid_v, mid_g=mid_g, lanes=lanes):
                start = pl.multiple_of(i * r, r)
                emit(mid_v(i), mid_g(i), da_ref[pl.ds(start, r + 8), lanes], start)
                return carry

            lax.fori_loop(1, n - 1, loop, 0)
            da_last = jnp.concatenate([da_ref[ts - r:ts, lanes], dan_ref[:, lanes] * keep_next], axis=0)
            emit(last_v, last_g, da_last, ts - r)

    return _hosted_call(
        body, comm, name="conv_glu_bwd", grid=(N_FT, nblk),
        out_shape=[jax.ShapeDtypeStruct((s, D_FF), BF16)] * 2 + [jax.ShapeDtypeStruct((8, D_FF), F32)] * 2
        + [jax.ShapeDtypeStruct((1, D_FF), F32)] * 2,
        in_specs=[cur, halo, nxt, cur, halo, nxt, cur, nxt, *wcs, *bcs],
        out_specs=[cur, cur, acc_w, acc_w, acc_b, acc_b],
        scratch_shapes=[], dims=("parallel", "arbitrary"),
        args=(up_v, up_v, up_v, up_g, up_g, up_g, da, da, wc, wc, bc, bc))


def _tri(kind):
    b = SB_B
    m = {"lower_strict": np.tril(np.ones((b, b)), -1), "upper_incl": np.triu(np.ones((b, b)), 0),
         "upper_strict": np.triu(np.ones((b, b)), 1)}[kind]
    return jnp.asarray(np.concatenate([m, np.ones((b, 128))], axis=1), BF16)


def _key_sums(x, tri):
    cb = _dot(x.astype(BF16), tri)
    return cb[:, :SB_B], cb[:, SB_B:]


def _sb_fwd(proj, comm=None):
    s = proj.shape[0]
    b = SB_B

    def body(q_ref, k_ref, v_ref, tri_ref, o_ref, t_ref, first_ref, c_ref, a_ref):
        i = pl.program_id(1)
        lane = lax.broadcasted_iota(jnp.int32, (b, 128), 1)
        heads = (lane < 64, lane >= 64)
        causal = lax.broadcasted_iota(jnp.int32, (b, b), 1) < lax.broadcasted_iota(jnp.int32, (b, b), 0)
        q = q_ref[...] * 0.125
        qm = [jnp.where(h, q, 0.0).astype(BF16) for h in heads]
        c_ref[...] = jnp.zeros_like(c_ref)
        a_ref[...] = jnp.zeros_like(a_ref)

        def prepare(jj, masked):
            rows = pl.ds(pl.multiple_of(jj * b, b), b)
            kb = k_ref[rows, :].astype(BF16)
            vb = v_ref[rows, :]
            out = []
            for hh in range(2):
                z = _dot_nt(qm[hh], kb)
                lg = _log_sigmoid(-z)
                if masked:
                    lg = jnp.where(causal, lg, 0.0)
                after, total = _key_sums(lg, tri_ref[...])
                out.append((lg + z + after, total, jnp.where(heads[hh], vb, 0.0).astype(BF16)))
            return out

        def walk_blocks(blocks):
            for hh in range(2):
                c, a = c_ref[hh], a_ref[hh]
                for pre, masked in blocks:
                    logw, total, vm = pre[hh]
                    w = jnp.exp(logw + jnp.concatenate([c, c], axis=1))
                    if masked:
                        w = jnp.where(causal, w, 0.0)
                    a = a + _dot(w.astype(BF16), vm)
                    c = c + total
                c_ref[hh], a_ref[hh] = c, a

        def largest_sum():
            return jnp.max(jnp.maximum(c_ref[0], c_ref[1]))

        @pl.when(i == 0)
        def _():
            walk_blocks([(prepare(i, True), True)])

        @pl.when(i > 0)
        def _():
            walk_blocks([(prepare(i, True), True), (prepare(i - 1, False), False)])

        def more(state):
            jj, top = state
            return jnp.logical_and(jj >= 0, top > SB_DEAD)

        def walk(state):
            jj, _ = state
            walk_blocks([(prepare(jj, False), False)])
            return jj - 1, largest_sum()

        jj, _ = lax.while_loop(more, walk, (jnp.maximum(i - 2, -1), largest_sum()))
        o_ref[...] = (a_ref[0] + a_ref[1]).astype(BF16)
        t_ref[...] = jnp.concatenate([c_ref[0], c_ref[1]], axis=1)
        first_ref[pl.program_id(0), i] = (jj + 1).astype(F32)

    return _hosted_call(
        body, comm, name="sb_fwd", grid=(4, s // b),
        out_shape=[jax.ShapeDtypeStruct((s, SB_W), BF16), jax.ShapeDtypeStruct((s, 2 * SB_W), F32),
                   jax.ShapeDtypeStruct((4, s // b), F32)],
        in_specs=[pl.BlockSpec((b, 128), lambda p, i: (i, p)),
                  pl.BlockSpec((s, 128), lambda p, i: (0, 4 + p)),
                  pl.BlockSpec((s, 128), lambda p, i: (0, 8 + p)),
                  pl.BlockSpec((b, b + 128), lambda p, i: (0, 0))],
        out_specs=[pl.BlockSpec((b, 128), lambda p, i: (i, p)), pl.BlockSpec((b, 256), lambda p, i: (i, p)),
                   pl.BlockSpec(memory_space=pltpu.SMEM)],
        scratch_shapes=[pltpu.VMEM((2, b, 128), F32), pltpu.VMEM((2, b, 128), F32)],
        dims=("arbitrary", "arbitrary"), args=(proj, proj, proj, _tri("lower_strict")))


def _sb_bwd(proj, do, tot, first, comm=None):
    s = proj.shape[0]
    b = SB_B

    def body(q_ref, k_ref, v_ref, do_ref, t_ref, first_ref, ti_ref, ts_ref, dq_ref, dko_ref, dvo_ref, cl_ref, ce_ref,
             a_ref, dk_ref, dv_ref):
        i = pl.program_id(1)
        first = jnp.clip(first_ref[pl.program_id(0), i].astype(jnp.int32), 0, i)
        lane = lax.broadcasted_iota(jnp.int32, (b, 128), 1)
        heads = (lane < 64, lane >= 64)
        causal = lax.broadcasted_iota(jnp.int32, (b, b), 1) < lax.broadcasted_iota(jnp.int32, (b, b), 0)
        q = q_ref[...] * 0.125
        do = do_ref[...]
        qm = [jnp.where(h, q, 0.0).astype(BF16) for h in heads]
        dom = [jnp.where(h, do, 0.0).astype(BF16) for h in heads]
        cl_ref[...] = jnp.zeros_like(cl_ref)
        ce_ref[...] = jnp.zeros_like(ce_ref)
        a_ref[...] = jnp.zeros_like(a_ref)

        @pl.when(i == 0)
        def _():
            dk_ref[...] = jnp.zeros_like(dk_ref)
            dv_ref[...] = jnp.zeros_like(dv_ref)

        def prepare(jj, masked):
            rows = pl.ds(pl.multiple_of(jj * b, b), b)
            kf = k_ref[rows, :]
            kb = kf.astype(BF16)
            vb = v_ref[rows, :].astype(BF16)
            out = []
            for hh in range(2):
                z = _dot_nt(qm[hh], kb)
                lg = _log_sigmoid(-z)
                if masked:
                    lg = jnp.where(causal, lg, 0.0)
                upto, total = _key_sums(lg, ti_ref[...])
                lsz = lg + z
                out.append((lsz - upto, total, jnp.exp(lsz), _dot_nt(dom[hh], vb),
                            jnp.where(heads[hh], kf, 0.0).astype(BF16)))
            return rows, out

        def walk_blocks(blocks):
            grads = [[jnp.zeros((b, 128), F32), jnp.zeros((b, 128), F32)] for _ in blocks]
            cl = [cl_ref[0], cl_ref[1]]
            ce = [ce_ref[0], ce_ref[1]]
            a = [a_ref[0], a_ref[1]]
            for n, ((_, pre), masked) in enumerate(blocks):
                for hh in range(2):
                    t = t_ref[:, hh * 128:(hh + 1) * 128]
                    logw, total, sig, dw, km = pre[hh]
                    w = jnp.exp(logw + jnp.concatenate([t - cl[hh], t - cl[hh]], axis=1))
                    if masked:
                        w = jnp.where(causal, w, 0.0)
                    e = w * dw
                    sums = _dot(e.astype(BF16), ts_ref[...])
                    before, etot = sums[:, :SB_B], sums[:, SB_B:]
                    dz = e - sig * (e + before + jnp.concatenate([ce[hh], ce[hh]], axis=1))
                    if masked:
                        dz = jnp.where(causal, dz, 0.0)
                    dzb = dz.astype(BF16)
                    a[hh] = a[hh] + _dot(dzb, km)
                    grads[n][0] = grads[n][0] + _dot_tn(dzb, qm[hh])
                    grads[n][1] = grads[n][1] + _dot_tn(w.astype(BF16), dom[hh])
                    cl[hh] = cl[hh] + total
                    ce[hh] = ce[hh] + etot
            for hh in range(2):
                cl_ref[hh], ce_ref[hh], a_ref[hh] = cl[hh], ce[hh], a[hh]
            for ((rows, _), _), (dk, dv) in zip(blocks, grads):
                dk_ref[rows, :] += dk
                dv_ref[rows, :] += dv

        def loop(jj, carry):
            walk_blocks([(prepare(jj, False), False)])
            return carry

        lax.fori_loop(first, i - 1, loop, 0)

        @pl.when(i == 0)
        def _():
            walk_blocks([(prepare(i, True), True)])

        @pl.when(i > 0)
        def _():
            walk_blocks([(prepare(i - 1, False), False), (prepare(i, True), True)])

        dq_ref[...] = ((a_ref[0] + a_ref[1]) * 0.125).astype(BF16)

        @pl.when(i == s // b - 1)
        def _():
            dko_ref[...] = dk_ref[...].astype(BF16)
            dvo_ref[...] = dv_ref[...].astype(BF16)

    blk = pl.BlockSpec((b, 128), lambda p, i: (i, p))
    full = pl.BlockSpec((s, 128), lambda p, i: (0, p))
    tri = pl.BlockSpec((b, b + 128), lambda p, i: (0, 0))
    return _hosted_call(
        body, comm, name="sb_bwd", grid=(4, s // b),
        out_shape=[jax.ShapeDtypeStruct((s, SB_W), BF16)] * 3,
        in_specs=[blk, pl.BlockSpec((s, 128), lambda p, i: (0, 4 + p)), pl.BlockSpec((s, 128), lambda p, i: (0, 8 + p)),
                  blk, pl.BlockSpec((b, 256), lambda p, i: (i, p)), pl.BlockSpec(memory_space=pltpu.SMEM), tri, tri],
        out_specs=[blk, full, full],
        scratch_shapes=[pltpu.VMEM((2, b, 128), F32)] * 3 + [pltpu.VMEM((s, 128), F32)] * 2,
        dims=("arbitrary", "arbitrary"),
        args=(proj, proj, proj, do, tot, first, _tri("upper_incl"), _tri("upper_strict")))


_COL_Q, _COL_K, _COL_V, _COL_G, _COL_F = 12, 14, 8, 10, 24


def _gla_consts():
    c = CHUNK
    incl = np.tril(np.ones((c, c)), 0)
    strict = np.tril(np.ones((c, c)), -1)
    bd = np.zeros((256, 128))
    bd[:128, :64] = 1.0
    bd[128:, 64:] = 1.0
    return jnp.asarray(incl, BF16), jnp.asarray(strict, BF16), jnp.asarray(bd, F32)


def _time_sums(tri, x):
    hi, lo = _hilo(x)
    return _dot(tri, hi) + _dot(tri, lo)


def _gla_gate(o, gg, g):
    parts = []
    for h in range(2):
        oh = o[:, h * 128:(h + 1) * 128]
        parts.append(oh * lax.rsqrt(jnp.mean(oh * oh, axis=-1, keepdims=True) + EPS))
    return (jnp.concatenate(parts, axis=1) * g) * (gg * _sigmoid(gg))


def _gla_chunk(la_c, k_c, incl):
    cum = _time_sums(incl, la_c)
    total = cum[CHUNK - 1:CHUNK]
    edec = jnp.exp(total - cum)
    return edec, k_c * edec, jnp.exp(total)


def _gla_fwd(proj, wfg, bfg, gain):
    s = proj.shape[0]
    t = GLA_T
    nch = t // CHUNK
    incl, _, bd = _gla_consts()

    def body(q_ref, k_ref, v_ref, gg_ref, f_ref, wf_ref, bf_ref, g_ref, incl_ref, bd_ref, o_ref, st_ref, state_ref):
        @pl.when(pl.program_id(0) == 0)
        def _():
            state_ref[...] = jnp.zeros_like(state_ref)

        la = _log_sigmoid(_dot(f_ref[...].astype(BF16), wf_ref[...].astype(BF16)) + bf_ref[...]) * (1.0 / 16.0)
        states = [state_ref[0], state_ref[1]]
        local = {}
        for cc in range(nch):
            rows = slice(cc * CHUNK, (cc + 1) * CHUNK)
            for p in range(2):
                kl, vl = slice(p * 128, (p + 1) * 128), slice(p * 256, (p + 1) * 256)
                _, kdec, dec = _gla_chunk(la[rows, kl], k_ref[rows, kl], incl_ref[...])
                local[cc, p] = (dec, bd_ref[...] * _dot_tn(v_ref[rows, vl].astype(BF16), kdec.astype(BF16)))
        for cc in range(nch):
            rows = slice(cc * CHUNK, (cc + 1) * CHUNK)
            for p in range(2):
                kl, vl = slice(p * 128, (p + 1) * 128), slice(p * 256, (p + 1) * 256)
                dec, kv = local[cc, p]
                states[p] = states[p] * dec + kv
                st_ref[p, cc] = states[p]
                o = _dot_nt((q_ref[rows, kl] * 0.125).astype(BF16), states[p].astype(BF16))
                o_ref[rows, vl] = _gla_gate(o, gg_ref[rows, vl], g_ref[:, vl]).astype(BF16)
        state_ref[0] = states[0]
        state_ref[1] = states[1]

    def col(width, blk):
        return pl.BlockSpec((t, width), lambda i: (i, blk))

    def whole(shape):
        return pl.BlockSpec(shape, lambda i: tuple(0 for _ in shape))

    return pl.pallas_call(
        body, name="gla_fwd", grid=(s // t,),
        out_shape=[jax.ShapeDtypeStruct((s, GLA_VW), BF16), jax.ShapeDtypeStruct((2, s // CHUNK, 256, 128), F32)],
        in_specs=[col(256, _COL_Q // 2), col(256, _COL_K // 2), col(512, _COL_V // 2), col(512, _COL_G // 2),
                  col(128, _COL_F), whole((128, GLA_KW)), whole((1, GLA_KW)), whole((1, GLA_VW)),
                  whole((CHUNK, CHUNK)), whole((256, 128))],
        out_specs=[col(512, 0), pl.BlockSpec((2, nch, 256, 128), lambda i: (0, i, 0, 0))],
        scratch_shapes=[pltpu.VMEM((2, 256, 128), F32)],
        compiler_params=_params(("arbitrary",)),
    )(proj, proj, proj, proj, proj, wfg, bfg, gain, incl, bd)


def _gla_bwd(proj, do, states, wfg, bfg, gain):
    s = proj.shape[0]
    t = GLA_T
    nch = t // CHUNK
    nblk = s // t
    incl, strict, bd = _gla_consts()

    def body(q_ref, k_ref, v_ref, gg_ref, f_ref, do_ref, st_ref, sp_ref, wf_ref, bf_ref, g_ref, incl_ref, str_ref,
             bd_ref, dq_ref, dk_ref, dv_ref, dgg_ref, df_ref, dwf_ref, dbf_ref, dg_ref, carry_ref, dfs_ref):
        i = pl.program_id(0)

        @pl.when(i == 0)
        def _():
            carry_ref[...] = jnp.zeros_like(carry_ref)
            dwf_ref[...] = jnp.zeros_like(dwf_ref)
            dbf_ref[...] = jnp.zeros_like(dbf_ref)
            dg_ref[...] = jnp.zeros_like(dg_ref)

        fb = f_ref[...].astype(BF16)
        wf = wf_ref[...].astype(BF16)
        f = _dot(fb, wf) + bf_ref[...]
        la = _log_sigmoid(f) * (1.0 / 16.0)
        dla_df = _sigmoid(-f) * (1.0 / 16.0)
        first_block = jnp.where(i == nblk - 1, 0.0, 1.0)
        carries = [carry_ref[0], carry_ref[1]]
        dgains = [jnp.zeros((1, 256), F32), jnp.zeros((1, 256), F32)]
        local = {}
        for cc in reversed(range(nch)):
            rows = slice(cc * CHUNK, (cc + 1) * CHUNK)
            for p in range(2):
                kl, vl = slice(p * 128, (p + 1) * 128), slice(p * 256, (p + 1) * 256)
                edec, kdec, dec = _gla_chunk(la[rows, kl], k_ref[rows, kl], incl_ref[...])
                qs = (q_ref[rows, kl] * 0.125).astype(BF16)
                sb16 = st_ref[p, cc].astype(BF16)
                o = _dot_nt(qs, sb16)
                _, vjp = jax.vjp(_gla_gate, o, gg_ref[rows, vl], g_ref[:, vl])
                do_c, dgg_c, dg_c = vjp(do_ref[rows, vl])
                dgains[p] = dgains[p] + dg_c
                dgg_ref[rows, vl] = dgg_c.astype(BF16)
                do16 = do_c.astype(BF16)
                dq_ref[rows, kl] = (_dot(do16, sb16) * 0.125).astype(BF16)
                local[cc, p] = (edec, kdec, dec, bd_ref[...] * _dot_tn(do16, qs))
        for cc in reversed(range(nch)):
            rows = slice(cc * CHUNK, (cc + 1) * CHUNK)
            for p in range(2):
                kl, vl = slice(p * 128, (p + 1) * 128), slice(p * 256, (p + 1) * 256)
                edec, kdec, dec, readers = local[cc, p]
                prev = st_ref[p, cc - 1] if cc > 0 else sp_ref[p, 0] * first_block
                grad = readers + carries[p]
                g16 = grad.astype(BF16)
                dv_ref[rows, vl] = _dot_nt(kdec.astype(BF16), g16).astype(BF16)
                dkdec = _dot(v_ref[rows, vl].astype(BF16), g16)
                ddec = jnp.sum(grad * prev, axis=0, keepdims=True) * dec
                dk_ref[rows, kl] = (dkdec * edec).astype(BF16)
                dla = _time_sums(str_ref[...], dkdec * kdec) + ddec
                dfs_ref[rows, kl] = dla * dla_df[rows, kl]
                carries[p] = grad * dec
        carry_ref[0] = carries[0]
        carry_ref[1] = carries[1]
        df = dfs_ref[...]
        df16 = df.astype(BF16)
        df_ref[...] = _dot_nt(df16, wf).astype(BF16)
        dwf_ref[...] += _dot_tn(fb, df16)
        dbf_ref[...] += jnp.sum(df, axis=0, keepdims=True)
        dg_ref[...] += jnp.concatenate(dgains, axis=1)

    def col(width, blk):
        return pl.BlockSpec((t, width), lambda i: (nblk - 1 - i, blk))

    def whole(shape):
        return pl.BlockSpec(shape, lambda i: tuple(0 for _ in shape))

    return pl.pallas_call(
        body, name="gla_bwd", grid=(nblk,),
        out_shape=[jax.ShapeDtypeStruct((s, GLA_KW), BF16), jax.ShapeDtypeStruct((s, GLA_KW), BF16),
                   jax.ShapeDtypeStruct((s, GLA_VW), BF16), jax.ShapeDtypeStruct((s, GLA_VW), BF16),
                   jax.ShapeDtypeStruct((s, 128), BF16), jax.ShapeDtypeStruct((128, GLA_KW), F32),
                   jax.ShapeDtypeStruct((1, GLA_KW), F32), jax.ShapeDtypeStruct((1, GLA_VW), F32)],
        in_specs=[col(256, _COL_Q // 2), col(256, _COL_K // 2), col(512, _COL_V // 2), col(512, _COL_G // 2),
                  col(128, _COL_F), col(512, 1),
                  pl.BlockSpec((2, nch, 256, 128), lambda i: (0, nblk - 1 - i, 0, 0)),
                  pl.BlockSpec((2, 1, 256, 128), lambda i: (0, jnp.maximum((nblk - 1 - i) * nch - 1, 0), 0, 0)),
                  whole((128, GLA_KW)), whole((1, GLA_KW)), whole((1, GLA_VW)),
                  whole((CHUNK, CHUNK)), whole((CHUNK, CHUNK)), whole((256, 128))],
        out_specs=[col(256, 0), col(256, 0), col(512, 0), col(512, 0), col(128, 0),
                   whole((128, GLA_KW)), whole((1, GLA_KW)), whole((1, GLA_VW))],
        scratch_shapes=[pltpu.VMEM((2, 256, 128), F32), pltpu.VMEM((t, GLA_KW), F32)],
        compiler_params=_params(("arbitrary",)),
    )(proj, proj, proj, proj, proj, do, states, states, wfg, bfg, gain, incl, strict, bd)


def _cols_by_dev(a, n_dev=N_DEV):
    r, c = a.shape
    return a.reshape(r, n_dev, c // n_dev).transpose(1, 0, 2)


def _cols_from_dev(a):
    _, r, n = a.shape
    return a.transpose(1, 0, 2).reshape(r, N_DEV * n)


def kernel(x, c, w_ada, b_ada, g_norm1, w_in, w_fg2, b_fg2, g_gla_out, w_out, g_norm2, w_up, w_conv, b_conv, w_down, g_final, loss_target, m_w_ada, m_b_ada, m_g_norm1, m_w_in, m_w_fg2, m_b_fg2, m_g_gla_out, m_w_out, m_g_norm2, m_w_up, m_w_conv, m_b_conv, m_w_down, m_g_final, v_w_ada, v_b_ada, v_g_norm1, v_w_in, v_w_fg2, v_b_fg2, v_g_gla_out, v_w_out, v_g_norm2, v_w_up, v_w_conv, v_b_conv, v_w_down, v_g_final):
    s = x.shape[1]
    me = 4 * lax.axis_index("x") + 2 * lax.axis_index("y") + lax.axis_index("c")
    xs, tgt = x[0], loss_target[0]
    ts = min(512, s)
    tm = min(1024, s)
    tc = min(4096, s // 2)

    c_all, wconv_g, wfg_g = _exchange([c, w_conv[0], w_fg2[0]], "gather_first", True)
    wconv_p = jnp.pad(_cols_from_dev(wconv_g), ((0, 5), (0, 0)))
    wfg_p = jnp.pad(_cols_from_dev(wfg_g), ((0, 128 - RANK), (0, 0)))

    c16 = jnp.pad(c_all.reshape(N_DEV, D), ((0, 8), (0, 0)))
    modp = _ada_fwd(c16, w_ada[0])[:N_DEV]
    (mod_all,) = _exchange([modp], "gather_mod", True)
    mod = lax.dynamic_index_in_dim(mod_all, me, axis=1, keepdims=False).reshape(1, 6 * D) + b_ada
    shift1, scale1, gate1, shift2, scale2, gate2 = [mod[:, k * D:(k + 1) * D] for k in range(6)]

    w_in_t, m_in_t, v_in_t = [jnp.swapaxes(t[0], 0, 1) for t in (w_in, m_w_in, v_w_in)]
    w_up_t, m_up_t, v_up_t = [jnp.swapaxes(t[0], 0, 1) for t in (w_up, m_w_up, v_w_up)]
    (h,), (win_g,) = _rows_call(_fwd_norm1, "norm1", [xs], [g_norm1, scale1, shift1], [(D, BF16)], [], ts,
                                comm=([w_in_t.astype(BF16)], True))
    win_t = jnp.pad(win_g.reshape(IN_W, D), ((0, IN_WP - IN_W), (0, 0)))
    proj = _mm_nt(h, win_t, "in_proj", F32, ts, IN_WP)
    (o_sb, sb_tot, sb_first), (wout_g, wup_g, wdown_g) = _sb_fwd(
        proj, comm=([w_out[0].astype(BF16), w_up_t.astype(BF16), w_down[0].astype(BF16)], True))
    wout = wout_g.reshape(D, D)
    wup_t = wup_g.reshape(2 * D_FF, D)
    wdown = wdown_g.reshape(D_FF, D)
    o_gla, states = _gla_fwd(proj, wfg_p, b_fg2, g_gla_out)
    mixed, _ = _mm([o_sb, o_gla], wout, "out_proj", F32, tm, 512)
    x1, h2 = _rows_call(_fwd_resid_norm, "resid_norm2", [xs, mixed], [gate1, g_norm2, scale2, shift2],
                        [(D, F32), (D, BF16)], [], ts)
    up_v, up_g = _mm_nt(h2, wup_t, "up_proj", F32, tm, 1408, n_out=2)
    act = _conv_glu_fwd(up_v, up_g, wconv_p, b_conv, tc)
    ffn, _ = _mm([act], wdown, "down_proj", F32, tm, 512)
    dx2, dffn, dgate2, dg_final, loss_part = _rows_call(
        _final, "final", [x1, ffn, tgt], [gate2, g_final.reshape(1, D)],
        [(D, F32), (D, BF16)], [(1, D), (1, D), (1, 128)], ts)

    (dw_down,) = _mm_tn(act, [dffn], "down_wgrad", 1408, 1024, tm)
    da = _mm_nt(dffn, wdown, "down_dgrad", F32, tm, 1408)
    (dup_v, dup_g, dwc_v, dwc_g, dbc_v, dbc_g), (r_down,) = _conv_glu_bwd(
        up_v, up_g, da, wconv_p, b_conv, tc, comm=([dw_down.reshape(N_DEV, 352, D)], False))
    (dw_up_tv,) = _mm_tn(dup_v, [h2], "up_wgrad_v", 1408, 1024, tm)
    (dw_up_tg,) = _mm_tn(dup_g, [h2], "up_wgrad_g", 1408, 1024, tm)
    dh2, _ = _mm([dup_v, dup_g], wup_t, "up_dgrad", F32, tm, 512)
    dx1, dmixed, dgate1, dg_norm2, dscale2, dshift2 = _rows_call(
        _bwd_resid_norm, "resid_norm2_bwd", [xs, mixed, dx2, dh2], [gate1, g_norm2, scale2, shift2],
        [(D, F32), (D, BF16)], [(1, D)] * 4, ts)
    dw_out = jnp.concatenate(_mm_tn_rows([o_sb, o_gla], dmixed, "out_wgrad", tm), axis=0)
    dcat = _mm_nt(dmixed, wout, "out_dgrad", F32, tm, 512)
    dw_up_s = jnp.concatenate([dw_up_tv, dw_up_tg], axis=0).reshape(N_DEV, 704, D)
    dwconv_s = _cols_by_dev(jnp.concatenate([dwc_v[:3], dwc_g[:3]], axis=1))
    (dq, dk, dv), (r_up, r_out, r_conv) = _sb_bwd(
        proj, dcat, sb_tot, sb_first, comm=([dw_up_s, dw_out.reshape(N_DEV, 128, D), dwconv_s], False))
    dgq, dgk, dgv, dgg, dgf, dwfg_p, dbfg, dg_gla = _gla_bwd(proj, dcat, states, wfg_p, b_fg2, g_gla_out)
    dproj = [dq, dk, dv, dgq, dgk, dgv, dgg, dgf]
    dw_in_t = jnp.concatenate(_mm_tn_rows(dproj, h, "in_wgrad", tm), axis=0)
    dh, (r_in, r_fg) = _mm(dproj, win_t, "in_dgrad", F32, tm, 512,
                           comm=([dw_in_t[:IN_W].reshape(N_DEV, 386, D).astype(BF16), _cols_by_dev(dwfg_p[:RANK])],
                                 False))
    grad_x, dg_norm1, dscale1, dshift1 = _rows_call(
        _bwd_norm1, "norm1_bwd", [xs, dh, dx1], [g_norm1, scale1, shift1], [(D, F32)], [(1, D)] * 3, ts)

    dmod = jnp.concatenate([dshift1, dscale1, dgate1, dshift2, dscale2, dgate2], axis=1)
    small_parts = [dmod, dg_norm1, dbfg, dg_gla, dg_norm2, dbc_v, dbc_g, dg_final, loss_part]
    n_small = sum(p.shape[1] for p in small_parts[:-1])
    (sg,) = _exchange([jnp.concatenate(small_parts, axis=1)], "gather_small_grads", True)
    sg = sg.reshape(N_DEV, n_small + 128)
    loss = jnp.sum(sg[:, n_small])
    small_w = [b_ada, g_norm1, b_fg2, g_gla_out, g_norm2, b_conv, g_final.reshape(1, D)]
    small_m = [m_b_ada, m_g_norm1, m_b_fg2, m_g_gla_out, m_g_norm2, m_b_conv, m_g_final.reshape(1, D)]
    small_v = [v_b_ada, v_g_norm1, v_b_fg2, v_g_gla_out, v_g_norm2, v_b_conv, v_g_final.reshape(1, D)]
    s_out = list(_adam_rows(sg, small_w, small_m, small_v, "adam_small"))
    s_out[24:] = [t.reshape(D) for t in s_out[24:]]
    s_g, s_d, s_m, s_v = [s_out[k::4] for k in range(4)]

    dmod_all = sg[:, :6 * D].reshape(N_DEV, N_DEV, 768)
    dmod_mine = lax.dynamic_index_in_dim(dmod_all, me, axis=1, keepdims=False)
    dw_ada = _ada_bwd(c16, jnp.pad(dmod_mine, ((0, 8), (0, 0))))
    a_g, a_d, a_m, a_v = [t[None] for t in _adam(dw_ada[None], w_ada[0], m_w_ada[0], v_w_ada[0], "adam_ada", 256)]

    recv = [r_in, r_fg, r_out, r_up, r_conv, r_down]
    big_w = [w_in_t, w_fg2[0], w_out[0], w_up_t, w_conv[0], w_down[0]]
    big_m = [m_in_t, m_w_fg2[0], m_w_out[0], m_up_t, m_w_conv[0], m_w_down[0]]
    big_v = [v_in_t, v_w_fg2[0], v_w_out[0], v_up_t, v_w_conv[0], v_w_down[0]]
    big_name = ["adam_in", "adam_fg2", "adam_out", "adam_up", "adam_conv", "adam_down"]
    big_rows = [386, RANK, 128, 176, 3, 176]
    b_out = [_adam(r, w, m, v, name, tr)
             for r, w, m, v, name, tr in zip(recv, big_w, big_m, big_v, big_name, big_rows)]
    for k in (0, 3):
        b_out[k] = [jnp.swapaxes(t, 0, 1) for t in b_out[k]]
    b_g, b_d, b_m, b_v = [[o[k][None] for o in b_out] for k in range(4)]

    def ordered(a, sm, bg):
        return [a, sm[0], sm[1], bg[0], bg[1], sm[2], sm[3], bg[2], sm[4], bg[3], bg[4], sm[5], bg[5], sm[6]]

    return (loss, grad_x[None], *ordered(a_g, s_g, b_g), *ordered(a_d, s_d, b_d),
            *ordered(a_m, s_m, b_m), *ordered(a_v, s_v, b_v))
```
